```python
import math
import functools
import jax
import jax.numpy as jnp
from jax import lax
import numpy as np

D_MODEL = 1024
BATCH = 1
SEQ = 16384
DEPTH = 1
DEC_BATCH = 128
DEC_SEQ = 4
PAST_LEN = 8192
PAGE_SIZE = 128

NSA_HEAD_DIM = 64
NSA_HEADS = (D_MODEL // 2) // NSA_HEAD_DIM
NSA_KV_HEADS = NSA_HEADS // 4
NSA_GROUP = NSA_HEADS // NSA_KV_HEADS
NSA_WIDTH = NSA_HEADS * NSA_HEAD_DIM
NSA_KV_WIDTH = NSA_KV_HEADS * NSA_HEAD_DIM
BLOCK = 64
TOP_K = 16
WINDOW = 512
Q_BLOCK = 128
FORCED_SCORE = NSA_GROUP + 1.0
ROPE_THETA = 500000.0
ROPE_DIM = NSA_HEAD_DIM // 4
GDN_HEAD_DIM = 128
GDN_HEADS = (D_MODEL // 2) // GDN_HEAD_DIM
GDN_WIDTH = GDN_HEADS * GDN_HEAD_DIM
CONV_WIDTH = 4
GDN_CHUNK = 64
MIX_WIDTH = NSA_WIDTH + GDN_WIDTH
D_FF = 4 * D_MODEL
EPS = 1e-6
SPLIT_SIZES = (NSA_WIDTH, NSA_KV_WIDTH, NSA_KV_WIDTH, NSA_KV_WIDTH, NSA_KV_WIDTH, NSA_KV_WIDTH, NSA_KV_WIDTH, 3 * NSA_HEADS, 3 * GDN_WIDTH, GDN_WIDTH, GDN_HEADS, GDN_HEADS)
IN_COLS = sum(SPLIT_SIZES)

kernel_name = 'nsa_gdn_hybrid_step'


def rms_norm(x, w):
    xf = x.astype(jnp.float32)
    y = xf * lax.rsqrt(jnp.mean(jnp.square(xf), axis=-1, keepdims=True) + EPS)
    return (y * w.astype(jnp.float32)).astype(x.dtype)


def l2_normalize(x):
    xf = x.astype(jnp.float32)
    return xf * lax.rsqrt(jnp.sum(jnp.square(xf), axis=-1, keepdims=True) + EPS)


def partial_rope(x, pos):
    half = ROPE_DIM // 2
    inv_freq = ROPE_THETA ** (-jnp.arange(half, dtype=jnp.float32) / half)
    ang = pos.astype(jnp.float32)[:, None] * inv_freq[None, :]
    cos = jnp.cos(ang)[None, :, None, :]
    sin = jnp.sin(ang)[None, :, None, :]
    xr = x[..., :ROPE_DIM].astype(jnp.float32)
    x1, x2 = xr[..., :half], xr[..., half:]
    rot = jnp.concatenate([x1 * cos - x2 * sin, x2 * cos + x1 * sin], axis=-1).astype(x.dtype)
    return jnp.concatenate([rot, x[..., ROPE_DIM:]], axis=-1)


def masked_softmax(s, mask):
    s = jnp.where(mask, s.astype(jnp.float32), -jnp.inf)
    m = jnp.max(s, axis=-1, keepdims=True)
    e = jnp.exp(s - jnp.where(jnp.isfinite(m), m, 0.0))
    return e / jnp.maximum(jnp.sum(e, axis=-1, keepdims=True), 1e-30)


def split_projection(p):
    points = [int(c) for c in np.cumsum(SPLIT_SIZES)[:-1]]
    return jnp.split(p, points, axis=-1)


def compress_blocks(k, w):
    b, n = k.shape[0], k.shape[1] // BLOCK
    kb = k[:, : n * BLOCK].reshape(b, n, BLOCK, k.shape[2], k.shape[3])
    return jnp.einsum('bnlhd,ld->bnhd', kb, w.astype(k.dtype))


def nsa_attend_block(q, q_pos, ck, cv, sel_gather, n_blocks, wk, wv, w_pos, gates):
    scale = NSA_HEAD_DIM ** -0.5
    n_cmp = ck.shape[1]
    s_c = jnp.einsum('bqhgd,bnhd->bqhgn', q, ck) * scale
    c_end = (jnp.arange(n_cmp) + 1) * BLOCK - 1
    p_c = masked_softmax(s_c, (c_end[None, :] <= q_pos[:, None])[None, :, None, None, :])
    o_c = jnp.einsum('bqhgn,bnhd->bqhgd', p_c.astype(cv.dtype), cv)
    imp = jnp.pad(jnp.sum(p_c, axis=3), ((0, 0), (0, 0), (0, 0), (0, n_blocks - n_cmp)))
    blk = jnp.arange(n_blocks)[None, :]
    cur = (q_pos // BLOCK)[:, None]
    forced = ((blk == 0) | (blk == cur) | (blk == cur - 1))[None, :, None, :]
    valid = (blk <= cur)[None, :, None, :]
    score = jnp.where(forced, FORCED_SCORE, jnp.where(valid, imp, -1.0))
    top_s, top_i = lax.top_k(score, min(TOP_K, n_blocks))
    k_sel, v_sel = sel_gather(top_i)
    bq, nq, nh, nk = top_i.shape
    key_pos = top_i[..., None] * BLOCK + jnp.arange(BLOCK)
    sel_mask = (top_s[..., None] >= 0) & (key_pos <= q_pos[None, :, None, None, None])
    sel_mask = sel_mask.reshape(bq, nq, nh, 1, nk * BLOCK)
    k_sel = k_sel.reshape(bq, nq, nh, nk * BLOCK, NSA_HEAD_DIM)
    v_sel = v_sel.reshape(bq, nq, nh, nk * BLOCK, NSA_HEAD_DIM)
    p_s = masked_softmax(jnp.einsum('bqhgd,bqhkd->bqhgk', q, k_sel) * scale, sel_mask)
    o_s = jnp.einsum('bqhgk,bqhkd->bqhgd', p_s.astype(v_sel.dtype), v_sel)
    dist = q_pos[:, None] - w_pos[None, :]
    w_mask = ((dist >= 0) & (dist < WINDOW) & (w_pos[None, :] >= 0))[None, :, None, None, :]
    p_w = masked_softmax(jnp.einsum('bqhgd,bwhd->bqhgw', q, wk) * scale, w_mask)
    o_w = jnp.einsum('bqhgw,bwhd->bqhgd', p_w.astype(wv.dtype), wv)
    g = jax.nn.sigmoid(gates.astype(jnp.float32))
    o = g[..., 0:1] * o_c.astype(jnp.float32) + g[..., 1:2] * o_s.astype(jnp.float32) + g[..., 2:3] * o_w.astype(jnp.float32)
    return o.astype(q.dtype)


def nsa_prompt(q, kc, vc, ks, vs, kw, vw, gates, w_cmp_k, w_cmp_v):
    b, s = q.shape[:2]
    ck = compress_blocks(kc, w_cmp_k)
    cv = compress_blocks(vc, w_cmp_v)
    n_blocks = s // BLOCK
    ksb = ks.reshape(b, n_blocks, BLOCK, NSA_KV_HEADS, NSA_HEAD_DIM).transpose(0, 1, 3, 2, 4)
    vsb = vs.reshape(b, n_blocks, BLOCK, NSA_KV_HEADS, NSA_HEAD_DIM).transpose(0, 1, 3, 2, 4)
    bidx = jnp.arange(b)[:, None, None, None]
    hidx = jnp.arange(NSA_KV_HEADS)[None, None, :, None]

    def sel_gather(top_i):
        return ksb[bidx, top_i, hidx], vsb[bidx, top_i, hidx]

    kw_pad = jnp.pad(kw, ((0, 0), (WINDOW, 0), (0, 0), (0, 0)))
    vw_pad = jnp.pad(vw, ((0, 0), (WINDOW, 0), (0, 0), (0, 0)))
    n_qb = s // Q_BLOCK
    q_blk = jnp.moveaxis(q.reshape(b, n_qb, Q_BLOCK, NSA_KV_HEADS, NSA_GROUP, NSA_HEAD_DIM), 1, 0)
    g_blk = jnp.moveaxis(gates.reshape(b, n_qb, Q_BLOCK, NSA_KV_HEADS, NSA_GROUP, 3), 1, 0)

    def body(args):
        i, q_i, g_i = args
        start = i * Q_BLOCK
        q_pos = start + jnp.arange(Q_BLOCK)
        wk = lax.dynamic_slice_in_dim(kw_pad, start, WINDOW + Q_BLOCK, axis=1)
        wv = lax.dynamic_slice_in_dim(vw_pad, start, WINDOW + Q_BLOCK, axis=1)
        w_pos = start - WINDOW + jnp.arange(WINDOW + Q_BLOCK)
        return nsa_attend_block(q_i, q_pos, ck, cv, sel_gather, n_blocks, wk, wv, w_pos, g_i)

    o = lax.map(body, (jnp.arange(n_qb), q_blk, g_blk))
    o = jnp.moveaxis(o, 0, 1).reshape(b, s, NSA_WIDTH)
    wb = min(WINDOW, s)
    return o, (kc, vc, ks, vs, kw[:, s - wb:], vw[:, s - wb:])


def nsa_sample(q, kc, vc, ks, vs, kw, vw, gates, pool_ck, pool_cv, pool_sk, pool_sv, win_k, win_v, page_table, layer, w_cmp_k, w_cmp_v):
    b, t = q.shape[:2]
    past = page_table.shape[1] * PAGE_SIZE

    def paged_rows(pool):
        return pool[layer, page_table].reshape(b, past, NSA_KV_HEADS, NSA_HEAD_DIM)

    ck = compress_blocks(jnp.concatenate([paged_rows(pool_ck), kc], axis=1), w_cmp_k)
    cv = compress_blocks(jnp.concatenate([paged_rows(pool_cv), vc], axis=1), w_cmp_v)
    n_past_blk = past // BLOCK
    n_new_blk = -(-t // BLOCK)
    pad = n_new_blk * BLOCK - t

    def new_blocks(x):
        x = jnp.pad(x, ((0, 0), (0, pad), (0, 0), (0, 0)))
        return x.reshape(b, n_new_blk, BLOCK, NSA_KV_HEADS, NSA_HEAD_DIM).transpose(0, 1, 3, 2, 4)

    ksn, vsn = new_blocks(ks), new_blocks(vs)
    bidx = jnp.arange(b)[:, None, None, None]
    hidx = jnp.arange(NSA_KV_HEADS)[None, None, :, None]

    def sel_gather(top_i):
        tok = jnp.minimum(top_i, n_past_blk - 1) * BLOCK
        phys = page_table[bidx, tok // PAGE_SIZE][..., None]
        rows = (tok % PAGE_SIZE)[..., None] + jnp.arange(BLOCK)
        hh = hidx[..., None]
        new_i = jnp.clip(top_i - n_past_blk, 0, n_new_blk - 1)
        is_past = (top_i < n_past_blk)[..., None, None]
        k_sel = jnp.where(is_past, pool_sk[layer, phys, rows, hh], ksn[bidx, new_i, hidx])
        v_sel = jnp.where(is_past, pool_sv[layer, phys, rows, hh], vsn[bidx, new_i, hidx])
        return k_sel, v_sel

    wb = win_k.shape[1]
    wk = jnp.concatenate([win_k.astype(kw.dtype), kw], axis=1)
    wv = jnp.concatenate([win_v.astype(vw.dtype), vw], axis=1)
    w_pos = past - wb + jnp.arange(wb + t)
    q_pos = past + jnp.arange(t)
    o = nsa_attend_block(q, q_pos, ck, cv, sel_gather, n_past_blk + n_new_blk, wk, wv, w_pos, gates)
    return o.reshape(b, t, NSA_WIDTH), (kc, vc, ks, vs, wk[:, t:], wv[:, t:])


def nsa_heads(q, kc, vc, ks, vs, kw, vw, gates, pos):
    b, l = q.shape[:2]

    def kv(x):
        return x.reshape(b, l, NSA_KV_HEADS, NSA_HEAD_DIM)

    q = partial_rope(q.reshape(b, l, NSA_HEADS, NSA_HEAD_DIM), pos).reshape(b, l, NSA_KV_HEADS, NSA_GROUP, NSA_HEAD_DIM)
    return (q, partial_rope(kv(kc), pos), kv(vc), partial_rope(kv(ks), pos), kv(vs), partial_rope(kv(kw), pos), kv(vw), gates.reshape(b, l, NSA_KV_HEADS, NSA_GROUP, 3))


def gated_delta_chunked(q, k, v, g, beta, s0):
    b, l, h, dk = q.shape
    dv = v.shape[-1]
    c = min(GDN_CHUNK, l)
    pad = (-l) % c
    n = (l + pad) // c
    f32 = jnp.float32

    def chunks(x):
        x = jnp.pad(x.astype(f32), [(0, 0), (0, pad)] + [(0, 0)] * (x.ndim - 2))
        x = x.reshape((b, n, c) + x.shape[2:])
        return jnp.moveaxis(x, 3, 1)

    q = chunks(q) * (dk ** -0.5)
    k, v, g, beta = chunks(k), chunks(v), chunks(g), chunks(beta)
    gc = jnp.cumsum(g, axis=-1)
    causal = jnp.tril(jnp.ones((c, c), bool))
    strict = jnp.tril(jnp.ones((c, c), bool), -1)
    decay = jnp.exp(jnp.where(causal, gc[..., :, None] - gc[..., None, :], -jnp.inf))
    kb = k * beta[..., None]
    a = jnp.where(strict, jnp.einsum('bhnid,bhnjd->bhnij', kb, k) * decay, 0.0)
    rhs = jnp.concatenate([v * beta[..., None], kb * jnp.exp(gc)[..., None]], axis=-1)
    sol = lax.linalg.triangular_solve(a + jnp.eye(c, dtype=f32), rhs, left_side=True, lower=True)
    u, w = sol[..., :dv], sol[..., dv:]
    intra = jnp.einsum('bhnid,bhnjd->bhnij', q, k) * decay

    def step(s, xs):
        q_i, k_i, u_i, w_i, g_i, a_i = xs
        v_new = u_i - jnp.einsum('bhck,bhkv->bhcv', w_i, s)
        o_i = jnp.einsum('bhck,bhkv->bhcv', q_i * jnp.exp(g_i)[..., None], s) + jnp.einsum('bhij,bhjv->bhiv', a_i, v_new)
        g_last = g_i[..., -1:]
        s = s * jnp.exp(g_last)[..., None] + jnp.einsum('bhck,bhcv->bhkv', k_i * jnp.exp(g_last - g_i)[..., None], v_new)
        return s, o_i

    xs = tuple(jnp.moveaxis(x, 2, 0) for x in (q, k, u, w, gc, intra))
    s_fin, o = lax.scan(step, s0.astype(f32), xs)
    o = jnp.moveaxis(jnp.moveaxis(o, 0, 2), 1, 3).reshape(b, n * c, h, dv)[:, :l]
    return o, s_fin


def gdn_mix(qkv, z, beta_in, alpha_in, conv_prev, s_prev, conv_w, a_log, dt_bias, gdn_norm):
    b, l, _ = qkv.shape
    xp = jnp.concatenate([conv_prev.astype(qkv.dtype), qkv], axis=1)
    conv = sum(xp[:, j:j + l] * conv_w[j] for j in range(CONV_WIDTH))
    conv_state = xp[:, xp.shape[1] - (CONV_WIDTH - 1):]
    q, k, v = jnp.split(jax.nn.silu(conv), 3, axis=-1)

    def heads(x):
        return x.reshape(b, l, GDN_HEADS, GDN_HEAD_DIM)

    beta = jax.nn.sigmoid(beta_in.astype(jnp.float32))
    g = -jnp.exp(a_log.astype(jnp.float32)) * jax.nn.softplus(alpha_in.astype(jnp.float32) + dt_bias.astype(jnp.float32))
    o, s = gated_delta_chunked(l2_normalize(heads(q)), l2_normalize(heads(k)), heads(v), g, beta, s_prev)
    o = rms_norm(o, gdn_norm) * jax.nn.silu(heads(z).astype(jnp.float32))
    return o.reshape(b, l, GDN_WIDTH).astype(qkv.dtype), conv_state, s


def mixing_sublayer(x, pos, nsa_fn, conv_prev, ssm_prev, norm_pre, norm_post, w_in, nsa_out_norm, conv_w, a_log, dt_bias, gdn_norm, w_out):
    h = rms_norm(x, norm_pre)
    (q, kc, vc, ks, vs, kw, vw, gates, qkv, z, beta_in, alpha_in) = split_projection(h @ w_in)
    o_nsa, nsa_state = nsa_fn(*nsa_heads(q, kc, vc, ks, vs, kw, vw, gates, pos))
    o_nsa = rms_norm(o_nsa, nsa_out_norm)
    o_gdn, conv_state, ssm_state = gdn_mix(qkv, z, beta_in, alpha_in, conv_prev, ssm_prev, conv_w, a_log, dt_bias, gdn_norm)
    y = jnp.concatenate([o_nsa, o_gdn.astype(o_nsa.dtype)], axis=-1) @ w_out
    return x + rms_norm(y, norm_post), nsa_state + (conv_state, ssm_state)


def ffn_sublayer(x, norm_pre, norm_post, w_up, w_down):
    u = jnp.square(jax.nn.relu(rms_norm(x, norm_pre) @ w_up))
    return x + rms_norm(u @ w_down, norm_post)


def setup_inputs(seed: int = 0) -> dict:
    key = jax.random.key(seed)
    ks = jax.random.split(key, 26)
    f32 = jnp.float32
    n_pages = PAST_LEN // PAGE_SIZE
    n_used = DEC_BATCH * n_pages
    n_pool = n_used + n_used // 4
    kv_shape = (DEPTH, n_pool, PAGE_SIZE, NSA_KV_HEADS, NSA_HEAD_DIM)
    wb = min(WINDOW, PAST_LEN)

    def nrm(k, shape, scale=1.0):
        return jax.random.normal(k, shape, f32) * scale

    def gain(k, shape):
        return 1.0 + 0.02 * jax.random.normal(k, shape, f32)

    dt = jnp.exp(jax.random.uniform(ks[21], (DEPTH, GDN_HEADS), f32, math.log(1e-3), math.log(1e-1)))
    return {
        'x_prompt': nrm(ks[0], (BATCH, SEQ, D_MODEL)),
        'x_sample': nrm(ks[1], (DEC_BATCH, DEC_SEQ, D_MODEL)),
        'cache_cmp_k': nrm(ks[2], kv_shape),
        'cache_cmp_v': nrm(ks[3], kv_shape),
        'cache_sel_k': nrm(ks[4], kv_shape),
        'cache_sel_v': nrm(ks[5], kv_shape),
        'state_win_k': nrm(ks[6], (DEPTH, DEC_BATCH, wb, NSA_KV_HEADS, NSA_HEAD_DIM)),
        'state_win_v': nrm(ks[7], (DEPTH, DEC_BATCH, wb, NSA_KV_HEADS, NSA_HEAD_DIM)),
        'state_conv': nrm(ks[8], (DEPTH, DEC_BATCH, CONV_WIDTH - 1, 3 * GDN_WIDTH)),
        'state_ssm': nrm(ks[9], (DEPTH, DEC_BATCH, GDN_HEADS, GDN_HEAD_DIM, GDN_HEAD_DIM), 0.5),
        'page_table': jax.random.permutation(ks[10], n_pool)[:n_used].reshape(DEC_BATCH, n_pages).astype(jnp.int32),
        'norm_mix_pre': gain(ks[11], (DEPTH, D_MODEL)),
        'norm_mix_post': gain(ks[12], (DEPTH, D_MODEL)),
        'norm_ffn_pre': gain(ks[13], (DEPTH, D_MODEL)),
        'norm_ffn_post': gain(ks[14], (DEPTH, D_MODEL)),
        'w_in': nrm(ks[15], (DEPTH, D_MODEL, IN_COLS), D_MODEL ** -0.5),
        'w_cmp_k': (1.0 + 0.1 * nrm(ks[16], (DEPTH, BLOCK, NSA_HEAD_DIM))) / BLOCK,
        'w_cmp_v': (1.0 + 0.1 * nrm(ks[17], (DEPTH, BLOCK, NSA_HEAD_DIM))) / BLOCK,
        'nsa_out_norm': gain(ks[18], (DEPTH, NSA_WIDTH)),
        'conv_w': nrm(ks[19], (DEPTH, CONV_WIDTH, 3 * GDN_WIDTH), CONV_WIDTH ** -0.5),
        'a_log': jnp.log(jax.random.uniform(ks[20], (DEPTH, GDN_HEADS), f32, 1.0, 16.0)),
        'dt_bias': dt + jnp.log(-jnp.expm1(-dt)),
        'gdn_norm': gain(ks[22], (DEPTH, GDN_HEAD_DIM)),
        'w_out': nrm(ks[23], (DEPTH, MIX_WIDTH, D_MODEL), MIX_WIDTH ** -0.5),
        'w_up': nrm(ks[24], (DEPTH, D_MODEL, D_FF), D_MODEL ** -0.5),
        'w_down': nrm(ks[25], (DEPTH, D_FF, D_MODEL), D_FF ** -0.5),
    }


def reference(x_prompt, x_sample, cache_cmp_k, cache_cmp_v, cache_sel_k, cache_sel_v, state_win_k, state_win_v, state_conv, state_ssm, page_table, norm_mix_pre, norm_mix_post, norm_ffn_pre, norm_ffn_post, w_in, w_cmp_k, w_cmp_v, nsa_out_norm, conv_w, a_log, dt_bias, gdn_norm, w_out, w_up, w_down):
    pos_p = jnp.arange(x_prompt.shape[1])
    pos_s = PAST_LEN + jnp.arange(x_sample.shape[1])
    y_p, y_s = x_prompt, x_sample
    p_states, s_states = [], []
    for l in range(DEPTH):
        mix_w = (norm_mix_pre[l], norm_mix_post[l], w_in[l], nsa_out_norm[l], conv_w[l], a_log[l], dt_bias[l], gdn_norm[l], w_out[l])
        prompt_nsa = functools.partial(nsa_prompt, w_cmp_k=w_cmp_k[l], w_cmp_v=w_cmp_v[l])
        conv0 = jnp.zeros((y_p.shape[0], CONV_WIDTH - 1, 3 * GDN_WIDTH), y_p.dtype)
        ssm0 = jnp.zeros((y_p.shape[0], GDN_HEADS, GDN_HEAD_DIM, GDN_HEAD_DIM), jnp.float32)
        y_p, st = mixing_sublayer(y_p, pos_p, prompt_nsa, conv0, ssm0, *mix_w)
        p_states.append(st)
        y_p = ffn_sublayer(y_p, norm_ffn_pre[l], norm_ffn_post[l], w_up[l], w_down[l])
        sample_nsa = functools.partial(nsa_sample, pool_ck=cache_cmp_k, pool_cv=cache_cmp_v, pool_sk=cache_sel_k, pool_sv=cache_sel_v, win_k=state_win_k[l], win_v=state_win_v[l], page_table=page_table, layer=l, w_cmp_k=w_cmp_k[l], w_cmp_v=w_cmp_v[l])
        y_s, st = mixing_sublayer(y_s, pos_s, sample_nsa, state_conv[l], state_ssm[l], *mix_w)
        s_states.append(st)
        y_s = ffn_sublayer(y_s, norm_ffn_pre[l], norm_ffn_post[l], w_up[l], w_down[l])
    (p_ck, p_cv, p_sk, p_sv, p_wk, p_wv, p_conv, p_ssm) = [jnp.stack(t) for t in zip(*p_states)]
    (s_ck, s_cv, s_sk, s_sv, s_wk, s_wv, s_conv, s_ssm) = [jnp.stack(t) for t in zip(*s_states)]
    return (y_p, y_s, p_ck, p_cv, p_sk, p_sv, p_wk, p_wv, p_conv, p_ssm, s_ck, s_cv, s_sk, s_sv, s_wk, s_wv, s_conv, s_ssm)
```

```python
import functools
import math

import jax
import jax.numpy as jnp
import numpy as np
from jax import lax
from jax.experimental import pallas as pl
from jax.experimental.pallas import tpu as pltpu

D_MODEL = 1024
PAGE_SIZE = 128
NSA_HEAD_DIM = 64
NSA_HEADS = 8
NSA_KV_HEADS = 2
NSA_GROUP = NSA_HEADS // NSA_KV_HEADS
NSA_WIDTH = NSA_HEADS * NSA_HEAD_DIM
NSA_KV_WIDTH = NSA_KV_HEADS * NSA_HEAD_DIM
BLOCK = 64
TOP_K = 16
WINDOW = 512
Q_BLOCK = 128
FORCED_SCORE = NSA_GROUP + 1.0
ROPE_THETA = 500000.0
ROPE_DIM = NSA_HEAD_DIM // 4
GDN_HEAD_DIM = 128
GDN_HEADS = 4
GDN_WIDTH = GDN_HEADS * GDN_HEAD_DIM
CONV_WIDTH = 4
GDN_CHUNK = 64
D_FF = 4 * D_MODEL
EPS = 1e-6
SPLIT_SIZES = (NSA_WIDTH,) + (NSA_KV_WIDTH,) * 6 + (3 * NSA_HEADS, 3 * GDN_WIDTH, GDN_WIDTH, GDN_HEADS, GDN_HEADS)

LANES = 128
SMALL_COLS = LANES
GATE_OFF, BETA_OFF, ALPHA_OFF = 0, 3 * NSA_HEADS, 3 * NSA_HEADS + GDN_HEADS
PROJ_COLS = NSA_WIDTH + 6 * NSA_KV_WIDTH + 3 * GDN_WIDTH + GDN_WIDTH + SMALL_COLS
VMEM_LIMIT = 56 * 1024 * 1024
NEG = -1e30
F32 = jnp.float32
BF16 = jnp.bfloat16
HI = lax.Precision.HIGHEST


def _rms(x, w):
    return x * lax.rsqrt(jnp.mean(x * x, axis=-1, keepdims=True) + EPS) * w


def _sigmoid(x):
    return 1.0 / (1.0 + jnp.exp(-x))


def _silu(x):
    return x * _sigmoid(x)


def _rope_slab(x, c, sa, sb):
    half = ROPE_DIM // 2
    return x * c + pltpu.roll(x, LANES - half, axis=1) * sa + pltpu.roll(x, half, axis=1) * sb


def _in_proj_body(x_ref, nw_ref, w_ref, rc_ref, rsa_ref, rsb_ref, wk_ref, wv_ref,
                  q_ref, kc_ref, vc_ref, ks_ref, vs_ref, kw_ref, vw_ref, qkv_ref, z_ref, sm_ref, ck_ref, cv_ref):
    h = _rms(x_ref[...], nw_ref[...])
    p = jnp.dot(h.astype(BF16), w_ref[...], preferred_element_type=F32)
    c, sa, sb = rc_ref[...], rsa_ref[...], rsb_ref[...]
    rows = p.shape[0]
    off = 0
    for j in range(NSA_WIDTH // LANES):
        q_ref[:, j * LANES:(j + 1) * LANES] = _rope_slab(p[:, off:off + LANES], c, sa, sb)
        off += LANES
    kc = _rope_slab(p[:, off:off + LANES], c, sa, sb)
    kc_ref[...] = kc
    vc = p[:, off + LANES:off + 2 * LANES]
    vc_ref[...] = vc
    ks_ref[...] = _rope_slab(p[:, off + 2 * LANES:off + 3 * LANES], c, sa, sb)
    vs_ref[...] = p[:, off + 3 * LANES:off + 4 * LANES]
    kw_ref[...] = _rope_slab(p[:, off + 4 * LANES:off + 5 * LANES], c, sa, sb)
    vw_ref[...] = p[:, off + 5 * LANES:off + 6 * LANES]
    off += 6 * LANES
    qkv_ref[...] = p[:, off:off + 3 * GDN_WIDTH]
    off += 3 * GDN_WIDTH
    z_ref[...] = p[:, off:off + GDN_WIDTH]
    off += GDN_WIDTH
    sm_ref[...] = p[:, off:off + SMALL_COLS]
    ck_ref[...] = jnp.sum(kc.reshape(rows // BLOCK, BLOCK, LANES) * wk_ref[...][None], axis=1)
    cv_ref[...] = jnp.sum(vc.reshape(rows // BLOCK, BLOCK, LANES) * wv_ref[...][None], axis=1)


def _permute_w_in(w_in):
    pts = np.cumsum((0,) + SPLIT_SIZES)
    seg = [w_in[:, int(pts[i]):int(pts[i + 1])] for i in range(len(SPLIT_SIZES))]
    q, kc, vc, ks, vs, kw, vw, gates, qkv, z, beta, alpha = seg
    small = jnp.concatenate([gates, beta, alpha], axis=1)
    small = jnp.pad(small, ((0, 0), (0, SMALL_COLS - small.shape[1])))
    return jnp.concatenate([q, kc, vc, ks, vs, kw, vw, qkv, z, small], axis=1)


def _rope_tables(pos):
    half = ROPE_DIM // 2
    inv_freq = ROPE_THETA ** (-jnp.arange(half, dtype=F32) / half)
    ang = pos.astype(F32)[:, None] * inv_freq[None, :]
    cos, sin = jnp.cos(ang), jnp.sin(ang)
    n = pos.shape[0]
    one = jnp.ones((n, NSA_HEAD_DIM - ROPE_DIM), F32)
    zero = jnp.zeros((n, NSA_HEAD_DIM - ROPE_DIM), F32)
    zh = jnp.zeros((n, half), F32)
    c = jnp.concatenate([cos, cos, one], axis=1)
    sa = jnp.concatenate([-sin, zh, zero], axis=1)
    sb = jnp.concatenate([zh, sin, zero], axis=1)
    return tuple(jnp.tile(t, (1, LANES // NSA_HEAD_DIM)) for t in (c, sa, sb))


def _in_proj(x2d, pos, norm_w, w_perm, wk2, wv2):
    n = x2d.shape[0]
    tm = min(512, n)
    assert n % tm == 0 and tm % BLOCK == 0
    rc, rsa, rsb = _rope_tables(pos)
    row = lambda w: pl.BlockSpec((tm, w), lambda i: (i, 0))
    const = lambda a: pl.BlockSpec(a.shape, lambda i: (0,) * a.ndim)
    nw = norm_w.reshape(1, D_MODEL)
    widths = (NSA_WIDTH,) + (LANES,) * 6 + (3 * GDN_WIDTH, GDN_WIDTH, SMALL_COLS)
    out_shape = [jax.ShapeDtypeStruct((n, w), F32) for w in widths]
    out_shape += [jax.ShapeDtypeStruct((n // BLOCK, LANES), F32)] * 2
    out_specs = [row(w) for w in widths] + [pl.BlockSpec((tm // BLOCK, LANES), lambda i: (i, 0))] * 2
    return pl.pallas_call(
        _in_proj_body,
        grid=(n // tm,),
        in_specs=[row(D_MODEL), const(nw), const(w_perm), row(LANES), row(LANES), row(LANES), const(wk2), const(wv2)],
        out_specs=out_specs,
        out_shape=out_shape,
        compiler_params=pltpu.CompilerParams(dimension_semantics=("arbitrary",), vmem_limit_bytes=VMEM_LIMIT),
        name="in_proj",
    )(x2d, nw, w_perm, rc, rsa, rsb, wk2, wv2)


FF_CHUNK = 1024


def _post_body(on_ref, og_ref, x_ref, nn_ref, wo_ref, nmp_ref, nfp_ref, wu_ref, wd_ref, nfo_ref, y_ref):
    o_nsa = _rms(on_ref[...], nn_ref[...])
    mix = jnp.concatenate([o_nsa, og_ref[...]], axis=1).astype(BF16)
    y = jnp.dot(mix, wo_ref[...], preferred_element_type=F32)
    y1 = x_ref[...] + _rms(y, nmp_ref[...])
    h = _rms(y1, nfp_ref[...]).astype(BF16)
    acc = jnp.zeros_like(y1)
    for c in range(D_FF // FF_CHUNK):
        u = jnp.dot(h, wu_ref[:, c * FF_CHUNK:(c + 1) * FF_CHUNK], preferred_element_type=F32)
        u = jnp.square(jnp.maximum(u, 0.0))
        acc = acc + jnp.dot(u.astype(BF16), wd_ref[c * FF_CHUNK:(c + 1) * FF_CHUNK, :], preferred_element_type=F32)
    y_ref[...] = y1 + _rms(acc, nfo_ref[...])


def _post(o_nsa, o_gdn, x2d, nsa_norm, w_out, n_mix_post, n_ffn_pre, w_up, w_down, n_ffn_post):
    n = x2d.shape[0]
    tm = min(256, n)
    assert n % tm == 0
    row = lambda w: pl.BlockSpec((tm, w), lambda i: (i, 0))
    const = lambda a: pl.BlockSpec(a.shape, lambda i: (0,) * a.ndim)
    r = lambda v: v.reshape(1, -1)
    args = (o_nsa, o_gdn, x2d, r(nsa_norm), w_out, r(n_mix_post), r(n_ffn_pre), w_up, w_down, r(n_ffn_post))
    return pl.pallas_call(
        _post_body,
        grid=(n // tm,),
        in_specs=[row(NSA_WIDTH), row(GDN_WIDTH), row(D_MODEL)] + [const(a) for a in args[3:]],
        out_specs=row(D_MODEL),
        out_shape=jax.ShapeDtypeStruct((n, D_MODEL), F32),
        compiler_params=pltpu.CompilerParams(dimension_semantics=("arbitrary",), vmem_limit_bytes=VMEM_LIMIT),
        name="out_proj_ffn",
    )(*args)


SEL_CHUNK = 512


def _row_iota(shape):
    return lax.broadcasted_iota(jnp.int32, shape, 0)


def _topk_rows(score, blk, k):
    nb = score.shape[0]
    sel = jnp.zeros(score.shape, F32)
    for _ in range(k):
        m = jnp.max(score, axis=0, keepdims=True)
        idx = jnp.min(jnp.where(score == m, blk, nb), axis=0, keepdims=True)
        pick = blk == idx
        sel = jnp.where(pick, 1.0, sel)
        score = jnp.where(pick, -jnp.inf, score)
    return sel


def _softmax_rows(s, mask):
    m = jnp.max(jnp.where(mask, s, NEG), axis=0, keepdims=True)
    e = jnp.where(mask, jnp.exp(s - m), 0.0)
    return e, 1.0 / jnp.maximum(jnp.sum(e, axis=0, keepdims=True), 1e-30)


def _nsa_prompt_body(qt_ref, kp_ref, vt_ref, ck_ref, cvt_ref, g_ref, o_ref, sel_ref, *, seq, kc, wn):
    i = pl.program_id(1)
    start = i * Q_BLOCK
    nb = seq // BLOCK
    g4 = range(NSA_GROUP)
    slab = lambda a, g: a[:, g * Q_BLOCK:(g + 1) * Q_BLOCK]
    qt = qt_ref[0, 0]
    zq = jnp.zeros_like(qt)
    q_sel = jnp.concatenate([qt, zq], axis=0)
    q_win = jnp.concatenate([zq, qt], axis=0)
    pos = start + lax.broadcasted_iota(jnp.int32, (1, Q_BLOCK), 1)

    sc = jnp.dot(ck_ref[0], qt, preferred_element_type=F32)
    blk = _row_iota((nb, Q_BLOCK))
    cmask = (blk + 1) * BLOCK - 1 <= pos
    ps = []
    for g in g4:
        e, inv = _softmax_rows(slab(sc, g), cmask)
        ps.append(e * inv)
    o_c = jnp.dot(cvt_ref[0], jnp.concatenate(ps, axis=1).astype(BF16), preferred_element_type=F32)
    imp = ps[0]
    for g in range(1, NSA_GROUP):
        imp = imp + ps[g]

    cur = pos // BLOCK
    forced = (blk == 0) | (blk == cur) | (blk == cur - 1)
    score = jnp.where(forced, FORCED_SCORE, jnp.where(blk <= cur, imp, -1.0))
    sel = _topk_rows(score, blk, min(TOP_K, nb))
    sel_ref[...] = jnp.where(score >= 0, sel, 0.0)

    def sel_step(c, carry):
        m, l, acc = carry
        k0 = pl.multiple_of(c * kc, kc)
        s = jnp.dot(kp_ref[0, pl.ds(k0, kc), :], q_sel, preferred_element_type=F32)
        rows = sel_ref[pl.ds(pl.multiple_of(c * (kc // BLOCK), kc // BLOCK), kc // BLOCK), :]
        selx = jnp.concatenate([jnp.broadcast_to(rows[j:j + 1], (BLOCK, Q_BLOCK)) for j in range(kc // BLOCK)], axis=0)
        mask = (selx > 0.0) & (k0 + _row_iota((kc, Q_BLOCK)) <= pos)
        ms, ls, es = [], [], []
        for g in g4:
            sg = slab(s, g)
            mo = slab(m, g)
            mn = jnp.maximum(mo, jnp.max(jnp.where(mask, sg, NEG), axis=0, keepdims=True))
            e = jnp.where(mask, jnp.exp(sg - mn), 0.0)
            a = jnp.exp(mo - mn)
            ms.append(mn)
            ls.append(a * slab(l, g) + jnp.sum(e, axis=0, keepdims=True))
            es.append((a, e))
        alpha = jnp.concatenate([a for a, _ in es], axis=1)
        e_all = jnp.concatenate([e for _, e in es], axis=1).astype(BF16)
        pv = jnp.dot(vt_ref[0, 0:NSA_HEAD_DIM, pl.ds(k0, kc)], e_all, preferred_element_type=F32)
        return jnp.concatenate(ms, axis=1), jnp.concatenate(ls, axis=1), alpha * acc + pv

    width = NSA_GROUP * Q_BLOCK
    init = (jnp.full((1, width), NEG, F32), jnp.zeros((1, width), F32), jnp.zeros((NSA_HEAD_DIM, width), F32))
    n_chunks = (start + Q_BLOCK - 1) // kc + 1
    _, l_s, acc_s = lax.fori_loop(0, n_chunks, sel_step, init)
    o_s = acc_s * (1.0 / jnp.maximum(l_s, 1e-30))

    w0 = pl.multiple_of(jnp.maximum(start + Q_BLOCK - wn, 0), Q_BLOCK)
    sw = jnp.dot(kp_ref[0, pl.ds(w0, wn), :], q_win, preferred_element_type=F32)
    dist = pos - (w0 + _row_iota((wn, Q_BLOCK)))
    wmask = (dist >= 0) & (dist < WINDOW)
    ews, invs = [], []
    for g in g4:
        e, inv = _softmax_rows(slab(sw, g), wmask)
        ews.append(e)
        invs.append(inv)
    o_w = jnp.dot(vt_ref[0, NSA_HEAD_DIM:2 * NSA_HEAD_DIM, pl.ds(w0, wn)], jnp.concatenate(ews, axis=1).astype(BF16),
                  preferred_element_type=F32) * jnp.concatenate(invs, axis=1)

    gate = _sigmoid(g_ref[0])
    for g in g4:
        o_ref[0, g] = (gate[3 * g:3 * g + 1] * slab(o_c, g) + gate[3 * g + 1:3 * g + 2] * slab(o_s, g)
                       + gate[3 * g + 2:3 * g + 3] * slab(o_w, g))


def _split_heads_t(x, n):
    return x.reshape(n, NSA_KV_HEADS, NSA_HEAD_DIM).transpose(1, 0, 2)


def _nsa_prompt(q, ks, vs, kw, vw, ck, cv, gates):
    s = q.shape[0]
    assert s % Q_BLOCK == 0
    n_qb, nb = s // Q_BLOCK, s // BLOCK
    kc = min(SEL_CHUNK, s)
    wn = min(WINDOW + Q_BLOCK, s)
    assert s % kc == 0
    scale = NSA_HEAD_DIM ** -0.5
    qt = (q * scale).astype(BF16).reshape(n_qb, Q_BLOCK, NSA_KV_HEADS, NSA_GROUP, NSA_HEAD_DIM)
    qt = qt.transpose(2, 0, 4, 3, 1).reshape(NSA_KV_HEADS, n_qb, NSA_HEAD_DIM, NSA_GROUP * Q_BLOCK)
    kp = jnp.concatenate([_split_heads_t(ks, s), _split_heads_t(kw, s)], axis=-1).astype(BF16)
    vt = jnp.concatenate([_split_heads_t(vs, s), _split_heads_t(vw, s)], axis=-1).astype(BF16).transpose(0, 2, 1)
    ckh = _split_heads_t(ck, nb).astype(BF16)
    cvt = _split_heads_t(cv, nb).astype(BF16).transpose(0, 2, 1)
    gt = gates.reshape(s, NSA_KV_HEADS, NSA_GROUP * 3).transpose(1, 2, 0)
    gt = jnp.pad(gt, ((0, 0), (0, 16 - NSA_GROUP * 3), (0, 0)))
    body = functools.partial(_nsa_prompt_body, seq=s, kc=kc, wn=wn)
    o = pl.pallas_call(
        body,
        grid=(NSA_KV_HEADS, n_qb),
        in_specs=[
            pl.BlockSpec((1, 1, NSA_HEAD_DIM, NSA_GROUP * Q_BLOCK), lambda h, i: (h, i, 0, 0)),
            pl.BlockSpec((1, s, LANES), lambda h, i: (h, 0, 0)),
            pl.BlockSpec((1, LANES, s), lambda h, i: (h, 0, 0)),
            pl.BlockSpec((1, nb, NSA_HEAD_DIM), lambda h, i: (h, 0, 0)),
            pl.BlockSpec((1, NSA_HEAD_DIM, nb), lambda h, i: (h, 0, 0)),
            pl.BlockSpec((1, 16, Q_BLOCK), lambda h, i: (h, 0, i)),
        ],
        out_specs=pl.BlockSpec((1, NSA_GROUP, NSA_HEAD_DIM, Q_BLOCK), lambda h, i: (h, 0, 0, i)),
        out_shape=jax.ShapeDtypeStruct((NSA_KV_HEADS, NSA_GROUP, NSA_HEAD_DIM, s), F32),
        scratch_shapes=[pltpu.VMEM((nb, Q_BLOCK), F32)],
        compiler_params=pltpu.CompilerParams(dimension_semantics=("arbitrary", "arbitrary"), vmem_limit_bytes=VMEM_LIMIT),
        name="nsa_prompt",
    )(qt, kp, vt, ckh, cvt, gt)
    return o.transpose(3, 0, 1, 2).reshape(s, NSA_WIDTH)


def _dot(a, b):
    return jnp.dot(a, b, precision=HI, preferred_element_type=F32)


def _dot_nt(a, b):
    return lax.dot_general(a, b, (((1,), (1,)), ((), ())), precision=HI, preferred_element_type=F32)


def _dot_tn(a, b):
    return lax.dot_general(a, b, (((0,), (0,)), ((), ())), precision=HI, preferred_element_type=F32)


def _softplus(x):
    return jnp.maximum(x, 0.0) + jnp.log(1.0 + jnp.exp(-jnp.abs(x)))


def _unit_lower_inverse(a, ri, ci):
    c = a.shape[0]
    eye = (ri == ci).astype(F32)
    base = 8
    d = jnp.where(ri // base == ci // base, a, 0.0)
    d2 = _dot(d, d)
    p = eye - d
    x = p + _dot(p, d2)
    x = x + _dot(x, _dot(d2, d2))
    b = base
    while b < c:
        lo = jnp.where((ri // (2 * b) == ci // (2 * b)) & (ri // b != ci // b), a, 0.0)
        x = x - _dot(x, _dot(lo, x))
        b *= 2
    return x


def _conv_silu(x, prev, w):
    rowi = _row_iota(x.shape)
    conv = None
    for j in range(CONV_WIDTH - 1, -1, -1):
        if j == 0:
            xs = x
        else:
            xs = jnp.where(rowi < j, pltpu.roll(prev, j, axis=0), pltpu.roll(x, j, axis=0))
        term = xs * w[CONV_WIDTH - 1 - j:CONV_WIDTH - j]
        conv = term if conv is None else conv + term
    return _silu(conv)


def _l2n(x):
    return x * lax.rsqrt(jnp.sum(x * x, axis=-1, keepdims=True) + EPS)


def _gdn_prompt_body(qkv_ref, z_ref, sm_ref, smt_ref, prev0_ref, s0_ref, cw_ref, ar_ref, dr_ref, at_ref, dt_ref, gn_ref,
                     o_ref, sf_ref, s_ref, xp_ref):
    c = pl.program_id(0)
    cs = GDN_CHUNK

    @pl.when(c == 0)
    def _():
        s_ref[...] = s0_ref[...]
        xp_ref[...] = prev0_ref[...]

    x = qkv_ref[...]
    act = _conv_silu(x, xp_ref[...], cw_ref[...])
    xp_ref[...] = x
    sm = sm_ref[...]
    ri = _row_iota((cs, cs))
    ci = lax.broadcasted_iota(jnp.int32, (cs, cs), 1)
    tri = (ri >= ci).astype(F32)
    g_cols = -jnp.exp(ar_ref[...]) * _softplus(sm + dr_ref[...])
    gc_cols = _dot(tri, g_cols)
    g_rows = -jnp.exp(at_ref[...]) * _softplus(smt_ref[0] + dt_ref[...])
    gc_rows = _dot(g_rows, (ri <= ci).astype(F32))
    beta_cols = _sigmoid(sm)
    scale = GDN_HEAD_DIM ** -0.5
    for h in range(GDN_HEADS):
        hs = slice(h * GDN_HEAD_DIM, (h + 1) * GDN_HEAD_DIM)
        q = _l2n(act[:, hs]) * scale
        k = _l2n(act[:, GDN_WIDTH + h * GDN_HEAD_DIM:GDN_WIDTH + (h + 1) * GDN_HEAD_DIM])
        v = act[:, 2 * GDN_WIDTH + h * GDN_HEAD_DIM:2 * GDN_WIDTH + (h + 1) * GDN_HEAD_DIM]
        beta = beta_cols[:, BETA_OFF + h:BETA_OFF + h + 1]
        gc = gc_cols[:, ALPHA_OFF + h:ALPHA_OFF + h + 1]
        gr = gc_rows[GDN_HEADS + h:GDN_HEADS + h + 1, :]
        g_last = gc[cs - 1:cs, :]
        decay = jnp.where(ri >= ci, jnp.exp(gc - gr), 0.0)
        kb = k * beta
        a = jnp.where(ri > ci, _dot_nt(kb, k) * decay, 0.0)
        t = _unit_lower_inverse(a, ri, ci)
        u = _dot(t, v * beta)
        w = _dot(t, kb * jnp.exp(gc))
        intra = _dot_nt(q, k) * decay
        s = s_ref[h]
        v_new = u - _dot(w, s)
        o = _dot(q * jnp.exp(gc), s) + _dot(intra, v_new)
        s_ref[h] = s * jnp.exp(g_last) + _dot_tn(k * jnp.exp(g_last - gc), v_new)
        o_ref[:, hs] = _rms(o, gn_ref[...]) * _silu(z_ref[:, hs])

    @pl.when(c == pl.num_programs(0) - 1)
    def _():
        sf_ref[...] = s_ref[...]


def _gdn_consts(a_log, dt_bias):
    lane = jnp.zeros((1, LANES), F32)
    ar = lane.at[0, ALPHA_OFF:ALPHA_OFF + GDN_HEADS].set(a_log)
    dr = lane.at[0, ALPHA_OFF:ALPHA_OFF + GDN_HEADS].set(dt_bias)
    col = jnp.zeros((2 * GDN_HEADS, 1), F32)
    at = jnp.broadcast_to(col.at[GDN_HEADS:, 0].set(a_log), (2 * GDN_HEADS, GDN_CHUNK))
    dt = jnp.broadcast_to(col.at[GDN_HEADS:, 0].set(dt_bias), (2 * GDN_HEADS, GDN_CHUNK))
    return ar, dr, at, dt


def _gdn_prompt(qkv, z, small, conv_prev, s0, conv_w, a_log, dt_bias, gdn_norm):
    n = qkv.shape[0]
    cs = GDN_CHUNK
    assert n % cs == 0
    nc = n // cs
    smt = small[:, BETA_OFF:BETA_OFF + 2 * GDN_HEADS].reshape(nc, cs, 2 * GDN_HEADS).transpose(0, 2, 1)
    prev0 = jnp.pad(conv_prev, ((cs - (CONV_WIDTH - 1), 0), (0, 0)))
    ar, dr, at, dt = _gdn_consts(a_log, dt_bias)
    gn = gdn_norm.reshape(1, GDN_HEAD_DIM)
    row = lambda w: pl.BlockSpec((cs, w), lambda i: (i, 0))
    const = lambda a: pl.BlockSpec(a.shape, lambda i: (0,) * a.ndim)
    return pl.pallas_call(
        _gdn_prompt_body,
        grid=(nc,),
        in_specs=[row(3 * GDN_WIDTH), row(GDN_WIDTH), row(SMALL_COLS),
                  pl.BlockSpec((1, 2 * GDN_HEADS, cs), lambda i: (i, 0, 0)),
                  const(prev0), const(s0), const(conv_w), const(ar), const(dr), const(at), const(dt), const(gn)],
        out_specs=[row(GDN_WIDTH), const(s0)],
        out_shape=[jax.ShapeDtypeStruct((n, GDN_WIDTH), F32), jax.ShapeDtypeStruct(s0.shape, F32)],
        scratch_shapes=[pltpu.VMEM(s0.shape, F32), pltpu.VMEM((cs, 3 * GDN_WIDTH), F32)],
        compiler_params=pltpu.CompilerParams(dimension_semantics=("arbitrary",), vmem_limit_bytes=VMEM_LIMIT),
        name="gdn_prompt",
    )(qkv, z, small, smt, prev0, s0, conv_w, ar, dr, at, dt, gn)


def _gdn_sample_pre_body(x0_ref, x1_ref, x2_ref, x3_ref, sm_ref, cw_ref, ar_ref, dr_ref, q_ref, k_ref, v_ref, a_ref, b_ref):
    w = cw_ref[...]
    conv = x0_ref[...] * w[0:1] + x1_ref[...] * w[1:2] + x2_ref[...] * w[2:3] + x3_ref[...] * w[3:4]
    act = _silu(conv)
    sm = sm_ref[...]
    a_cols = jnp.exp(-jnp.exp(ar_ref[...]) * _softplus(sm + dr_ref[...]))
    b_cols = _sigmoid(sm)
    rows = sm.shape[0]
    for h in range(GDN_HEADS):
        hs = slice(h * GDN_HEAD_DIM, (h + 1) * GDN_HEAD_DIM)
        q_ref[:, hs] = _l2n(act[:, hs]) * GDN_HEAD_DIM ** -0.5
        k_ref[:, hs] = _l2n(act[:, GDN_WIDTH + h * GDN_HEAD_DIM:GDN_WIDTH + (h + 1) * GDN_HEAD_DIM])
        a_ref[:, hs] = jnp.broadcast_to(a_cols[:, ALPHA_OFF + h:ALPHA_OFF + h + 1], (rows, GDN_HEAD_DIM))
        b_ref[:, hs] = jnp.broadcast_to(b_cols[:, BETA_OFF + h:BETA_OFF + h + 1], (rows, GDN_HEAD_DIM))
    v_ref[...] = act[:, 2 * GDN_WIDTH:]


GDN_SAMPLE_SEQS = 8


def _gdn_sample_body(qt_ref, kt_ref, v_ref, a_ref, b_ref, z_ref, s0_ref, gn_ref, o_ref, sf_ref, *, t_len):
    gn = gn_ref[...]

    def seq(bl, carry):
        v_all, a_all, b_all, z_all = v_ref[bl], a_ref[bl], b_ref[bl], z_ref[bl]
        for h in range(GDN_HEADS):
            hs = slice(h * GDN_HEAD_DIM, (h + 1) * GDN_HEAD_DIM)
            s = s0_ref[bl, h]
            qt, kt = qt_ref[bl, h], kt_ref[bl, h]
            outs = []
            for t in range(t_len):
                k_col, q_col = kt[:, t:t + 1], qt[:, t:t + 1]
                a, b = a_all[t:t + 1, hs], b_all[t:t + 1, hs]
                sk = jnp.sum(s * k_col, axis=0, keepdims=True)
                delta = b * (v_all[t:t + 1, hs] - a * sk)
                s = a * s + k_col * delta
                outs.append(jnp.sum(s * q_col, axis=0, keepdims=True))
            sf_ref[bl, h] = s
            o = jnp.concatenate(outs, axis=0)
            o_ref[bl, :, hs] = _rms(o, gn) * _silu(z_all[:, hs])
        return carry

    lax.fori_loop(0, GDN_SAMPLE_SEQS, seq, 0)


def _gdn_sample(qkv, z, small, conv_prev, s0, conv_w, a_log, dt_bias, gdn_norm):
    b, t = qkv.shape[:2]
    n = b * t
    xp = jnp.concatenate([conv_prev, qkv], axis=1)
    xs = [xp[:, j:j + t].reshape(n, 3 * GDN_WIDTH) for j in range(CONV_WIDTH)]
    ar, dr, _, _ = _gdn_consts(a_log, dt_bias)
    tm = min(256, n)
    assert n % tm == 0
    row = lambda w: pl.BlockSpec((tm, w), lambda i: (i, 0))
    const = lambda a: pl.BlockSpec(a.shape, lambda i: (0,) * a.ndim)
    q, k, v, av, bv = pl.pallas_call(
        _gdn_sample_pre_body,
        grid=(n // tm,),
        in_specs=[row(3 * GDN_WIDTH)] * 4 + [row(SMALL_COLS), const(conv_w), const(ar), const(dr)],
        out_specs=[row(GDN_WIDTH)] * 5,
        out_shape=[jax.ShapeDtypeStruct((n, GDN_WIDTH), F32)] * 5,
        compiler_params=pltpu.CompilerParams(dimension_semantics=("arbitrary",), vmem_limit_bytes=VMEM_LIMIT),
        name="gdn_sample_pre",
    )(*xs, small.reshape(n, SMALL_COLS), conv_w, ar, dr)
    tr = lambda x: x.reshape(b, t, GDN_HEADS, GDN_HEAD_DIM).transpose(0, 2, 3, 1)
    r3 = lambda x: x.reshape(b, t, GDN_WIDTH)
    bb = GDN_SAMPLE_SEQS
    assert b % bb == 0
    gn = gdn_norm.reshape(1, GDN_HEAD_DIM)
    tspec = pl.BlockSpec((bb, GDN_HEADS, GDN_HEAD_DIM, t), lambda i: (i, 0, 0, 0))
    rspec = pl.BlockSpec((bb, t, GDN_WIDTH), lambda i: (i, 0, 0))
    sspec = pl.BlockSpec((bb, GDN_HEADS, GDN_HEAD_DIM, GDN_HEAD_DIM), lambda i: (i, 0, 0, 0))
    o, sf = pl.pallas_call(
        functools.partial(_gdn_sample_body, t_len=t),
        grid=(b // bb,),
        in_specs=[tspec, tspec, rspec, rspec, rspec, rspec, sspec, const(gn)],
        out_specs=[rspec, sspec],
        out_shape=[jax.ShapeDtypeStruct((b, t, GDN_WIDTH), F32), jax.ShapeDtypeStruct(s0.shape, F32)],
        compiler_params=pltpu.CompilerParams(dimension_semantics=("arbitrary",), vmem_limit_bytes=VMEM_LIMIT),
        name="gdn_sample",
    )(tr(q), tr(k), r3(v), r3(av), r3(bv), z, s0, gn)
    return o, sf


PAGES_PER_STEP = 32
T_PAD = 8
Q_ROWS = NSA_KV_HEADS * NSA_GROUP * T_PAD
NEW_PAD = LANES


def _nsa_sample_body(pt_ref, ckp_ref, cvp_ref, skp_ref, svp_ref, q_ref, kn_ref, vn_ref, wk_ref, wv_ref, kwn_ref, vwn_ref,
                     wck_ref, wcv_ref, g_ref, ex_ref, o_ref,
                     buf_ref, sem_ref, ck_ref, cv_ref, selm_ref, oc_ref, m_ref, l_ref, acc_ref,
                     *, n_groups, t_len, win_len, page_base):
    b, ph, g = pl.program_id(0), pl.program_id(1), pl.program_id(2)
    nb_, np_, ng_ = pl.num_programs(0), pl.num_programs(1), pl.num_programs(2)
    step = (b * np_ + ph) * ng_ + g
    pps = PAGES_PER_STEP
    rows_per_step = pps * PAGE_SIZE
    blocks_per_step = rows_per_step // BLOCK

    def copies(pool_k, pool_v, bb, gg, slot):
        out = []
        for p in range(pps):
            page = page_base + pt_ref[bb, gg * pps + p]
            for kv, pool in enumerate((pool_k, pool_v)):
                out.append(pltpu.make_async_copy(pool.at[page], buf_ref.at[slot, kv, pl.ds(p * PAGE_SIZE, PAGE_SIZE), :],
                                                 sem_ref.at[slot, kv]))
        return out

    def start(bb, pp, gg, slot):
        @pl.when(pp == 0)
        def _():
            for cp in copies(ckp_ref, cvp_ref, bb, gg, slot):
                cp.start()

        @pl.when(pp == 1)
        def _():
            for cp in copies(skp_ref, svp_ref, bb, gg, slot):
                cp.start()

    slot = step % 2

    @pl.when(step == 0)
    def _():
        start(b, ph, g, slot)

    nxt = step + 1

    @pl.when(nxt < nb_ * np_ * ng_)
    def _():
        start(nxt // (np_ * ng_), (nxt // ng_) % np_, nxt % ng_, 1 - slot)

    for cp in copies(ckp_ref, cvp_ref, b, g, slot):
        cp.wait()

    q = q_ref[0]
    rowt = _row_iota((Q_ROWS, LANES)) % T_PAD
    lane = lax.broadcasted_iota(jnp.int32, (Q_ROWS, LANES), 1)
    new_mask = (lane <= rowt) & (lane < t_len)

    @pl.when(ph == 0)
    def _():
        kb = buf_ref[slot, 0].reshape(blocks_per_step, BLOCK, LANES)
        vb = buf_ref[slot, 1].reshape(blocks_per_step, BLOCK, LANES)
        r0 = pl.multiple_of(g * blocks_per_step, blocks_per_step)
        ck_ref[pl.ds(r0, blocks_per_step), :] = jnp.sum(kb * wck_ref[...][None], axis=1)
        cv_ref[pl.ds(r0, blocks_per_step), :] = jnp.sum(vb * wcv_ref[...][None], axis=1)

        @pl.when(g == n_groups - 1)
        def _():
            nb = ck_ref.shape[0]
            s = lax.dot_general(q, ck_ref[...].astype(BF16), (((1,), (1,)), ((), ())), preferred_element_type=F32)
            m = jnp.max(s, axis=1, keepdims=True)
            e = jnp.exp(s - m)
            p = e / jnp.maximum(jnp.sum(e, axis=1, keepdims=True), 1e-30)
            oc_ref[...] = jnp.dot(p.astype(BF16), cv_ref[...].astype(BF16), preferred_element_type=F32)
            bl = lax.broadcasted_iota(jnp.int32, (T_PAD, nb), 1)
            for h in range(NSA_KV_HEADS):
                r = h * NSA_GROUP * T_PAD
                imp = p[r:r + T_PAD]
                for gg in range(1, NSA_GROUP):
                    imp = imp + p[r + gg * T_PAD:r + (gg + 1) * T_PAD]
                score = jnp.where((bl == 0) | (bl == nb - 1), FORCED_SCORE, imp)
                sel = jnp.zeros((T_PAD, nb), F32)
                for _ in range(min(TOP_K - 1, nb)):
                    mx = jnp.max(score, axis=1, keepdims=True)
                    idx = jnp.min(jnp.where(score == mx, bl, nb), axis=1, keepdims=True)
                    pick = bl == idx
                    sel = jnp.where(pick, 1.0, sel)
                    score = jnp.where(pick, -jnp.inf, score)
                for gg in range(NSA_GROUP):
                    selm_ref[r + gg * T_PAD:r + (gg + 1) * T_PAD, :] = sel
            m_ref[...] = jnp.full(m_ref.shape, NEG, F32)
            l_ref[...] = jnp.zeros(l_ref.shape, F32)
            acc_ref[...] = jnp.zeros(acc_ref.shape, F32)

    def flash(s, mask, vals):
        m_old = m_ref[...]
        m_new = jnp.maximum(m_old, jnp.max(jnp.where(mask, s, NEG), axis=1, keepdims=True))
        e = jnp.where(mask, jnp.exp(s - m_new), 0.0)
        a = jnp.exp(m_old - m_new)
        m_ref[...] = m_new
        l_ref[...] = a * l_ref[...] + jnp.sum(e, axis=1, keepdims=True)
        acc_ref[...] = a * acc_ref[...] + jnp.dot(e.astype(BF16), vals, preferred_element_type=F32)

    nt = (((1,), (1,)), ((), ()))

    @pl.when(ph == 1)
    def _():
        kb = buf_ref[slot, 0].astype(BF16)
        vb = buf_ref[slot, 1].astype(BF16)
        s = lax.dot_general(q, kb, nt, preferred_element_type=F32)
        mask = jnp.dot(selm_ref[...].astype(BF16), ex_ref[g], preferred_element_type=F32) > 0.5
        flash(s, mask, vb)

        @pl.when(g == n_groups - 1)
        def _():
            flash(lax.dot_general(q, kn_ref[0], nt, preferred_element_type=F32), new_mask, vn_ref[0])
            o_s = acc_ref[...] / jnp.maximum(l_ref[...], 1e-30)
            wk = wk_ref[0].astype(BF16)
            s_old = lax.dot_general(q, wk, nt, preferred_element_type=F32)
            s_new = lax.dot_general(q, kwn_ref[0], nt, preferred_element_type=F32)
            j = lax.broadcasted_iota(jnp.int32, (Q_ROWS, win_len), 1)
            old_mask = j > win_len - WINDOW + _row_iota((Q_ROWS, win_len)) % T_PAD
            mw = jnp.maximum(jnp.max(jnp.where(old_mask, s_old, NEG), axis=1, keepdims=True),
                             jnp.max(jnp.where(new_mask, s_new, NEG), axis=1, keepdims=True))
            e_old = jnp.where(old_mask, jnp.exp(s_old - mw), 0.0)
            e_new = jnp.where(new_mask, jnp.exp(s_new - mw), 0.0)
            lw = jnp.sum(e_old, axis=1, keepdims=True) + jnp.sum(e_new, axis=1, keepdims=True)
            o_w = (jnp.dot(e_old.astype(BF16), wv_ref[0].astype(BF16), preferred_element_type=F32)
                   + jnp.dot(e_new.astype(BF16), vwn_ref[0], preferred_element_type=F32)) / jnp.maximum(lw, 1e-30)
            gate = _sigmoid(g_ref[0])
            o_ref[0] = gate[:, 0:1] * oc_ref[...] + gate[:, 1:2] * o_s + gate[:, 2:3] * o_w


def _nsa_sample(q, ks, vs, kw, vw, gates, pools, layer, win_k, win_v, page_table, wck2, wcv2):
    b, t = q.shape[:2]
    n_pages = page_table.shape[1]
    assert t <= T_PAD and t <= BLOCK and PAGE_SIZE % BLOCK == 0
    pps = PAGES_PER_STEP
    assert n_pages % pps == 0
    n_groups = n_pages // pps
    nb = n_pages * PAGE_SIZE // BLOCK
    wb = win_k.shape[1]
    n_pool = pools[0].shape[1]
    flat = [p.reshape(p.shape[0] * n_pool, PAGE_SIZE, NSA_KV_WIDTH) for p in pools]
    scale = NSA_HEAD_DIM ** -0.5
    q5 = (q * scale).reshape(b, t, NSA_KV_HEADS, NSA_GROUP, NSA_HEAD_DIM).transpose(0, 2, 3, 1, 4)
    q5 = jnp.pad(q5, ((0, 0), (0, 0), (0, 0), (0, T_PAD - t), (0, 0)))
    own = jnp.eye(NSA_KV_HEADS, dtype=F32)[None, :, None, None, :, None]
    qbd = (q5[:, :, :, :, None, :] * own).reshape(b, Q_ROWS, NSA_KV_WIDTH).astype(BF16)
    padn = lambda x: jnp.pad(x, ((0, 0), (0, NEW_PAD - t), (0, 0))).astype(BF16)
    g5 = gates.reshape(b, t, NSA_KV_HEADS, NSA_GROUP, 3).transpose(0, 2, 3, 1, 4)
    g5 = jnp.pad(g5, ((0, 0), (0, 0), (0, 0), (0, T_PAD - t), (0, 0))).reshape(b, Q_ROWS, 3)
    blocks_per_step = pps * PAGE_SIZE // BLOCK
    key_blk = jnp.arange(pps * PAGE_SIZE) // BLOCK
    expand = (jnp.arange(nb)[None, :, None] == (jnp.arange(n_groups)[:, None, None] * blocks_per_step + key_blk[None, None, :]))
    expand = expand.astype(BF16)
    per_b = lambda shape: pl.BlockSpec((1,) + shape, lambda i, p, g, pt: (i, 0, 0))
    const = lambda a: pl.BlockSpec(a.shape, lambda i, p, g, pt: (0,) * a.ndim)
    hbm = pl.BlockSpec(memory_space=pl.ANY)
    body = functools.partial(_nsa_sample_body, n_groups=n_groups, t_len=t, win_len=wb, page_base=layer * n_pool)
    o = pl.pallas_call(
        body,
        grid_spec=pltpu.PrefetchScalarGridSpec(
            num_scalar_prefetch=1,
            grid=(b, 2, n_groups),
            in_specs=[hbm, hbm, hbm, hbm, per_b((Q_ROWS, NSA_KV_WIDTH)), per_b((NEW_PAD, NSA_KV_WIDTH)), per_b((NEW_PAD, NSA_KV_WIDTH)),
                      per_b((wb, NSA_KV_WIDTH)), per_b((wb, NSA_KV_WIDTH)), per_b((NEW_PAD, NSA_KV_WIDTH)), per_b((NEW_PAD, NSA_KV_WIDTH)),
                      const(wck2), const(wcv2), per_b((Q_ROWS, 3)), const(expand)],
            out_specs=per_b((Q_ROWS, NSA_KV_WIDTH)),
            scratch_shapes=[
                pltpu.VMEM((2, 2, pps * PAGE_SIZE, NSA_KV_WIDTH), F32),
                pltpu.SemaphoreType.DMA((2, 2)),
                pltpu.VMEM((nb, NSA_KV_WIDTH), F32), pltpu.VMEM((nb, NSA_KV_WIDTH), F32),
                pltpu.VMEM((Q_ROWS, nb), F32), pltpu.VMEM((Q_ROWS, NSA_KV_WIDTH), F32),
                pltpu.VMEM((Q_ROWS, 1), F32), pltpu.VMEM((Q_ROWS, 1), F32), pltpu.VMEM((Q_ROWS, NSA_KV_WIDTH), F32),
            ],
        ),
        out_shape=jax.ShapeDtypeStruct((b, Q_ROWS, NSA_KV_WIDTH), F32),
        compiler_params=pltpu.CompilerParams(dimension_semantics=("arbitrary",) * 3, vmem_limit_bytes=VMEM_LIMIT),
        name="nsa_sample",
    )(page_table, *flat, qbd, padn(ks), padn(vs), win_k, win_v, padn(kw), padn(vw), wck2, wcv2, g5, expand)
    o6 = o.reshape(b, NSA_KV_HEADS, NSA_GROUP, T_PAD, NSA_KV_HEADS, NSA_HEAD_DIM)[:, :, :, :t]
    o5 = jnp.stack([o6[:, h, :, :, h, :] for h in range(NSA_KV_HEADS)], axis=1)
    return o5.transpose(0, 3, 1, 2, 4).reshape(b, t, NSA_WIDTH)


def kernel(x_prompt, x_sample, cache_cmp_k, cache_cmp_v, cache_sel_k, cache_sel_v, state_win_k, state_win_v, state_conv,
           state_ssm, page_table, norm_mix_pre, norm_mix_post, norm_ffn_pre, norm_ffn_post, w_in, w_cmp_k, w_cmp_v,
           nsa_out_norm, conv_w, a_log, dt_bias, gdn_norm, w_out, w_up, w_down):
    depth = w_in.shape[0]
    bp, sp, _ = x_prompt.shape
    bs, ts, _ = x_sample.shape
    past = page_table.shape[1] * PAGE_SIZE
    pos_p = jnp.arange(sp)
    pos_s = past + jnp.arange(ts)
    pools = (cache_cmp_k, cache_cmp_v, cache_sel_k, cache_sel_v)
    kv5 = lambda x, b, n: x.reshape(b, n, NSA_KV_HEADS, NSA_HEAD_DIM)
    y_p, y_s = x_prompt, x_sample
    p_states, s_states = [], []
    for l in range(depth):
        w_perm = _permute_w_in(w_in[l]).astype(BF16)
        wck2 = jnp.tile(w_cmp_k[l], (1, LANES // NSA_HEAD_DIM))
        wcv2 = jnp.tile(w_cmp_v[l], (1, LANES // NSA_HEAD_DIM))
        post_w = (nsa_out_norm[l], w_out[l].astype(BF16), norm_mix_post[l], norm_ffn_pre[l], w_up[l].astype(BF16),
                  w_down[l].astype(BF16), norm_ffn_post[l])
        gdn_w = (conv_w[l], a_log[l], dt_bias[l], gdn_norm[l])

        wb = min(WINDOW, sp)
        conv0 = jnp.zeros((CONV_WIDTH - 1, 3 * GDN_WIDTH), F32)
        ssm0 = jnp.zeros((GDN_HEADS, GDN_HEAD_DIM, GDN_HEAD_DIM), F32)
        ys, sts = [], []
        for b in range(bp):
            x2 = y_p[b]
            q, kc, vc, ks, vs, kw, vw, qkv, z, small, ck, cv = _in_proj(x2, pos_p, norm_mix_pre[l], w_perm, wck2, wcv2)
            o_nsa = _nsa_prompt(q, ks, vs, kw, vw, ck, cv, small[:, GATE_OFF:GATE_OFF + 3 * NSA_HEADS])
            o_gdn, ssm = _gdn_prompt(qkv, z, small, conv0, ssm0, *gdn_w)
            ys.append(_post(o_nsa, o_gdn, x2, *post_w))
            conv_state = jnp.concatenate([conv0, qkv], axis=0)[-(CONV_WIDTH - 1):]
            sts.append((kv5(kc, 1, sp)[0], kv5(vc, 1, sp)[0], kv5(ks, 1, sp)[0], kv5(vs, 1, sp)[0],
                        kv5(kw, 1, sp)[0, sp - wb:], kv5(vw, 1, sp)[0, sp - wb:], conv_state, ssm))
        y_p = jnp.stack(ys)
        p_states.append(tuple(jnp.stack(t) for t in zip(*sts)))

        n = bs * ts
        q, kc, vc, ks, vs, kw, vw, qkv, z, small, _, _ = _in_proj(
            jnp.pad(y_s.reshape(n, D_MODEL), ((0, (-n) % BLOCK), (0, 0))), jnp.pad(jnp.tile(pos_s, bs), (0, (-n) % BLOCK)),
            norm_mix_pre[l], w_perm, wck2, wcv2)
        r3 = lambda x: x[:n].reshape(bs, ts, -1)
        q, kc, vc, ks, vs, kw, vw, qkv, z, small = map(r3, (q, kc, vc, ks, vs, kw, vw, qkv, z, small))
        wbs = state_win_k.shape[2]
        win_k = state_win_k[l].reshape(bs, wbs, NSA_KV_WIDTH)
        win_v = state_win_v[l].reshape(bs, wbs, NSA_KV_WIDTH)
        o_nsa = _nsa_sample(q, ks, vs, kw, vw, small[..., GATE_OFF:GATE_OFF + 3 * NSA_HEADS], pools, l, win_k, win_v,
                            page_table, wck2, wcv2)
        o_gdn, ssm = _gdn_sample(qkv, z, small, state_conv[l], state_ssm[l], *gdn_w)
        y_s = _post(o_nsa.reshape(n, NSA_WIDTH), o_gdn.reshape(n, GDN_WIDTH), y_s.reshape(n, D_MODEL), *post_w).reshape(bs, ts, D_MODEL)
        conv_state = jnp.concatenate([state_conv[l], qkv], axis=1)[:, -(CONV_WIDTH - 1):]
        new_wk = jnp.concatenate([win_k, kw], axis=1)[:, ts:]
        new_wv = jnp.concatenate([win_v, vw], axis=1)[:, ts:]
        s_states.append((kv5(kc, bs, ts), kv5(vc, bs, ts), kv5(ks, bs, ts), kv5(vs, bs, ts),
                         kv5(new_wk, bs, wbs), kv5(new_wv, bs, wbs), conv_state, ssm))
    p_out = [jnp.stack(t) for t in zip(*p_states)]
    s_out = [jnp.stack(t) for t in zip(*s_states)]
    return (y_p, y_s, *p_out, *s_out)
```

```python
import functools
import math

import jax
import jax.numpy as jnp
import numpy as np
from jax import lax
from jax.experimental import pallas as pl
from jax.experimental.pallas import tpu as pltpu

D_MODEL = 1024
PAGE_SIZE = 128
NSA_HEAD_DIM = 64
NSA_HEADS = 8
NSA_KV_HEADS = 2
NSA_GROUP = NSA_HEADS // NSA_KV_HEADS
NSA_WIDTH = NSA_HEADS * NSA_HEAD_DIM
NSA_KV_WIDTH = NSA_KV_HEADS * NSA_HEAD_DIM
BLOCK = 64
TOP_K = 16
WINDOW = 512
Q_BLOCK = 128
FORCED_SCORE = NSA_GROUP + 1.0
ROPE_THETA = 500000.0
ROPE_DIM = NSA_HEAD_DIM // 4
GDN_HEAD_DIM = 128
GDN_HEADS = 4
GDN_WIDTH = GDN_HEADS * GDN_HEAD_DIM
CONV_WIDTH = 4
GDN_CHUNK = 64
D_FF = 4 * D_MODEL
EPS = 1e-6
SPLIT_SIZES = (NSA_WIDTH,) + (NSA_KV_WIDTH,) * 6 + (3 * NSA_HEADS, 3 * GDN_WIDTH, GDN_WIDTH, GDN_HEADS, GDN_HEADS)

LANES = 128
SMALL_COLS = LANES
GATE_OFF, BETA_OFF, ALPHA_OFF = 0, 3 * NSA_HEADS, 3 * NSA_HEADS + GDN_HEADS
PROJ_COLS = NSA_WIDTH + 6 * NSA_KV_WIDTH + 3 * GDN_WIDTH + GDN_WIDTH + SMALL_COLS
VMEM_LIMIT = 56 * 1024 * 1024
NEG = -1e30
F32 = jnp.float32
BF16 = jnp.bfloat16
HI = lax.Precision.HIGHEST


def _rms(x, w):
    return x * lax.rsqrt(jnp.mean(x * x, axis=-1, keepdims=True) + EPS) * w


def _sigmoid(x):
    return 1.0 / (1.0 + jnp.exp(-x))


def _silu(x):
    return x * _sigmoid(x)


def _rope_slab(x, c, sa, sb):
    half = ROPE_DIM // 2
    return x * c + pltpu.roll(x, LANES - half, axis=1) * sa + pltpu.roll(x, half, axis=1) * sb


def _in_proj_body(x_ref, nw_ref, w_ref, rc_ref, rsa_ref, rsb_ref, wk_ref, wv_ref,
                  q_ref, kc_ref, vc_ref, ks_ref, vs_ref, kw_ref, vw_ref, qkv_ref, z_ref, sm_ref, ck_ref, cv_ref):
    h = _rms(x_ref[...], nw_ref[...])
    p = jnp.dot(h.astype(BF16), w_ref[...], preferred_element_type=F32)
    c, sa, sb = rc_ref[...], rsa_ref[...], rsb_ref[...]
    rows = p.shape[0]
    off = 0
    for j in range(NSA_WIDTH // LANES):
        q_ref[:, j * LANES:(j + 1) * LANES] = _rope_slab(p[:, off:off + LANES], c, sa, sb)
        off += LANES
    kc = _rope_slab(p[:, off:off + LANES], c, sa, sb)
    kc_ref[...] = kc
    vc = p[:, off + LANES:off + 2 * LANES]
    vc_ref[...] = vc
    ks_ref[...] = _rope_slab(p[:, off + 2 * LANES:off + 3 * LANES], c, sa, sb)
    vs_ref[...] = p[:, off + 3 * LANES:off + 4 * LANES]
    kw_ref[...] = _rope_slab(p[:, off + 4 * LANES:off + 5 * LANES], c, sa, sb)
    vw_ref[...] = p[:, off + 5 * LANES:off + 6 * LANES]
    off += 6 * LANES
    qkv_ref[...] = p[:, off:off + 3 * GDN_WIDTH]
    off += 3 * GDN_WIDTH
    z_ref[...] = p[:, off:off + GDN_WIDTH]
    off += GDN_WIDTH
    sm_ref[...] = p[:, off:off + SMALL_COLS]
    ck_ref[...] = jnp.sum(kc.reshape(rows // BLOCK, BLOCK, LANES) * wk_ref[...][None], axis=1)
    cv_ref[...] = jnp.sum(vc.reshape(rows // BLOCK, BLOCK, LANES) * wv_ref[...][None], axis=1)


def _permute_w_in(w_in):
    pts = np.cumsum((0,) + SPLIT_SIZES)
    seg = [w_in[:, int(pts[i]):int(pts[i + 1])] for i in range(len(SPLIT_SIZES))]
    q, kc, vc, ks, vs, kw, vw, gates, qkv, z, beta, alpha = seg
    small = jnp.concatenate([gates, beta, alpha], axis=1)
    small = jnp.pad(small, ((0, 0), (0, SMALL_COLS - small.shape[1])))
    return jnp.concatenate([q, kc, vc, ks, vs, kw, vw, qkv, z, small], axis=1)


def _rope_tables(pos):
    half = ROPE_DIM // 2
    inv_freq = ROPE_THETA ** (-jnp.arange(half, dtype=F32) / half)
    ang = pos.astype(F32)[:, None] * inv_freq[None, :]
    cos, sin = jnp.cos(ang), jnp.sin(ang)
    n = pos.shape[0]
    one = jnp.ones((n, NSA_HEAD_DIM - ROPE_DIM), F32)
    zero = jnp.zeros((n, NSA_HEAD_DIM - ROPE_DIM), F32)
    zh = jnp.zeros((n, half), F32)
    c = jnp.concatenate([cos, cos, one], axis=1)
    sa = jnp.concatenate([-sin, zh, zero], axis=1)
    sb = jnp.concatenate([zh, sin, zero], axis=1)
    return tuple(jnp.tile(t, (1, LANES // NSA_HEAD_DIM)) for t in (c, sa, sb))


def _in_proj(x2d, pos, norm_w, w_perm, wk2, wv2):
    n = x2d.shape[0]
    tm = min(512, n)
    assert n % tm == 0 and tm % BLOCK == 0
    rc, rsa, rsb = _rope_tables(pos)
    row = lambda w: pl.BlockSpec((tm, w), lambda i: (i, 0))
    const = lambda a: pl.BlockSpec(a.shape, lambda i: (0,) * a.ndim)
    nw = norm_w.reshape(1, D_MODEL)
    widths = (NSA_WIDTH,) + (LANES,) * 6 + (3 * GDN_WIDTH, GDN_WIDTH, SMALL_COLS)
    out_shape = [jax.ShapeDtypeStruct((n, w), F32) for w in widths]
    out_shape += [jax.ShapeDtypeStruct((n // BLOCK, LANES), F32)] * 2
    out_specs = [row(w) for w in widths] + [pl.BlockSpec((tm // BLOCK, LANES), lambda i: (i, 0))] * 2
    return pl.pallas_call(
        _in_proj_body,
        grid=(n // tm,),
        in_specs=[row(D_MODEL), const(nw), const(w_perm), row(LANES), row(LANES), row(LANES), const(wk2), const(wv2)],
        out_specs=out_specs,
        out_shape=out_shape,
        compiler_params=pltpu.CompilerParams(dimension_semantics=("arbitrary",), vmem_limit_bytes=VMEM_LIMIT),
        name="in_proj",
    )(x2d, nw, w_perm, rc, rsa, rsb, wk2, wv2)


FF_CHUNK = 1024


def _post_body(on_ref, og_ref, x_ref, nn_ref, wo_ref, nmp_ref, nfp_ref, wu_ref, wd_ref, nfo_ref, y_ref):
    o_nsa = _rms(on_ref[...], nn_ref[...])
    mix = jnp.concatenate([o_nsa, og_ref[...]], axis=1).astype(BF16)
    y = jnp.dot(mix, wo_ref[...], preferred_element_type=F32)
    y1 = x_ref[...] + _rms(y, nmp_ref[...])
    h = _rms(y1, nfp_ref[...]).astype(BF16)
    acc = jnp.zeros_like(y1)
    for c in range(D_FF // FF_CHUNK):
        u = jnp.dot(h, wu_ref[:, c * FF_CHUNK:(c + 1) * FF_CHUNK], preferred_element_type=F32)
        u = jnp.square(jnp.maximum(u, 0.0))
        acc = acc + jnp.dot(u.astype(BF16), wd_ref[c * FF_CHUNK:(c + 1) * FF_CHUNK, :], preferred_element_type=F32)
    y_ref[...] = y1 + _rms(acc, nfo_ref[...])


def _post(o_nsa, o_gdn, x2d, nsa_norm, w_out, n_mix_post, n_ffn_pre, w_up, w_down, n_ffn_post):
    n = x2d.shape[0]
    tm = min(256, n)
    assert n % tm == 0
    row = lambda w: pl.BlockSpec((tm, w), lambda i: (i, 0))
    const = lambda a: pl.BlockSpec(a.shape, lambda i: (0,) * a.ndim)
    r = lambda v: v.reshape(1, -1)
    args = (o_nsa, o_gdn, x2d, r(nsa_norm), w_out, r(n_mix_post), r(n_ffn_pre), w_up, w_down, r(n_ffn_post))
    return pl.pallas_call(
        _post_body,
        grid=(n // tm,),
        in_specs=[row(NSA_WIDTH), row(GDN_WIDTH), row(D_MODEL)] + [const(a) for a in args[3:]],
        out_specs=row(D_MODEL),
        out_shape=jax.ShapeDtypeStruct((n, D_MODEL), F32),
        compiler_params=pltpu.CompilerParams(dimension_semantics=("arbitrary",), vmem_limit_bytes=VMEM_LIMIT),
        name="out_proj_ffn",
    )(*args)


SEL_CHUNK = 512


def _row_iota(shape):
    return lax.broadcasted_iota(jnp.int32, shape, 0)


def _topk_rows(score, blk, k):
    nb = score.shape[0]
    sel = jnp.zeros(score.shape, F32)
    for _ in range(k):
        m = jnp.max(score, axis=0, keepdims=True)
        idx = jnp.min(jnp.where(score == m, blk, nb), axis=0, keepdims=True)
        pick = blk == idx
        sel = jnp.where(pick, 1.0, sel)
        score = jnp.where(pick, -jnp.inf, score)
    return sel


def _softmax_rows(s, mask):
    m = jnp.max(jnp.where(mask, s, NEG), axis=0, keepdims=True)
    e = jnp.where(mask, jnp.exp2(s - m), 0.0)
    return e, 1.0 / jnp.maximum(jnp.sum(e, axis=0, keepdims=True), 1e-30)


def _nsa_prompt_body(qt_ref, kp_ref, vt_ref, ck_ref, cvt_ref, g_ref, o_ref, sel_ref, *, seq, kc, wn):
    i = pl.program_id(1)
    start = i * Q_BLOCK
    nb = seq // BLOCK
    g4 = range(NSA_GROUP)
    slab = lambda a, g: a[:, g * Q_BLOCK:(g + 1) * Q_BLOCK]
    qt = qt_ref[0, 0]
    zq = jnp.zeros_like(qt)
    q_sel = jnp.concatenate([qt, zq], axis=0)
    q_win = jnp.concatenate([zq, qt], axis=0)
    pos = start + lax.broadcasted_iota(jnp.int32, (1, Q_BLOCK), 1)

    sc = jnp.dot(ck_ref[0], qt, preferred_element_type=F32)
    blk = _row_iota((nb, Q_BLOCK))
    cmask = (blk + 1) * BLOCK - 1 <= pos
    ps = []
    for g in g4:
        e, inv = _softmax_rows(slab(sc, g), cmask)
        ps.append(e * inv)
    o_c = jnp.dot(cvt_ref[0], jnp.concatenate(ps, axis=1).astype(BF16), preferred_element_type=F32)
    imp = ps[0]
    for g in range(1, NSA_GROUP):
        imp = imp + ps[g]

    cur = pos // BLOCK
    forced = (blk == 0) | (blk == cur) | (blk == cur - 1)
    score = jnp.where(forced, FORCED_SCORE, jnp.where(blk <= cur, imp, -1.0))
    sel = _topk_rows(score, blk, min(TOP_K, nb))
    sel_ref[...] = jnp.where(score >= 0, sel, 0.0)

    def sel_step(c, carry):
        m, l, acc = carry
        k0 = pl.multiple_of(c * kc, kc)
        s = jnp.dot(kp_ref[0, pl.ds(k0, kc), :], q_sel, preferred_element_type=F32)
        rows = sel_ref[pl.ds(pl.multiple_of(c * (kc // BLOCK), kc // BLOCK), kc // BLOCK), :]
        selx = jnp.concatenate([jnp.broadcast_to(rows[j:j + 1], (BLOCK, Q_BLOCK)) for j in range(kc // BLOCK)], axis=0)
        bias = jnp.where((selx > 0.0) & (k0 + _row_iota((kc, Q_BLOCK)) <= pos), 0.0, NEG)
        ms, ls, es = [], [], []
        for g in g4:
            sg = slab(s, g) + bias
            mo = slab(m, g)
            mn = jnp.maximum(mo, jnp.max(sg, axis=0, keepdims=True))
            e = jnp.exp2(sg - mn)
            a = jnp.exp2(mo - mn)
            ms.append(mn)
            ls.append(a * slab(l, g) + jnp.sum(e, axis=0, keepdims=True))
            es.append((a, e))
        alpha = jnp.concatenate([a for a, _ in es], axis=1)
        e_all = jnp.concatenate([e for _, e in es], axis=1).astype(BF16)
        pv = jnp.dot(vt_ref[0, 0:NSA_HEAD_DIM, pl.ds(k0, kc)], e_all, preferred_element_type=F32)
        return jnp.concatenate(ms, axis=1), jnp.concatenate(ls, axis=1), alpha * acc + pv

    width = NSA_GROUP * Q_BLOCK
    init = (jnp.full((1, width), NEG, F32), jnp.zeros((1, width), F32), jnp.zeros((NSA_HEAD_DIM, width), F32))
    n_chunks = (start + Q_BLOCK - 1) // kc + 1
    _, l_s, acc_s = lax.fori_loop(0, n_chunks, sel_step, init)
    o_s = acc_s * (1.0 / jnp.maximum(l_s, 1e-30))

    w0 = pl.multiple_of(jnp.maximum(start + Q_BLOCK - wn, 0), Q_BLOCK)
    sw = jnp.dot(kp_ref[0, pl.ds(w0, wn), :], q_win, preferred_element_type=F32)
    dist = pos - (w0 + _row_iota((wn, Q_BLOCK)))
    wmask = (dist >= 0) & (dist < WINDOW)
    ews, invs = [], []
    for g in g4:
        e, inv = _softmax_rows(slab(sw, g), wmask)
        ews.append(e)
        invs.append(inv)
    o_w = jnp.dot(vt_ref[0, NSA_HEAD_DIM:2 * NSA_HEAD_DIM, pl.ds(w0, wn)], jnp.concatenate(ews, axis=1).astype(BF16),
                  preferred_element_type=F32) * jnp.concatenate(invs, axis=1)

    gate = _sigmoid(g_ref[0])
    for g in g4:
        o_ref[0, g] = (gate[3 * g:3 * g + 1] * slab(o_c, g) + gate[3 * g + 1:3 * g + 2] * slab(o_s, g)
                       + gate[3 * g + 2:3 * g + 3] * slab(o_w, g))


def _split_heads_t(x, n):
    return x.reshape(n, NSA_KV_HEADS, NSA_HEAD_DIM).transpose(1, 0, 2)


def _nsa_prompt(q, ks, vs, kw, vw, ck, cv, gates):
    s = q.shape[0]
    assert s % Q_BLOCK == 0
    n_qb, nb = s // Q_BLOCK, s // BLOCK
    kc = min(SEL_CHUNK, s)
    wn = min(WINDOW + Q_BLOCK, s)
    assert s % kc == 0
    scale = NSA_HEAD_DIM ** -0.5 * math.log2(math.e)
    qt = (q * scale).astype(BF16).reshape(n_qb, Q_BLOCK, NSA_KV_HEADS, NSA_GROUP, NSA_HEAD_DIM)
    qt = qt.transpose(2, 0, 4, 3, 1).reshape(NSA_KV_HEADS, n_qb, NSA_HEAD_DIM, NSA_GROUP * Q_BLOCK)
    kp = jnp.concatenate([_split_heads_t(ks, s), _split_heads_t(kw, s)], axis=-1).astype(BF16)
    vt = jnp.concatenate([_split_heads_t(vs, s), _split_heads_t(vw, s)], axis=-1).astype(BF16).transpose(0, 2, 1)
    ckh = _split_heads_t(ck, nb).astype(BF16)
    cvt = _split_heads_t(cv, nb).astype(BF16).transpose(0, 2, 1)
    gt = gates.reshape(s, NSA_KV_HEADS, NSA_GROUP * 3).transpose(1, 2, 0)
    gt = jnp.pad(gt, ((0, 0), (0, 16 - NSA_GROUP * 3), (0, 0)))
    body = functools.partial(_nsa_prompt_body, seq=s, kc=kc, wn=wn)
    o = pl.pallas_call(
        body,
        grid=(NSA_KV_HEADS, n_qb),
        in_specs=[
            pl.BlockSpec((1, 1, NSA_HEAD_DIM, NSA_GROUP * Q_BLOCK), lambda h, i: (h, i, 0, 0)),
            pl.BlockSpec((1, s, LANES), lambda h, i: (h, 0, 0)),
            pl.BlockSpec((1, LANES, s), lambda h, i: (h, 0, 0)),
            pl.BlockSpec((1, nb, NSA_HEAD_DIM), lambda h, i: (h, 0, 0)),
            pl.BlockSpec((1, NSA_HEAD_DIM, nb), lambda h, i: (h, 0, 0)),
            pl.BlockSpec((1, 16, Q_BLOCK), lambda h, i: (h, 0, i)),
        ],
        out_specs=pl.BlockSpec((1, NSA_GROUP, NSA_HEAD_DIM, Q_BLOCK), lambda h, i: (h, 0, 0, i)),
        out_shape=jax.ShapeDtypeStruct((NSA_KV_HEADS, NSA_GROUP, NSA_HEAD_DIM, s), F32),
        scratch_shapes=[pltpu.VMEM((nb, Q_BLOCK), F32)],
        compiler_params=pltpu.CompilerParams(dimension_semantics=("arbitrary", "arbitrary"), vmem_limit_bytes=VMEM_LIMIT),
        name="nsa_prompt",
    )(qt, kp, vt, ckh, cvt, gt)
    return o.transpose(3, 0, 1, 2).reshape(s, NSA_WIDTH)


def _dot(a, b):
    return jnp.dot(a, b, precision=HI, preferred_element_type=F32)


def _dot_nt(a, b):
    return lax.dot_general(a, b, (((1,), (1,)), ((), ())), precision=HI, preferred_element_type=F32)


def _dot_tn(a, b):
    return lax.dot_general(a, b, (((0,), (0,)), ((), ())), precision=HI, preferred_element_type=F32)


def _dot1_nt(a, b):
    return lax.dot_general(a, b, (((1,), (1,)), ((), ())), preferred_element_type=F32)


def _split16(a):
    hi = a.astype(BF16)
    return hi, (a - hi.astype(F32)).astype(BF16)


def _dot3(a, b):
    ah, al = _split16(a)
    bh, bl = _split16(b)
    d = lambda x, y: jnp.dot(x, y, preferred_element_type=F32)
    return d(ah, bh) + (d(ah, bl) + d(al, bh))


def _softplus(x):
    return jnp.maximum(x, 0.0) + jnp.log(1.0 + jnp.exp(-jnp.abs(x)))


def _unit_lower_inverses(mats, ri, ci):
    c = mats[0].shape[0]
    eye = (ri == ci).astype(F32)
    base = 8
    same = ri // base == ci // base
    ds = [jnp.where(same, a, 0.0) for a in mats]
    d2s = [_dot3(d, d) for d in ds]
    xs = [(eye - d) + _dot3(eye - d, d2) for d, d2 in zip(ds, d2s)]
    d4s = [_dot3(d2, d2) for d2 in d2s]
    xs = [x + _dot3(x, d4) for x, d4 in zip(xs, d4s)]
    b = base
    while b < c:
        band = (ri // (2 * b) == ci // (2 * b)) & (ri // b != ci // b)
        lxs = [_dot3(jnp.where(band, a, 0.0), x) for a, x in zip(mats, xs)]
        xs = [x - _dot3(x, lx) for x, lx in zip(xs, lxs)]
        b *= 2
    return xs


def _conv_silu(x, prev, w):
    rowi = _row_iota(x.shape)
    conv = None
    for j in range(CONV_WIDTH - 1, -1, -1):
        if j == 0:
            xs = x
        else:
            xs = jnp.where(rowi < j, pltpu.roll(prev, j, axis=0), pltpu.roll(x, j, axis=0))
        term = xs * w[CONV_WIDTH - 1 - j:CONV_WIDTH - j]
        conv = term if conv is None else conv + term
    return _silu(conv)


def _l2n(x):
    return x * lax.rsqrt(jnp.sum(x * x, axis=-1, keepdims=True) + EPS)


GDN_STEP_CHUNKS = 4


def _gdn_chunk_body(qkv_ref, sm_ref, smt_ref, prev0_ref, cw_ref, ar_ref, dr_ref, at_ref, dt_ref,
                    u_ref, w_ref, qe_ref, kd_ref, in_ref, dl_ref, xp_ref):
    cs = GDN_CHUNK

    @pl.when(pl.program_id(0) == 0)
    def _():
        xp_ref[...] = prev0_ref[...]

    x = qkv_ref[...]
    act = _conv_silu(x, xp_ref[...], cw_ref[...])
    xp_ref[...] = x
    sm = sm_ref[...]
    ri = _row_iota((cs, cs))
    ci = lax.broadcasted_iota(jnp.int32, (cs, cs), 1)
    tri = (ri >= ci).astype(F32)
    tri_t = (ri <= ci).astype(F32)
    g_all = -jnp.exp(ar_ref[...]) * _softplus(sm + dr_ref[...])
    beta_all = _sigmoid(sm)
    scale = GDN_HEAD_DIM ** -0.5
    a_mats, rhs, where = [], [], []
    for c in range(GDN_STEP_CHUNKS):
        rows = slice(c * cs, (c + 1) * cs)
        gc_cols = _dot(tri, g_all[rows])
        g_rows = -jnp.exp(at_ref[...]) * _softplus(smt_ref[c] + dt_ref[...])
        gc_rows = _dot(g_rows, tri_t)
        intras, dls = [], []
        for h in range(GDN_HEADS):
            hs = slice(h * GDN_HEAD_DIM, (h + 1) * GDN_HEAD_DIM)
            q = _l2n(act[rows, hs]) * scale
            k = _l2n(act[rows, GDN_WIDTH + h * GDN_HEAD_DIM:GDN_WIDTH + (h + 1) * GDN_HEAD_DIM])
            v = act[rows, 2 * GDN_WIDTH + h * GDN_HEAD_DIM:2 * GDN_WIDTH + (h + 1) * GDN_HEAD_DIM]
            beta = beta_all[rows, BETA_OFF + h:BETA_OFF + h + 1]
            gc = gc_cols[:, ALPHA_OFF + h:ALPHA_OFF + h + 1]
            gr = gc_rows[GDN_HEADS + h:GDN_HEADS + h + 1, :]
            g_last = gc[cs - 1:cs, :]
            decay = jnp.where(ri >= ci, jnp.exp(gc - gr), 0.0)
            kb = k * beta
            k16 = k.astype(BF16)
            a_mats.append(jnp.where(ri > ci, _dot1_nt(kb.astype(BF16), k16) * decay, 0.0))
            rhs.append(jnp.concatenate([v * beta, kb * jnp.exp(gc)], axis=1))
            where.append((rows, hs))
            qe_ref[rows, hs] = (q * jnp.exp(gc)).astype(BF16)
            kd_ref[rows, hs] = (k * jnp.exp(g_last - gc)).astype(BF16)
            intras.append((_dot1_nt(q.astype(BF16), k16) * decay).astype(BF16))
            dls.append(jnp.broadcast_to(jnp.exp(g_last), (1, LANES)))
        in_ref[rows, :] = jnp.concatenate(intras, axis=1)
        dl_ref[c] = jnp.concatenate(dls + [jnp.zeros((2 * GDN_HEADS - GDN_HEADS, LANES), F32)], axis=0)
    for t, r, (rows, hs) in zip(_unit_lower_inverses(a_mats, ri, ci), rhs, where):
        sol = _dot3(t, r)
        u_ref[rows, hs] = sol[:, :GDN_HEAD_DIM]
        w_ref[rows, hs] = sol[:, GDN_HEAD_DIM:].astype(BF16)


def _gdn_scan_body(u_ref, w_ref, qe_ref, kd_ref, in_ref, dl_ref, z_ref, s0_ref, gn_ref, o_ref, sf_ref, s_ref):
    cs = GDN_CHUNK
    i = pl.program_id(0)

    @pl.when(i == 0)
    def _():
        s_ref[...] = s0_ref[...]

    heads = range(GDN_HEADS)
    hsl = [slice(h * GDN_HEAD_DIM, (h + 1) * GDN_HEAD_DIM) for h in heads]
    tn = (((0,), (0,)), ((), ()))
    ss = [s_ref[h] for h in heads]
    for c in range(GDN_STEP_CHUNKS):
        rows = slice(c * cs, (c + 1) * cs)
        s16 = [s.astype(BF16) for s in ss]
        v16 = [(u_ref[rows, hsl[h]] - jnp.dot(w_ref[rows, hsl[h]], s16[h], preferred_element_type=F32)).astype(BF16)
               for h in heads]
        ss = [ss[h] * dl_ref[c, h:h + 1, :] + lax.dot_general(kd_ref[rows, hsl[h]], v16[h], tn, preferred_element_type=F32)
              for h in heads]
        for h in heads:
            o = (jnp.dot(qe_ref[rows, hsl[h]], s16[h], preferred_element_type=F32)
                 + jnp.dot(in_ref[rows, h * cs:(h + 1) * cs], v16[h], preferred_element_type=F32))
            o_ref[rows, hsl[h]] = _rms(o, gn_ref[...]) * _silu(z_ref[rows, hsl[h]])
    for h in heads:
        s_ref[h] = ss[h]

    @pl.when(i == pl.num_programs(0) - 1)
    def _():
        sf_ref[...] = s_ref[...]


def _gdn_consts(a_log, dt_bias):
    lane = jnp.zeros((1, LANES), F32)
    ar = lane.at[0, ALPHA_OFF:ALPHA_OFF + GDN_HEADS].set(a_log)
    dr = lane.at[0, ALPHA_OFF:ALPHA_OFF + GDN_HEADS].set(dt_bias)
    col = jnp.zeros((2 * GDN_HEADS, 1), F32)
    at = jnp.broadcast_to(col.at[GDN_HEADS:, 0].set(a_log), (2 * GDN_HEADS, GDN_CHUNK))
    dt = jnp.broadcast_to(col.at[GDN_HEADS:, 0].set(dt_bias), (2 * GDN_HEADS, GDN_CHUNK))
    return ar, dr, at, dt


def _gdn_prompt(qkv, z, small, conv_prev, s0, conv_w, a_log, dt_bias, gdn_norm):
    n = qkv.shape[0]
    cs = GDN_CHUNK
    rows = GDN_STEP_CHUNKS * cs
    assert n % rows == 0
    nc = n // cs
    smt = small[:, BETA_OFF:BETA_OFF + 2 * GDN_HEADS].reshape(nc, cs, 2 * GDN_HEADS).transpose(0, 2, 1)
    prev0 = jnp.pad(conv_prev, ((rows - (CONV_WIDTH - 1), 0), (0, 0)))
    ar, dr, at, dt = _gdn_consts(a_log, dt_bias)
    gn = gdn_norm.reshape(1, GDN_HEAD_DIM)
    row = lambda w: pl.BlockSpec((rows, w), lambda i: (i, 0))
    per_chunk = pl.BlockSpec((GDN_STEP_CHUNKS, 2 * GDN_HEADS, LANES), lambda i: (i, 0, 0))
    const = lambda a: pl.BlockSpec(a.shape, lambda i: (0,) * a.ndim)
    params = pltpu.CompilerParams(dimension_semantics=("arbitrary",), vmem_limit_bytes=VMEM_LIMIT)
    wide = lambda dt_: jax.ShapeDtypeStruct((n, GDN_WIDTH), dt_)
    u, w, qe, kd, intra, dl = pl.pallas_call(
        _gdn_chunk_body,
        grid=(n // rows,),
        in_specs=[row(3 * GDN_WIDTH), row(SMALL_COLS),
                  pl.BlockSpec((GDN_STEP_CHUNKS, 2 * GDN_HEADS, cs), lambda i: (i, 0, 0)),
                  const(prev0), const(conv_w), const(ar), const(dr), const(at), const(dt)],
        out_specs=[row(GDN_WIDTH)] * 4 + [row(GDN_HEADS * cs), per_chunk],
        out_shape=[wide(F32), wide(BF16), wide(BF16), wide(BF16), jax.ShapeDtypeStruct((n, GDN_HEADS * cs), BF16),
                   jax.ShapeDtypeStruct((nc, 2 * GDN_HEADS, LANES), F32)],
        scratch_shapes=[pltpu.VMEM((rows, 3 * GDN_WIDTH), F32)],
        compiler_params=params,
        name="gdn_chunks",
    )(qkv, small, smt, prev0, conv_w, ar, dr, at, dt)
    return pl.pallas_call(
        _gdn_scan_body,
        grid=(n // rows,),
        in_specs=[row(GDN_WIDTH)] * 4 + [row(GDN_HEADS * cs), per_chunk, row(GDN_WIDTH), const(s0), const(gn)],
        out_specs=[row(GDN_WIDTH), const(s0)],
        out_shape=[wide(F32), jax.ShapeDtypeStruct(s0.shape, F32)],
        scratch_shapes=[pltpu.VMEM(s0.shape, F32)],
        compiler_params=params,
        name="gdn_scan",
    )(u, w, qe, kd, intra, dl, z, s0, gn)


def _gdn_sample_pre_body(x0_ref, x1_ref, x2_ref, x3_ref, sm_ref, cw_ref, ar_ref, dr_ref, q_ref, k_ref, v_ref, a_ref, b_ref):
    w = cw_ref[...]
    conv = x0_ref[...] * w[0:1] + x1_ref[...] * w[1:2] + x2_ref[...] * w[2:3] + x3_ref[...] * w[3:4]
    act = _silu(conv)
    sm = sm_ref[...]
    a_cols = jnp.exp(-jnp.exp(ar_ref[...]) * _softplus(sm + dr_ref[...]))
    b_cols = _sigmoid(sm)
    rows = sm.shape[0]
    for h in range(GDN_HEADS):
        hs = slice(h * GDN_HEAD_DIM, (h + 1) * GDN_HEAD_DIM)
        q_ref[:, hs] = _l2n(act[:, hs]) * GDN_HEAD_DIM ** -0.5
        k_ref[:, hs] = _l2n(act[:, GDN_WIDTH + h * GDN_HEAD_DIM:GDN_WIDTH + (h + 1) * GDN_HEAD_DIM])
        a_ref[:, hs] = jnp.broadcast_to(a_cols[:, ALPHA_OFF + h:ALPHA_OFF + h + 1], (rows, GDN_HEAD_DIM))
        b_ref[:, hs] = jnp.broadcast_to(b_cols[:, BETA_OFF + h:BETA_OFF + h + 1], (rows, GDN_HEAD_DIM))
    v_ref[...] = act[:, 2 * GDN_WIDTH:]


GDN_SAMPLE_SEQS = 8


def _gdn_sample_body(qt_ref, kt_ref, v_ref, a_ref, b_ref, z_ref, s0_ref, gn_ref, o_ref, sf_ref, *, t_len):
    gn = gn_ref[...]

    def seq(bl, carry):
        v_all, a_all, b_all, z_all = v_ref[bl], a_ref[bl], b_ref[bl], z_ref[bl]
        for h in range(GDN_HEADS):
            hs = slice(h * GDN_HEAD_DIM, (h + 1) * GDN_HEAD_DIM)
            s = s0_ref[bl, h]
            qt, kt = qt_ref[bl, h], kt_ref[bl, h]
            outs = []
            for t in range(t_len):
                k_col, q_col = kt[:, t:t + 1], qt[:, t:t + 1]
                a, b = a_all[t:t + 1, hs], b_all[t:t + 1, hs]
                sk = jnp.sum(s * k_col, axis=0, keepdims=True)
                delta = b * (v_all[t:t + 1, hs] - a * sk)
                s = a * s + k_col * delta
                outs.append(jnp.sum(s * q_col, axis=0, keepdims=True))
            sf_ref[bl, h] = s
            o = jnp.concatenate(outs, axis=0)
            o_ref[bl, :, hs] = _rms(o, gn) * _silu(z_all[:, hs])
        return carry

    lax.fori_loop(0, GDN_SAMPLE_SEQS, seq, 0)


def _gdn_sample(qkv, z, small, conv_prev, s0, conv_w, a_log, dt_bias, gdn_norm):
    b, t = qkv.shape[:2]
    n = b * t
    xp = jnp.concatenate([conv_prev, qkv], axis=1)
    xs = [xp[:, j:j + t].reshape(n, 3 * GDN_WIDTH) for j in range(CONV_WIDTH)]
    ar, dr, _, _ = _gdn_consts(a_log, dt_bias)
    tm = min(256, n)
    assert n % tm == 0
    row = lambda w: pl.BlockSpec((tm, w), lambda i: (i, 0))
    const = lambda a: pl.BlockSpec(a.shape, lambda i: (0,) * a.ndim)
    q, k, v, av, bv = pl.pallas_call(
        _gdn_sample_pre_body,
        grid=(n // tm,),
        in_specs=[row(3 * GDN_WIDTH)] * 4 + [row(SMALL_COLS), const(conv_w), const(ar), const(dr)],
        out_specs=[row(GDN_WIDTH)] * 5,
        out_shape=[jax.ShapeDtypeStruct((n, GDN_WIDTH), F32)] * 5,
        compiler_params=pltpu.CompilerParams(dimension_semantics=("arbitrary",), vmem_limit_bytes=VMEM_LIMIT),
        name="gdn_sample_pre",
    )(*xs, small.reshape(n, SMALL_COLS), conv_w, ar, dr)
    tr = lambda x: x.reshape(b, t, GDN_HEADS, GDN_HEAD_DIM).transpose(0, 2, 3, 1)
    r3 = lambda x: x.reshape(b, t, GDN_WIDTH)
    bb = GDN_SAMPLE_SEQS
    assert b % bb == 0
    gn = gdn_norm.reshape(1, GDN_HEAD_DIM)
    tspec = pl.BlockSpec((bb, GDN_HEADS, GDN_HEAD_DIM, t), lambda i: (i, 0, 0, 0))
    rspec = pl.BlockSpec((bb, t, GDN_WIDTH), lambda i: (i, 0, 0))
    sspec = pl.BlockSpec((bb, GDN_HEADS, GDN_HEAD_DIM, GDN_HEAD_DIM), lambda i: (i, 0, 0, 0))
    o, sf = pl.pallas_call(
        functools.partial(_gdn_sample_body, t_len=t),
        grid=(b // bb,),
        in_specs=[tspec, tspec, rspec, rspec, rspec, rspec, sspec, const(gn)],
        out_specs=[rspec, sspec],
        out_shape=[jax.ShapeDtypeStruct((b, t, GDN_WIDTH), F32), jax.ShapeDtypeStruct(s0.shape, F32)],
        compiler_params=pltpu.CompilerParams(dimension_semantics=("arbitrary",), vmem_limit_bytes=VMEM_LIMIT),
        name="gdn_sample",
    )(tr(q), tr(k), r3(v), r3(av), r3(bv), z, s0, gn)
    return o, sf


PAGES_PER_STEP = 32
T_PAD = 8
Q_ROWS = NSA_KV_HEADS * NSA_GROUP * T_PAD
NEW_PAD = LANES


def _nsa_sample_body(pt_ref, ckp_ref, cvp_ref, skp_ref, svp_ref, q_ref, kn_ref, vn_ref, wk_ref, wv_ref, kwn_ref, vwn_ref,
                     wck_ref, wcv_ref, g_ref, ex_ref, ext_ref, o_ref,
                     buf_ref, sem_ref, ck_ref, cv_ref, selm_ref, oc_ref, m_ref, l_ref, acc_ref,
                     *, n_groups, t_len, win_len, page_base):
    b, ph, g = pl.program_id(0), pl.program_id(1), pl.program_id(2)
    nb_, np_, ng_ = pl.num_programs(0), pl.num_programs(1), pl.num_programs(2)
    step = (b * np_ + ph) * ng_ + g
    pps = PAGES_PER_STEP
    rows_per_step = pps * PAGE_SIZE
    blocks_per_step = rows_per_step // BLOCK

    def copies(pool_k, pool_v, bb, gg, slot):
        out = []
        for p in range(pps):
            page = page_base + pt_ref[bb, gg * pps + p]
            for kv, pool in enumerate((pool_k, pool_v)):
                out.append(pltpu.make_async_copy(pool.at[page], buf_ref.at[slot, kv, :, pl.ds(p * PAGE_SIZE, PAGE_SIZE)],
                                                 sem_ref.at[slot, kv]))
        return out

    def start(bb, pp, gg, slot):
        @pl.when(pp == 0)
        def _():
            for cp in copies(ckp_ref, cvp_ref, bb, gg, slot):
                cp.start()

        @pl.when(pp == 1)
        def _():
            for cp in copies(skp_ref, svp_ref, bb, gg, slot):
                cp.start()

    slot = step % 2

    @pl.when(step == 0)
    def _():
        start(b, ph, g, slot)

    nxt = step + 1

    @pl.when(nxt < nb_ * np_ * ng_)
    def _():
        start(nxt // (np_ * ng_), (nxt // ng_) % np_, nxt % ng_, 1 - slot)

    for cp in copies(ckp_ref, cvp_ref, b, g, slot):
        cp.wait()

    q = q_ref[0]
    rowt = _row_iota((Q_ROWS, LANES)) % T_PAD
    lane = lax.broadcasted_iota(jnp.int32, (Q_ROWS, LANES), 1)
    new_mask = (lane <= rowt) & (lane < t_len)

    nt = (((1,), (1,)), ((), ()))

    def weighted(kv, w_ref):
        w = w_ref[...]
        return jnp.concatenate([buf_ref[slot, kv, :, p * PAGE_SIZE:(p + 1) * PAGE_SIZE] * w for p in range(pps)],
                               axis=1).astype(BF16)

    @pl.when(ph == 0)
    def _():
        @pl.when(g == 0)
        def _():
            ck_ref[...] = jnp.zeros(ck_ref.shape, F32)
            cv_ref[...] = jnp.zeros(cv_ref.shape, F32)

        ck_ref[...] += jnp.dot(weighted(0, wck_ref), ext_ref[g], preferred_element_type=F32)
        cv_ref[...] += jnp.dot(weighted(1, wcv_ref), ext_ref[g], preferred_element_type=F32)

        @pl.when(g == n_groups - 1)
        def _():
            nb = ck_ref.shape[1]
            s = jnp.dot(q, ck_ref[...].astype(BF16), preferred_element_type=F32)
            m = jnp.max(s, axis=1, keepdims=True)
            e = jnp.exp(s - m)
            p = e / jnp.maximum(jnp.sum(e, axis=1, keepdims=True), 1e-30)
            oc_ref[...] = lax.dot_general(p.astype(BF16), cv_ref[...].astype(BF16), nt, preferred_element_type=F32)
            bl = lax.broadcasted_iota(jnp.int32, (T_PAD, nb), 1)
            for h in range(NSA_KV_HEADS):
                r = h * NSA_GROUP * T_PAD
                imp = p[r:r + T_PAD]
                for gg in range(1, NSA_GROUP):
                    imp = imp + p[r + gg * T_PAD:r + (gg + 1) * T_PAD]
                score = jnp.where((bl == 0) | (bl == nb - 1), FORCED_SCORE, imp)
                sel = jnp.zeros((T_PAD, nb), F32)
                for _ in range(min(TOP_K - 1, nb)):
                    mx = jnp.max(score, axis=1, keepdims=True)
                    idx = jnp.min(jnp.where(score == mx, bl, nb), axis=1, keepdims=True)
                    pick = bl == idx
                    sel = jnp.where(pick, 1.0, sel)
                    score = jnp.where(pick, -jnp.inf, score)
                for gg in range(NSA_GROUP):
                    selm_ref[r + gg * T_PAD:r + (gg + 1) * T_PAD, :] = sel
            m_ref[...] = jnp.full(m_ref.shape, NEG, F32)
            l_ref[...] = jnp.zeros(l_ref.shape, F32)
            acc_ref[...] = jnp.zeros(acc_ref.shape, F32)

    def flash(s, mask, vals):
        m_old = m_ref[...]
        m_new = jnp.maximum(m_old, jnp.max(jnp.where(mask, s, NEG), axis=1, keepdims=True))
        e = jnp.where(mask, jnp.exp(s - m_new), 0.0)
        a = jnp.exp(m_old - m_new)
        m_ref[...] = m_new
        l_ref[...] = a * l_ref[...] + jnp.sum(e, axis=1, keepdims=True)
        acc_ref[...] = a * acc_ref[...] + lax.dot_general(e.astype(BF16), vals, nt, preferred_element_type=F32)

    @pl.when(ph == 1)
    def _():
        kb = buf_ref[slot, 0].astype(BF16)
        vb = buf_ref[slot, 1].astype(BF16)
        s = jnp.dot(q, kb, preferred_element_type=F32)
        mask = jnp.dot(selm_ref[...].astype(BF16), ex_ref[g], preferred_element_type=F32) > 0.5
        flash(s, mask, vb)

        @pl.when(g == n_groups - 1)
        def _():
            flash(jnp.dot(q, kn_ref[0], preferred_element_type=F32), new_mask, vn_ref[0])
            o_s = acc_ref[...] / jnp.maximum(l_ref[...], 1e-30)
            wk = wk_ref[0].astype(BF16)
            s_old = jnp.dot(q, wk, preferred_element_type=F32)
            s_new = jnp.dot(q, kwn_ref[0], preferred_element_type=F32)
            j = lax.broadcasted_iota(jnp.int32, (Q_ROWS, win_len), 1)
            old_mask = j > win_len - WINDOW + _row_iota((Q_ROWS, win_len)) % T_PAD
            mw = jnp.maximum(jnp.max(jnp.where(old_mask, s_old, NEG), axis=1, keepdims=True),
                             jnp.max(jnp.where(new_mask, s_new, NEG), axis=1, keepdims=True))
            e_old = jnp.where(old_mask, jnp.exp(s_old - mw), 0.0)
            e_new = jnp.where(new_mask, jnp.exp(s_new - mw), 0.0)
            lw = jnp.sum(e_old, axis=1, keepdims=True) + jnp.sum(e_new, axis=1, keepdims=True)
            o_w = (lax.dot_general(e_old.astype(BF16), wv_ref[0].astype(BF16), nt, preferred_element_type=F32)
                   + lax.dot_general(e_new.astype(BF16), vwn_ref[0], nt, preferred_element_type=F32)) / jnp.maximum(lw, 1e-30)
            gate = _sigmoid(g_ref[0])
            o_ref[0] = gate[:, 0:1] * oc_ref[...] + gate[:, 1:2] * o_s + gate[:, 2:3] * o_w


def _feature_major(x):
    lead = x.shape[:-3]
    n = len(lead)
    return x.transpose(*range(n), n + 1, n + 2, n).reshape(*lead, NSA_KV_WIDTH, x.shape[-3])


def _nsa_sample(q, ks, vs, kw, vw, gates, pools, layer, win_k, win_v, page_table, w_cmp_k, w_cmp_v):
    b, t = q.shape[:2]
    n_pages = page_table.shape[1]
    assert t <= T_PAD and t <= BLOCK and PAGE_SIZE % BLOCK == 0
    pps = PAGES_PER_STEP
    assert n_pages % pps == 0
    n_groups = n_pages // pps
    nb = n_pages * PAGE_SIZE // BLOCK
    wb = win_k.shape[1]
    n_pool = pools[0].shape[1]
    flat = [_feature_major(p).reshape(p.shape[0] * n_pool, NSA_KV_WIDTH, PAGE_SIZE) for p in pools]
    wt = lambda w: jnp.tile(w.T, (NSA_KV_HEADS, PAGE_SIZE // BLOCK))
    wck2, wcv2 = wt(w_cmp_k), wt(w_cmp_v)
    scale = NSA_HEAD_DIM ** -0.5
    q5 = (q * scale).reshape(b, t, NSA_KV_HEADS, NSA_GROUP, NSA_HEAD_DIM).transpose(0, 2, 3, 1, 4)
    q5 = jnp.pad(q5, ((0, 0), (0, 0), (0, 0), (0, T_PAD - t), (0, 0)))
    own = jnp.eye(NSA_KV_HEADS, dtype=F32)[None, :, None, None, :, None]
    qbd = (q5[:, :, :, :, None, :] * own).reshape(b, Q_ROWS, NSA_KV_WIDTH).astype(BF16)
    padn = lambda x: jnp.pad(x, ((0, 0), (0, NEW_PAD - t), (0, 0))).astype(BF16).transpose(0, 2, 1)
    g5 = gates.reshape(b, t, NSA_KV_HEADS, NSA_GROUP, 3).transpose(0, 2, 3, 1, 4)
    g5 = jnp.pad(g5, ((0, 0), (0, 0), (0, 0), (0, T_PAD - t), (0, 0))).reshape(b, Q_ROWS, 3)
    blocks_per_step = pps * PAGE_SIZE // BLOCK
    key_blk = jnp.arange(pps * PAGE_SIZE) // BLOCK
    expand = (jnp.arange(nb)[None, :, None] == (jnp.arange(n_groups)[:, None, None] * blocks_per_step + key_blk[None, None, :]))
    expand = expand.astype(BF16)
    expand_t = expand.transpose(0, 2, 1)
    per_b = lambda shape: pl.BlockSpec((1,) + shape, lambda i, p, g, pt: (i, 0, 0))
    const = lambda a: pl.BlockSpec(a.shape, lambda i, p, g, pt: (0,) * a.ndim)
    hbm = pl.BlockSpec(memory_space=pl.ANY)
    body = functools.partial(_nsa_sample_body, n_groups=n_groups, t_len=t, win_len=wb, page_base=layer * n_pool)
    o = pl.pallas_call(
        body,
        grid_spec=pltpu.PrefetchScalarGridSpec(
            num_scalar_prefetch=1,
            grid=(b, 2, n_groups),
            in_specs=[hbm, hbm, hbm, hbm, per_b((Q_ROWS, NSA_KV_WIDTH)), per_b((NSA_KV_WIDTH, NEW_PAD)), per_b((NSA_KV_WIDTH, NEW_PAD)),
                      per_b((NSA_KV_WIDTH, wb)), per_b((NSA_KV_WIDTH, wb)), per_b((NSA_KV_WIDTH, NEW_PAD)), per_b((NSA_KV_WIDTH, NEW_PAD)),
                      const(wck2), const(wcv2), per_b((Q_ROWS, 3)), const(expand), const(expand_t)],
            out_specs=per_b((Q_ROWS, NSA_KV_WIDTH)),
            scratch_shapes=[
                pltpu.VMEM((2, 2, NSA_KV_WIDTH, pps * PAGE_SIZE), F32),
                pltpu.SemaphoreType.DMA((2, 2)),
                pltpu.VMEM((NSA_KV_WIDTH, nb), F32), pltpu.VMEM((NSA_KV_WIDTH, nb), F32),
                pltpu.VMEM((Q_ROWS, nb), F32), pltpu.VMEM((Q_ROWS, NSA_KV_WIDTH), F32),
                pltpu.VMEM((Q_ROWS, 1), F32), pltpu.VMEM((Q_ROWS, 1), F32), pltpu.VMEM((Q_ROWS, NSA_KV_WIDTH), F32),
            ],
        ),
        out_shape=jax.ShapeDtypeStruct((b, Q_ROWS, NSA_KV_WIDTH), F32),
        compiler_params=pltpu.CompilerParams(dimension_semantics=("arbitrary",) * 3, vmem_limit_bytes=VMEM_LIMIT),
        name="nsa_sample",
    )(page_table, *flat, qbd, padn(ks), padn(vs), _feature_major(win_k), _feature_major(win_v), padn(kw), padn(vw),
      wck2, wcv2, g5, expand, expand_t)
    o6 = o.reshape(b, NSA_KV_HEADS, NSA_GROUP, T_PAD, NSA_KV_HEADS, NSA_HEAD_DIM)[:, :, :, :t]
    o5 = jnp.stack([o6[:, h, :, :, h, :] for h in range(NSA_KV_HEADS)], axis=1)
    return o5.transpose(0, 3, 1, 2, 4).reshape(b, t, NSA_WIDTH)


def kernel(x_prompt, x_sample, cache_cmp_k, cache_cmp_v, cache_sel_k, cache_sel_v, state_win_k, state_win_v, state_conv,
           state_ssm, page_table, norm_mix_pre, norm_mix_post, norm_ffn_pre, norm_ffn_post, w_in, w_cmp_k, w_cmp_v,
           nsa_out_norm, conv_w, a_log, dt_bias, gdn_norm, w_out, w_up, w_down):
    depth = w_in.shape[0]
    bp, sp, _ = x_prompt.shape
    bs, ts, _ = x_sample.shape
    past = page_table.shape[1] * PAGE_SIZE
    pos_p = jnp.arange(sp)
    pos_s = past + jnp.arange(ts)
    pools = (cache_cmp_k, cache_cmp_v, cache_sel_k, cache_sel_v)
    kv5 = lambda x, b, n: x.reshape(b, n, NSA_KV_HEADS, NSA_HEAD_DIM)
    y_p, y_s = x_prompt, x_sample
    p_states, s_states = [], []
    for l in range(depth):
        w_perm = _permute_w_in(w_in[l]).astype(BF16)
        wck2 = jnp.tile(w_cmp_k[l], (1, LANES // NSA_HEAD_DIM))
        wcv2 = jnp.tile(w_cmp_v[l], (1, LANES // NSA_HEAD_DIM))
        post_w = (nsa_out_norm[l], w_out[l].astype(BF16), norm_mix_post[l], norm_ffn_pre[l], w_up[l].astype(BF16),
                  w_down[l].astype(BF16), norm_ffn_post[l])
        gdn_w = (conv_w[l], a_log[l], dt_bias[l], gdn_norm[l])

        wb = min(WINDOW, sp)
        conv0 = jnp.zeros((CONV_WIDTH - 1, 3 * GDN_WIDTH), F32)
        ssm0 = jnp.zeros((GDN_HEADS, GDN_HEAD_DIM, GDN_HEAD_DIM), F32)
        ys, sts = [], []
        for b in range(bp):
            x2 = y_p[b]
            q, kc, vc, ks, vs, kw, vw, qkv, z, small, ck, cv = _in_proj(x2, pos_p, norm_mix_pre[l], w_perm, wck2, wcv2)
            o_nsa = _nsa_prompt(q, ks, vs, kw, vw, ck, cv, small[:, GATE_OFF:GATE_OFF + 3 * NSA_HEADS])
            o_gdn, ssm = _gdn_prompt(qkv, z, small, conv0, ssm0, *gdn_w)
            ys.append(_post(o_nsa, o_gdn, x2, *post_w))
            conv_state = jnp.concatenate([conv0, qkv], axis=0)[-(CONV_WIDTH - 1):]
            sts.append((kv5(kc, 1, sp)[0], kv5(vc, 1, sp)[0], kv5(ks, 1, sp)[0], kv5(vs, 1, sp)[0],
                        kv5(kw, 1, sp)[0, sp - wb:], kv5(vw, 1, sp)[0, sp - wb:], conv_state, ssm))
        y_p = jnp.stack(ys)
        p_states.append(tuple(jnp.stack(t) for t in zip(*sts)))

        n = bs * ts
        q, kc, vc, ks, vs, kw, vw, qkv, z, small, _, _ = _in_proj(
            jnp.pad(y_s.reshape(n, D_MODEL), ((0, (-n) % BLOCK), (0, 0))), jnp.pad(jnp.tile(pos_s, bs), (0, (-n) % BLOCK)),
            norm_mix_pre[l], w_perm, wck2, wcv2)
        r3 = lambda x: x[:n].reshape(bs, ts, -1)
        q, kc, vc, ks, vs, kw, vw, qkv, z, small = map(r3, (q, kc, vc, ks, vs, kw, vw, qkv, z, small))
        win_k, win_v = state_win_k[l], state_win_v[l]
        o_nsa = _nsa_sample(q, ks, vs, kw, vw, small[..., GATE_OFF:GATE_OFF + 3 * NSA_HEADS], pools, l, win_k, win_v,
                            page_table, w_cmp_k[l], w_cmp_v[l])
        o_gdn, ssm = _gdn_sample(qkv, z, small, state_conv[l], state_ssm[l], *gdn_w)
        y_s = _post(o_nsa.reshape(n, NSA_WIDTH), o_gdn.reshape(n, GDN_WIDTH), y_s.reshape(n, D_MODEL), *post_w).reshape(bs, ts, D_MODEL)
        conv_state = jnp.concatenate([state_conv[l], qkv], axis=1)[:, -(CONV_WIDTH - 1):]
        new_wk = jnp.concatenate([win_k, kv5(kw, bs, ts)], axis=1)[:, ts:]
        new_wv = jnp.concatenate([win_v, kv5(vw, bs, ts)], axis=1)[:, ts:]
        s_states.append((kv5(kc, bs, ts), kv5(vc, bs, ts), kv5(ks, bs, ts), kv5(vs, bs, ts), new_wk, new_wv, conv_state, ssm))
    p_out = [jnp.stack(t) for t in zip(*p_states)]
    s_out = [jnp.stack(t) for t in zip(*s_states)]
    return (y_p, y_s, *p_out, *s_out)
```

```python
import functools
import math

import jax
import jax.numpy as jnp
import numpy as np
from jax import lax
from jax.experimental import pallas as pl
from jax.experimental.pallas import tpu as pltpu

D_MODEL = 1024
PAGE_SIZE = 128
NSA_HEAD_DIM = 64
NSA_HEADS = 8
NSA_KV_HEADS = 2
NSA_GROUP = NSA_HEADS // NSA_KV_HEADS
NSA_WIDTH = NSA_HEADS * NSA_HEAD_DIM
NSA_KV_WIDTH = NSA_KV_HEADS * NSA_HEAD_DIM
BLOCK = 64
TOP_K = 16
WINDOW = 512
Q_BLOCK = 128
FORCED_SCORE = NSA_GROUP + 1.0
ROPE_THETA = 500000.0
ROPE_DIM = NSA_HEAD_DIM // 4
GDN_HEAD_DIM = 128
GDN_HEADS = 4
GDN_WIDTH = GDN_HEADS * GDN_HEAD_DIM
CONV_WIDTH = 4
GDN_CHUNK = 64
D_FF = 4 * D_MODEL
EPS = 1e-6
SPLIT_SIZES = (NSA_WIDTH,) + (NSA_KV_WIDTH,) * 6 + (3 * NSA_HEADS, 3 * GDN_WIDTH, GDN_WIDTH, GDN_HEADS, GDN_HEADS)

LANES = 128
SMALL_COLS = LANES
GATE_OFF, BETA_OFF, ALPHA_OFF = 0, 3 * NSA_HEADS, 3 * NSA_HEADS + GDN_HEADS
PROJ_COLS = NSA_WIDTH + 6 * NSA_KV_WIDTH + 3 * GDN_WIDTH + GDN_WIDTH + SMALL_COLS
VMEM_LIMIT = 56 * 1024 * 1024
NEG = -1e30
F32 = jnp.float32
BF16 = jnp.bfloat16
HI = lax.Precision.HIGHEST


def _rms(x, w):
    return x * lax.rsqrt(jnp.mean(x * x, axis=-1, keepdims=True) + EPS) * w


def _sigmoid(x):
    return 1.0 / (1.0 + jnp.exp(-x))


def _silu(x):
    return x * _sigmoid(x)


def _rope_slab(x, c, sa, sb):
    half = ROPE_DIM // 2
    return x * c + pltpu.roll(x, LANES - half, axis=1) * sa + pltpu.roll(x, half, axis=1) * sb


def _in_proj_body(x_ref, nw_ref, w_ref, rc_ref, rsa_ref, rsb_ref, wk_ref, wv_ref,
                  q_ref, kc_ref, vc_ref, ks_ref, vs_ref, kw_ref, vw_ref, qkv_ref, z_ref, sm_ref, ck_ref, cv_ref):
    h = _rms(x_ref[...], nw_ref[...])
    p = jnp.dot(h.astype(BF16), w_ref[...], preferred_element_type=F32)
    c, sa, sb = rc_ref[...], rsa_ref[...], rsb_ref[...]
    rows = p.shape[0]
    off = 0
    for j in range(NSA_WIDTH // LANES):
        q_ref[:, j * LANES:(j + 1) * LANES] = _rope_slab(p[:, off:off + LANES], c, sa, sb)
        off += LANES
    kc = _rope_slab(p[:, off:off + LANES], c, sa, sb)
    kc_ref[...] = kc
    vc = p[:, off + LANES:off + 2 * LANES]
    vc_ref[...] = vc
    ks_ref[...] = _rope_slab(p[:, off + 2 * LANES:off + 3 * LANES], c, sa, sb)
    vs_ref[...] = p[:, off + 3 * LANES:off + 4 * LANES]
    kw_ref[...] = _rope_slab(p[:, off + 4 * LANES:off + 5 * LANES], c, sa, sb)
    vw_ref[...] = p[:, off + 5 * LANES:off + 6 * LANES]
    off += 6 * LANES
    qkv_ref[...] = p[:, off:off + 3 * GDN_WIDTH]
    off += 3 * GDN_WIDTH
    z_ref[...] = p[:, off:off + GDN_WIDTH]
    off += GDN_WIDTH
    sm_ref[...] = p[:, off:off + SMALL_COLS]
    ck_ref[...] = jnp.sum(kc.reshape(rows // BLOCK, BLOCK, LANES) * wk_ref[...][None], axis=1)
    cv_ref[...] = jnp.sum(vc.reshape(rows // BLOCK, BLOCK, LANES) * wv_ref[...][None], axis=1)


def _permute_w_in(w_in):
    pts = np.cumsum((0,) + SPLIT_SIZES)
    seg = [w_in[:, int(pts[i]):int(pts[i + 1])] for i in range(len(SPLIT_SIZES))]
    q, kc, vc, ks, vs, kw, vw, gates, qkv, z, beta, alpha = seg
    small = jnp.concatenate([gates, beta, alpha], axis=1)
    small = jnp.pad(small, ((0, 0), (0, SMALL_COLS - small.shape[1])))
    return jnp.concatenate([q, kc, vc, ks, vs, kw, vw, qkv, z, small], axis=1)


def _rope_tables(pos):
    half = ROPE_DIM // 2
    inv_freq = ROPE_THETA ** (-jnp.arange(half, dtype=F32) / half)
    ang = pos.astype(F32)[:, None] * inv_freq[None, :]
    cos, sin = jnp.cos(ang), jnp.sin(ang)
    n = pos.shape[0]
    one = jnp.ones((n, NSA_HEAD_DIM - ROPE_DIM), F32)
    zero = jnp.zeros((n, NSA_HEAD_DIM - ROPE_DIM), F32)
    zh = jnp.zeros((n, half), F32)
    c = jnp.concatenate([cos, cos, one], axis=1)
    sa = jnp.concatenate([-sin, zh, zero], axis=1)
    sb = jnp.concatenate([zh, sin, zero], axis=1)
    return tuple(jnp.tile(t, (1, LANES // NSA_HEAD_DIM)) for t in (c, sa, sb))


def _in_proj(x2d, pos, norm_w, w_perm, wk2, wv2):
    n = x2d.shape[0]
    tm = min(512, n)
    assert n % tm == 0 and tm % BLOCK == 0
    rc, rsa, rsb = _rope_tables(pos)
    row = lambda w: pl.BlockSpec((tm, w), lambda i: (i, 0))
    const = lambda a: pl.BlockSpec(a.shape, lambda i: (0,) * a.ndim)
    nw = norm_w.reshape(1, D_MODEL)
    widths = (NSA_WIDTH,) + (LANES,) * 6 + (3 * GDN_WIDTH, GDN_WIDTH, SMALL_COLS)
    out_shape = [jax.ShapeDtypeStruct((n, w), F32) for w in widths]
    out_shape += [jax.ShapeDtypeStruct((n // BLOCK, LANES), F32)] * 2
    out_specs = [row(w) for w in widths] + [pl.BlockSpec((tm // BLOCK, LANES), lambda i: (i, 0))] * 2
    return pl.pallas_call(
        _in_proj_body,
        grid=(n // tm,),
        in_specs=[row(D_MODEL), const(nw), const(w_perm), row(LANES), row(LANES), row(LANES), const(wk2), const(wv2)],
        out_specs=out_specs,
        out_shape=out_shape,
        compiler_params=pltpu.CompilerParams(dimension_semantics=("arbitrary",), vmem_limit_bytes=VMEM_LIMIT),
        name="in_proj",
    )(x2d, nw, w_perm, rc, rsa, rsb, wk2, wv2)


FF_CHUNK = 1024


def _post_body(on_ref, og_ref, x_ref, nn_ref, wo_ref, nmp_ref, nfp_ref, wu_ref, wd_ref, nfo_ref, y_ref):
    o_nsa = _rms(on_ref[...], nn_ref[...])
    mix = jnp.concatenate([o_nsa, og_ref[...]], axis=1).astype(BF16)
    y = jnp.dot(mix, wo_ref[...], preferred_element_type=F32)
    y1 = x_ref[...] + _rms(y, nmp_ref[...])
    h = _rms(y1, nfp_ref[...]).astype(BF16)
    acc = jnp.zeros_like(y1)
    for c in range(D_FF // FF_CHUNK):
        u = jnp.dot(h, wu_ref[:, c * FF_CHUNK:(c + 1) * FF_CHUNK], preferred_element_type=F32)
        u = jnp.square(jnp.maximum(u, 0.0))
        acc = acc + jnp.dot(u.astype(BF16), wd_ref[c * FF_CHUNK:(c + 1) * FF_CHUNK, :], preferred_element_type=F32)
    y_ref[...] = y1 + _rms(acc, nfo_ref[...])


def _post(o_nsa, o_gdn, x2d, nsa_norm, w_out, n_mix_post, n_ffn_pre, w_up, w_down, n_ffn_post):
    n = x2d.shape[0]
    tm = min(256, n)
    assert n % tm == 0
    row = lambda w: pl.BlockSpec((tm, w), lambda i: (i, 0))
    const = lambda a: pl.BlockSpec(a.shape, lambda i: (0,) * a.ndim)
    r = lambda v: v.reshape(1, -1)
    args = (o_nsa, o_gdn, x2d, r(nsa_norm), w_out, r(n_mix_post), r(n_ffn_pre), w_up, w_down, r(n_ffn_post))
    return pl.pallas_call(
        _post_body,
        grid=(n // tm,),
        in_specs=[row(NSA_WIDTH), row(GDN_WIDTH), row(D_MODEL)] + [const(a) for a in args[3:]],
        out_specs=row(D_MODEL),
        out_shape=jax.ShapeDtypeStruct((n, D_MODEL), F32),
        compiler_params=pltpu.CompilerParams(dimension_semantics=("arbitrary",), vmem_limit_bytes=VMEM_LIMIT),
        name="out_proj_ffn",
    )(*args)


SEL_CHUNK = 512
SEL_AUG = 16


def _row_iota(shape):
    return lax.broadcasted_iota(jnp.int32, shape, 0)


def _topk_rows(score, blk, k):
    nb = score.shape[0]
    sel = jnp.zeros(score.shape, F32)
    for _ in range(k):
        m = jnp.max(score, axis=0, keepdims=True)
        idx = jnp.min(jnp.where(score == m, blk, nb), axis=0, keepdims=True)
        pick = blk == idx
        sel = jnp.where(pick, 1.0, sel)
        score = jnp.where(pick, -jnp.inf, score)
    return sel


def _softmax_rows(s, mask):
    m = jnp.max(jnp.where(mask, s, NEG), axis=0, keepdims=True)
    e = jnp.where(mask, jnp.exp2(s - m), 0.0)
    return e, 1.0 / jnp.maximum(jnp.sum(e, axis=0, keepdims=True), 1e-30)


def _nsa_prompt_body(qt_ref, ks_ref, kw_ref, vt_ref, ck_ref, cvt_ref, g_ref, o_ref, sb_ref, s0_ref, s1_ref, e0_ref, e1_ref, *, seq, kc, wn):
    i = pl.program_id(1)
    start = i * Q_BLOCK
    nb = seq // BLOCK
    g4 = range(NSA_GROUP)
    slab = lambda a, g: a[:, g * Q_BLOCK:(g + 1) * Q_BLOCK]
    qt = qt_ref[0, 0]
    pos = start + lax.broadcasted_iota(jnp.int32, (1, Q_BLOCK), 1)

    sc = jnp.dot(ck_ref[0], qt, preferred_element_type=F32)
    blk = _row_iota((nb, Q_BLOCK))
    cmask = (blk + 1) * BLOCK - 1 <= pos
    ps = []
    for g in g4:
        e, inv = _softmax_rows(slab(sc, g), cmask)
        ps.append(e * inv)
    o_c = jnp.dot(cvt_ref[0], jnp.concatenate(ps, axis=1).astype(BF16), preferred_element_type=F32)
    imp = ps[0]
    for g in range(1, NSA_GROUP):
        imp = imp + ps[g]

    cur = pos // BLOCK
    forced = (blk == 0) | (blk == cur) | (blk == cur - 1)
    score = jnp.where(forced, FORCED_SCORE, jnp.where(blk <= cur, imp, -1.0))
    sel = _topk_rows(score, blk, min(TOP_K, nb))
    sel_bias = jnp.where((score >= 0) & (sel > 0.0), 0.0, NEG)
    for g in g4:
        sb_ref[:, g * Q_BLOCK:(g + 1) * Q_BLOCK] = sel_bias

    bpc = kc // BLOCK
    pad_rows = jnp.zeros((SEL_AUG - bpc, NSA_GROUP * Q_BLOCK), F32)
    pad_tail = jnp.zeros((LANES - NSA_HEAD_DIM - SEL_AUG, NSA_GROUP * Q_BLOCK), BF16)

    n_parts = NSA_GROUP
    part = kc // n_parts

    def q_aug(c):
        rows = sb_ref[pl.ds(pl.multiple_of(c * bpc, bpc), bpc), :]
        return jnp.concatenate([qt, jnp.concatenate([rows, pad_rows], axis=0).astype(BF16), pad_tail], axis=0)

    def scores_part(c, qa, j):
        k0 = pl.multiple_of(c * kc + j * part, part)
        return jnp.dot(ks_ref[0, pl.ds(k0, part), :], qa, preferred_element_type=F32)

    def values_dot(c, e_ref):
        k0 = pl.multiple_of(c * kc, kc)
        return jnp.dot(vt_ref[0, 0:NSA_HEAD_DIM, pl.ds(k0, kc)], e_ref[...], preferred_element_type=F32)

    def sweep_step(c, carry, cur, nxt, last):
        m, l, acc, alpha_prev = carry
        pv = values_dot(jnp.maximum(c - 1, 0), nxt[1])
        qa = None if last else q_aug(c + 1)
        k0 = pl.multiple_of(c * kc, kc)
        ms, ls, alphas = [], [], []
        for g in g4:
            if not last:
                nxt[0][g * part:(g + 1) * part, :] = scores_part(c + 1, qa, g)
            sg = cur[0][:, g * Q_BLOCK:(g + 1) * Q_BLOCK]
            if last:
                sg = sg + jnp.where(k0 + _row_iota((kc, Q_BLOCK)) <= pos, 0.0, NEG)
            mo = slab(m, g)
            mn = jnp.maximum(mo, jnp.max(sg, axis=0, keepdims=True))
            e = jnp.exp2(sg - mn)
            cur[1][:, g * Q_BLOCK:(g + 1) * Q_BLOCK] = e.astype(BF16)
            a = jnp.exp2(mo - mn)
            ms.append(mn)
            ls.append(a * slab(l, g) + jnp.sum(e, axis=0, keepdims=True))
            alphas.append(a)
        cat = lambda xs: jnp.concatenate(xs, axis=1)
        return cat(ms), cat(ls), alpha_prev * acc + pv, cat(alphas)

    def finish(c, carry, cur, nxt):
        _, l, acc, alpha = sweep_step(c, carry, cur, nxt, True)
        return l, alpha * acc + values_dot(c, cur[1])

    width = NSA_GROUP * Q_BLOCK
    even, odd = (s0_ref, e0_ref), (s1_ref, e1_ref)
    qa0 = q_aug(0)
    for g in g4:
        s0_ref[g * part:(g + 1) * part, :] = scores_part(0, qa0, g)
    e1_ref[...] = jnp.zeros(e1_ref.shape, BF16)
    init = (jnp.full((1, width), NEG, F32), jnp.zeros((1, width), F32), jnp.zeros((NSA_HEAD_DIM, width), F32),
            jnp.ones((1, width), F32))
    c_diag = start // kc

    def pair(p, carry):
        carry = sweep_step(2 * p, carry, even, odd, False)
        return sweep_step(2 * p + 1, carry, odd, even, False)

    carry = lax.fori_loop(0, c_diag // 2, pair, init)
    l_s, acc_s = lax.cond(
        c_diag % 2 == 1,
        lambda cr: finish(c_diag, sweep_step(c_diag - 1, cr, even, odd, False), odd, even),
        lambda cr: finish(c_diag, cr, even, odd),
        carry)
    o_s = acc_s * (1.0 / jnp.maximum(l_s, 1e-30))

    w0 = pl.multiple_of(jnp.maximum(start + Q_BLOCK - wn, 0), Q_BLOCK)
    sw = jnp.dot(kw_ref[0, pl.ds(w0, wn), :], qt, preferred_element_type=F32)
    dist = pos - (w0 + _row_iota((wn, Q_BLOCK)))
    wmask = (dist >= 0) & (dist < WINDOW)
    ews, invs = [], []
    for g in g4:
        e, inv = _softmax_rows(slab(sw, g), wmask)
        ews.append(e)
        invs.append(inv)
    o_w = jnp.dot(vt_ref[0, NSA_HEAD_DIM:2 * NSA_HEAD_DIM, pl.ds(w0, wn)], jnp.concatenate(ews, axis=1).astype(BF16),
                  preferred_element_type=F32) * jnp.concatenate(invs, axis=1)

    gate = _sigmoid(g_ref[0])
    for g in g4:
        o_ref[0, g] = (gate[3 * g:3 * g + 1] * slab(o_c, g) + gate[3 * g + 1:3 * g + 2] * slab(o_s, g)
                       + gate[3 * g + 2:3 * g + 3] * slab(o_w, g))


def _split_heads_t(x, n):
    return x.reshape(n, NSA_KV_HEADS, NSA_HEAD_DIM).transpose(1, 0, 2)


def _nsa_prompt(q, ks, vs, kw, vw, ck, cv, gates):
    s = q.shape[0]
    assert s % Q_BLOCK == 0
    n_qb, nb = s // Q_BLOCK, s // BLOCK
    kc = min(SEL_CHUNK, s)
    wn = min(WINDOW + Q_BLOCK, s)
    assert s % kc == 0
    scale = NSA_HEAD_DIM ** -0.5 * math.log2(math.e)
    qt = (q * scale).astype(BF16).reshape(n_qb, Q_BLOCK, NSA_KV_HEADS, NSA_GROUP, NSA_HEAD_DIM)
    qt = qt.transpose(2, 0, 4, 3, 1).reshape(NSA_KV_HEADS, n_qb, NSA_HEAD_DIM, NSA_GROUP * Q_BLOCK)
    bpc = kc // BLOCK
    assert kc % Q_BLOCK == 0 and bpc <= SEL_AUG
    onehot = ((jnp.arange(s) // BLOCK) % bpc)[:, None] == jnp.arange(LANES - NSA_HEAD_DIM)[None, :]
    ksa = jnp.concatenate([_split_heads_t(ks, s).astype(BF16),
                           jnp.broadcast_to(onehot.astype(BF16), (NSA_KV_HEADS, s, LANES - NSA_HEAD_DIM))], axis=-1)
    kwh = _split_heads_t(kw, s).astype(BF16)
    vt = jnp.concatenate([_split_heads_t(vs, s), _split_heads_t(vw, s)], axis=-1).astype(BF16).transpose(0, 2, 1)
    ckh = _split_heads_t(ck, nb).astype(BF16)
    cvt = _split_heads_t(cv, nb).astype(BF16).transpose(0, 2, 1)
    gt = gates.reshape(s, NSA_KV_HEADS, NSA_GROUP * 3).transpose(1, 2, 0)
    gt = jnp.pad(gt, ((0, 0), (0, 16 - NSA_GROUP * 3), (0, 0)))
    body = functools.partial(_nsa_prompt_body, seq=s, kc=kc, wn=wn)
    o = pl.pallas_call(
        body,
        grid=(NSA_KV_HEADS, n_qb),
        in_specs=[
            pl.BlockSpec((1, 1, NSA_HEAD_DIM, NSA_GROUP * Q_BLOCK), lambda h, i: (h, i, 0, 0)),
            pl.BlockSpec((1, s, LANES), lambda h, i: (h, 0, 0)),
            pl.BlockSpec((1, s, NSA_HEAD_DIM), lambda h, i: (h, 0, 0)),
            pl.BlockSpec((1, LANES, s), lambda h, i: (h, 0, 0)),
            pl.BlockSpec((1, nb, NSA_HEAD_DIM), lambda h, i: (h, 0, 0)),
            pl.BlockSpec((1, NSA_HEAD_DIM, nb), lambda h, i: (h, 0, 0)),
            pl.BlockSpec((1, 16, Q_BLOCK), lambda h, i: (h, 0, i)),
        ],
        out_specs=pl.BlockSpec((1, NSA_GROUP, NSA_HEAD_DIM, Q_BLOCK), lambda h, i: (h, 0, 0, i)),
        out_shape=jax.ShapeDtypeStruct((NSA_KV_HEADS, NSA_GROUP, NSA_HEAD_DIM, s), F32),
        scratch_shapes=[pltpu.VMEM((nb, NSA_GROUP * Q_BLOCK), F32)]
        + [pltpu.VMEM((kc, NSA_GROUP * Q_BLOCK), F32)] * 2 + [pltpu.VMEM((kc, NSA_GROUP * Q_BLOCK), BF16)] * 2,
        compiler_params=pltpu.CompilerParams(dimension_semantics=("arbitrary", "arbitrary"), vmem_limit_bytes=VMEM_LIMIT),
        name="nsa_prompt",
    )(qt, ksa, kwh, vt, ckh, cvt, gt)
    return o.transpose(3, 0, 1, 2).reshape(s, NSA_WIDTH)


def _dot(a, b):
    return jnp.dot(a, b, precision=HI, preferred_element_type=F32)


def _dot_nt(a, b):
    return lax.dot_general(a, b, (((1,), (1,)), ((), ())), precision=HI, preferred_element_type=F32)


def _dot_tn(a, b):
    return lax.dot_general(a, b, (((0,), (0,)), ((), ())), precision=HI, preferred_element_type=F32)


def _dot1_nt(a, b):
    return lax.dot_general(a, b, (((1,), (1,)), ((), ())), preferred_element_type=F32)


def _split16(a):
    hi = a.astype(BF16)
    return hi, (a - hi.astype(F32)).astype(BF16)


def _dot3(a, b):
    ah, al = _split16(a)
    bh, bl = _split16(b)
    d = lambda x, y: jnp.dot(x, y, preferred_element_type=F32)
    return d(ah, bh) + (d(ah, bl) + d(al, bh))


def _softplus(x):
    return jnp.maximum(x, 0.0) + jnp.log(1.0 + jnp.exp(-jnp.abs(x)))


def _unit_lower_inverses(mats, ri, ci):
    c = mats[0].shape[0]
    eye = (ri == ci).astype(F32)
    base = 8
    same = ri // base == ci // base
    ds = [jnp.where(same, a, 0.0) for a in mats]
    d2s = [_dot3(d, d) for d in ds]
    xs = [(eye - d) + _dot3(eye - d, d2) for d, d2 in zip(ds, d2s)]
    d4s = [_dot3(d2, d2) for d2 in d2s]
    xs = [x + _dot3(x, d4) for x, d4 in zip(xs, d4s)]
    b = base
    while b < c:
        band = (ri // (2 * b) == ci // (2 * b)) & (ri // b != ci // b)
        lxs = [_dot3(jnp.where(band, a, 0.0), x) for a, x in zip(mats, xs)]
        xs = [x - _dot3(x, lx) for x, lx in zip(xs, lxs)]
        b *= 2
    return xs


def _conv_silu(x, prev, w):
    rowi = _row_iota(x.shape)
    conv = None
    for j in range(CONV_WIDTH - 1, -1, -1):
        if j == 0:
            xs = x
        else:
            xs = jnp.where(rowi < j, pltpu.roll(prev, j, axis=0), pltpu.roll(x, j, axis=0))
        term = xs * w[CONV_WIDTH - 1 - j:CONV_WIDTH - j]
        conv = term if conv is None else conv + term
    return _silu(conv)


def _l2n(x):
    return x * lax.rsqrt(jnp.sum(x * x, axis=-1, keepdims=True) + EPS)


GDN_STEP_CHUNKS = 4


def _gdn_chunk_body(qkv_ref, sm_ref, smt_ref, prev0_ref, cw_ref, ar_ref, dr_ref, at_ref, dt_ref,
                    u_ref, w_ref, qe_ref, kd_ref, in_ref, dl_ref, xp_ref):
    cs = GDN_CHUNK

    @pl.when(pl.program_id(0) == 0)
    def _():
        xp_ref[...] = prev0_ref[...]

    x = qkv_ref[...]
    act = _conv_silu(x, xp_ref[...], cw_ref[...])
    xp_ref[...] = x
    sm = sm_ref[...]
    ri = _row_iota((cs, cs))
    ci = lax.broadcasted_iota(jnp.int32, (cs, cs), 1)
    tri = (ri >= ci).astype(F32)
    tri_t = (ri <= ci).astype(F32)
    g_all = -jnp.exp(ar_ref[...]) * _softplus(sm + dr_ref[...])
    beta_all = _sigmoid(sm)
    scale = GDN_HEAD_DIM ** -0.5
    a_mats, rhs, where = [], [], []
    for c in range(GDN_STEP_CHUNKS):
        rows = slice(c * cs, (c + 1) * cs)
        gc_cols = _dot(tri, g_all[rows])
        g_rows = -jnp.exp(at_ref[...]) * _softplus(smt_ref[c] + dt_ref[...])
        gc_rows = _dot(g_rows, tri_t)
        intras, dls = [], []
        for h in range(GDN_HEADS):
            hs = slice(h * GDN_HEAD_DIM, (h + 1) * GDN_HEAD_DIM)
            q = _l2n(act[rows, hs]) * scale
            k = _l2n(act[rows, GDN_WIDTH + h * GDN_HEAD_DIM:GDN_WIDTH + (h + 1) * GDN_HEAD_DIM])
            v = act[rows, 2 * GDN_WIDTH + h * GDN_HEAD_DIM:2 * GDN_WIDTH + (h + 1) * GDN_HEAD_DIM]
            beta = beta_all[rows, BETA_OFF + h:BETA_OFF + h + 1]
            gc = gc_cols[:, ALPHA_OFF + h:ALPHA_OFF + h + 1]
            gr = gc_rows[GDN_HEADS + h:GDN_HEADS + h + 1, :]
            g_last = gc[cs - 1:cs, :]
            decay = jnp.where(ri >= ci, jnp.exp(gc - gr), 0.0)
            kb = k * beta
            k16 = k.astype(BF16)
            a_mats.append(jnp.where(ri > ci, _dot1_nt(kb.astype(BF16), k16) * decay, 0.0))
            rhs.append(jnp.concatenate([v * beta, kb * jnp.exp(gc)], axis=1))
            where.append((rows, hs))
            qe_ref[rows, hs] = (q * jnp.exp(gc)).astype(BF16)
            kd_ref[rows, hs] = (k * jnp.exp(g_last - gc)).astype(BF16)
            intras.append((_dot1_nt(q.astype(BF16), k16) * decay).astype(BF16))
            dls.append(jnp.broadcast_to(jnp.exp(g_last), (1, LANES)))
        in_ref[rows, :] = jnp.concatenate(intras, axis=1)
        dl_ref[c] = jnp.concatenate(dls + [jnp.zeros((2 * GDN_HEADS - GDN_HEADS, LANES), F32)], axis=0)
    for t, r, (rows, hs) in zip(_unit_lower_inverses(a_mats, ri, ci), rhs, where):
        sol = _dot3(t, r)
        u_ref[rows, hs] = sol[:, :GDN_HEAD_DIM]
        w_ref[rows, hs] = sol[:, GDN_HEAD_DIM:].astype(BF16)


def _gdn_scan_body(u_ref, w_ref, qe_ref, kd_ref, in_ref, dl_ref, z_ref, s0_ref, gn_ref, o_ref, sf_ref, s_ref):
    cs = GDN_CHUNK
    i = pl.program_id(0)

    @pl.when(i == 0)
    def _():
        s_ref[...] = s0_ref[...]

    heads = range(GDN_HEADS)
    hsl = [slice(h * GDN_HEAD_DIM, (h + 1) * GDN_HEAD_DIM) for h in heads]
    tn = (((0,), (0,)), ((), ()))
    ss = [s_ref[h] for h in heads]
    for c in range(GDN_STEP_CHUNKS):
        rows = slice(c * cs, (c + 1) * cs)
        s16 = [s.astype(BF16) for s in ss]
        v16 = [(u_ref[rows, hsl[h]] - jnp.dot(w_ref[rows, hsl[h]], s16[h], preferred_element_type=F32)).astype(BF16)
               for h in heads]
        ss = [ss[h] * dl_ref[c, h:h + 1, :] + lax.dot_general(kd_ref[rows, hsl[h]], v16[h], tn, preferred_element_type=F32)
              for h in heads]
        for h in heads:
            o = (jnp.dot(qe_ref[rows, hsl[h]], s16[h], preferred_element_type=F32)
                 + jnp.dot(in_ref[rows, h * cs:(h + 1) * cs], v16[h], preferred_element_type=F32))
            o_ref[rows, hsl[h]] = _rms(o, gn_ref[...]) * _silu(z_ref[rows, hsl[h]])
    for h in heads:
        s_ref[h] = ss[h]

    @pl.when(i == pl.num_programs(0) - 1)
    def _():
        sf_ref[...] = s_ref[...]


def _gdn_consts(a_log, dt_bias):
    lane = jnp.zeros((1, LANES), F32)
    ar = lane.at[0, ALPHA_OFF:ALPHA_OFF + GDN_HEADS].set(a_log)
    dr = lane.at[0, ALPHA_OFF:ALPHA_OFF + GDN_HEADS].set(dt_bias)
    col = jnp.zeros((2 * GDN_HEADS, 1), F32)
    at = jnp.broadcast_to(col.at[GDN_HEADS:, 0].set(a_log), (2 * GDN_HEADS, GDN_CHUNK))
    dt = jnp.broadcast_to(col.at[GDN_HEADS:, 0].set(dt_bias), (2 * GDN_HEADS, GDN_CHUNK))
    return ar, dr, at, dt


def _gdn_prompt(qkv, z, small, conv_prev, s0, conv_w, a_log, dt_bias, gdn_norm):
    n = qkv.shape[0]
    cs = GDN_CHUNK
    rows = GDN_STEP_CHUNKS * cs
    assert n % rows == 0
    nc = n // cs
    smt = small[:, BETA_OFF:BETA_OFF + 2 * GDN_HEADS].reshape(nc, cs, 2 * GDN_HEADS).transpose(0, 2, 1)
    prev0 = jnp.pad(conv_prev, ((rows - (CONV_WIDTH - 1), 0), (0, 0)))
    ar, dr, at, dt = _gdn_consts(a_log, dt_bias)
    gn = gdn_norm.reshape(1, GDN_HEAD_DIM)
    row = lambda w: pl.BlockSpec((rows, w), lambda i: (i, 0))
    per_chunk = pl.BlockSpec((GDN_STEP_CHUNKS, 2 * GDN_HEADS, LANES), lambda i: (i, 0, 0))
    const = lambda a: pl.BlockSpec(a.shape, lambda i: (0,) * a.ndim)
    params = pltpu.CompilerParams(dimension_semantics=("arbitrary",), vmem_limit_bytes=VMEM_LIMIT)
    wide = lambda dt_: jax.ShapeDtypeStruct((n, GDN_WIDTH), dt_)
    u, w, qe, kd, intra, dl = pl.pallas_call(
        _gdn_chunk_body,
        grid=(n // rows,),
        in_specs=[row(3 * GDN_WIDTH), row(SMALL_COLS),
                  pl.BlockSpec((GDN_STEP_CHUNKS, 2 * GDN_HEADS, cs), lambda i: (i, 0, 0)),
                  const(prev0), const(conv_w), const(ar), const(dr), const(at), const(dt)],
        out_specs=[row(GDN_WIDTH)] * 4 + [row(GDN_HEADS * cs), per_chunk],
        out_shape=[wide(F32), wide(BF16), wide(BF16), wide(BF16), jax.ShapeDtypeStruct((n, GDN_HEADS * cs), BF16),
                   jax.ShapeDtypeStruct((nc, 2 * GDN_HEADS, LANES), F32)],
        scratch_shapes=[pltpu.VMEM((rows, 3 * GDN_WIDTH), F32)],
        compiler_params=params,
        name="gdn_chunks",
    )(qkv, small, smt, prev0, conv_w, ar, dr, at, dt)
    return pl.pallas_call(
        _gdn_scan_body,
        grid=(n // rows,),
        in_specs=[row(GDN_WIDTH)] * 4 + [row(GDN_HEADS * cs), per_chunk, row(GDN_WIDTH), const(s0), const(gn)],
        out_specs=[row(GDN_WIDTH), const(s0)],
        out_shape=[wide(F32), jax.ShapeDtypeStruct(s0.shape, F32)],
        scratch_shapes=[pltpu.VMEM(s0.shape, F32)],
        compiler_params=params,
        name="gdn_scan",
    )(u, w, qe, kd, intra, dl, z, s0, gn)


def _gdn_sample_pre_body(x0_ref, x1_ref, x2_ref, x3_ref, sm_ref, cw_ref, ar_ref, dr_ref, q_ref, k_ref, v_ref, a_ref, b_ref):
    w = cw_ref[...]
    conv = x0_ref[...] * w[0:1] + x1_ref[...] * w[1:2] + x2_ref[...] * w[2:3] + x3_ref[...] * w[3:4]
    act = _silu(conv)
    sm = sm_ref[...]
    a_cols = jnp.exp(-jnp.exp(ar_ref[...]) * _softplus(sm + dr_ref[...]))
    b_cols = _sigmoid(sm)
    rows = sm.shape[0]
    for h in range(GDN_HEADS):
        hs = slice(h * GDN_HEAD_DIM, (h + 1) * GDN_HEAD_DIM)
        q_ref[:, hs] = _l2n(act[:, hs]) * GDN_HEAD_DIM ** -0.5
        k_ref[:, hs] = _l2n(act[:, GDN_WIDTH + h * GDN_HEAD_DIM:GDN_WIDTH + (h + 1) * GDN_HEAD_DIM])
        a_ref[:, hs] = jnp.broadcast_to(a_cols[:, ALPHA_OFF + h:ALPHA_OFF + h + 1], (rows, GDN_HEAD_DIM))
        b_ref[:, hs] = jnp.broadcast_to(b_cols[:, BETA_OFF + h:BETA_OFF + h + 1], (rows, GDN_HEAD_DIM))
    v_ref[...] = act[:, 2 * GDN_WIDTH:]


GDN_SAMPLE_SEQS = 8


def _gdn_sample_body(qt_ref, kt_ref, v_ref, a_ref, b_ref, z_ref, s0_ref, gn_ref, o_ref, sf_ref, *, t_len):
    gn = gn_ref[...]

    def seq(bl, carry):
        v_all, a_all, b_all, z_all = v_ref[bl], a_ref[bl], b_ref[bl], z_ref[bl]
        for h in range(GDN_HEADS):
            hs = slice(h * GDN_HEAD_DIM, (h + 1) * GDN_HEAD_DIM)
            s = s0_ref[bl, h]
            qt, kt = qt_ref[bl, h], kt_ref[bl, h]
            outs = []
            for t in range(t_len):
                k_col, q_col = kt[:, t:t + 1], qt[:, t:t + 1]
                a, b = a_all[t:t + 1, hs], b_all[t:t + 1, hs]
                sk = jnp.sum(s * k_col, axis=0, keepdims=True)
                delta = b * (v_all[t:t + 1, hs] - a * sk)
                s = a * s + k_col * delta
                outs.append(jnp.sum(s * q_col, axis=0, keepdims=True))
            sf_ref[bl, h] = s
            o = jnp.concatenate(outs, axis=0)
            o_ref[bl, :, hs] = _rms(o, gn) * _silu(z_all[:, hs])
        return carry

    lax.fori_loop(0, GDN_SAMPLE_SEQS, seq, 0)


def _gdn_sample(qkv, z, small, conv_prev, s0, conv_w, a_log, dt_bias, gdn_norm):
    b, t = qkv.shape[:2]
    n = b * t
    xp = jnp.concatenate([conv_prev, qkv], axis=1)
    xs = [xp[:, j:j + t].reshape(n, 3 * GDN_WIDTH) for j in range(CONV_WIDTH)]
    ar, dr, _, _ = _gdn_consts(a_log, dt_bias)
    tm = min(256, n)
    assert n % tm == 0
    row = lambda w: pl.BlockSpec((tm, w), lambda i: (i, 0))
    const = lambda a: pl.BlockSpec(a.shape, lambda i: (0,) * a.ndim)
    q, k, v, av, bv = pl.pallas_call(
        _gdn_sample_pre_body,
        grid=(n // tm,),
        in_specs=[row(3 * GDN_WIDTH)] * 4 + [row(SMALL_COLS), const(conv_w), const(ar), const(dr)],
        out_specs=[row(GDN_WIDTH)] * 5,
        out_shape=[jax.ShapeDtypeStruct((n, GDN_WIDTH), F32)] * 5,
        compiler_params=pltpu.CompilerParams(dimension_semantics=("arbitrary",), vmem_limit_bytes=VMEM_LIMIT),
        name="gdn_sample_pre",
    )(*xs, small.reshape(n, SMALL_COLS), conv_w, ar, dr)
    tr = lambda x: x.reshape(b, t, GDN_HEADS, GDN_HEAD_DIM).transpose(0, 2, 3, 1)
    r3 = lambda x: x.reshape(b, t, GDN_WIDTH)
    bb = GDN_SAMPLE_SEQS
    assert b % bb == 0
    gn = gdn_norm.reshape(1, GDN_HEAD_DIM)
    tspec = pl.BlockSpec((bb, GDN_HEADS, GDN_HEAD_DIM, t), lambda i: (i, 0, 0, 0))
    rspec = pl.BlockSpec((bb, t, GDN_WIDTH), lambda i: (i, 0, 0))
    sspec = pl.BlockSpec((bb, GDN_HEADS, GDN_HEAD_DIM, GDN_HEAD_DIM), lambda i: (i, 0, 0, 0))
    o, sf = pl.pallas_call(
        functools.partial(_gdn_sample_body, t_len=t),
        grid=(b // bb,),
        in_specs=[tspec, tspec, rspec, rspec, rspec, rspec, sspec, const(gn)],
        out_specs=[rspec, sspec],
        out_shape=[jax.ShapeDtypeStruct((b, t, GDN_WIDTH), F32), jax.ShapeDtypeStruct(s0.shape, F32)],
        compiler_params=pltpu.CompilerParams(dimension_semantics=("arbitrary",), vmem_limit_bytes=VMEM_LIMIT),
        name="gdn_sample",
    )(tr(q), tr(k), r3(v), r3(av), r3(bv), z, s0, gn)
    return o, sf


PAGES_PER_STEP = 32
T_PAD = 8
Q_ROWS = NSA_KV_HEADS * NSA_GROUP * T_PAD
NEW_PAD = LANES


def _nsa_sample_body(pt_ref, ckp_ref, cvp_ref, skp_ref, svp_ref, q_ref, kn_ref, vn_ref, wk_ref, wv_ref, kwn_ref, vwn_ref,
                     wck_ref, wcv_ref, g_ref, ex_ref, ext_ref, o_ref,
                     buf_ref, sem_ref, ck_ref, cv_ref, selm_ref, oc_ref, m_ref, l_ref, acc_ref,
                     *, n_groups, t_len, win_len, page_base):
    b, ph, g = pl.program_id(0), pl.program_id(1), pl.program_id(2)
    nb_, np_, ng_ = pl.num_programs(0), pl.num_programs(1), pl.num_programs(2)
    step = (b * np_ + ph) * ng_ + g
    pps = PAGES_PER_STEP
    rows_per_step = pps * PAGE_SIZE
    blocks_per_step = rows_per_step // BLOCK

    def copies(pool_k, pool_v, bb, gg, slot):
        out = []
        for p in range(pps):
            page = page_base + pt_ref[bb, gg * pps + p]
            for kv, pool in enumerate((pool_k, pool_v)):
                out.append(pltpu.make_async_copy(pool.at[page], buf_ref.at[slot, kv, :, pl.ds(p * PAGE_SIZE, PAGE_SIZE)],
                                                 sem_ref.at[slot, kv]))
        return out

    def start(bb, pp, gg, slot):
        @pl.when(pp == 0)
        def _():
            for cp in copies(ckp_ref, cvp_ref, bb, gg, slot):
                cp.start()

        @pl.when(pp == 1)
        def _():
            for cp in copies(skp_ref, svp_ref, bb, gg, slot):
                cp.start()

    slot = step % 2

    @pl.when(step == 0)
    def _():
        start(b, ph, g, slot)

    nxt = step + 1

    @pl.when(nxt < nb_ * np_ * ng_)
    def _():
        start(nxt // (np_ * ng_), (nxt // ng_) % np_, nxt % ng_, 1 - slot)

    for cp in copies(ckp_ref, cvp_ref, b, g, slot):
        cp.wait()

    q = q_ref[0]
    rowt = _row_iota((Q_ROWS, LANES)) % T_PAD
    lane = lax.broadcasted_iota(jnp.int32, (Q_ROWS, LANES), 1)
    new_mask = (lane <= rowt) & (lane < t_len)

    nt = (((1,), (1,)), ((), ()))

    def weighted(kv, w_ref):
        w = w_ref[...]
        return jnp.concatenate([buf_ref[slot, kv, :, p * PAGE_SIZE:(p + 1) * PAGE_SIZE] * w for p in range(pps)],
                               axis=1).astype(BF16)

    @pl.when(ph == 0)
    def _():
        @pl.when(g == 0)
        def _():
            ck_ref[...] = jnp.zeros(ck_ref.shape, F32)
            cv_ref[...] = jnp.zeros(cv_ref.shape, F32)

        ck_ref[...] += jnp.dot(weighted(0, wck_ref), ext_ref[g], preferred_element_type=F32)
        cv_ref[...] += jnp.dot(weighted(1, wcv_ref), ext_ref[g], preferred_element_type=F32)

        @pl.when(g == n_groups - 1)
        def _():
            nb = ck_ref.shape[1]
            s = jnp.dot(q, ck_ref[...].astype(BF16), preferred_element_type=F32)
            m = jnp.max(s, axis=1, keepdims=True)
            e = jnp.exp(s - m)
            p = e / jnp.maximum(jnp.sum(e, axis=1, keepdims=True), 1e-30)
            oc_ref[...] = lax.dot_general(p.astype(BF16), cv_ref[...].astype(BF16), nt, preferred_element_type=F32)
            imps = []
            for h in range(NSA_KV_HEADS):
                r = h * NSA_GROUP * T_PAD
                imp = p[r:r + T_PAD]
                for gg in range(1, NSA_GROUP):
                    imp = imp + p[r + gg * T_PAD:r + (gg + 1) * T_PAD]
                imps.append(imp)
            imps.append(jnp.zeros((nb - NSA_KV_HEADS * T_PAD, nb), F32))
            imp_t = jnp.concatenate(imps, axis=0).T
            bl = _row_iota((nb, nb))
            score = jnp.where((bl == 0) | (bl == nb - 1), FORCED_SCORE, imp_t)
            sel = _topk_rows(score, bl, min(TOP_K - 1, nb)).T
            for h in range(NSA_KV_HEADS):
                for gg in range(NSA_GROUP):
                    r = (h * NSA_GROUP + gg) * T_PAD
                    selm_ref[r:r + T_PAD, :] = sel[h * T_PAD:(h + 1) * T_PAD]
            m_ref[...] = jnp.full(m_ref.shape, NEG, F32)
            l_ref[...] = jnp.zeros(l_ref.shape, F32)
            acc_ref[...] = jnp.zeros(acc_ref.shape, F32)

    def flash(s, mask, vals):
        m_old = m_ref[...]
        m_new = jnp.maximum(m_old, jnp.max(jnp.where(mask, s, NEG), axis=1, keepdims=True))
        e = jnp.where(mask, jnp.exp(s - m_new), 0.0)
        a = jnp.exp(m_old - m_new)
        m_ref[...] = m_new
        l_ref[...] = a * l_ref[...] + jnp.sum(e, axis=1, keepdims=True)
        acc_ref[...] = a * acc_ref[...] + lax.dot_general(e.astype(BF16), vals, nt, preferred_element_type=F32)

    @pl.when(ph == 1)
    def _():
        kb = buf_ref[slot, 0].astype(BF16)
        vb = buf_ref[slot, 1].astype(BF16)
        s = jnp.dot(q, kb, preferred_element_type=F32)
        mask = jnp.dot(selm_ref[...].astype(BF16), ex_ref[g], preferred_element_type=F32) > 0.5
        flash(s, mask, vb)

        @pl.when(g == n_groups - 1)
        def _():
            flash(jnp.dot(q, kn_ref[0], preferred_element_type=F32), new_mask, vn_ref[0])
            o_s = acc_ref[...] / jnp.maximum(l_ref[...], 1e-30)
            wk = wk_ref[0].astype(BF16)
            s_old = jnp.dot(q, wk, preferred_element_type=F32)
            s_new = jnp.dot(q, kwn_ref[0], preferred_element_type=F32)
            j = lax.broadcasted_iota(jnp.int32, (Q_ROWS, win_len), 1)
            old_mask = j > win_len - WINDOW + _row_iota((Q_ROWS, win_len)) % T_PAD
            mw = jnp.maximum(jnp.max(jnp.where(old_mask, s_old, NEG), axis=1, keepdims=True),
                             jnp.max(jnp.where(new_mask, s_new, NEG), axis=1, keepdims=True))
            e_old = jnp.where(old_mask, jnp.exp(s_old - mw), 0.0)
            e_new = jnp.where(new_mask, jnp.exp(s_new - mw), 0.0)
            lw = jnp.sum(e_old, axis=1, keepdims=True) + jnp.sum(e_new, axis=1, keepdims=True)
            o_w = (lax.dot_general(e_old.astype(BF16), wv_ref[0].astype(BF16), nt, preferred_element_type=F32)
                   + lax.dot_general(e_new.astype(BF16), vwn_ref[0], nt, preferred_element_type=F32)) / jnp.maximum(lw, 1e-30)
            gate = _sigmoid(g_ref[0])
            o_ref[0] = gate[:, 0:1] * oc_ref[...] + gate[:, 1:2] * o_s + gate[:, 2:3] * o_w


def _feature_major(x):
    lead = x.shape[:-3]
    n = len(lead)
    return x.transpose(*range(n), n + 1, n + 2, n).reshape(*lead, NSA_KV_WIDTH, x.shape[-3])


def _nsa_sample(q, ks, vs, kw, vw, gates, pools, layer, win_k, win_v, page_table, w_cmp_k, w_cmp_v):
    b, t = q.shape[:2]
    n_pages = page_table.shape[1]
    assert t <= T_PAD and t <= BLOCK and PAGE_SIZE % BLOCK == 0
    pps = PAGES_PER_STEP
    assert n_pages % pps == 0
    n_groups = n_pages // pps
    nb = n_pages * PAGE_SIZE // BLOCK
    assert nb % LANES == 0
    wb = win_k.shape[1]
    n_pool = pools[0].shape[1]
    flat = [_feature_major(p).reshape(p.shape[0] * n_pool, NSA_KV_WIDTH, PAGE_SIZE) for p in pools]
    wt = lambda w: jnp.tile(w.T, (NSA_KV_HEADS, PAGE_SIZE // BLOCK))
    wck2, wcv2 = wt(w_cmp_k), wt(w_cmp_v)
    scale = NSA_HEAD_DIM ** -0.5
    q5 = (q * scale).reshape(b, t, NSA_KV_HEADS, NSA_GROUP, NSA_HEAD_DIM).transpose(0, 2, 3, 1, 4)
    q5 = jnp.pad(q5, ((0, 0), (0, 0), (0, 0), (0, T_PAD - t), (0, 0)))
    own = jnp.eye(NSA_KV_HEADS, dtype=F32)[None, :, None, None, :, None]
    qbd = (q5[:, :, :, :, None, :] * own).reshape(b, Q_ROWS, NSA_KV_WIDTH).astype(BF16)
    padn = lambda x: jnp.pad(x, ((0, 0), (0, NEW_PAD - t), (0, 0))).astype(BF16).transpose(0, 2, 1)
    g5 = gates.reshape(b, t, NSA_KV_HEADS, NSA_GROUP, 3).transpose(0, 2, 3, 1, 4)
    g5 = jnp.pad(g5, ((0, 0), (0, 0), (0, 0), (0, T_PAD - t), (0, 0))).reshape(b, Q_ROWS, 3)
    blocks_per_step = pps * PAGE_SIZE // BLOCK
    key_blk = jnp.arange(pps * PAGE_SIZE) // BLOCK
    expand = (jnp.arange(nb)[None, :, None] == (jnp.arange(n_groups)[:, None, None] * blocks_per_step + key_blk[None, None, :]))
    expand = expand.astype(BF16)
    expand_t = expand.transpose(0, 2, 1)
    per_b = lambda shape: pl.BlockSpec((1,) + shape, lambda i, p, g, pt: (i, 0, 0))
    const = lambda a: pl.BlockSpec(a.shape, lambda i, p, g, pt: (0,) * a.ndim)
    hbm = pl.BlockSpec(memory_space=pl.ANY)
    body = functools.partial(_nsa_sample_body, n_groups=n_groups, t_len=t, win_len=wb, page_base=layer * n_pool)
    o = pl.pallas_call(
        body,
        grid_spec=pltpu.PrefetchScalarGridSpec(
            num_scalar_prefetch=1,
            grid=(b, 2, n_groups),
            in_specs=[hbm, hbm, hbm, hbm, per_b((Q_ROWS, NSA_KV_WIDTH)), per_b((NSA_KV_WIDTH, NEW_PAD)), per_b((NSA_KV_WIDTH, NEW_PAD)),
                      per_b((NSA_KV_WIDTH, wb)), per_b((NSA_KV_WIDTH, wb)), per_b((NSA_KV_WIDTH, NEW_PAD)), per_b((NSA_KV_WIDTH, NEW_PAD)),
                      const(wck2), const(wcv2), per_b((Q_ROWS, 3)), const(expand), const(expand_t)],
            out_specs=per_b((Q_ROWS, NSA_KV_WIDTH)),
            scratch_shapes=[
                pltpu.VMEM((2, 2, NSA_KV_WIDTH, pps * PAGE_SIZE), F32),
                pltpu.SemaphoreType.DMA((2, 2)),
                pltpu.VMEM((NSA_KV_WIDTH, nb), F32), pltpu.VMEM((NSA_KV_WIDTH, nb), F32),
                pltpu.VMEM((Q_ROWS, nb), F32), pltpu.VMEM((Q_ROWS, NSA_KV_WIDTH), F32),
                pltpu.VMEM((Q_ROWS, 1), F32), pltpu.VMEM((Q_ROWS, 1), F32), pltpu.VMEM((Q_ROWS, NSA_KV_WIDTH), F32),
            ],
        ),
        out_shape=jax.ShapeDtypeStruct((b, Q_ROWS, NSA_KV_WIDTH), F32),
        compiler_params=pltpu.CompilerParams(dimension_semantics=("arbitrary",) * 3, vmem_limit_bytes=VMEM_LIMIT),
        name="nsa_sample",
    )(page_table, *flat, qbd, padn(ks), padn(vs), _feature_major(win_k), _feature_major(win_v), padn(kw), padn(vw),
      wck2, wcv2, g5, expand, expand_t)
    o6 = o.reshape(b, NSA_KV_HEADS, NSA_GROUP, T_PAD, NSA_KV_HEADS, NSA_HEAD_DIM)[:, :, :, :t]
    o5 = jnp.stack([o6[:, h, :, :, h, :] for h in range(NSA_KV_HEADS)], axis=1)
    return o5.transpose(0, 3, 1, 2, 4).reshape(b, t, NSA_WIDTH)


def kernel(x_prompt, x_sample, cache_cmp_k, cache_cmp_v, cache_sel_k, cache_sel_v, state_win_k, state_win_v, state_conv,
           state_ssm, page_table, norm_mix_pre, norm_mix_post, norm_ffn_pre, norm_ffn_post, w_in, w_cmp_k, w_cmp_v,
           nsa_out_norm, conv_w, a_log, dt_bias, gdn_norm, w_out, w_up, w_down):
    depth = w_in.shape[0]
    bp, sp, _ = x_prompt.shape
    bs, ts, _ = x_sample.shape
    past = page_table.shape[1] * PAGE_SIZE
    pos_p = jnp.arange(sp)
    pos_s = past + jnp.arange(ts)
    pools = (cache_cmp_k, cache_cmp_v, cache_sel_k, cache_sel_v)
    kv5 = lambda x, b, n: x.reshape(b, n, NSA_KV_HEADS, NSA_HEAD_DIM)
    y_p, y_s = x_prompt, x_sample
    p_states, s_states = [], []
    for l in range(depth):
        w_perm = _permute_w_in(w_in[l]).astype(BF16)
        wck2 = jnp.tile(w_cmp_k[l], (1, LANES // NSA_HEAD_DIM))
        wcv2 = jnp.tile(w_cmp_v[l], (1, LANES // NSA_HEAD_DIM))
        post_w = (nsa_out_norm[l], w_out[l].astype(BF16), norm_mix_post[l], norm_ffn_pre[l], w_up[l].astype(BF16),
                  w_down[l].astype(BF16), norm_ffn_post[l])
        gdn_w = (conv_w[l], a_log[l], dt_bias[l], gdn_norm[l])

        wb = min(WINDOW, sp)
        conv0 = jnp.zeros((CONV_WIDTH - 1, 3 * GDN_WIDTH), F32)
        ssm0 = jnp.zeros((GDN_HEADS, GDN_HEAD_DIM, GDN_HEAD_DIM), F32)
        ys, sts = [], []
        for b in range(bp):
            x2 = y_p[b]
            q, kc, vc, ks, vs, kw, vw, qkv, z, small, ck, cv = _in_proj(x2, pos_p, norm_mix_pre[l], w_perm, wck2, wcv2)
            o_nsa = _nsa_prompt(q, ks, vs, kw, vw, ck, cv, small[:, GATE_OFF:GATE_OFF + 3 * NSA_HEADS])
            o_gdn, ssm = _gdn_prompt(qkv, z, small, conv0, ssm0, *gdn_w)
            ys.append(_post(o_nsa, o_gdn, x2, *post_w))
            conv_state = jnp.concatenate([conv0, qkv], axis=0)[-(CONV_WIDTH - 1):]
            sts.append((kv5(kc, 1, sp)[0], kv5(vc, 1, sp)[0], kv5(ks, 1, sp)[0], kv5(vs, 1, sp)[0],
                        kv5(kw, 1, sp)[0, sp - wb:], kv5(vw, 1, sp)[0, sp - wb:], conv_state, ssm))
        y_p = jnp.stack(ys)
        p_states.append(tuple(jnp.stack(t) for t in zip(*sts)))

        n = bs * ts
        q, kc, vc, ks, vs, kw, vw, qkv, z, small, _, _ = _in_proj(
            jnp.pad(y_s.reshape(n, D_MODEL), ((0, (-n) % BLOCK), (0, 0))), jnp.pad(jnp.tile(pos_s, bs), (0, (-n) % BLOCK)),
            norm_mix_pre[l], w_perm, wck2, wcv2)
        r3 = lambda x: x[:n].reshape(bs, ts, -1)
        q, kc, vc, ks, vs, kw, vw, qkv, z, small = map(r3, (q, kc, vc, ks, vs, kw, vw, qkv, z, small))
        win_k, win_v = state_win_k[l], state_win_v[l]
        o_nsa = _nsa_sample(q, ks, vs, kw, vw, small[..., GATE_OFF:GATE_OFF + 3 * NSA_HEADS], pools, l, win_k, win_v,
                            page_table, w_cmp_k[l], w_cmp_v[l])
        o_gdn, ssm = _gdn_sample(qkv, z, small, state_conv[l], state_ssm[l], *gdn_w)
        y_s = _post(o_nsa.reshape(n, NSA_WIDTH), o_gdn.reshape(n, GDN_WIDTH), y_s.reshape(n, D_MODEL), *post_w).reshape(bs, ts, D_MODEL)
        conv_state = jnp.concatenate([state_conv[l], qkv], axis=1)[:, -(CONV_WIDTH - 1):]
        new_wk = jnp.concatenate([win_k, kv5(kw, bs, ts)], axis=1)[:, ts:]
        new_wv = jnp.concatenate([win_v, kv5(vw, bs, ts)], axis=1)[:, ts:]
        s_states.append((kv5(kc, bs, ts), kv5(vc, bs, ts), kv5(ks, bs, ts), kv5(vs, bs, ts), new_wk, new_wv, conv_state, ssm))
    p_out = [jnp.stack(t) for t in zip(*p_states)]
    s_out = [jnp.stack(t) for t in zip(*s_states)]
    return (y_p, y_s, *p_out, *s_out)
```

```python
import functools
import math

import jax
import jax.numpy as jnp
import numpy as np
from jax import lax
from jax.experimental import pallas as pl
from jax.experimental.pallas import tpu as pltpu

D_MODEL = 1024
PAGE_SIZE = 128
NSA_HEAD_DIM = 64
NSA_HEADS = 8
NSA_KV_HEADS = 2
NSA_GROUP = NSA_HEADS // NSA_KV_HEADS
NSA_WIDTH = NSA_HEADS * NSA_HEAD_DIM
NSA_KV_WIDTH = NSA_KV_HEADS * NSA_HEAD_DIM
BLOCK = 64
TOP_K = 16
WINDOW = 512
Q_BLOCK = 128
FORCED_SCORE = NSA_GROUP + 1.0
ROPE_THETA = 500000.0
ROPE_DIM = NSA_HEAD_DIM // 4
GDN_HEAD_DIM = 128
GDN_HEADS = 4
GDN_WIDTH = GDN_HEADS * GDN_HEAD_DIM
CONV_WIDTH = 4
GDN_CHUNK = 64
D_FF = 4 * D_MODEL
EPS = 1e-6
SPLIT_SIZES = (NSA_WIDTH,) + (NSA_KV_WIDTH,) * 6 + (3 * NSA_HEADS, 3 * GDN_WIDTH, GDN_WIDTH, GDN_HEADS, GDN_HEADS)

LANES = 128
SMALL_COLS = LANES
GATE_OFF, BETA_OFF, ALPHA_OFF = 0, 3 * NSA_HEADS, 3 * NSA_HEADS + GDN_HEADS
PROJ_COLS = NSA_WIDTH + 6 * NSA_KV_WIDTH + 3 * GDN_WIDTH + GDN_WIDTH + SMALL_COLS
VMEM_LIMIT = 56 * 1024 * 1024
NEG = -1e30
F32 = jnp.float32
BF16 = jnp.bfloat16
HI = lax.Precision.HIGHEST


def _rms(x, w):
    return x * lax.rsqrt(jnp.mean(x * x, axis=-1, keepdims=True) + EPS) * w


def _sigmoid(x):
    return 1.0 / (1.0 + jnp.exp(-x))


def _silu(x):
    return x * _sigmoid(x)


def _rope_slab(x, c, sa, sb):
    half = ROPE_DIM // 2
    return x * c + pltpu.roll(x, LANES - half, axis=1) * sa + pltpu.roll(x, half, axis=1) * sb


def _in_proj_body(x_ref, nw_ref, w_ref, rc_ref, rsa_ref, rsb_ref, wk_ref, wv_ref,
                  q_ref, kc_ref, vc_ref, ks_ref, vs_ref, kw_ref, vw_ref, qkv_ref, z_ref, sm_ref, ck_ref, cv_ref):
    h = _rms(x_ref[...], nw_ref[...])
    p = jnp.dot(h.astype(BF16), w_ref[...], preferred_element_type=F32)
    c, sa, sb = rc_ref[...], rsa_ref[...], rsb_ref[...]
    rows = p.shape[0]
    off = 0
    for j in range(NSA_WIDTH // LANES):
        q_ref[:, j * LANES:(j + 1) * LANES] = _rope_slab(p[:, off:off + LANES], c, sa, sb)
        off += LANES
    kc = _rope_slab(p[:, off:off + LANES], c, sa, sb)
    kc_ref[...] = kc
    vc = p[:, off + LANES:off + 2 * LANES]
    vc_ref[...] = vc
    ks_ref[...] = _rope_slab(p[:, off + 2 * LANES:off + 3 * LANES], c, sa, sb)
    vs_ref[...] = p[:, off + 3 * LANES:off + 4 * LANES]
    kw_ref[...] = _rope_slab(p[:, off + 4 * LANES:off + 5 * LANES], c, sa, sb)
    vw_ref[...] = p[:, off + 5 * LANES:off + 6 * LANES]
    off += 6 * LANES
    qkv_ref[...] = p[:, off:off + 3 * GDN_WIDTH]
    off += 3 * GDN_WIDTH
    z_ref[...] = p[:, off:off + GDN_WIDTH]
    off += GDN_WIDTH
    sm_ref[...] = p[:, off:off + SMALL_COLS]
    ck_ref[...] = jnp.sum(kc.reshape(rows // BLOCK, BLOCK, LANES) * wk_ref[...][None], axis=1)
    cv_ref[...] = jnp.sum(vc.reshape(rows // BLOCK, BLOCK, LANES) * wv_ref[...][None], axis=1)


def _permute_w_in(w_in):
    pts = np.cumsum((0,) + SPLIT_SIZES)
    seg = [w_in[:, int(pts[i]):int(pts[i + 1])] for i in range(len(SPLIT_SIZES))]
    q, kc, vc, ks, vs, kw, vw, gates, qkv, z, beta, alpha = seg
    small = jnp.concatenate([gates, beta, alpha], axis=1)
    small = jnp.pad(small, ((0, 0), (0, SMALL_COLS - small.shape[1])))
    return jnp.concatenate([q, kc, vc, ks, vs, kw, vw, qkv, z, small], axis=1)


def _rope_tables(pos):
    half = ROPE_DIM // 2
    inv_freq = ROPE_THETA ** (-jnp.arange(half, dtype=F32) / half)
    ang = pos.astype(F32)[:, None] * inv_freq[None, :]
    cos, sin = jnp.cos(ang), jnp.sin(ang)
    n = pos.shape[0]
    one = jnp.ones((n, NSA_HEAD_DIM - ROPE_DIM), F32)
    zero = jnp.zeros((n, NSA_HEAD_DIM - ROPE_DIM), F32)
    zh = jnp.zeros((n, half), F32)
    c = jnp.concatenate([cos, cos, one], axis=1)
    sa = jnp.concatenate([-sin, zh, zero], axis=1)
    sb = jnp.concatenate([zh, sin, zero], axis=1)
    return tuple(jnp.tile(t, (1, LANES // NSA_HEAD_DIM)) for t in (c, sa, sb))


def _in_proj(x2d, pos, norm_w, w_perm, wk2, wv2):
    n = x2d.shape[0]
    tm = min(512, n)
    assert n % tm == 0 and tm % BLOCK == 0
    rc, rsa, rsb = _rope_tables(pos)
    row = lambda w: pl.BlockSpec((tm, w), lambda i: (i, 0))
    const = lambda a: pl.BlockSpec(a.shape, lambda i: (0,) * a.ndim)
    nw = norm_w.reshape(1, D_MODEL)
    widths = (NSA_WIDTH,) + (LANES,) * 6 + (3 * GDN_WIDTH, GDN_WIDTH, SMALL_COLS)
    out_shape = [jax.ShapeDtypeStruct((n, w), F32) for w in widths]
    out_shape += [jax.ShapeDtypeStruct((n // BLOCK, LANES), F32)] * 2
    out_specs = [row(w) for w in widths] + [pl.BlockSpec((tm // BLOCK, LANES), lambda i: (i, 0))] * 2
    return pl.pallas_call(
        _in_proj_body,
        grid=(n // tm,),
        in_specs=[row(D_MODEL), const(nw), const(w_perm), row(LANES), row(LANES), row(LANES), const(wk2), const(wv2)],
        out_specs=out_specs,
        out_shape=out_shape,
        compiler_params=pltpu.CompilerParams(dimension_semantics=("arbitrary",), vmem_limit_bytes=VMEM_LIMIT),
        name="in_proj",
    )(x2d, nw, w_perm, rc, rsa, rsb, wk2, wv2)


FF_CHUNK = 1024


def _post_body(on_ref, og_ref, x_ref, nn_ref, wo_ref, nmp_ref, nfp_ref, wu_ref, wd_ref, nfo_ref, y_ref):
    o_nsa = _rms(on_ref[...], nn_ref[...])
    mix = jnp.concatenate([o_nsa, og_ref[...]], axis=1).astype(BF16)
    y = jnp.dot(mix, wo_ref[...], preferred_element_type=F32)
    y1 = x_ref[...] + _rms(y, nmp_ref[...])
    h = _rms(y1, nfp_ref[...]).astype(BF16)
    acc = jnp.zeros_like(y1)
    for c in range(D_FF // FF_CHUNK):
        u = jnp.dot(h, wu_ref[:, c * FF_CHUNK:(c + 1) * FF_CHUNK], preferred_element_type=F32)
        u = jnp.square(jnp.maximum(u, 0.0))
        acc = acc + jnp.dot(u.astype(BF16), wd_ref[c * FF_CHUNK:(c + 1) * FF_CHUNK, :], preferred_element_type=F32)
    y_ref[...] = y1 + _rms(acc, nfo_ref[...])


def _post(o_nsa, o_gdn, x2d, nsa_norm, w_out, n_mix_post, n_ffn_pre, w_up, w_down, n_ffn_post):
    n = x2d.shape[0]
    tm = min(256, n)
    assert n % tm == 0
    row = lambda w: pl.BlockSpec((tm, w), lambda i: (i, 0))
    const = lambda a: pl.BlockSpec(a.shape, lambda i: (0,) * a.ndim)
    r = lambda v: v.reshape(1, -1)
    args = (o_nsa, o_gdn, x2d, r(nsa_norm), w_out, r(n_mix_post), r(n_ffn_pre), w_up, w_down, r(n_ffn_post))
    return pl.pallas_call(
        _post_body,
        grid=(n // tm,),
        in_specs=[row(NSA_WIDTH), row(GDN_WIDTH), row(D_MODEL)] + [const(a) for a in args[3:]],
        out_specs=row(D_MODEL),
        out_shape=jax.ShapeDtypeStruct((n, D_MODEL), F32),
        compiler_params=pltpu.CompilerParams(dimension_semantics=("arbitrary",), vmem_limit_bytes=VMEM_LIMIT),
        name="out_proj_ffn",
    )(*args)


SEL_CHUNK = 512
SEL_AUG = 16
V_ROWS = NSA_HEAD_DIM + 16


def _row_iota(shape):
    return lax.broadcasted_iota(jnp.int32, shape, 0)


def _topk_rows(score, blk, k):
    nb = score.shape[0]
    sel = jnp.zeros(score.shape, F32)
    for _ in range(k):
        m = jnp.max(score, axis=0, keepdims=True)
        idx = jnp.min(jnp.where(score == m, blk, nb), axis=0, keepdims=True)
        pick = blk == idx
        sel = jnp.where(pick, 1.0, sel)
        score = jnp.where(pick, -jnp.inf, score)
    return sel


REDUCE_WAYS = 8


def _col_reduce(x, op):
    rows = x.shape[0]
    if rows % (REDUCE_WAYS * 8) == 0 and rows > REDUCE_WAYS * 8:
        x = op(x.reshape(rows // (REDUCE_WAYS * 8), REDUCE_WAYS * 8, x.shape[1]), axis=0)
    return op(x, axis=0, keepdims=True)


def _softmax_rows(s, mask):
    m = jnp.max(jnp.where(mask, s, NEG), axis=0, keepdims=True)
    e = jnp.where(mask, jnp.exp2(s - m), 0.0)
    return e, 1.0 / jnp.maximum(jnp.sum(e, axis=0, keepdims=True), 1e-30)


def _nsa_prompt_body(qt_ref, ks_ref, kw_ref, vt_ref, ck_ref, cvt_ref, g_ref, o_ref, sb_ref, s0_ref, s1_ref, e0_ref, e1_ref, *, seq, kc, wn):
    i = pl.program_id(1)
    start = i * Q_BLOCK
    nb = seq // BLOCK
    g4 = range(NSA_GROUP)
    slab = lambda a, g: a[:, g * Q_BLOCK:(g + 1) * Q_BLOCK]
    qt = qt_ref[0, 0]
    pos = start + lax.broadcasted_iota(jnp.int32, (1, Q_BLOCK), 1)

    sc = jnp.dot(ck_ref[0], qt, preferred_element_type=F32)
    blk = _row_iota((nb, Q_BLOCK))
    cmask = (blk + 1) * BLOCK - 1 <= pos
    ps = []
    for g in g4:
        e, inv = _softmax_rows(slab(sc, g), cmask)
        ps.append(e * inv)
    o_c = jnp.dot(cvt_ref[0], jnp.concatenate(ps, axis=1).astype(BF16), preferred_element_type=F32)
    imp = ps[0]
    for g in range(1, NSA_GROUP):
        imp = imp + ps[g]

    cur = pos // BLOCK
    forced = (blk == 0) | (blk == cur) | (blk == cur - 1)
    score = jnp.where(forced | (blk > cur), -1.0, imp)
    sel = _topk_rows(score, blk, max(min(TOP_K, nb) - 3, 0))
    sel_bias = jnp.where(forced | ((score >= 0) & (sel > 0.0)), 0.0, NEG)
    for g in g4:
        sb_ref[:, g * Q_BLOCK:(g + 1) * Q_BLOCK] = sel_bias

    bpc = kc // BLOCK
    pad_rows = jnp.zeros((max(SEL_AUG - bpc, 1), NSA_GROUP * Q_BLOCK), F32)
    pad_tail = jnp.zeros((LANES - NSA_HEAD_DIM - SEL_AUG, NSA_GROUP * Q_BLOCK), BF16)

    n_parts = NSA_GROUP
    part = kc // n_parts

    def q_aug(c):
        rows = sb_ref[pl.ds(pl.multiple_of(c * bpc, bpc), bpc), :]
        if bpc < SEL_AUG:
            rows = jnp.concatenate([rows, pad_rows], axis=0)
        return jnp.concatenate([qt, rows.astype(BF16), pad_tail], axis=0)

    def scores_part(c, qa, j):
        k0 = pl.multiple_of(c * kc + j * part, part)
        return jnp.dot(ks_ref[0, pl.ds(k0, part), :], qa, preferred_element_type=F32)

    def values_dot(c, e_ref):
        k0 = pl.multiple_of(c * kc, kc)
        return jnp.dot(vt_ref[0, 0:V_ROWS, pl.ds(k0, kc)], e_ref[...], preferred_element_type=F32)

    def sweep_step(c, carry, cur, nxt, last):
        m, acc, alpha_prev = carry
        pv = values_dot(jnp.maximum(c - 1, 0), nxt[1])
        qa = None if last else q_aug(c + 1)
        k0 = pl.multiple_of(c * kc, kc)
        ms, alphas = [], []
        for g in g4:
            if not last:
                nxt[0][g * part:(g + 1) * part, :] = scores_part(c + 1, qa, g)
            sg = cur[0][:, g * Q_BLOCK:(g + 1) * Q_BLOCK]
            if last:
                sg = sg + jnp.where(k0 + _row_iota((kc, Q_BLOCK)) <= pos, 0.0, NEG)
            mo = slab(m, g)
            mn = jnp.maximum(mo, _col_reduce(sg, jnp.max))
            cur[1][:, g * Q_BLOCK:(g + 1) * Q_BLOCK] = jnp.exp2((sg - mn).astype(BF16))
            ms.append(mn)
            alphas.append(jnp.exp2(mo - mn))
        cat = lambda xs: jnp.concatenate(xs, axis=1)
        return cat(ms), alpha_prev * acc + pv, cat(alphas)

    def finish(c, carry, cur, nxt):
        _, acc, alpha = sweep_step(c, carry, cur, nxt, True)
        return alpha * acc + values_dot(c, cur[1])

    width = NSA_GROUP * Q_BLOCK
    even, odd = (s0_ref, e0_ref), (s1_ref, e1_ref)
    qa0 = q_aug(0)
    for g in g4:
        s0_ref[g * part:(g + 1) * part, :] = scores_part(0, qa0, g)
    e1_ref[...] = jnp.zeros(e1_ref.shape, BF16)
    init = (jnp.full((1, width), NEG, F32), jnp.zeros((V_ROWS, width), F32), jnp.ones((1, width), F32))
    c_diag = start // kc

    def pair(p, carry):
        carry = sweep_step(2 * p, carry, even, odd, False)
        return sweep_step(2 * p + 1, carry, odd, even, False)

    carry = lax.fori_loop(0, c_diag // 4, lambda p, cr: pair(2 * p + 1, pair(2 * p, cr)), init)
    carry = lax.fori_loop(c_diag // 4 * 2, c_diag // 2, pair, carry)
    acc_s = lax.cond(
        c_diag % 2 == 1,
        lambda cr: finish(c_diag, sweep_step(c_diag - 1, cr, even, odd, False), odd, even),
        lambda cr: finish(c_diag, cr, even, odd),
        carry)
    o_s = acc_s[0:NSA_HEAD_DIM] * (1.0 / jnp.maximum(acc_s[NSA_HEAD_DIM:NSA_HEAD_DIM + 1], 1e-30))

    w0 = pl.multiple_of(jnp.maximum(start + Q_BLOCK - wn, 0), Q_BLOCK)
    sw = jnp.dot(kw_ref[0, pl.ds(w0, wn), :], qt, preferred_element_type=F32)
    dist = pos - (w0 + _row_iota((wn, Q_BLOCK)))
    wmask = (dist >= 0) & (dist < WINDOW)
    ews, invs = [], []
    for g in g4:
        e, inv = _softmax_rows(slab(sw, g), wmask)
        ews.append(e)
        invs.append(inv)
    o_w = jnp.dot(vt_ref[0, V_ROWS:V_ROWS + NSA_HEAD_DIM, pl.ds(w0, wn)], jnp.concatenate(ews, axis=1).astype(BF16),
                  preferred_element_type=F32) * jnp.concatenate(invs, axis=1)

    gate = _sigmoid(g_ref[0])
    for g in g4:
        o_ref[0, g] = (gate[3 * g:3 * g + 1] * slab(o_c, g) + gate[3 * g + 1:3 * g + 2] * slab(o_s, g)
                       + gate[3 * g + 2:3 * g + 3] * slab(o_w, g))


def _split_heads_t(x, n):
    return x.reshape(n, NSA_KV_HEADS, NSA_HEAD_DIM).transpose(1, 0, 2)


def _nsa_prompt(q, ks, vs, kw, vw, ck, cv, small):
    s = q.shape[0]
    assert s % Q_BLOCK == 0
    n_qb, nb = s // Q_BLOCK, s // BLOCK
    kc = min(SEL_CHUNK, s)
    wn = min(WINDOW + Q_BLOCK, s)
    assert s % kc == 0
    scale = NSA_HEAD_DIM ** -0.5 * math.log2(math.e)
    qt = (q * scale).astype(BF16).reshape(n_qb, Q_BLOCK, NSA_KV_HEADS, NSA_GROUP, NSA_HEAD_DIM)
    qt = qt.transpose(2, 0, 4, 3, 1).reshape(NSA_KV_HEADS, n_qb, NSA_HEAD_DIM, NSA_GROUP * Q_BLOCK)
    bpc = kc // BLOCK
    assert kc % Q_BLOCK == 0 and bpc <= SEL_AUG
    onehot = ((jnp.arange(s) // BLOCK) % bpc)[:, None] == jnp.arange(LANES - NSA_HEAD_DIM)[None, :]
    ksa = jnp.concatenate([_split_heads_t(ks, s).astype(BF16),
                           jnp.broadcast_to(onehot.astype(BF16), (NSA_KV_HEADS, s, LANES - NSA_HEAD_DIM))], axis=-1)
    kwh = _split_heads_t(kw, s).astype(BF16)
    ones_fill = jnp.zeros((NSA_KV_HEADS, s, V_ROWS - NSA_HEAD_DIM), BF16).at[:, :, 0].set(1.0)
    vt = jnp.concatenate([_split_heads_t(vs, s).astype(BF16), ones_fill, _split_heads_t(vw, s).astype(BF16)],
                         axis=-1).transpose(0, 2, 1)
    ckh = _split_heads_t(ck, nb).astype(BF16)
    cvt = _split_heads_t(cv, nb).astype(BF16).transpose(0, 2, 1)
    gt = small.T[GATE_OFF:GATE_OFF + 3 * NSA_HEADS].reshape(NSA_KV_HEADS, NSA_GROUP * 3, s)
    gt = jnp.pad(gt, ((0, 0), (0, 16 - NSA_GROUP * 3), (0, 0)))
    body = functools.partial(_nsa_prompt_body, seq=s, kc=kc, wn=wn)
    o = pl.pallas_call(
        body,
        grid=(NSA_KV_HEADS, n_qb),
        in_specs=[
            pl.BlockSpec((1, 1, NSA_HEAD_DIM, NSA_GROUP * Q_BLOCK), lambda h, i: (h, i, 0, 0)),
            pl.BlockSpec((1, s, LANES), lambda h, i: (h, 0, 0)),
            pl.BlockSpec((1, s, NSA_HEAD_DIM), lambda h, i: (h, 0, 0)),
            pl.BlockSpec((1, V_ROWS + NSA_HEAD_DIM, s), lambda h, i: (h, 0, 0)),
            pl.BlockSpec((1, nb, NSA_HEAD_DIM), lambda h, i: (h, 0, 0)),
            pl.BlockSpec((1, NSA_HEAD_DIM, nb), lambda h, i: (h, 0, 0)),
            pl.BlockSpec((1, 16, Q_BLOCK), lambda h, i: (h, 0, i)),
        ],
        out_specs=pl.BlockSpec((1, NSA_GROUP, NSA_HEAD_DIM, Q_BLOCK), lambda h, i: (h, 0, 0, i)),
        out_shape=jax.ShapeDtypeStruct((NSA_KV_HEADS, NSA_GROUP, NSA_HEAD_DIM, s), F32),
        scratch_shapes=[pltpu.VMEM((nb, NSA_GROUP * Q_BLOCK), F32)]
        + [pltpu.VMEM((kc, NSA_GROUP * Q_BLOCK), F32)] * 2 + [pltpu.VMEM((kc, NSA_GROUP * Q_BLOCK), BF16)] * 2,
        compiler_params=pltpu.CompilerParams(dimension_semantics=("arbitrary", "arbitrary"), vmem_limit_bytes=VMEM_LIMIT),
        name="nsa_prompt",
    )(qt, ksa, kwh, vt, ckh, cvt, gt)
    return o.transpose(3, 0, 1, 2).reshape(s, NSA_WIDTH)


def _dot(a, b):
    return jnp.dot(a, b, precision=HI, preferred_element_type=F32)


def _dot_nt(a, b):
    return lax.dot_general(a, b, (((1,), (1,)), ((), ())), precision=HI, preferred_element_type=F32)


def _dot_tn(a, b):
    return lax.dot_general(a, b, (((0,), (0,)), ((), ())), precision=HI, preferred_element_type=F32)


def _dot1_nt(a, b):
    return lax.dot_general(a, b, (((1,), (1,)), ((), ())), preferred_element_type=F32)


def _split16(a):
    hi = a.astype(BF16)
    return hi, (a - hi.astype(F32)).astype(BF16)


def _dot3(a, b):
    ah, al = _split16(a)
    bh, bl = _split16(b)
    d = lambda x, y: jnp.dot(x, y, preferred_element_type=F32)
    return d(ah, bh) + (d(ah, bl) + d(al, bh))


def _softplus(x):
    return jnp.maximum(x, 0.0) + jnp.log(1.0 + jnp.exp(-jnp.abs(x)))


def _unit_lower_inverses(mats, ri, ci):
    c = mats[0].shape[0]
    eye = (ri == ci).astype(F32)
    base = 8
    same = ri // base == ci // base
    ds = [jnp.where(same, a, 0.0) for a in mats]
    d2s = [_dot3(d, d) for d in ds]
    xs = [(eye - d) + _dot3(eye - d, d2) for d, d2 in zip(ds, d2s)]
    d4s = [_dot3(d2, d2) for d2 in d2s]
    xs = [x + _dot3(x, d4) for x, d4 in zip(xs, d4s)]
    b = base
    while b < c:
        band = (ri // (2 * b) == ci // (2 * b)) & (ri // b != ci // b)
        lxs = [_dot3(jnp.where(band, a, 0.0), x) for a, x in zip(mats, xs)]
        xs = [x - _dot3(x, lx) for x, lx in zip(xs, lxs)]
        b *= 2
    return xs


def _conv_silu(x, prev, w):
    rowi = _row_iota(x.shape)
    conv = None
    for j in range(CONV_WIDTH - 1, -1, -1):
        if j == 0:
            xs = x
        else:
            xs = jnp.where(rowi < j, pltpu.roll(prev, j, axis=0), pltpu.roll(x, j, axis=0))
        term = xs * w[CONV_WIDTH - 1 - j:CONV_WIDTH - j]
        conv = term if conv is None else conv + term
    return _silu(conv)


def _l2n(x):
    return x * lax.rsqrt(jnp.sum(x * x, axis=-1, keepdims=True) + EPS)


GDN_STEP_CHUNKS = 4


def _gdn_chunk_body(qkv_ref, sm_ref, smt_ref, prev0_ref, cw_ref, ar_ref, dr_ref, at_ref, dt_ref,
                    u_ref, w_ref, qe_ref, kd_ref, in_ref, dl_ref, xp_ref):
    cs = GDN_CHUNK

    @pl.when(pl.program_id(0) == 0)
    def _():
        xp_ref[...] = prev0_ref[...]

    x = qkv_ref[...]
    act = _conv_silu(x, xp_ref[...], cw_ref[...])
    xp_ref[...] = x
    sm = sm_ref[...]
    ri = _row_iota((cs, cs))
    ci = lax.broadcasted_iota(jnp.int32, (cs, cs), 1)
    tri = (ri >= ci).astype(F32)
    tri_t = (ri <= ci).astype(F32)
    g_all = -jnp.exp(ar_ref[...]) * _softplus(sm + dr_ref[...])
    beta_all = _sigmoid(sm)
    scale = GDN_HEAD_DIM ** -0.5
    a_mats, rhs, where = [], [], []
    for c in range(GDN_STEP_CHUNKS):
        rows = slice(c * cs, (c + 1) * cs)
        gc_cols = _dot(tri, g_all[rows])
        g_rows = -jnp.exp(at_ref[...]) * _softplus(smt_ref[c] + dt_ref[...])
        gc_rows = _dot(g_rows, tri_t)
        intras, dls = [], []
        for h in range(GDN_HEADS):
            hs = slice(h * GDN_HEAD_DIM, (h + 1) * GDN_HEAD_DIM)
            q = _l2n(act[rows, hs]) * scale
            k = _l2n(act[rows, GDN_WIDTH + h * GDN_HEAD_DIM:GDN_WIDTH + (h + 1) * GDN_HEAD_DIM])
            v = act[rows, 2 * GDN_WIDTH + h * GDN_HEAD_DIM:2 * GDN_WIDTH + (h + 1) * GDN_HEAD_DIM]
            beta = beta_all[rows, BETA_OFF + h:BETA_OFF + h + 1]
            gc = gc_cols[:, ALPHA_OFF + h:ALPHA_OFF + h + 1]
            gr = gc_rows[GDN_HEADS + h:GDN_HEADS + h + 1, :]
            g_last = gc[cs - 1:cs, :]
            decay = jnp.where(ri >= ci, jnp.exp(gc - gr), 0.0)
            kb = k * beta
            k16 = k.astype(BF16)
            a_mats.append(jnp.where(ri > ci, _dot1_nt(kb.astype(BF16), k16) * decay, 0.0))
            rhs.append(jnp.concatenate([v * beta, kb * jnp.exp(gc)], axis=1))
            where.append((rows, hs))
            qe_ref[rows, hs] = (q * jnp.exp(gc)).astype(BF16)
            kd_ref[rows, hs] = (k * jnp.exp(g_last - gc)).astype(BF16)
            intras.append((_dot1_nt(q.astype(BF16), k16) * decay).astype(BF16))
            dls.append(jnp.broadcast_to(jnp.exp(g_last), (1, LANES)))
        in_ref[rows, :] = jnp.concatenate(intras, axis=1)
        dl_ref[c] = jnp.concatenate(dls + [jnp.zeros((2 * GDN_HEADS - GDN_HEADS, LANES), F32)], axis=0)
    for t, r, (rows, hs) in zip(_unit_lower_inverses(a_mats, ri, ci), rhs, where):
        sol = _dot3(t, r)
        u_ref[rows, hs] = sol[:, :GDN_HEAD_DIM]
        w_ref[rows, hs] = sol[:, GDN_HEAD_DIM:].astype(BF16)


def _gdn_scan_body(u_ref, w_ref, qe_ref, kd_ref, in_ref, dl_ref, z_ref, s0_ref, gn_ref, o_ref, sf_ref, s_ref):
    cs = GDN_CHUNK
    i = pl.program_id(0)

    @pl.when(i == 0)
    def _():
        s_ref[...] = s0_ref[...]

    heads = range(GDN_HEADS)
    hsl = [slice(h * GDN_HEAD_DIM, (h + 1) * GDN_HEAD_DIM) for h in heads]
    tn = (((0,), (0,)), ((), ()))
    ss = [s_ref[h] for h in heads]
    for c in range(GDN_STEP_CHUNKS):
        rows = slice(c * cs, (c + 1) * cs)
        s16 = [s.astype(BF16) for s in ss]
        v16 = [(u_ref[rows, hsl[h]] - jnp.dot(w_ref[rows, hsl[h]], s16[h], preferred_element_type=F32)).astype(BF16)
               for h in heads]
        ss = [ss[h] * dl_ref[c, h:h + 1, :] + lax.dot_general(kd_ref[rows, hsl[h]], v16[h], tn, preferred_element_type=F32)
              for h in heads]
        for h in heads:
            o = (jnp.dot(qe_ref[rows, hsl[h]], s16[h], preferred_element_type=F32)
                 + jnp.dot(in_ref[rows, h * cs:(h + 1) * cs], v16[h], preferred_element_type=F32))
            o_ref[rows, hsl[h]] = _rms(o, gn_ref[...]) * _silu(z_ref[rows, hsl[h]])
    for h in heads:
        s_ref[h] = ss[h]

    @pl.when(i == pl.num_programs(0) - 1)
    def _():
        sf_ref[...] = s_ref[...]


def _gdn_consts(a_log, dt_bias):
    lane = jnp.zeros((1, LANES), F32)
    ar = lane.at[0, ALPHA_OFF:ALPHA_OFF + GDN_HEADS].set(a_log)
    dr = lane.at[0, ALPHA_OFF:ALPHA_OFF + GDN_HEADS].set(dt_bias)
    col = jnp.zeros((2 * GDN_HEADS, 1), F32)
    at = jnp.broadcast_to(col.at[GDN_HEADS:, 0].set(a_log), (2 * GDN_HEADS, GDN_CHUNK))
    dt = jnp.broadcast_to(col.at[GDN_HEADS:, 0].set(dt_bias), (2 * GDN_HEADS, GDN_CHUNK))
    return ar, dr, at, dt


def _gdn_prompt(qkv, z, small, conv_prev, s0, conv_w, a_log, dt_bias, gdn_norm):
    n = qkv.shape[0]
    cs = GDN_CHUNK
    rows = GDN_STEP_CHUNKS * cs
    assert n % rows == 0
    nc = n // cs
    smt = small.T[BETA_OFF:BETA_OFF + 2 * GDN_HEADS].reshape(2 * GDN_HEADS, nc, cs).transpose(1, 0, 2)
    prev0 = jnp.pad(conv_prev, ((rows - (CONV_WIDTH - 1), 0), (0, 0)))
    ar, dr, at, dt = _gdn_consts(a_log, dt_bias)
    gn = gdn_norm.reshape(1, GDN_HEAD_DIM)
    row = lambda w: pl.BlockSpec((rows, w), lambda i: (i, 0))
    per_chunk = pl.BlockSpec((GDN_STEP_CHUNKS, 2 * GDN_HEADS, LANES), lambda i: (i, 0, 0))
    const = lambda a: pl.BlockSpec(a.shape, lambda i: (0,) * a.ndim)
    params = pltpu.CompilerParams(dimension_semantics=("arbitrary",), vmem_limit_bytes=VMEM_LIMIT)
    wide = lambda dt_: jax.ShapeDtypeStruct((n, GDN_WIDTH), dt_)
    u, w, qe, kd, intra, dl = pl.pallas_call(
        _gdn_chunk_body,
        grid=(n // rows,),
        in_specs=[row(3 * GDN_WIDTH), row(SMALL_COLS),
                  pl.BlockSpec((GDN_STEP_CHUNKS, 2 * GDN_HEADS, cs), lambda i: (i, 0, 0)),
                  const(prev0), const(conv_w), const(ar), const(dr), const(at), const(dt)],
        out_specs=[row(GDN_WIDTH)] * 4 + [row(GDN_HEADS * cs), per_chunk],
        out_shape=[wide(F32), wide(BF16), wide(BF16), wide(BF16), jax.ShapeDtypeStruct((n, GDN_HEADS * cs), BF16),
                   jax.ShapeDtypeStruct((nc, 2 * GDN_HEADS, LANES), F32)],
        scratch_shapes=[pltpu.VMEM((rows, 3 * GDN_WIDTH), F32)],
        compiler_params=params,
        name="gdn_chunks",
    )(qkv, small, smt, prev0, conv_w, ar, dr, at, dt)
    return pl.pallas_call(
        _gdn_scan_body,
        grid=(n // rows,),
        in_specs=[row(GDN_WIDTH)] * 4 + [row(GDN_HEADS * cs), per_chunk, row(GDN_WIDTH), const(s0), const(gn)],
        out_specs=[row(GDN_WIDTH), const(s0)],
        out_shape=[wide(F32), jax.ShapeDtypeStruct(s0.shape, F32)],
        scratch_shapes=[pltpu.VMEM(s0.shape, F32)],
        compiler_params=params,
        name="gdn_scan",
    )(u, w, qe, kd, intra, dl, z, s0, gn)


def _gdn_sample_pre_body(x0_ref, x1_ref, x2_ref, x3_ref, sm_ref, cw_ref, ar_ref, dr_ref, q_ref, k_ref, v_ref, a_ref, b_ref):
    w = cw_ref[...]
    conv = x0_ref[...] * w[0:1] + x1_ref[...] * w[1:2] + x2_ref[...] * w[2:3] + x3_ref[...] * w[3:4]
    act = _silu(conv)
    sm = sm_ref[...]
    a_cols = jnp.exp(-jnp.exp(ar_ref[...]) * _softplus(sm + dr_ref[...]))
    b_cols = _sigmoid(sm)
    rows = sm.shape[0]
    for h in range(GDN_HEADS):
        hs = slice(h * GDN_HEAD_DIM, (h + 1) * GDN_HEAD_DIM)
        q_ref[:, hs] = _l2n(act[:, hs]) * GDN_HEAD_DIM ** -0.5
        k_ref[:, hs] = _l2n(act[:, GDN_WIDTH + h * GDN_HEAD_DIM:GDN_WIDTH + (h + 1) * GDN_HEAD_DIM])
        a_ref[:, hs] = jnp.broadcast_to(a_cols[:, ALPHA_OFF + h:ALPHA_OFF + h + 1], (rows, GDN_HEAD_DIM))
        b_ref[:, hs] = jnp.broadcast_to(b_cols[:, BETA_OFF + h:BETA_OFF + h + 1], (rows, GDN_HEAD_DIM))
    v_ref[...] = act[:, 2 * GDN_WIDTH:]


GDN_SAMPLE_SEQS = 8


def _gdn_sample_body(qt_ref, kt_ref, v_ref, a_ref, b_ref, z_ref, s0_ref, gn_ref, o_ref, sf_ref, *, t_len):
    gn = gn_ref[...]

    def seq(bl, carry):
        v_all, a_all, b_all, z_all = v_ref[bl], a_ref[bl], b_ref[bl], z_ref[bl]
        for h in range(GDN_HEADS):
            hs = slice(h * GDN_HEAD_DIM, (h + 1) * GDN_HEAD_DIM)
            s = s0_ref[bl, h]
            qt, kt = qt_ref[bl, h], kt_ref[bl, h]
            outs = []
            for t in range(t_len):
                k_col, q_col = kt[:, t:t + 1], qt[:, t:t + 1]
                a, b = a_all[t:t + 1, hs], b_all[t:t + 1, hs]
                sk = jnp.sum(s * k_col, axis=0, keepdims=True)
                delta = b * (v_all[t:t + 1, hs] - a * sk)
                s = a * s + k_col * delta
                outs.append(jnp.sum(s * q_col, axis=0, keepdims=True))
            sf_ref[bl, h] = s
            o = jnp.concatenate(outs, axis=0)
            o_ref[bl, :, hs] = _rms(o, gn) * _silu(z_all[:, hs])
        return carry

    lax.fori_loop(0, GDN_SAMPLE_SEQS, seq, 0)


def _gdn_sample(qkv, z, small, conv_prev, s0, conv_w, a_log, dt_bias, gdn_norm):
    b, t = qkv.shape[:2]
    n = b * t
    xp = jnp.concatenate([conv_prev, qkv], axis=1)
    xs = [xp[:, j:j + t].reshape(n, 3 * GDN_WIDTH) for j in range(CONV_WIDTH)]
    ar, dr, _, _ = _gdn_consts(a_log, dt_bias)
    tm = min(256, n)
    assert n % tm == 0
    row = lambda w: pl.BlockSpec((tm, w), lambda i: (i, 0))
    const = lambda a: pl.BlockSpec(a.shape, lambda i: (0,) * a.ndim)
    q, k, v, av, bv = pl.pallas_call(
        _gdn_sample_pre_body,
        grid=(n // tm,),
        in_specs=[row(3 * GDN_WIDTH)] * 4 + [row(SMALL_COLS), const(conv_w), const(ar), const(dr)],
        out_specs=[row(GDN_WIDTH)] * 5,
        out_shape=[jax.ShapeDtypeStruct((n, GDN_WIDTH), F32)] * 5,
        compiler_params=pltpu.CompilerParams(dimension_semantics=("arbitrary",), vmem_limit_bytes=VMEM_LIMIT),
        name="gdn_sample_pre",
    )(*xs, small.reshape(n, SMALL_COLS), conv_w, ar, dr)
    tr = lambda x: x.reshape(b, t, GDN_HEADS, GDN_HEAD_DIM).transpose(0, 2, 3, 1)
    r3 = lambda x: x.reshape(b, t, GDN_WIDTH)
    bb = GDN_SAMPLE_SEQS
    assert b % bb == 0
    gn = gdn_norm.reshape(1, GDN_HEAD_DIM)
    tspec = pl.BlockSpec((bb, GDN_HEADS, GDN_HEAD_DIM, t), lambda i: (i, 0, 0, 0))
    rspec = pl.BlockSpec((bb, t, GDN_WIDTH), lambda i: (i, 0, 0))
    sspec = pl.BlockSpec((bb, GDN_HEADS, GDN_HEAD_DIM, GDN_HEAD_DIM), lambda i: (i, 0, 0, 0))
    o, sf = pl.pallas_call(
        functools.partial(_gdn_sample_body, t_len=t),
        grid=(b // bb,),
        in_specs=[tspec, tspec, rspec, rspec, rspec, rspec, sspec, const(gn)],
        out_specs=[rspec, sspec],
        out_shape=[jax.ShapeDtypeStruct((b, t, GDN_WIDTH), F32), jax.ShapeDtypeStruct(s0.shape, F32)],
        compiler_params=pltpu.CompilerParams(dimension_semantics=("arbitrary",), vmem_limit_bytes=VMEM_LIMIT),
        name="gdn_sample",
    )(tr(q), tr(k), r3(v), r3(av), r3(bv), z, s0, gn)
    return o, sf


PAGES_PER_STEP = 64
T_PAD = 8
Q_ROWS = NSA_KV_HEADS * NSA_GROUP * T_PAD
NEW_PAD = LANES


def _nsa_sample_body(pt_ref, ckp_ref, cvp_ref, skp_ref, svp_ref, q_ref, kn_ref, vn_ref, wk_ref, wv_ref, kwn_ref, vwn_ref,
                     wck_ref, wcv_ref, g_ref, ex_ref, ext_ref, o_ref,
                     buf_ref, sem_ref, ck_ref, cv_ref, selm_ref, oc_ref, m_ref, l_ref, acc_ref,
                     *, n_groups, t_len, win_len, page_base):
    b, ph, g = pl.program_id(0), pl.program_id(1), pl.program_id(2)
    nb_, np_, ng_ = pl.num_programs(0), pl.num_programs(1), pl.num_programs(2)
    step = (b * np_ + ph) * ng_ + g
    pps = PAGES_PER_STEP
    rows_per_step = pps * PAGE_SIZE
    blocks_per_step = rows_per_step // BLOCK

    def copies(pool_k, pool_v, bb, gg, slot):
        out = []
        for p in range(pps):
            page = page_base + pt_ref[bb, gg * pps + p]
            for kv, pool in enumerate((pool_k, pool_v)):
                out.append(pltpu.make_async_copy(pool.at[page], buf_ref.at[slot, kv, :, pl.ds(p * PAGE_SIZE, PAGE_SIZE)],
                                                 sem_ref.at[slot, kv]))
        return out

    def start(bb, pp, gg, slot):
        @pl.when(pp == 0)
        def _():
            for cp in copies(ckp_ref, cvp_ref, bb, gg, slot):
                cp.start()

        @pl.when(pp == 1)
        def _():
            for cp in copies(skp_ref, svp_ref, bb, gg, slot):
                cp.start()

    slot = step % 2

    @pl.when(step == 0)
    def _():
        start(b, ph, g, slot)

    nxt = step + 1

    @pl.when(nxt < nb_ * np_ * ng_)
    def _():
        start(nxt // (np_ * ng_), (nxt // ng_) % np_, nxt % ng_, 1 - slot)

    for cp in copies(ckp_ref, cvp_ref, b, g, slot):
        cp.wait()

    q = q_ref[0]
    rowt = _row_iota((Q_ROWS, LANES)) % T_PAD
    lane = lax.broadcasted_iota(jnp.int32, (Q_ROWS, LANES), 1)
    new_mask = (lane <= rowt) & (lane < t_len)

    nt = (((1,), (1,)), ((), ()))

    def weighted(kv, w_ref):
        w = w_ref[...]
        return jnp.concatenate([buf_ref[slot, kv, :, p * PAGE_SIZE:(p + 1) * PAGE_SIZE] * w for p in range(pps)],
                               axis=1).astype(BF16)

    @pl.when(ph == 0)
    def _():
        @pl.when(g == 0)
        def _():
            ck_ref[...] = jnp.zeros(ck_ref.shape, F32)
            cv_ref[...] = jnp.zeros(cv_ref.shape, F32)

        ck_ref[...] += jnp.dot(weighted(0, wck_ref), ext_ref[g], preferred_element_type=F32)
        cv_ref[...] += jnp.dot(weighted(1, wcv_ref), ext_ref[g], preferred_element_type=F32)

        @pl.when(g == n_groups - 1)
        def _():
            nb = ck_ref.shape[1]
            s = jnp.dot(q, ck_ref[...].astype(BF16), preferred_element_type=F32)
            m = jnp.max(s, axis=1, keepdims=True)
            e = jnp.exp(s - m)
            p = e / jnp.maximum(jnp.sum(e, axis=1, keepdims=True), 1e-30)
            oc_ref[...] = lax.dot_general(p.astype(BF16), cv_ref[...].astype(BF16), nt, preferred_element_type=F32)
            imps = []
            for h in range(NSA_KV_HEADS):
                r = h * NSA_GROUP * T_PAD
                imp = p[r:r + T_PAD]
                for gg in range(1, NSA_GROUP):
                    imp = imp + p[r + gg * T_PAD:r + (gg + 1) * T_PAD]
                imps.append(imp)
            imps.append(jnp.zeros((nb - NSA_KV_HEADS * T_PAD, nb), F32))
            imp_t = jnp.concatenate(imps, axis=0).T
            bl = _row_iota((nb, nb))
            score = jnp.where((bl == 0) | (bl == nb - 1), FORCED_SCORE, imp_t)
            sel = _topk_rows(score, bl, min(TOP_K - 1, nb)).T
            for h in range(NSA_KV_HEADS):
                for gg in range(NSA_GROUP):
                    r = (h * NSA_GROUP + gg) * T_PAD
                    selm_ref[r:r + T_PAD, :] = sel[h * T_PAD:(h + 1) * T_PAD]
            m_ref[...] = jnp.full(m_ref.shape, NEG, F32)
            l_ref[...] = jnp.zeros(l_ref.shape, F32)
            acc_ref[...] = jnp.zeros(acc_ref.shape, F32)

    def flash(s, mask, vals):
        m_old = m_ref[...]
        m_new = jnp.maximum(m_old, jnp.max(jnp.where(mask, s, NEG), axis=1, keepdims=True))
        e = jnp.where(mask, jnp.exp(s - m_new), 0.0)
        a = jnp.exp(m_old - m_new)
        m_ref[...] = m_new
        l_ref[...] = a * l_ref[...] + jnp.sum(e, axis=1, keepdims=True)
        acc_ref[...] = a * acc_ref[...] + lax.dot_general(e.astype(BF16), vals, nt, preferred_element_type=F32)

    @pl.when(ph == 1)
    def _():
        kb = buf_ref[slot, 0].astype(BF16)
        vb = buf_ref[slot, 1].astype(BF16)
        s = jnp.dot(q, kb, preferred_element_type=F32)
        mask = jnp.dot(selm_ref[...].astype(BF16), ex_ref[g], preferred_element_type=F32) > 0.5
        flash(s, mask, vb)

        @pl.when(g == n_groups - 1)
        def _():
            flash(jnp.dot(q, kn_ref[0], preferred_element_type=F32), new_mask, vn_ref[0])
            o_s = acc_ref[...] / jnp.maximum(l_ref[...], 1e-30)
            wk = wk_ref[0].astype(BF16)
            s_old = jnp.dot(q, wk, preferred_element_type=F32)
            s_new = jnp.dot(q, kwn_ref[0], preferred_element_type=F32)
            j = lax.broadcasted_iota(jnp.int32, (Q_ROWS, win_len), 1)
            old_mask = j > win_len - WINDOW + _row_iota((Q_ROWS, win_len)) % T_PAD
            mw = jnp.maximum(jnp.max(jnp.where(old_mask, s_old, NEG), axis=1, keepdims=True),
                             jnp.max(jnp.where(new_mask, s_new, NEG), axis=1, keepdims=True))
            e_old = jnp.where(old_mask, jnp.exp(s_old - mw), 0.0)
            e_new = jnp.where(new_mask, jnp.exp(s_new - mw), 0.0)
            lw = jnp.sum(e_old, axis=1, keepdims=True) + jnp.sum(e_new, axis=1, keepdims=True)
            o_w = (lax.dot_general(e_old.astype(BF16), wv_ref[0].astype(BF16), nt, preferred_element_type=F32)
                   + lax.dot_general(e_new.astype(BF16), vwn_ref[0], nt, preferred_element_type=F32)) / jnp.maximum(lw, 1e-30)
            gate = _sigmoid(g_ref[0])
            o_ref[0] = gate[:, 0:1] * oc_ref[...] + gate[:, 1:2] * o_s + gate[:, 2:3] * o_w


def _feature_major(x):
    lead = x.shape[:-3]
    n = len(lead)
    return x.transpose(*range(n), n + 1, n + 2, n).reshape(*lead, NSA_KV_WIDTH, x.shape[-3])


def _nsa_sample(q, ks, vs, kw, vw, gates, pools, layer, win_k, win_v, page_table, w_cmp_k, w_cmp_v):
    b, t = q.shape[:2]
    n_pages = page_table.shape[1]
    assert t <= T_PAD and t <= BLOCK and PAGE_SIZE % BLOCK == 0
    pps = PAGES_PER_STEP
    assert n_pages % pps == 0
    n_groups = n_pages // pps
    nb = n_pages * PAGE_SIZE // BLOCK
    assert nb % LANES == 0
    wb = win_k.shape[1]
    n_pool = pools[0].shape[1]
    flat = [_feature_major(p).reshape(p.shape[0] * n_pool, NSA_KV_WIDTH, PAGE_SIZE) for p in pools]
    wt = lambda w: jnp.tile(w.T, (NSA_KV_HEADS, PAGE_SIZE // BLOCK))
    wck2, wcv2 = wt(w_cmp_k), wt(w_cmp_v)
    scale = NSA_HEAD_DIM ** -0.5
    q5 = (q * scale).reshape(b, t, NSA_KV_HEADS, NSA_GROUP, NSA_HEAD_DIM).transpose(0, 2, 3, 1, 4)
    q5 = jnp.pad(q5, ((0, 0), (0, 0), (0, 0), (0, T_PAD - t), (0, 0)))
    own = jnp.eye(NSA_KV_HEADS, dtype=F32)[None, :, None, None, :, None]
    qbd = (q5[:, :, :, :, None, :] * own).reshape(b, Q_ROWS, NSA_KV_WIDTH).astype(BF16)
    padn = lambda x: jnp.pad(x, ((0, 0), (0, NEW_PAD - t), (0, 0))).astype(BF16).transpose(0, 2, 1)
    g5 = gates.reshape(b, t, NSA_KV_HEADS, NSA_GROUP, 3).transpose(0, 2, 3, 1, 4)
    g5 = jnp.pad(g5, ((0, 0), (0, 0), (0, 0), (0, T_PAD - t), (0, 0))).reshape(b, Q_ROWS, 3)
    blocks_per_step = pps * PAGE_SIZE // BLOCK
    key_blk = jnp.arange(pps * PAGE_SIZE) // BLOCK
    expand = (jnp.arange(nb)[None, :, None] == (jnp.arange(n_groups)[:, None, None] * blocks_per_step + key_blk[None, None, :]))
    expand = expand.astype(BF16)
    expand_t = expand.transpose(0, 2, 1)
    per_b = lambda shape: pl.BlockSpec((1,) + shape, lambda i, p, g, pt: (i, 0, 0))
    const = lambda a: pl.BlockSpec(a.shape, lambda i, p, g, pt: (0,) * a.ndim)
    hbm = pl.BlockSpec(memory_space=pl.ANY)
    body = functools.partial(_nsa_sample_body, n_groups=n_groups, t_len=t, win_len=wb, page_base=layer * n_pool)
    o = pl.pallas_call(
        body,
        grid_spec=pltpu.PrefetchScalarGridSpec(
            num_scalar_prefetch=1,
            grid=(b, 2, n_groups),
            in_specs=[hbm, hbm, hbm, hbm, per_b((Q_ROWS, NSA_KV_WIDTH)), per_b((NSA_KV_WIDTH, NEW_PAD)), per_b((NSA_KV_WIDTH, NEW_PAD)),
                      per_b((NSA_KV_WIDTH, wb)), per_b((NSA_KV_WIDTH, wb)), per_b((NSA_KV_WIDTH, NEW_PAD)), per_b((NSA_KV_WIDTH, NEW_PAD)),
                      const(wck2), const(wcv2), per_b((Q_ROWS, 3)), const(expand), const(expand_t)],
            out_specs=per_b((Q_ROWS, NSA_KV_WIDTH)),
            scratch_shapes=[
                pltpu.VMEM((2, 2, NSA_KV_WIDTH, pps * PAGE_SIZE), F32),
                pltpu.SemaphoreType.DMA((2, 2)),
                pltpu.VMEM((NSA_KV_WIDTH, nb), F32), pltpu.VMEM((NSA_KV_WIDTH, nb), F32),
                pltpu.VMEM((Q_ROWS, nb), F32), pltpu.VMEM((Q_ROWS, NSA_KV_WIDTH), F32),
                pltpu.VMEM((Q_ROWS, 1), F32), pltpu.VMEM((Q_ROWS, 1), F32), pltpu.VMEM((Q_ROWS, NSA_KV_WIDTH), F32),
            ],
        ),
        out_shape=jax.ShapeDtypeStruct((b, Q_ROWS, NSA_KV_WIDTH), F32),
        compiler_params=pltpu.CompilerParams(dimension_semantics=("arbitrary",) * 3, vmem_limit_bytes=VMEM_LIMIT),
        name="nsa_sample",
    )(page_table, *flat, qbd, padn(ks), padn(vs), _feature_major(win_k), _feature_major(win_v), padn(kw), padn(vw),
      wck2, wcv2, g5, expand, expand_t)
    o6 = o.reshape(b, NSA_KV_HEADS, NSA_GROUP, T_PAD, NSA_KV_HEADS, NSA_HEAD_DIM)[:, :, :, :t]
    o5 = jnp.stack([o6[:, h, :, :, h, :] for h in range(NSA_KV_HEADS)], axis=1)
    return o5.transpose(0, 3, 1, 2, 4).reshape(b, t, NSA_WIDTH)


def kernel(x_prompt, x_sample, cache_cmp_k, cache_cmp_v, cache_sel_k, cache_sel_v, state_win_k, state_win_v, state_conv,
           state_ssm, page_table, norm_mix_pre, norm_mix_post, norm_ffn_pre, norm_ffn_post, w_in, w_cmp_k, w_cmp_v,
           nsa_out_norm, conv_w, a_log, dt_bias, gdn_norm, w_out, w_up, w_down):
    depth = w_in.shape[0]
    bp, sp, _ = x_prompt.shape
    bs, ts, _ = x_sample.shape
    past = page_table.shape[1] * PAGE_SIZE
    pos_p = jnp.arange(sp)
    pos_s = past + jnp.arange(ts)
    pools = (cache_cmp_k, cache_cmp_v, cache_sel_k, cache_sel_v)
    kv5 = lambda x, b, n: x.reshape(b, n, NSA_KV_HEADS, NSA_HEAD_DIM)
    y_p, y_s = x_prompt, x_sample
    p_states, s_states = [], []
    for l in range(depth):
        w_perm = _permute_w_in(w_in[l]).astype(BF16)
        wck2 = jnp.tile(w_cmp_k[l], (1, LANES // NSA_HEAD_DIM))
        wcv2 = jnp.tile(w_cmp_v[l], (1, LANES // NSA_HEAD_DIM))
        post_w = (nsa_out_norm[l], w_out[l].astype(BF16), norm_mix_post[l], norm_ffn_pre[l], w_up[l].astype(BF16),
                  w_down[l].astype(BF16), norm_ffn_post[l])
        gdn_w = (conv_w[l], a_log[l], dt_bias[l], gdn_norm[l])

        wb = min(WINDOW, sp)
        conv0 = jnp.zeros((CONV_WIDTH - 1, 3 * GDN_WIDTH), F32)
        ssm0 = jnp.zeros((GDN_HEADS, GDN_HEAD_DIM, GDN_HEAD_DIM), F32)
        ys, sts = [], []
        for b in range(bp):
            x2 = y_p[b]
            q, kc, vc, ks, vs, kw, vw, qkv, z, small, ck, cv = _in_proj(x2, pos_p, norm_mix_pre[l], w_perm, wck2, wcv2)
            o_nsa = _nsa_prompt(q, ks, vs, kw, vw, ck, cv, small)
            o_gdn, ssm = _gdn_prompt(qkv, z, small, conv0, ssm0, *gdn_w)
            ys.append(_post(o_nsa, o_gdn, x2, *post_w))
            conv_state = jnp.concatenate([conv0, qkv], axis=0)[-(CONV_WIDTH - 1):]
            sts.append((kv5(kc, 1, sp)[0], kv5(vc, 1, sp)[0], kv5(ks, 1, sp)[0], kv5(vs, 1, sp)[0],
                        kv5(kw, 1, sp)[0, sp - wb:], kv5(vw, 1, sp)[0, sp - wb:], conv_state, ssm))
        y_p = jnp.stack(ys)
        p_states.append(tuple(jnp.stack(t) for t in zip(*sts)))

        n = bs * ts
        q, kc, vc, ks, vs, kw, vw, qkv, z, small, _, _ = _in_proj(
            jnp.pad(y_s.reshape(n, D_MODEL), ((0, (-n) % BLOCK), (0, 0))), jnp.pad(jnp.tile(pos_s, bs), (0, (-n) % BLOCK)),
            norm_mix_pre[l], w_perm, wck2, wcv2)
        r3 = lambda x: x[:n].reshape(bs, ts, -1)
        q, kc, vc, ks, vs, kw, vw, qkv, z, small = map(r3, (q, kc, vc, ks, vs, kw, vw, qkv, z, small))
        win_k, win_v = state_win_k[l], state_win_v[l]
        o_nsa = _nsa_sample(q, ks, vs, kw, vw, small[..., GATE_OFF:GATE_OFF + 3 * NSA_HEADS], pools, l, win_k, win_v,
                            page_table, w_cmp_k[l], w_cmp_v[l])
        o_gdn, ssm = _gdn_sample(qkv, z, small, state_conv[l], state_ssm[l], *gdn_w)
        y_s = _post(o_nsa.reshape(n, NSA_WIDTH), o_gdn.reshape(n, GDN_WIDTH), y_s.reshape(n, D_MODEL), *post_w).reshape(bs, ts, D_MODEL)
        conv_state = jnp.concatenate([state_conv[l], qkv], axis=1)[:, -(CONV_WIDTH - 1):]
        new_wk = jnp.concatenate([win_k, kv5(kw, bs, ts)], axis=1)[:, ts:]
        new_wv = jnp.concatenate([win_v, kv5(vw, bs, ts)], axis=1)[:, ts:]
        s_states.append((kv5(kc, bs, ts), kv5(vc, bs, ts), kv5(ks, bs, ts), kv5(vs, bs, ts), new_wk, new_wv, conv_state, ssm))
    p_out = [jnp.stack(t) for t in zip(*p_states)]
    s_out = [jnp.stack(t) for t in zip(*s_states)]
    return (y_p, y_s, *p_out, *s_out)
```

```python
import functools
import math

import jax
import jax.numpy as jnp
import numpy as np
from jax import lax
from jax.experimental import pallas as pl
from jax.experimental.pallas import tpu as pltpu

D_MODEL = 1024
PAGE_SIZE = 128
NSA_HEAD_DIM = 64
NSA_HEADS = 8
NSA_KV_HEADS = 2
NSA_GROUP = NSA_HEADS // NSA_KV_HEADS
NSA_WIDTH = NSA_HEADS * NSA_HEAD_DIM
NSA_KV_WIDTH = NSA_KV_HEADS * NSA_HEAD_DIM
BLOCK = 64
TOP_K = 16
WINDOW = 512
Q_BLOCK = 128
FORCED_SCORE = NSA_GROUP + 1.0
ROPE_THETA = 500000.0
ROPE_DIM = NSA_HEAD_DIM // 4
GDN_HEAD_DIM = 128
GDN_HEADS = 4
GDN_WIDTH = GDN_HEADS * GDN_HEAD_DIM
CONV_WIDTH = 4
GDN_CHUNK = 64
D_FF = 4 * D_MODEL
EPS = 1e-6
SPLIT_SIZES = (NSA_WIDTH,) + (NSA_KV_WIDTH,) * 6 + (3 * NSA_HEADS, 3 * GDN_WIDTH, GDN_WIDTH, GDN_HEADS, GDN_HEADS)

LANES = 128
SMALL_COLS = LANES
GATE_OFF, BETA_OFF, ALPHA_OFF = 0, 3 * NSA_HEADS, 3 * NSA_HEADS + GDN_HEADS
PROJ_COLS = NSA_WIDTH + 6 * NSA_KV_WIDTH + 3 * GDN_WIDTH + GDN_WIDTH + SMALL_COLS
VMEM_LIMIT = 56 * 1024 * 1024
NEG = -1e30
F32 = jnp.float32
BF16 = jnp.bfloat16
HI = lax.Precision.HIGHEST


def _rms(x, w):
    return x * lax.rsqrt(jnp.mean(x * x, axis=-1, keepdims=True) + EPS) * w


def _sigmoid(x):
    return 1.0 / (1.0 + jnp.exp(-x))


def _silu(x):
    return x * _sigmoid(x)


def _rope_slab(x, c, sa, sb):
    half = ROPE_DIM // 2
    return x * c + pltpu.roll(x, LANES - half, axis=1) * sa + pltpu.roll(x, half, axis=1) * sb


Q_SCALE = NSA_HEAD_DIM ** -0.5 * math.log2(math.e)


def _in_proj_body(x_ref, nw_ref, w_ref, rc_ref, rsa_ref, rsb_ref, wk_ref, wv_ref,
                  q_ref, kc_ref, vc_ref, ks_ref, vs_ref, kw_ref, vw_ref, qkv_ref, z_ref, sm_ref, ck_ref, cv_ref,
                  *fm_refs, feature_major):
    h = _rms(x_ref[...], nw_ref[...])
    p = jnp.dot(h.astype(BF16), w_ref[...], preferred_element_type=F32)
    c, sa, sb = rc_ref[...], rsa_ref[...], rsb_ref[...]
    rows = p.shape[0]
    off = 0
    for j in range(NSA_WIDTH // LANES):
        qs = _rope_slab(p[:, off:off + LANES], c, sa, sb)
        q_ref[:, j * LANES:(j + 1) * LANES] = qs
        if feature_major:
            qst = (qs * Q_SCALE).T.astype(BF16)
            heads_per_slab = LANES // NSA_HEAD_DIM
            kvh = j * heads_per_slab // NSA_GROUP
            for half in range(heads_per_slab):
                g = (j * heads_per_slab + half) % NSA_GROUP
                for blk in range(rows // Q_BLOCK):
                    fm_refs[0][kvh, blk, :, g * Q_BLOCK:(g + 1) * Q_BLOCK] = qst[
                        half * NSA_HEAD_DIM:(half + 1) * NSA_HEAD_DIM, blk * Q_BLOCK:(blk + 1) * Q_BLOCK]
        off += LANES
    kc = _rope_slab(p[:, off:off + LANES], c, sa, sb)
    kc_ref[...] = kc
    vc = p[:, off + LANES:off + 2 * LANES]
    vc_ref[...] = vc
    ks = _rope_slab(p[:, off + 2 * LANES:off + 3 * LANES], c, sa, sb)
    ks_ref[...] = ks
    vs = p[:, off + 3 * LANES:off + 4 * LANES]
    vs_ref[...] = vs
    kw = _rope_slab(p[:, off + 4 * LANES:off + 5 * LANES], c, sa, sb)
    kw_ref[...] = kw
    vw = p[:, off + 5 * LANES:off + 6 * LANES]
    vw_ref[...] = vw
    if feature_major:
        for ref, slab in zip(fm_refs[1:], (kc, vc, ks, vs, kw, vw)):
            ref[...] = slab.T
    off += 6 * LANES
    qkv_ref[...] = p[:, off:off + 3 * GDN_WIDTH]
    off += 3 * GDN_WIDTH
    z_ref[...] = p[:, off:off + GDN_WIDTH]
    off += GDN_WIDTH
    sm_ref[...] = p[:, off:off + SMALL_COLS]
    ck_ref[...] = jnp.sum(kc.reshape(rows // BLOCK, BLOCK, LANES) * wk_ref[...][None], axis=1)
    cv_ref[...] = jnp.sum(vc.reshape(rows // BLOCK, BLOCK, LANES) * wv_ref[...][None], axis=1)


def _permute_w_in(w_in):
    pts = np.cumsum((0,) + SPLIT_SIZES)
    seg = [w_in[:, int(pts[i]):int(pts[i + 1])] for i in range(len(SPLIT_SIZES))]
    q, kc, vc, ks, vs, kw, vw, gates, qkv, z, beta, alpha = seg
    small = jnp.concatenate([gates, beta, alpha], axis=1)
    small = jnp.pad(small, ((0, 0), (0, SMALL_COLS - small.shape[1])))
    return jnp.concatenate([q, kc, vc, ks, vs, kw, vw, qkv, z, small], axis=1)


def _rope_tables(pos):
    half = ROPE_DIM // 2
    inv_freq = ROPE_THETA ** (-jnp.arange(half, dtype=F32) / half)
    ang = pos.astype(F32)[:, None] * inv_freq[None, :]
    cos, sin = jnp.cos(ang), jnp.sin(ang)
    n = pos.shape[0]
    one = jnp.ones((n, NSA_HEAD_DIM - ROPE_DIM), F32)
    zero = jnp.zeros((n, NSA_HEAD_DIM - ROPE_DIM), F32)
    zh = jnp.zeros((n, half), F32)
    c = jnp.concatenate([cos, cos, one], axis=1)
    sa = jnp.concatenate([-sin, zh, zero], axis=1)
    sb = jnp.concatenate([zh, sin, zero], axis=1)
    return tuple(jnp.tile(t, (1, LANES // NSA_HEAD_DIM)) for t in (c, sa, sb))


def _in_proj(x2d, pos, norm_w, w_perm, wk2, wv2, feature_major=False):
    n = x2d.shape[0]
    tm = min(512, n)
    assert n % tm == 0 and tm % BLOCK == 0
    rc, rsa, rsb = _rope_tables(pos)
    row = lambda w: pl.BlockSpec((tm, w), lambda i: (i, 0))
    const = lambda a: pl.BlockSpec(a.shape, lambda i: (0,) * a.ndim)
    nw = norm_w.reshape(1, D_MODEL)
    widths = (NSA_WIDTH,) + (LANES,) * 6 + (3 * GDN_WIDTH, GDN_WIDTH, SMALL_COLS)
    out_shape = [jax.ShapeDtypeStruct((n, w), F32) for w in widths]
    out_shape += [jax.ShapeDtypeStruct((n // BLOCK, LANES), F32)] * 2
    out_specs = [row(w) for w in widths] + [pl.BlockSpec((tm // BLOCK, LANES), lambda i: (i, 0))] * 2
    if feature_major:
        assert tm % Q_BLOCK == 0
        qb = tm // Q_BLOCK
        out_shape += [jax.ShapeDtypeStruct((NSA_KV_HEADS, n // Q_BLOCK, NSA_HEAD_DIM, NSA_GROUP * Q_BLOCK), BF16)]
        out_specs += [pl.BlockSpec((NSA_KV_HEADS, qb, NSA_HEAD_DIM, NSA_GROUP * Q_BLOCK), lambda i: (0, i, 0, 0))]
        out_shape += [jax.ShapeDtypeStruct((LANES, n), F32)] * 6
        out_specs += [pl.BlockSpec((LANES, tm), lambda i: (0, i))] * 6
    return pl.pallas_call(
        functools.partial(_in_proj_body, feature_major=feature_major),
        grid=(n // tm,),
        in_specs=[row(D_MODEL), const(nw), const(w_perm), row(LANES), row(LANES), row(LANES), const(wk2), const(wv2)],
        out_specs=out_specs,
        out_shape=out_shape,
        compiler_params=pltpu.CompilerParams(dimension_semantics=("arbitrary",), vmem_limit_bytes=VMEM_LIMIT),
        name="in_proj",
    )(x2d, nw, w_perm, rc, rsa, rsb, wk2, wv2)


FF_CHUNK = 1024


def _post_body(on_ref, og_ref, x_ref, nn_ref, wo_ref, nmp_ref, nfp_ref, wu_ref, wd_ref, nfo_ref, y_ref):
    o_nsa = _rms(on_ref[...], nn_ref[...])
    mix = jnp.concatenate([o_nsa, og_ref[...]], axis=1).astype(BF16)
    y = jnp.dot(mix, wo_ref[...], preferred_element_type=F32)
    y1 = x_ref[...] + _rms(y, nmp_ref[...])
    h = _rms(y1, nfp_ref[...]).astype(BF16)
    acc = jnp.zeros_like(y1)
    for c in range(D_FF // FF_CHUNK):
        u = jnp.dot(h, wu_ref[:, c * FF_CHUNK:(c + 1) * FF_CHUNK], preferred_element_type=F32)
        u = jnp.square(jnp.maximum(u, 0.0))
        acc = acc + jnp.dot(u.astype(BF16), wd_ref[c * FF_CHUNK:(c + 1) * FF_CHUNK, :], preferred_element_type=F32)
    y_ref[...] = y1 + _rms(acc, nfo_ref[...])


def _post(o_nsa, o_gdn, x2d, nsa_norm, w_out, n_mix_post, n_ffn_pre, w_up, w_down, n_ffn_post):
    n = x2d.shape[0]
    tm = min(256, n)
    assert n % tm == 0
    row = lambda w: pl.BlockSpec((tm, w), lambda i: (i, 0))
    const = lambda a: pl.BlockSpec(a.shape, lambda i: (0,) * a.ndim)
    r = lambda v: v.reshape(1, -1)
    args = (o_nsa, o_gdn, x2d, r(nsa_norm), w_out, r(n_mix_post), r(n_ffn_pre), w_up, w_down, r(n_ffn_post))
    return pl.pallas_call(
        _post_body,
        grid=(n // tm,),
        in_specs=[row(NSA_WIDTH), row(GDN_WIDTH), row(D_MODEL)] + [const(a) for a in args[3:]],
        out_specs=row(D_MODEL),
        out_shape=jax.ShapeDtypeStruct((n, D_MODEL), F32),
        compiler_params=pltpu.CompilerParams(dimension_semantics=("arbitrary",), vmem_limit_bytes=VMEM_LIMIT),
        name="out_proj_ffn",
    )(*args)


SEL_CHUNK = 512
SEL_AUG = 16
V_ROWS = NSA_HEAD_DIM + 16


def _row_iota(shape):
    return lax.broadcasted_iota(jnp.int32, shape, 0)


def _topk_rows(score, blk, k):
    nb = score.shape[0]
    sel = jnp.zeros(score.shape, F32)
    for _ in range(k):
        m = jnp.max(score, axis=0, keepdims=True)
        idx = jnp.min(jnp.where(score == m, blk, nb), axis=0, keepdims=True)
        pick = blk == idx
        sel = jnp.where(pick, 1.0, sel)
        score = jnp.where(pick, -jnp.inf, score)
    return sel


REDUCE_WAYS = 8


def _col_reduce(x, op):
    rows = x.shape[0]
    if rows % (REDUCE_WAYS * 8) == 0 and rows > REDUCE_WAYS * 8:
        x = op(x.reshape(rows // (REDUCE_WAYS * 8), REDUCE_WAYS * 8, x.shape[1]), axis=0)
    return op(x, axis=0, keepdims=True)


def _softmax_rows(s, mask):
    m = jnp.max(jnp.where(mask, s, NEG), axis=0, keepdims=True)
    e = jnp.where(mask, jnp.exp2(s - m), 0.0)
    return e, 1.0 / jnp.maximum(jnp.sum(e, axis=0, keepdims=True), 1e-30)


def _nsa_prompt_body(qt_ref, ks_ref, kw_ref, vt_ref, ck_ref, cvt_ref, g_ref, o_ref, sb_ref, s0_ref, s1_ref, e0_ref, e1_ref, *, seq, kc, wn):
    i = pl.program_id(1)
    start = i * Q_BLOCK
    nb = seq // BLOCK
    g4 = range(NSA_GROUP)
    slab = lambda a, g: a[:, g * Q_BLOCK:(g + 1) * Q_BLOCK]
    qt = qt_ref[0, 0]
    pos = start + lax.broadcasted_iota(jnp.int32, (1, Q_BLOCK), 1)

    sc = jnp.dot(ck_ref[0], qt, preferred_element_type=F32)
    blk = _row_iota((nb, Q_BLOCK))
    cmask = (blk + 1) * BLOCK - 1 <= pos
    ps = []
    for g in g4:
        e, inv = _softmax_rows(slab(sc, g), cmask)
        ps.append(e * inv)
    o_c = jnp.dot(cvt_ref[0], jnp.concatenate(ps, axis=1).astype(BF16), preferred_element_type=F32)
    imp = ps[0]
    for g in range(1, NSA_GROUP):
        imp = imp + ps[g]

    cur = pos // BLOCK
    forced = (blk == 0) | (blk == cur) | (blk == cur - 1)
    score = jnp.where(forced | (blk > cur), -1.0, imp)
    sel = _topk_rows(score, blk, max(min(TOP_K, nb) - 3, 0))
    sel_bias = jnp.where(forced | ((score >= 0) & (sel > 0.0)), 0.0, NEG)
    for g in g4:
        sb_ref[:, g * Q_BLOCK:(g + 1) * Q_BLOCK] = sel_bias

    bpc = kc // BLOCK
    pad_rows = jnp.zeros((max(SEL_AUG - bpc, 1), NSA_GROUP * Q_BLOCK), F32)
    pad_tail = jnp.zeros((LANES - NSA_HEAD_DIM - SEL_AUG, NSA_GROUP * Q_BLOCK), BF16)

    n_parts = NSA_GROUP
    part = kc // n_parts

    def q_aug(c):
        rows = sb_ref[pl.ds(pl.multiple_of(c * bpc, bpc), bpc), :]
        if bpc < SEL_AUG:
            rows = jnp.concatenate([rows, pad_rows], axis=0)
        return jnp.concatenate([qt, rows.astype(BF16), pad_tail], axis=0)

    def scores_part(c, qa, j):
        k0 = pl.multiple_of(c * kc + j * part, part)
        return jnp.dot(ks_ref[0, pl.ds(k0, part), :], qa, preferred_element_type=F32)

    def values_dot(c, e_ref):
        k0 = pl.multiple_of(c * kc, kc)
        return jnp.dot(vt_ref[0, 0:V_ROWS, pl.ds(k0, kc)], e_ref[...], preferred_element_type=F32)

    def sweep_step(c, carry, cur, nxt, last):
        m, acc, alpha_prev = carry
        pv = values_dot(jnp.maximum(c - 1, 0), nxt[1])
        qa = None if last else q_aug(c + 1)
        k0 = pl.multiple_of(c * kc, kc)
        ms, alphas = [], []
        for g in g4:
            if not last:
                nxt[0][g * part:(g + 1) * part, :] = scores_part(c + 1, qa, g)
            sg = cur[0][:, g * Q_BLOCK:(g + 1) * Q_BLOCK]
            if last:
                sg = sg + jnp.where(k0 + _row_iota((kc, Q_BLOCK)) <= pos, 0.0, NEG)
            mo = slab(m, g)
            mn = jnp.maximum(mo, _col_reduce(sg, jnp.max))
            cur[1][:, g * Q_BLOCK:(g + 1) * Q_BLOCK] = jnp.exp2((sg - mn).astype(BF16))
            ms.append(mn)
            alphas.append(jnp.exp2(mo - mn))
        cat = lambda xs: jnp.concatenate(xs, axis=1)
        return cat(ms), alpha_prev * acc + pv, cat(alphas)

    def finish(c, carry, cur, nxt):
        _, acc, alpha = sweep_step(c, carry, cur, nxt, True)
        return alpha * acc + values_dot(c, cur[1])

    width = NSA_GROUP * Q_BLOCK
    even, odd = (s0_ref, e0_ref), (s1_ref, e1_ref)
    qa0 = q_aug(0)
    for g in g4:
        s0_ref[g * part:(g + 1) * part, :] = scores_part(0, qa0, g)
    e1_ref[...] = jnp.zeros(e1_ref.shape, BF16)
    init = (jnp.full((1, width), NEG, F32), jnp.zeros((V_ROWS, width), F32), jnp.ones((1, width), F32))
    c_diag = start // kc

    def pair(p, carry):
        carry = sweep_step(2 * p, carry, even, odd, False)
        return sweep_step(2 * p + 1, carry, odd, even, False)

    carry = lax.fori_loop(0, c_diag // 4, lambda p, cr: pair(2 * p + 1, pair(2 * p, cr)), init)
    carry = lax.fori_loop(c_diag // 4 * 2, c_diag // 2, pair, carry)
    acc_s = lax.cond(
        c_diag % 2 == 1,
        lambda cr: finish(c_diag, sweep_step(c_diag - 1, cr, even, odd, False), odd, even),
        lambda cr: finish(c_diag, cr, even, odd),
        carry)
    o_s = acc_s[0:NSA_HEAD_DIM] * (1.0 / jnp.maximum(acc_s[NSA_HEAD_DIM:NSA_HEAD_DIM + 1], 1e-30))

    w0 = pl.multiple_of(jnp.maximum(start + Q_BLOCK - wn, 0), Q_BLOCK)
    sw = jnp.dot(kw_ref[0, pl.ds(w0, wn), :], qt, preferred_element_type=F32)
    dist = pos - (w0 + _row_iota((wn, Q_BLOCK)))
    wbias = jnp.where((dist >= 0) & (dist < WINDOW), 0.0, NEG)
    ews = []
    for g in g4:
        sg = slab(sw, g) + wbias
        ews.append(jnp.exp2((sg - jnp.max(sg, axis=0, keepdims=True)).astype(BF16)))
    ow = jnp.dot(vt_ref[0, V_ROWS:2 * V_ROWS, pl.ds(w0, wn)], jnp.concatenate(ews, axis=1), preferred_element_type=F32)
    o_w = ow[0:NSA_HEAD_DIM] * (1.0 / jnp.maximum(ow[NSA_HEAD_DIM:NSA_HEAD_DIM + 1], 1e-30))

    gate = _sigmoid(g_ref[0])
    for g in g4:
        o_ref[0, g] = (gate[3 * g:3 * g + 1] * slab(o_c, g) + gate[3 * g + 1:3 * g + 2] * slab(o_s, g)
                       + gate[3 * g + 2:3 * g + 3] * slab(o_w, g))


def _split_heads_t(x, n):
    return x.reshape(n, NSA_KV_HEADS, NSA_HEAD_DIM).transpose(1, 0, 2)


def _nsa_prompt(qt, ks, kw, vst, vwt, ck, cv, small):
    s = ks.shape[0]
    assert s % Q_BLOCK == 0
    n_qb, nb = s // Q_BLOCK, s // BLOCK
    kc = min(SEL_CHUNK, s)
    wn = min(WINDOW + Q_BLOCK, s)
    assert s % kc == 0
    bpc = kc // BLOCK
    assert kc % Q_BLOCK == 0 and bpc <= SEL_AUG
    onehot = ((jnp.arange(s) // BLOCK) % bpc)[:, None] == jnp.arange(LANES - NSA_HEAD_DIM)[None, :]
    ksa = jnp.concatenate([_split_heads_t(ks, s).astype(BF16),
                           jnp.broadcast_to(onehot.astype(BF16), (NSA_KV_HEADS, s, LANES - NSA_HEAD_DIM))], axis=-1)
    kwh = _split_heads_t(kw, s).astype(BF16)
    ones_fill = jnp.zeros((NSA_KV_HEADS, V_ROWS - NSA_HEAD_DIM, s), BF16).at[:, 0, :].set(1.0)
    fm = lambda x: x.reshape(NSA_KV_HEADS, NSA_HEAD_DIM, s).astype(BF16)
    vt = jnp.concatenate([fm(vst), ones_fill, fm(vwt), ones_fill], axis=1)
    ckh = _split_heads_t(ck, nb).astype(BF16)
    cvt = _split_heads_t(cv, nb).astype(BF16).transpose(0, 2, 1)
    gt = small.T[GATE_OFF:GATE_OFF + 3 * NSA_HEADS].reshape(NSA_KV_HEADS, NSA_GROUP * 3, s)
    gt = jnp.pad(gt, ((0, 0), (0, 16 - NSA_GROUP * 3), (0, 0)))
    body = functools.partial(_nsa_prompt_body, seq=s, kc=kc, wn=wn)
    o = pl.pallas_call(
        body,
        grid=(NSA_KV_HEADS, n_qb),
        in_specs=[
            pl.BlockSpec((1, 1, NSA_HEAD_DIM, NSA_GROUP * Q_BLOCK), lambda h, i: (h, i, 0, 0)),
            pl.BlockSpec((1, s, LANES), lambda h, i: (h, 0, 0)),
            pl.BlockSpec((1, s, NSA_HEAD_DIM), lambda h, i: (h, 0, 0)),
            pl.BlockSpec((1, 2 * V_ROWS, s), lambda h, i: (h, 0, 0)),
            pl.BlockSpec((1, nb, NSA_HEAD_DIM), lambda h, i: (h, 0, 0)),
            pl.BlockSpec((1, NSA_HEAD_DIM, nb), lambda h, i: (h, 0, 0)),
            pl.BlockSpec((1, 16, Q_BLOCK), lambda h, i: (h, 0, i)),
        ],
        out_specs=pl.BlockSpec((1, NSA_GROUP, NSA_HEAD_DIM, Q_BLOCK), lambda h, i: (h, 0, 0, i)),
        out_shape=jax.ShapeDtypeStruct((NSA_KV_HEADS, NSA_GROUP, NSA_HEAD_DIM, s), F32),
        scratch_shapes=[pltpu.VMEM((nb, NSA_GROUP * Q_BLOCK), F32)]
        + [pltpu.VMEM((kc, NSA_GROUP * Q_BLOCK), F32)] * 2 + [pltpu.VMEM((kc, NSA_GROUP * Q_BLOCK), BF16)] * 2,
        compiler_params=pltpu.CompilerParams(dimension_semantics=("arbitrary", "arbitrary"), vmem_limit_bytes=VMEM_LIMIT),
        name="nsa_prompt",
    )(qt, ksa, kwh, vt, ckh, cvt, gt)
    return o.transpose(3, 0, 1, 2).reshape(s, NSA_WIDTH)


def _dot(a, b):
    return jnp.dot(a, b, precision=HI, preferred_element_type=F32)


def _dot_nt(a, b):
    return lax.dot_general(a, b, (((1,), (1,)), ((), ())), precision=HI, preferred_element_type=F32)


def _dot_tn(a, b):
    return lax.dot_general(a, b, (((0,), (0,)), ((), ())), precision=HI, preferred_element_type=F32)


def _dot1_nt(a, b):
    return lax.dot_general(a, b, (((1,), (1,)), ((), ())), preferred_element_type=F32)


def _split16(a):
    hi = a.astype(BF16)
    return hi, (a - hi.astype(F32)).astype(BF16)


def _dot3(a, b):
    ah, al = _split16(a)
    bh, bl = _split16(b)
    d = lambda x, y: jnp.dot(x, y, preferred_element_type=F32)
    return d(ah, bh) + (d(ah, bl) + d(al, bh))


def _softplus(x):
    return jnp.maximum(x, 0.0) + jnp.log(1.0 + jnp.exp(-jnp.abs(x)))


def _unit_lower_inverses(mats, ri, ci):
    c = mats[0].shape[0]
    eye = (ri == ci).astype(F32)
    base = 8
    same = ri // base == ci // base
    ds = [jnp.where(same, a, 0.0) for a in mats]
    d2s = [_dot3(d, d) for d in ds]
    xs = [(eye - d) + _dot3(eye - d, d2) for d, d2 in zip(ds, d2s)]
    d4s = [_dot3(d2, d2) for d2 in d2s]
    xs = [x + _dot3(x, d4) for x, d4 in zip(xs, d4s)]
    b = base
    while b < c:
        band = (ri // (2 * b) == ci // (2 * b)) & (ri // b != ci // b)
        lxs = [_dot3(jnp.where(band, a, 0.0), x) for a, x in zip(mats, xs)]
        xs = [x - _dot3(x, lx) for x, lx in zip(xs, lxs)]
        b *= 2
    return xs


def _conv_silu(x, prev, w):
    rowi = _row_iota(x.shape)
    conv = None
    for j in range(CONV_WIDTH - 1, -1, -1):
        if j == 0:
            xs = x
        else:
            xs = jnp.where(rowi < j, pltpu.roll(prev, j, axis=0), pltpu.roll(x, j, axis=0))
        term = xs * w[CONV_WIDTH - 1 - j:CONV_WIDTH - j]
        conv = term if conv is None else conv + term
    return _silu(conv)


def _l2n(x):
    return x * lax.rsqrt(jnp.sum(x * x, axis=-1, keepdims=True) + EPS)


GDN_STEP_CHUNKS = 4


def _gdn_chunk_body(qkv_ref, sm_ref, smt_ref, prev0_ref, cw_ref, ar_ref, dr_ref, at_ref, dt_ref,
                    u_ref, w_ref, qe_ref, kd_ref, in_ref, dl_ref, xp_ref):
    cs = GDN_CHUNK

    @pl.when(pl.program_id(0) == 0)
    def _():
        xp_ref[...] = prev0_ref[...]

    x = qkv_ref[...]
    act = _conv_silu(x, xp_ref[...], cw_ref[...])
    xp_ref[...] = x
    sm = sm_ref[...]
    ri = _row_iota((cs, cs))
    ci = lax.broadcasted_iota(jnp.int32, (cs, cs), 1)
    tri = (ri >= ci).astype(F32)
    tri_t = (ri <= ci).astype(F32)
    g_all = -jnp.exp(ar_ref[...]) * _softplus(sm + dr_ref[...])
    beta_all = _sigmoid(sm)
    scale = GDN_HEAD_DIM ** -0.5
    a_mats, rhs, where = [], [], []
    for c in range(GDN_STEP_CHUNKS):
        rows = slice(c * cs, (c + 1) * cs)
        gc_cols = _dot(tri, g_all[rows])
        g_rows = -jnp.exp(at_ref[...]) * _softplus(smt_ref[c] + dt_ref[...])
        gc_rows = _dot(g_rows, tri_t)
        intras, dls = [], []
        for h in range(GDN_HEADS):
            hs = slice(h * GDN_HEAD_DIM, (h + 1) * GDN_HEAD_DIM)
            q = _l2n(act[rows, hs]) * scale
            k = _l2n(act[rows, GDN_WIDTH + h * GDN_HEAD_DIM:GDN_WIDTH + (h + 1) * GDN_HEAD_DIM])
            v = act[rows, 2 * GDN_WIDTH + h * GDN_HEAD_DIM:2 * GDN_WIDTH + (h + 1) * GDN_HEAD_DIM]
            beta = beta_all[rows, BETA_OFF + h:BETA_OFF + h + 1]
            gc = gc_cols[:, ALPHA_OFF + h:ALPHA_OFF + h + 1]
            gr = gc_rows[GDN_HEADS + h:GDN_HEADS + h + 1, :]
            g_last = gc[cs - 1:cs, :]
            decay = jnp.where(ri >= ci, jnp.exp(gc - gr), 0.0)
            kb = k * beta
            k16 = k.astype(BF16)
            a_mats.append(jnp.where(ri > ci, _dot1_nt(kb.astype(BF16), k16) * decay, 0.0))
            rhs.append(jnp.concatenate([v * beta, kb * jnp.exp(gc)], axis=1))
            where.append((rows, hs))
            qe_ref[rows, hs] = (q * jnp.exp(gc)).astype(BF16)
            kd_ref[rows, hs] = (k * jnp.exp(g_last - gc)).astype(BF16)
            intras.append((_dot1_nt(q.astype(BF16), k16) * decay).astype(BF16))
            dls.append(jnp.broadcast_to(jnp.exp(g_last), (1, LANES)))
        in_ref[rows, :] = jnp.concatenate(intras, axis=1)
        dl_ref[c] = jnp.concatenate(dls + [jnp.zeros((2 * GDN_HEADS - GDN_HEADS, LANES), F32)], axis=0)
    for t, r, (rows, hs) in zip(_unit_lower_inverses(a_mats, ri, ci), rhs, where):
        sol = _dot3(t, r)
        u_ref[rows, hs] = sol[:, :GDN_HEAD_DIM]
        w_ref[rows, hs] = sol[:, GDN_HEAD_DIM:].astype(BF16)


def _gdn_scan_body(u_ref, w_ref, qe_ref, kd_ref, in_ref, dl_ref, z_ref, s0_ref, gn_ref, o_ref, sf_ref, s_ref):
    cs = GDN_CHUNK
    i = pl.program_id(0)

    @pl.when(i == 0)
    def _():
        s_ref[...] = s0_ref[...]

    heads = range(GDN_HEADS)
    hsl = [slice(h * GDN_HEAD_DIM, (h + 1) * GDN_HEAD_DIM) for h in heads]
    tn = (((0,), (0,)), ((), ()))
    ss = [s_ref[h] for h in heads]
    for c in range(GDN_STEP_CHUNKS):
        rows = slice(c * cs, (c + 1) * cs)
        s16 = [s.astype(BF16) for s in ss]
        v16 = [(u_ref[rows, hsl[h]] - jnp.dot(w_ref[rows, hsl[h]], s16[h], preferred_element_type=F32)).astype(BF16)
               for h in heads]
        ss = [ss[h] * dl_ref[c, h:h + 1, :] + lax.dot_general(kd_ref[rows, hsl[h]], v16[h], tn, preferred_element_type=F32)
              for h in heads]
        for h in heads:
            o = (jnp.dot(qe_ref[rows, hsl[h]], s16[h], preferred_element_type=F32)
                 + jnp.dot(in_ref[rows, h * cs:(h + 1) * cs], v16[h], preferred_element_type=F32))
            o_ref[rows, hsl[h]] = _rms(o, gn_ref[...]) * _silu(z_ref[rows, hsl[h]])
    for h in heads:
        s_ref[h] = ss[h]

    @pl.when(i == pl.num_programs(0) - 1)
    def _():
        sf_ref[...] = s_ref[...]


def _gdn_consts(a_log, dt_bias):
    lane = jnp.zeros((1, LANES), F32)
    ar = lane.at[0, ALPHA_OFF:ALPHA_OFF + GDN_HEADS].set(a_log)
    dr = lane.at[0, ALPHA_OFF:ALPHA_OFF + GDN_HEADS].set(dt_bias)
    col = jnp.zeros((2 * GDN_HEADS, 1), F32)
    at = jnp.broadcast_to(col.at[GDN_HEADS:, 0].set(a_log), (2 * GDN_HEADS, GDN_CHUNK))
    dt = jnp.broadcast_to(col.at[GDN_HEADS:, 0].set(dt_bias), (2 * GDN_HEADS, GDN_CHUNK))
    return ar, dr, at, dt


def _gdn_prompt(qkv, z, small, conv_prev, s0, conv_w, a_log, dt_bias, gdn_norm):
    n = qkv.shape[0]
    cs = GDN_CHUNK
    rows = GDN_STEP_CHUNKS * cs
    assert n % rows == 0
    nc = n // cs
    smt = small.T[BETA_OFF:BETA_OFF + 2 * GDN_HEADS].reshape(2 * GDN_HEADS, nc, cs).transpose(1, 0, 2)
    prev0 = jnp.pad(conv_prev, ((rows - (CONV_WIDTH - 1), 0), (0, 0)))
    ar, dr, at, dt = _gdn_consts(a_log, dt_bias)
    gn = gdn_norm.reshape(1, GDN_HEAD_DIM)
    row = lambda w: pl.BlockSpec((rows, w), lambda i: (i, 0))
    per_chunk = pl.BlockSpec((GDN_STEP_CHUNKS, 2 * GDN_HEADS, LANES), lambda i: (i, 0, 0))
    const = lambda a: pl.BlockSpec(a.shape, lambda i: (0,) * a.ndim)
    params = pltpu.CompilerParams(dimension_semantics=("arbitrary",), vmem_limit_bytes=VMEM_LIMIT)
    wide = lambda dt_: jax.ShapeDtypeStruct((n, GDN_WIDTH), dt_)
    u, w, qe, kd, intra, dl = pl.pallas_call(
        _gdn_chunk_body,
        grid=(n // rows,),
        in_specs=[row(3 * GDN_WIDTH), row(SMALL_COLS),
                  pl.BlockSpec((GDN_STEP_CHUNKS, 2 * GDN_HEADS, cs), lambda i: (i, 0, 0)),
                  const(prev0), const(conv_w), const(ar), const(dr), const(at), const(dt)],
        out_specs=[row(GDN_WIDTH)] * 4 + [row(GDN_HEADS * cs), per_chunk],
        out_shape=[wide(F32), wide(BF16), wide(BF16), wide(BF16), jax.ShapeDtypeStruct((n, GDN_HEADS * cs), BF16),
                   jax.ShapeDtypeStruct((nc, 2 * GDN_HEADS, LANES), F32)],
        scratch_shapes=[pltpu.VMEM((rows, 3 * GDN_WIDTH), F32)],
        compiler_params=params,
        name="gdn_chunks",
    )(qkv, small, smt, prev0, conv_w, ar, dr, at, dt)
    return pl.pallas_call(
        _gdn_scan_body,
        grid=(n // rows,),
        in_specs=[row(GDN_WIDTH)] * 4 + [row(GDN_HEADS * cs), per_chunk, row(GDN_WIDTH), const(s0), const(gn)],
        out_specs=[row(GDN_WIDTH), const(s0)],
        out_shape=[wide(F32), jax.ShapeDtypeStruct(s0.shape, F32)],
        scratch_shapes=[pltpu.VMEM(s0.shape, F32)],
        compiler_params=params,
        name="gdn_scan",
    )(u, w, qe, kd, intra, dl, z, s0, gn)


def _gdn_sample_pre_body(x0_ref, x1_ref, x2_ref, x3_ref, sm_ref, cw_ref, ar_ref, dr_ref, q_ref, k_ref, v_ref, a_ref, b_ref):
    w = cw_ref[...]
    conv = x0_ref[...] * w[0:1] + x1_ref[...] * w[1:2] + x2_ref[...] * w[2:3] + x3_ref[...] * w[3:4]
    act = _silu(conv)
    sm = sm_ref[...]
    a_cols = jnp.exp(-jnp.exp(ar_ref[...]) * _softplus(sm + dr_ref[...]))
    b_cols = _sigmoid(sm)
    rows = sm.shape[0]
    for h in range(GDN_HEADS):
        hs = slice(h * GDN_HEAD_DIM, (h + 1) * GDN_HEAD_DIM)
        q_ref[:, hs] = _l2n(act[:, hs]) * GDN_HEAD_DIM ** -0.5
        k_ref[:, hs] = _l2n(act[:, GDN_WIDTH + h * GDN_HEAD_DIM:GDN_WIDTH + (h + 1) * GDN_HEAD_DIM])
        a_ref[:, hs] = jnp.broadcast_to(a_cols[:, ALPHA_OFF + h:ALPHA_OFF + h + 1], (rows, GDN_HEAD_DIM))
        b_ref[:, hs] = jnp.broadcast_to(b_cols[:, BETA_OFF + h:BETA_OFF + h + 1], (rows, GDN_HEAD_DIM))
    v_ref[...] = act[:, 2 * GDN_WIDTH:]


GDN_SAMPLE_SEQS = 8


def _gdn_sample_body(qt_ref, kt_ref, v_ref, a_ref, b_ref, z_ref, s0_ref, gn_ref, o_ref, sf_ref, *, t_len):
    gn = gn_ref[...]

    def seq(bl, carry):
        v_all, a_all, b_all, z_all = v_ref[bl], a_ref[bl], b_ref[bl], z_ref[bl]
        heads = range(GDN_HEADS)
        hsl = [slice(h * GDN_HEAD_DIM, (h + 1) * GDN_HEAD_DIM) for h in heads]
        ss = [s0_ref[bl, h] for h in heads]
        qts, kts = [qt_ref[bl, h] for h in heads], [kt_ref[bl, h] for h in heads]
        outs = [[] for _ in heads]
        for t in range(t_len):
            sks = [jnp.sum(ss[h] * kts[h][:, t:t + 1], axis=0, keepdims=True) for h in heads]
            deltas = [b_all[t:t + 1, hsl[h]] * (v_all[t:t + 1, hsl[h]] - a_all[t:t + 1, hsl[h]] * sks[h]) for h in heads]
            ss = [a_all[t:t + 1, hsl[h]] * ss[h] + kts[h][:, t:t + 1] * deltas[h] for h in heads]
            for h in heads:
                outs[h].append(jnp.sum(ss[h] * qts[h][:, t:t + 1], axis=0, keepdims=True))
        for h in heads:
            sf_ref[bl, h] = ss[h]
            o = jnp.concatenate(outs[h], axis=0)
            o_ref[bl, :, hsl[h]] = _rms(o, gn) * _silu(z_all[:, hsl[h]])
        return carry

    lax.fori_loop(0, GDN_SAMPLE_SEQS, seq, 0)


def _gdn_sample(qkv, z, small, conv_prev, s0, conv_w, a_log, dt_bias, gdn_norm):
    b, t = qkv.shape[:2]
    n = b * t
    xp = jnp.concatenate([conv_prev, qkv], axis=1)
    xs = [xp[:, j:j + t].reshape(n, 3 * GDN_WIDTH) for j in range(CONV_WIDTH)]
    ar, dr, _, _ = _gdn_consts(a_log, dt_bias)
    tm = min(256, n)
    assert n % tm == 0
    row = lambda w: pl.BlockSpec((tm, w), lambda i: (i, 0))
    const = lambda a: pl.BlockSpec(a.shape, lambda i: (0,) * a.ndim)
    q, k, v, av, bv = pl.pallas_call(
        _gdn_sample_pre_body,
        grid=(n // tm,),
        in_specs=[row(3 * GDN_WIDTH)] * 4 + [row(SMALL_COLS), const(conv_w), const(ar), const(dr)],
        out_specs=[row(GDN_WIDTH)] * 5,
        out_shape=[jax.ShapeDtypeStruct((n, GDN_WIDTH), F32)] * 5,
        compiler_params=pltpu.CompilerParams(dimension_semantics=("arbitrary",), vmem_limit_bytes=VMEM_LIMIT),
        name="gdn_sample_pre",
    )(*xs, small.reshape(n, SMALL_COLS), conv_w, ar, dr)
    tr = lambda x: x.reshape(b, t, GDN_HEADS, GDN_HEAD_DIM).transpose(0, 2, 3, 1)
    r3 = lambda x: x.reshape(b, t, GDN_WIDTH)
    bb = GDN_SAMPLE_SEQS
    assert b % bb == 0
    gn = gdn_norm.reshape(1, GDN_HEAD_DIM)
    tspec = pl.BlockSpec((bb, GDN_HEADS, GDN_HEAD_DIM, t), lambda i: (i, 0, 0, 0))
    rspec = pl.BlockSpec((bb, t, GDN_WIDTH), lambda i: (i, 0, 0))
    sspec = pl.BlockSpec((bb, GDN_HEADS, GDN_HEAD_DIM, GDN_HEAD_DIM), lambda i: (i, 0, 0, 0))
    o, sf = pl.pallas_call(
        functools.partial(_gdn_sample_body, t_len=t),
        grid=(b // bb,),
        in_specs=[tspec, tspec, rspec, rspec, rspec, rspec, sspec, const(gn)],
        out_specs=[rspec, sspec],
        out_shape=[jax.ShapeDtypeStruct((b, t, GDN_WIDTH), F32), jax.ShapeDtypeStruct(s0.shape, F32)],
        compiler_params=pltpu.CompilerParams(dimension_semantics=("arbitrary",), vmem_limit_bytes=VMEM_LIMIT),
        name="gdn_sample",
    )(tr(q), tr(k), r3(v), r3(av), r3(bv), z, s0, gn)
    return o, sf


PAGES_PER_STEP = 64
T_PAD = 8
Q_ROWS = NSA_KV_HEADS * NSA_GROUP * T_PAD
NEW_PAD = LANES


def _nsa_sample_body(pt_ref, ckp_ref, cvp_ref, skp_ref, svp_ref, q_ref, kn_ref, vn_ref, wk_ref, wv_ref, kwn_ref, vwn_ref,
                     wck_ref, wcv_ref, g_ref, ex_ref, ext_ref, o_ref,
                     buf_ref, sem_ref, ck_ref, cv_ref, selm_ref, oc_ref, m_ref, l_ref, acc_ref,
                     *, n_groups, t_len, win_len, page_base):
    b, ph, g = pl.program_id(0), pl.program_id(1), pl.program_id(2)
    nb_, np_, ng_ = pl.num_programs(0), pl.num_programs(1), pl.num_programs(2)
    step = (b * np_ + ph) * ng_ + g
    pps = PAGES_PER_STEP
    rows_per_step = pps * PAGE_SIZE
    blocks_per_step = rows_per_step // BLOCK

    def copies(pool_k, pool_v, bb, gg, slot):
        out = []
        for p in range(pps):
            page = page_base + pt_ref[bb, gg * pps + p]
            for kv, pool in enumerate((pool_k, pool_v)):
                out.append(pltpu.make_async_copy(pool.at[page], buf_ref.at[slot, kv, :, pl.ds(p * PAGE_SIZE, PAGE_SIZE)],
                                                 sem_ref.at[slot, kv]))
        return out

    def start(bb, pp, gg, slot):
        @pl.when(pp == 0)
        def _():
            for cp in copies(ckp_ref, cvp_ref, bb, gg, slot):
                cp.start()

        @pl.when(pp == 1)
        def _():
            for cp in copies(skp_ref, svp_ref, bb, gg, slot):
                cp.start()

    slot = step % 2

    @pl.when(step == 0)
    def _():
        start(b, ph, g, slot)

    nxt = step + 1

    @pl.when(nxt < nb_ * np_ * ng_)
    def _():
        start(nxt // (np_ * ng_), (nxt // ng_) % np_, nxt % ng_, 1 - slot)

    for cp in copies(ckp_ref, cvp_ref, b, g, slot):
        cp.wait()

    q = q_ref[0]
    rowt = _row_iota((Q_ROWS, LANES)) % T_PAD
    lane = lax.broadcasted_iota(jnp.int32, (Q_ROWS, LANES), 1)
    new_mask = (lane <= rowt) & (lane < t_len)

    nt = (((1,), (1,)), ((), ()))

    def weighted(kv, w_ref):
        w = w_ref[...]
        return jnp.concatenate([buf_ref[slot, kv, :, p * PAGE_SIZE:(p + 1) * PAGE_SIZE] * w for p in range(pps)],
                               axis=1).astype(BF16)

    @pl.when(ph == 0)
    def _():
        @pl.when(g == 0)
        def _():
            ck_ref[...] = jnp.zeros(ck_ref.shape, F32)
            cv_ref[...] = jnp.zeros(cv_ref.shape, F32)

        ck_ref[...] += jnp.dot(weighted(0, wck_ref), ext_ref[g], preferred_element_type=F32)
        cv_ref[...] += jnp.dot(weighted(1, wcv_ref), ext_ref[g], preferred_element_type=F32)

        @pl.when(g == n_groups - 1)
        def _():
            nb = ck_ref.shape[1]
            s = jnp.dot(q, ck_ref[...].astype(BF16), preferred_element_type=F32)
            m = jnp.max(s, axis=1, keepdims=True)
            e = jnp.exp(s - m)
            p = e / jnp.maximum(jnp.sum(e, axis=1, keepdims=True), 1e-30)
            oc_ref[...] = lax.dot_general(p.astype(BF16), cv_ref[...].astype(BF16), nt, preferred_element_type=F32)
            imps = []
            for h in range(NSA_KV_HEADS):
                r = h * NSA_GROUP * T_PAD
                imp = p[r:r + T_PAD]
                for gg in range(1, NSA_GROUP):
                    imp = imp + p[r + gg * T_PAD:r + (gg + 1) * T_PAD]
                imps.append(imp)
            imps.append(jnp.zeros((nb - NSA_KV_HEADS * T_PAD, nb), F32))
            imp_t = jnp.concatenate(imps, axis=0).T
            bl = _row_iota((nb, nb))
            score = jnp.where((bl == 0) | (bl == nb - 1), FORCED_SCORE, imp_t)
            sel = _topk_rows(score, bl, min(TOP_K - 1, nb)).T
            for h in range(NSA_KV_HEADS):
                for gg in range(NSA_GROUP):
                    r = (h * NSA_GROUP + gg) * T_PAD
                    selm_ref[r:r + T_PAD, :] = sel[h * T_PAD:(h + 1) * T_PAD]
            m_ref[...] = jnp.full(m_ref.shape, NEG, F32)
            l_ref[...] = jnp.zeros(l_ref.shape, F32)
            acc_ref[...] = jnp.zeros(acc_ref.shape, F32)

    def flash(s, mask, vals):
        m_old = m_ref[...]
        m_new = jnp.maximum(m_old, jnp.max(jnp.where(mask, s, NEG), axis=1, keepdims=True))
        e = jnp.where(mask, jnp.exp(s - m_new), 0.0)
        a = jnp.exp(m_old - m_new)
        m_ref[...] = m_new
        l_ref[...] = a * l_ref[...] + jnp.sum(e, axis=1, keepdims=True)
        acc_ref[...] = a * acc_ref[...] + lax.dot_general(e.astype(BF16), vals, nt, preferred_element_type=F32)

    @pl.when(ph == 1)
    def _():
        kb = buf_ref[slot, 0].astype(BF16)
        vb = buf_ref[slot, 1].astype(BF16)
        s = jnp.dot(q, kb, preferred_element_type=F32)
        mask = jnp.dot(selm_ref[...].astype(BF16), ex_ref[g], preferred_element_type=F32) > 0.5
        flash(s, mask, vb)

        @pl.when(g == n_groups - 1)
        def _():
            flash(jnp.dot(q, kn_ref[0], preferred_element_type=F32), new_mask, vn_ref[0])
            o_s = acc_ref[...] / jnp.maximum(l_ref[...], 1e-30)
            wk = wk_ref[0].astype(BF16)
            s_old = jnp.dot(q, wk, preferred_element_type=F32)
            s_new = jnp.dot(q, kwn_ref[0], preferred_element_type=F32)
            j = lax.broadcasted_iota(jnp.int32, (Q_ROWS, win_len), 1)
            old_mask = j > win_len - WINDOW + _row_iota((Q_ROWS, win_len)) % T_PAD
            mw = jnp.maximum(jnp.max(jnp.where(old_mask, s_old, NEG), axis=1, keepdims=True),
                             jnp.max(jnp.where(new_mask, s_new, NEG), axis=1, keepdims=True))
            e_old = jnp.where(old_mask, jnp.exp(s_old - mw), 0.0)
            e_new = jnp.where(new_mask, jnp.exp(s_new - mw), 0.0)
            lw = jnp.sum(e_old, axis=1, keepdims=True) + jnp.sum(e_new, axis=1, keepdims=True)
            o_w = (lax.dot_general(e_old.astype(BF16), wv_ref[0].astype(BF16), nt, preferred_element_type=F32)
                   + lax.dot_general(e_new.astype(BF16), vwn_ref[0], nt, preferred_element_type=F32)) / jnp.maximum(lw, 1e-30)
            gate = _sigmoid(g_ref[0])
            o_ref[0] = gate[:, 0:1] * oc_ref[...] + gate[:, 1:2] * o_s + gate[:, 2:3] * o_w


def _feature_major(x):
    lead = x.shape[:-3]
    n = len(lead)
    return x.transpose(*range(n), n + 1, n + 2, n).reshape(*lead, NSA_KV_WIDTH, x.shape[-3])


def _nsa_sample(q, ks, vs, kw, vw, gates, pools, layer, win_k, win_v, page_table, w_cmp_k, w_cmp_v):
    b, t = q.shape[:2]
    n_pages = page_table.shape[1]
    assert t <= T_PAD and t <= BLOCK and PAGE_SIZE % BLOCK == 0
    pps = PAGES_PER_STEP
    assert n_pages % pps == 0
    n_groups = n_pages // pps
    nb = n_pages * PAGE_SIZE // BLOCK
    assert nb % LANES == 0
    wb = win_k.shape[1]
    n_pool = pools[0].shape[1]
    flat = [_feature_major(p).reshape(p.shape[0] * n_pool, NSA_KV_WIDTH, PAGE_SIZE) for p in pools]
    wt = lambda w: jnp.tile(w.T, (NSA_KV_HEADS, PAGE_SIZE // BLOCK))
    wck2, wcv2 = wt(w_cmp_k), wt(w_cmp_v)
    scale = NSA_HEAD_DIM ** -0.5
    q5 = (q * scale).reshape(b, t, NSA_KV_HEADS, NSA_GROUP, NSA_HEAD_DIM).transpose(0, 2, 3, 1, 4)
    q5 = jnp.pad(q5, ((0, 0), (0, 0), (0, 0), (0, T_PAD - t), (0, 0)))
    own = jnp.eye(NSA_KV_HEADS, dtype=F32)[None, :, None, None, :, None]
    qbd = (q5[:, :, :, :, None, :] * own).reshape(b, Q_ROWS, NSA_KV_WIDTH).astype(BF16)
    padn = lambda x: jnp.pad(x, ((0, 0), (0, NEW_PAD - t), (0, 0))).astype(BF16).transpose(0, 2, 1)
    g5 = gates.reshape(b, t, NSA_KV_HEADS, NSA_GROUP, 3).transpose(0, 2, 3, 1, 4)
    g5 = jnp.pad(g5, ((0, 0), (0, 0), (0, 0), (0, T_PAD - t), (0, 0))).reshape(b, Q_ROWS, 3)
    blocks_per_step = pps * PAGE_SIZE // BLOCK
    key_blk = jnp.arange(pps * PAGE_SIZE) // BLOCK
    expand = (jnp.arange(nb)[None, :, None] == (jnp.arange(n_groups)[:, None, None] * blocks_per_step + key_blk[None, None, :]))
    expand = expand.astype(BF16)
    expand_t = expand.transpose(0, 2, 1)
    per_b = lambda shape: pl.BlockSpec((1,) + shape, lambda i, p, g, pt: (i, 0, 0))
    const = lambda a: pl.BlockSpec(a.shape, lambda i, p, g, pt: (0,) * a.ndim)
    hbm = pl.BlockSpec(memory_space=pl.ANY)
    body = functools.partial(_nsa_sample_body, n_groups=n_groups, t_len=t, win_len=wb, page_base=layer * n_pool)
    o = pl.pallas_call(
        body,
        grid_spec=pltpu.PrefetchScalarGridSpec(
            num_scalar_prefetch=1,
            grid=(b, 2, n_groups),
            in_specs=[hbm, hbm, hbm, hbm, per_b((Q_ROWS, NSA_KV_WIDTH)), per_b((NSA_KV_WIDTH, NEW_PAD)), per_b((NSA_KV_WIDTH, NEW_PAD)),
                      per_b((NSA_KV_WIDTH, wb)), per_b((NSA_KV_WIDTH, wb)), per_b((NSA_KV_WIDTH, NEW_PAD)), per_b((NSA_KV_WIDTH, NEW_PAD)),
                      const(wck2), const(wcv2), per_b((Q_ROWS, 3)), const(expand), const(expand_t)],
            out_specs=per_b((Q_ROWS, NSA_KV_WIDTH)),
            scratch_shapes=[
                pltpu.VMEM((2, 2, NSA_KV_WIDTH, pps * PAGE_SIZE), F32),
                pltpu.SemaphoreType.DMA((2, 2)),
                pltpu.VMEM((NSA_KV_WIDTH, nb), F32), pltpu.VMEM((NSA_KV_WIDTH, nb), F32),
                pltpu.VMEM((Q_ROWS, nb), F32), pltpu.VMEM((Q_ROWS, NSA_KV_WIDTH), F32),
                pltpu.VMEM((Q_ROWS, 1), F32), pltpu.VMEM((Q_ROWS, 1), F32), pltpu.VMEM((Q_ROWS, NSA_KV_WIDTH), F32),
            ],
        ),
        out_shape=jax.ShapeDtypeStruct((b, Q_ROWS, NSA_KV_WIDTH), F32),
        compiler_params=pltpu.CompilerParams(dimension_semantics=("arbitrary",) * 3, vmem_limit_bytes=VMEM_LIMIT),
        name="nsa_sample",
    )(page_table, *flat, qbd, padn(ks), padn(vs), _feature_major(win_k), _feature_major(win_v), padn(kw), padn(vw),
      wck2, wcv2, g5, expand, expand_t)
    o6 = o.reshape(b, NSA_KV_HEADS, NSA_GROUP, T_PAD, NSA_KV_HEADS, NSA_HEAD_DIM)[:, :, :, :t]
    o5 = jnp.stack([o6[:, h, :, :, h, :] for h in range(NSA_KV_HEADS)], axis=1)
    return o5.transpose(0, 3, 1, 2, 4).reshape(b, t, NSA_WIDTH)


def kernel(x_prompt, x_sample, cache_cmp_k, cache_cmp_v, cache_sel_k, cache_sel_v, state_win_k, state_win_v, state_conv,
           state_ssm, page_table, norm_mix_pre, norm_mix_post, norm_ffn_pre, norm_ffn_post, w_in, w_cmp_k, w_cmp_v,
           nsa_out_norm, conv_w, a_log, dt_bias, gdn_norm, w_out, w_up, w_down):
    depth = w_in.shape[0]
    bp, sp, _ = x_prompt.shape
    bs, ts, _ = x_sample.shape
    past = page_table.shape[1] * PAGE_SIZE
    pos_p = jnp.arange(sp)
    pos_s = past + jnp.arange(ts)
    pools = (cache_cmp_k, cache_cmp_v, cache_sel_k, cache_sel_v)
    kv5 = lambda x, b, n: x.reshape(b, n, NSA_KV_HEADS, NSA_HEAD_DIM)
    y_p, y_s = x_prompt, x_sample
    p_states, s_states = [], []
    for l in range(depth):
        w_perm = _permute_w_in(w_in[l]).astype(BF16)
        wck2 = jnp.tile(w_cmp_k[l], (1, LANES // NSA_HEAD_DIM))
        wcv2 = jnp.tile(w_cmp_v[l], (1, LANES // NSA_HEAD_DIM))
        post_w = (nsa_out_norm[l], w_out[l].astype(BF16), norm_mix_post[l], norm_ffn_pre[l], w_up[l].astype(BF16),
                  w_down[l].astype(BF16), norm_ffn_post[l])
        gdn_w = (conv_w[l], a_log[l], dt_bias[l], gdn_norm[l])

        wb = min(WINDOW, sp)
        conv0 = jnp.zeros((CONV_WIDTH - 1, 3 * GDN_WIDTH), F32)
        ssm0 = jnp.zeros((GDN_HEADS, GDN_HEAD_DIM, GDN_HEAD_DIM), F32)
        ys, sts = [], []
        for b in range(bp):
            x2 = y_p[b]
            (_, _, _, ks, _, kw, _, qkv, z, small, ck, cv, qt, kct, vct, kst, vst, kwt, vwt) = _in_proj(
                x2, pos_p, norm_mix_pre[l], w_perm, wck2, wcv2, feature_major=True)
            o_nsa = _nsa_prompt(qt, ks, kw, vst, vwt, ck, cv, small)
            o_gdn, ssm = _gdn_prompt(qkv, z, small, conv0, ssm0, *gdn_w)
            ys.append(_post(o_nsa, o_gdn, x2, *post_w))
            conv_state = jnp.concatenate([conv0, qkv], axis=0)[-(CONV_WIDTH - 1):]
            rows_major = lambda x: x.reshape(NSA_KV_HEADS, NSA_HEAD_DIM, sp).transpose(2, 0, 1)
            sts.append((rows_major(kct), rows_major(vct), rows_major(kst), rows_major(vst),
                        rows_major(kwt)[sp - wb:], rows_major(vwt)[sp - wb:], conv_state, ssm))
        y_p = jnp.stack(ys)
        p_states.append(tuple(jnp.stack(t) for t in zip(*sts)))

        n = bs * ts
        q, kc, vc, ks, vs, kw, vw, qkv, z, small, _, _ = _in_proj(
            jnp.pad(y_s.reshape(n, D_MODEL), ((0, (-n) % BLOCK), (0, 0))), jnp.pad(jnp.tile(pos_s, bs), (0, (-n) % BLOCK)),
            norm_mix_pre[l], w_perm, wck2, wcv2)
        r3 = lambda x: x[:n].reshape(bs, ts, -1)
        q, kc, vc, ks, vs, kw, vw, qkv, z, small = map(r3, (q, kc, vc, ks, vs, kw, vw, qkv, z, small))
        win_k, win_v = state_win_k[l], state_win_v[l]
        o_nsa = _nsa_sample(q, ks, vs, kw, vw, small[..., GATE_OFF:GATE_OFF + 3 * NSA_HEADS], pools, l, win_k, win_v,
                            page_table, w_cmp_k[l], w_cmp_v[l])
        o_gdn, ssm = _gdn_sample(qkv, z, small, state_conv[l], state_ssm[l], *gdn_w)
        y_s = _post(o_nsa.reshape(n, NSA_WIDTH), o_gdn.reshape(n, GDN_WIDTH), y_s.reshape(n, D_MODEL), *post_w).reshape(bs, ts, D_MODEL)
        conv_state = jnp.concatenate([state_conv[l], qkv], axis=1)[:, -(CONV_WIDTH - 1):]
        new_wk = jnp.concatenate([win_k, kv5(kw, bs, ts)], axis=1)[:, ts:]
        new_wv = jnp.concatenate([win_v, kv5(vw, bs, ts)], axis=1)[:, ts:]
        s_states.append((kv5(kc, bs, ts), kv5(vc, bs, ts), kv5(ks, bs, ts), kv5(vs, bs, ts), new_wk, new_wv, conv_state, ssm))
    p_out = [jnp.stack(t) for t in zip(*p_states)]
    s_out = [jnp.stack(t) for t in zip(*s_states)]
    return (y_p, y_s, *p_out, *s_out)
```

```python
import functools
import math

import jax
import jax.numpy as jnp
import numpy as np
from jax import lax
from jax.experimental import pallas as pl
from jax.experimental.pallas import tpu as pltpu

D_MODEL = 1024
PAGE_SIZE = 128
NSA_HEAD_DIM = 64
NSA_HEADS = 8
NSA_KV_HEADS = 2
NSA_GROUP = NSA_HEADS // NSA_KV_HEADS
NSA_WIDTH = NSA_HEADS * NSA_HEAD_DIM
NSA_KV_WIDTH = NSA_KV_HEADS * NSA_HEAD_DIM
BLOCK = 64
TOP_K = 16
WINDOW = 512
Q_BLOCK = 128
FORCED_SCORE = NSA_GROUP + 1.0
ROPE_THETA = 500000.0
ROPE_DIM = NSA_HEAD_DIM // 4
GDN_HEAD_DIM = 128
GDN_HEADS = 4
GDN_WIDTH = GDN_HEADS * GDN_HEAD_DIM
CONV_WIDTH = 4
GDN_CHUNK = 64
D_FF = 4 * D_MODEL
EPS = 1e-6
SPLIT_SIZES = (NSA_WIDTH,) + (NSA_KV_WIDTH,) * 6 + (3 * NSA_HEADS, 3 * GDN_WIDTH, GDN_WIDTH, GDN_HEADS, GDN_HEADS)

LANES = 128
SMALL_COLS = LANES
GATE_OFF, BETA_OFF, ALPHA_OFF = 0, 3 * NSA_HEADS, 3 * NSA_HEADS + GDN_HEADS
PROJ_COLS = NSA_WIDTH + 6 * NSA_KV_WIDTH + 3 * GDN_WIDTH + GDN_WIDTH + SMALL_COLS
VMEM_LIMIT = 56 * 1024 * 1024
NEG = -1e30
F32 = jnp.float32
BF16 = jnp.bfloat16
HI = lax.Precision.HIGHEST


def _rms(x, w):
    return x * lax.rsqrt(jnp.mean(x * x, axis=-1, keepdims=True) + EPS) * w


def _sigmoid(x):
    return 1.0 / (1.0 + jnp.exp(-x))


def _silu(x):
    return x * _sigmoid(x)


def _rope_slab(x, c, sa, sb):
    half = ROPE_DIM // 2
    return x * c + pltpu.roll(x, LANES - half, axis=1) * sa + pltpu.roll(x, half, axis=1) * sb


Q_SCALE = NSA_HEAD_DIM ** -0.5 * math.log2(math.e)


def _in_proj_body(x_ref, nw_ref, w_ref, rc_ref, rsa_ref, rsb_ref, wk_ref, wv_ref,
                  q_ref, kc_ref, vc_ref, ks_ref, vs_ref, kw_ref, vw_ref, qkv_ref, z_ref, sm_ref, ck_ref, cv_ref,
                  *fm_refs, feature_major):
    h = _rms(x_ref[...], nw_ref[...])
    p = jnp.dot(h.astype(BF16), w_ref[...], preferred_element_type=F32)
    c, sa, sb = rc_ref[...], rsa_ref[...], rsb_ref[...]
    rows = p.shape[0]
    off = 0
    for j in range(NSA_WIDTH // LANES):
        qs = _rope_slab(p[:, off:off + LANES], c, sa, sb)
        q_ref[:, j * LANES:(j + 1) * LANES] = qs
        if feature_major:
            qst = (qs * Q_SCALE).T.astype(BF16)
            heads_per_slab = LANES // NSA_HEAD_DIM
            kvh = j * heads_per_slab // NSA_GROUP
            for half in range(heads_per_slab):
                g = (j * heads_per_slab + half) % NSA_GROUP
                for blk in range(rows // Q_BLOCK):
                    fm_refs[0][kvh, blk, :, g * Q_BLOCK:(g + 1) * Q_BLOCK] = qst[
                        half * NSA_HEAD_DIM:(half + 1) * NSA_HEAD_DIM, blk * Q_BLOCK:(blk + 1) * Q_BLOCK]
        off += LANES
    kc = _rope_slab(p[:, off:off + LANES], c, sa, sb)
    kc_ref[...] = kc
    vc = p[:, off + LANES:off + 2 * LANES]
    vc_ref[...] = vc
    ks = _rope_slab(p[:, off + 2 * LANES:off + 3 * LANES], c, sa, sb)
    ks_ref[...] = ks
    vs = p[:, off + 3 * LANES:off + 4 * LANES]
    vs_ref[...] = vs
    kw = _rope_slab(p[:, off + 4 * LANES:off + 5 * LANES], c, sa, sb)
    kw_ref[...] = kw
    vw = p[:, off + 5 * LANES:off + 6 * LANES]
    vw_ref[...] = vw
    if feature_major:
        for ref, slab in zip(fm_refs[1:], (kc, vc, ks, vs, kw, vw)):
            ref[...] = slab.T
    off += 6 * LANES
    qkv_ref[...] = p[:, off:off + 3 * GDN_WIDTH]
    off += 3 * GDN_WIDTH
    z_ref[...] = p[:, off:off + GDN_WIDTH]
    off += GDN_WIDTH
    sm_ref[...] = p[:, off:off + SMALL_COLS]
    ck_ref[...] = jnp.sum(kc.reshape(rows // BLOCK, BLOCK, LANES) * wk_ref[...][None], axis=1)
    cv_ref[...] = jnp.sum(vc.reshape(rows // BLOCK, BLOCK, LANES) * wv_ref[...][None], axis=1)


def _permute_w_in(w_in):
    pts = np.cumsum((0,) + SPLIT_SIZES)
    seg = [w_in[:, int(pts[i]):int(pts[i + 1])] for i in range(len(SPLIT_SIZES))]
    q, kc, vc, ks, vs, kw, vw, gates, qkv, z, beta, alpha = seg
    small = jnp.concatenate([gates, beta, alpha], axis=1)
    small = jnp.pad(small, ((0, 0), (0, SMALL_COLS - small.shape[1])))
    return jnp.concatenate([q, kc, vc, ks, vs, kw, vw, qkv, z, small], axis=1)


def _rope_tables(pos):
    half = ROPE_DIM // 2
    inv_freq = ROPE_THETA ** (-jnp.arange(half, dtype=F32) / half)
    ang = pos.astype(F32)[:, None] * inv_freq[None, :]
    cos, sin = jnp.cos(ang), jnp.sin(ang)
    n = pos.shape[0]
    one = jnp.ones((n, NSA_HEAD_DIM - ROPE_DIM), F32)
    zero = jnp.zeros((n, NSA_HEAD_DIM - ROPE_DIM), F32)
    zh = jnp.zeros((n, half), F32)
    c = jnp.concatenate([cos, cos, one], axis=1)
    sa = jnp.concatenate([-sin, zh, zero], axis=1)
    sb = jnp.concatenate([zh, sin, zero], axis=1)
    return tuple(jnp.tile(t, (1, LANES // NSA_HEAD_DIM)) for t in (c, sa, sb))


def _in_proj(x2d, pos, norm_w, w_perm, wk2, wv2, feature_major=False):
    n = x2d.shape[0]
    tm = min(512, n)
    assert n % tm == 0 and tm % BLOCK == 0
    rc, rsa, rsb = _rope_tables(pos)
    row = lambda w: pl.BlockSpec((tm, w), lambda i: (i, 0))
    const = lambda a: pl.BlockSpec(a.shape, lambda i: (0,) * a.ndim)
    nw = norm_w.reshape(1, D_MODEL)
    widths = (NSA_WIDTH,) + (LANES,) * 6 + (3 * GDN_WIDTH, GDN_WIDTH, SMALL_COLS)
    out_shape = [jax.ShapeDtypeStruct((n, w), F32) for w in widths]
    out_shape += [jax.ShapeDtypeStruct((n // BLOCK, LANES), F32)] * 2
    out_specs = [row(w) for w in widths] + [pl.BlockSpec((tm // BLOCK, LANES), lambda i: (i, 0))] * 2
    if feature_major:
        assert tm % Q_BLOCK == 0
        qb = tm // Q_BLOCK
        out_shape += [jax.ShapeDtypeStruct((NSA_KV_HEADS, n // Q_BLOCK, NSA_HEAD_DIM, NSA_GROUP * Q_BLOCK), BF16)]
        out_specs += [pl.BlockSpec((NSA_KV_HEADS, qb, NSA_HEAD_DIM, NSA_GROUP * Q_BLOCK), lambda i: (0, i, 0, 0))]
        out_shape += [jax.ShapeDtypeStruct((LANES, n), F32)] * 6
        out_specs += [pl.BlockSpec((LANES, tm), lambda i: (0, i))] * 6
    return pl.pallas_call(
        functools.partial(_in_proj_body, feature_major=feature_major),
        grid=(n // tm,),
        in_specs=[row(D_MODEL), const(nw), const(w_perm), row(LANES), row(LANES), row(LANES), const(wk2), const(wv2)],
        out_specs=out_specs,
        out_shape=out_shape,
        compiler_params=pltpu.CompilerParams(dimension_semantics=("arbitrary",), vmem_limit_bytes=VMEM_LIMIT),
        name="in_proj",
    )(x2d, nw, w_perm, rc, rsa, rsb, wk2, wv2)


FF_CHUNK = 1024


def _post_body(on_ref, og_ref, x_ref, nn_ref, wo_ref, nmp_ref, nfp_ref, wu_ref, wd_ref, nfo_ref, y_ref):
    o_nsa = _rms(on_ref[...], nn_ref[...])
    mix = jnp.concatenate([o_nsa, og_ref[...]], axis=1).astype(BF16)
    y = jnp.dot(mix, wo_ref[...], preferred_element_type=F32)
    y1 = x_ref[...] + _rms(y, nmp_ref[...])
    h = _rms(y1, nfp_ref[...]).astype(BF16)
    acc = jnp.zeros_like(y1)
    for c in range(D_FF // FF_CHUNK):
        u = jnp.dot(h, wu_ref[:, c * FF_CHUNK:(c + 1) * FF_CHUNK], preferred_element_type=F32)
        u = jnp.square(jnp.maximum(u, 0.0))
        acc = acc + jnp.dot(u.astype(BF16), wd_ref[c * FF_CHUNK:(c + 1) * FF_CHUNK, :], preferred_element_type=F32)
    y_ref[...] = y1 + _rms(acc, nfo_ref[...])


def _post(o_nsa, o_gdn, x2d, nsa_norm, w_out, n_mix_post, n_ffn_pre, w_up, w_down, n_ffn_post):
    n = x2d.shape[0]
    tm = min(256, n)
    assert n % tm == 0
    row = lambda w: pl.BlockSpec((tm, w), lambda i: (i, 0))
    const = lambda a: pl.BlockSpec(a.shape, lambda i: (0,) * a.ndim)
    r = lambda v: v.reshape(1, -1)
    args = (o_nsa, o_gdn, x2d, r(nsa_norm), w_out, r(n_mix_post), r(n_ffn_pre), w_up, w_down, r(n_ffn_post))
    return pl.pallas_call(
        _post_body,
        grid=(n // tm,),
        in_specs=[row(NSA_WIDTH), row(GDN_WIDTH), row(D_MODEL)] + [const(a) for a in args[3:]],
        out_specs=row(D_MODEL),
        out_shape=jax.ShapeDtypeStruct((n, D_MODEL), F32),
        compiler_params=pltpu.CompilerParams(dimension_semantics=("arbitrary",), vmem_limit_bytes=VMEM_LIMIT),
        name="out_proj_ffn",
    )(*args)


SEL_CHUNK = 512
SEL_AUG = 16
V_ROWS = NSA_HEAD_DIM + 16
SELECT_TIERS = 4


def _row_iota(shape):
    return lax.broadcasted_iota(jnp.int32, shape, 0)


def _topk_rows(score, blk, k):
    nb = score.shape[0]
    sel = jnp.zeros(score.shape, F32)
    for _ in range(k):
        m = jnp.max(score, axis=0, keepdims=True)
        idx = jnp.min(jnp.where(score == m, blk, nb), axis=0, keepdims=True)
        pick = blk == idx
        sel = jnp.where(pick, 1.0, sel)
        score = jnp.where(pick, -jnp.inf, score)
    return sel


REDUCE_WAYS = 8


def _col_reduce(x, op):
    rows = x.shape[0]
    if rows % (REDUCE_WAYS * 8) == 0 and rows > REDUCE_WAYS * 8:
        x = op(x.reshape(rows // (REDUCE_WAYS * 8), REDUCE_WAYS * 8, x.shape[1]), axis=0)
    return op(x, axis=0, keepdims=True)


def _softmax_rows(s, mask):
    m = jnp.max(jnp.where(mask, s, NEG), axis=0, keepdims=True)
    e = jnp.where(mask, jnp.exp2(s - m), 0.0)
    return e, 1.0 / jnp.maximum(jnp.sum(e, axis=0, keepdims=True), 1e-30)


def _nsa_prompt_body(qt_ref, ks_ref, kw_ref, vt_ref, ck_ref, cvt_ref, g_ref, o_ref,
                     sb_ref, s0_ref, s1_ref, e0_ref, e1_ref, oc_ref, *, seq, kc, wn):
    i = pl.program_id(1)
    start = i * Q_BLOCK
    nb = seq // BLOCK
    g4 = range(NSA_GROUP)
    slab = lambda a, g: a[:, g * Q_BLOCK:(g + 1) * Q_BLOCK]
    qt = qt_ref[0, 0]
    pos = start + lax.broadcasted_iota(jnp.int32, (1, Q_BLOCK), 1)

    def compress_and_select(r):
        sc = jnp.dot(ck_ref[0, 0:r], qt, preferred_element_type=F32)
        blk = _row_iota((r, Q_BLOCK))
        cmask = (blk + 1) * BLOCK - 1 <= pos
        ps = []
        for g in g4:
            e, inv = _softmax_rows(slab(sc, g), cmask)
            ps.append(e * inv)
        p_all = jnp.concatenate(ps, axis=1).astype(BF16)
        if r < nb:
            p_all = jnp.concatenate([p_all, jnp.zeros((nb - r, NSA_GROUP * Q_BLOCK), BF16)], axis=0)
        oc_ref[...] = jnp.dot(cvt_ref[0], p_all, preferred_element_type=F32)
        imp = ps[0]
        for g in range(1, NSA_GROUP):
            imp = imp + ps[g]
        cur = pos // BLOCK
        forced = (blk == 0) | (blk == cur) | (blk == cur - 1)
        score = jnp.where(forced | (blk > cur), -1.0, imp)
        sel = _topk_rows(score, blk, max(min(TOP_K, nb) - 3, 0))
        sel_bias = jnp.where(forced | ((score >= 0) & (sel > 0.0)), 0.0, NEG)
        for g in g4:
            sb_ref[0:r, g * Q_BLOCK:(g + 1) * Q_BLOCK] = sel_bias

    n_qb = seq // Q_BLOCK
    whole_chunks = nb % SELECT_TIERS == 0 and (nb // SELECT_TIERS) % (kc // BLOCK) == 0
    tiers = SELECT_TIERS if whole_chunks and n_qb % SELECT_TIERS == 0 else 1
    for t in range(tiers):
        pl.when(i * tiers // n_qb == t)(functools.partial(compress_and_select, (t + 1) * nb // tiers))
    o_c = oc_ref[...]

    bpc = kc // BLOCK
    pad_rows = jnp.zeros((max(SEL_AUG - bpc, 1), NSA_GROUP * Q_BLOCK), F32)
    pad_tail = jnp.zeros((LANES - NSA_HEAD_DIM - SEL_AUG, NSA_GROUP * Q_BLOCK), BF16)

    n_parts = NSA_GROUP
    part = kc // n_parts

    def q_aug(c):
        rows = sb_ref[pl.ds(pl.multiple_of(c * bpc, bpc), bpc), :]
        if bpc < SEL_AUG:
            rows = jnp.concatenate([rows, pad_rows], axis=0)
        return jnp.concatenate([qt, rows.astype(BF16), pad_tail], axis=0)

    def scores_part(c, qa, j):
        k0 = pl.multiple_of(c * kc + j * part, part)
        return jnp.dot(ks_ref[0, pl.ds(k0, part), :], qa, preferred_element_type=F32)

    def values_dot(c, e_ref):
        k0 = pl.multiple_of(c * kc, kc)
        return jnp.dot(vt_ref[0, 0:V_ROWS, pl.ds(k0, kc)], e_ref[...], preferred_element_type=F32)

    def sweep_step(c, carry, cur, nxt, last):
        m, acc, alpha_prev = carry
        pv = values_dot(jnp.maximum(c - 1, 0), nxt[1])
        qa = None if last else q_aug(c + 1)
        k0 = pl.multiple_of(c * kc, kc)
        ms, alphas = [], []
        for g in g4:
            if not last:
                nxt[0][g * part:(g + 1) * part, :] = scores_part(c + 1, qa, g)
            sg = cur[0][:, g * Q_BLOCK:(g + 1) * Q_BLOCK]
            if last:
                sg = sg + jnp.where(k0 + _row_iota((kc, Q_BLOCK)) <= pos, 0.0, NEG)
            mo = slab(m, g)
            cur[1][:, g * Q_BLOCK:(g + 1) * Q_BLOCK] = jnp.exp2((sg - mo).astype(BF16))
            mn = mo if last else jnp.maximum(mo, _col_reduce(sg, jnp.max))
            ms.append(mn)
            alphas.append(jnp.exp2(mo - mn))
        cat = lambda xs: jnp.concatenate(xs, axis=1)
        return cat(ms), (acc + pv) * alpha_prev, cat(alphas)

    def finish(c, carry, cur, nxt):
        _, acc, _ = sweep_step(c, carry, cur, nxt, True)
        return acc + values_dot(c, cur[1])

    width = NSA_GROUP * Q_BLOCK
    even, odd = (s0_ref, e0_ref), (s1_ref, e1_ref)
    qa0 = q_aug(0)
    for g in g4:
        s0_ref[g * part:(g + 1) * part, :] = scores_part(0, qa0, g)
    e1_ref[...] = jnp.zeros(e1_ref.shape, BF16)
    bias0 = jnp.where(_row_iota((kc, Q_BLOCK)) <= pos, 0.0, NEG)
    m0 = jnp.concatenate([_col_reduce(s0_ref[:, g * Q_BLOCK:(g + 1) * Q_BLOCK] + bias0, jnp.max) for g in g4], axis=1)
    init = (m0, jnp.zeros((V_ROWS, width), F32), jnp.ones((1, width), F32))
    c_diag = start // kc

    def pair(p, carry):
        carry = sweep_step(2 * p, carry, even, odd, False)
        return sweep_step(2 * p + 1, carry, odd, even, False)

    carry = lax.fori_loop(0, c_diag // 4, lambda p, cr: pair(2 * p + 1, pair(2 * p, cr)), init)
    carry = lax.fori_loop(c_diag // 4 * 2, c_diag // 2, pair, carry)
    acc_s = lax.cond(
        c_diag % 2 == 1,
        lambda cr: finish(c_diag, sweep_step(c_diag - 1, cr, even, odd, False), odd, even),
        lambda cr: finish(c_diag, cr, even, odd),
        carry)
    def exact_step(c, carry):
        m, acc = carry
        s = jnp.concatenate([scores_part(c, q_aug(c), j) for j in range(n_parts)], axis=0)
        s = s + jnp.concatenate([jnp.where(c * kc + _row_iota((kc, Q_BLOCK)) <= pos, 0.0, NEG)] * NSA_GROUP, axis=1)
        mn = jnp.maximum(m, jnp.max(s, axis=0, keepdims=True))
        e = jnp.exp2((s - mn).astype(BF16))
        k0 = pl.multiple_of(c * kc, kc)
        pv = jnp.dot(vt_ref[0, 0:V_ROWS, pl.ds(k0, kc)], e, preferred_element_type=F32)
        return mn, jnp.exp2(m - mn) * acc + pv

    l_s = acc_s[NSA_HEAD_DIM:NSA_HEAD_DIM + 1]
    sane = (jnp.abs(acc_s) < 3e38) & (l_s >= 1.0) & (l_s < 1e30)
    acc_s = lax.cond(
        jnp.max(jnp.where(sane, 0.0, 1.0)) > 0.0,
        lambda a: lax.fori_loop(0, c_diag + 1, exact_step,
                                (jnp.full((1, width), NEG, F32), jnp.zeros((V_ROWS, width), F32)))[1],
        lambda a: a,
        acc_s)
    o_s = acc_s[0:NSA_HEAD_DIM] * (1.0 / jnp.maximum(acc_s[NSA_HEAD_DIM:NSA_HEAD_DIM + 1], 1e-30))

    w0 = pl.multiple_of(jnp.maximum(start + Q_BLOCK - wn, 0), Q_BLOCK)
    sw = jnp.dot(kw_ref[0, pl.ds(w0, wn), :], qt, preferred_element_type=F32)
    dist = pos - (w0 + _row_iota((wn, Q_BLOCK)))
    wbias = jnp.where((dist >= 0) & (dist < WINDOW), 0.0, NEG)
    ews = []
    for g in g4:
        sg = slab(sw, g) + wbias
        ews.append(jnp.exp2((sg - jnp.max(sg, axis=0, keepdims=True)).astype(BF16)))
    ow = jnp.dot(vt_ref[0, V_ROWS:2 * V_ROWS, pl.ds(w0, wn)], jnp.concatenate(ews, axis=1), preferred_element_type=F32)
    o_w = ow[0:NSA_HEAD_DIM] * (1.0 / jnp.maximum(ow[NSA_HEAD_DIM:NSA_HEAD_DIM + 1], 1e-30))

    gate = _sigmoid(g_ref[0])
    for g in g4:
        o_ref[0, g] = (gate[3 * g:3 * g + 1] * slab(o_c, g) + gate[3 * g + 1:3 * g + 2] * slab(o_s, g)
                       + gate[3 * g + 2:3 * g + 3] * slab(o_w, g))


def _split_heads_t(x, n):
    return x.reshape(n, NSA_KV_HEADS, NSA_HEAD_DIM).transpose(1, 0, 2)


def _nsa_prompt(qt, ks, kw, vst, vwt, ck, cv, small):
    s = ks.shape[0]
    assert s % Q_BLOCK == 0
    n_qb, nb = s // Q_BLOCK, s // BLOCK
    kc = min(SEL_CHUNK, s)
    wn = min(WINDOW + Q_BLOCK, s)
    assert s % kc == 0
    bpc = kc // BLOCK
    assert kc % Q_BLOCK == 0 and bpc <= SEL_AUG
    onehot = ((jnp.arange(s) // BLOCK) % bpc)[:, None] == jnp.arange(LANES - NSA_HEAD_DIM)[None, :]
    ksa = jnp.concatenate([_split_heads_t(ks, s).astype(BF16),
                           jnp.broadcast_to(onehot.astype(BF16), (NSA_KV_HEADS, s, LANES - NSA_HEAD_DIM))], axis=-1)
    kwh = _split_heads_t(kw, s).astype(BF16)
    ones_fill = jnp.zeros((NSA_KV_HEADS, V_ROWS - NSA_HEAD_DIM, s), BF16).at[:, 0, :].set(1.0)
    fm = lambda x: x.reshape(NSA_KV_HEADS, NSA_HEAD_DIM, s).astype(BF16)
    vt = jnp.concatenate([fm(vst), ones_fill, fm(vwt), ones_fill], axis=1)
    ckh = _split_heads_t(ck, nb).astype(BF16)
    cvt = _split_heads_t(cv, nb).astype(BF16).transpose(0, 2, 1)
    gt = small.T[GATE_OFF:GATE_OFF + 3 * NSA_HEADS].reshape(NSA_KV_HEADS, NSA_GROUP * 3, s)
    gt = jnp.pad(gt, ((0, 0), (0, 16 - NSA_GROUP * 3), (0, 0)))
    body = functools.partial(_nsa_prompt_body, seq=s, kc=kc, wn=wn)
    o = pl.pallas_call(
        body,
        grid=(NSA_KV_HEADS, n_qb),
        in_specs=[
            pl.BlockSpec((1, 1, NSA_HEAD_DIM, NSA_GROUP * Q_BLOCK), lambda h, i: (h, i, 0, 0)),
            pl.BlockSpec((1, s, LANES), lambda h, i: (h, 0, 0)),
            pl.BlockSpec((1, s, NSA_HEAD_DIM), lambda h, i: (h, 0, 0)),
            pl.BlockSpec((1, 2 * V_ROWS, s), lambda h, i: (h, 0, 0)),
            pl.BlockSpec((1, nb, NSA_HEAD_DIM), lambda h, i: (h, 0, 0)),
            pl.BlockSpec((1, NSA_HEAD_DIM, nb), lambda h, i: (h, 0, 0)),
            pl.BlockSpec((1, 16, Q_BLOCK), lambda h, i: (h, 0, i)),
        ],
        out_specs=pl.BlockSpec((1, NSA_GROUP, NSA_HEAD_DIM, Q_BLOCK), lambda h, i: (h, 0, 0, i)),
        out_shape=jax.ShapeDtypeStruct((NSA_KV_HEADS, NSA_GROUP, NSA_HEAD_DIM, s), F32),
        scratch_shapes=[pltpu.VMEM((nb, NSA_GROUP * Q_BLOCK), F32)]
        + [pltpu.VMEM((kc, NSA_GROUP * Q_BLOCK), F32)] * 2 + [pltpu.VMEM((kc, NSA_GROUP * Q_BLOCK), BF16)] * 2
        + [pltpu.VMEM((NSA_HEAD_DIM, NSA_GROUP * Q_BLOCK), F32)],
        compiler_params=pltpu.CompilerParams(dimension_semantics=("arbitrary", "arbitrary"), vmem_limit_bytes=VMEM_LIMIT),
        name="nsa_prompt",
    )(qt, ksa, kwh, vt, ckh, cvt, gt)
    return o.transpose(3, 0, 1, 2).reshape(s, NSA_WIDTH)


def _dot(a, b):
    return jnp.dot(a, b, precision=HI, preferred_element_type=F32)


def _dot_nt(a, b):
    return lax.dot_general(a, b, (((1,), (1,)), ((), ())), precision=HI, preferred_element_type=F32)


def _dot_tn(a, b):
    return lax.dot_general(a, b, (((0,), (0,)), ((), ())), precision=HI, preferred_element_type=F32)


def _dot1_nt(a, b):
    return lax.dot_general(a, b, (((1,), (1,)), ((), ())), preferred_element_type=F32)


def _split16(a):
    hi = a.astype(BF16)
    return hi, (a - hi.astype(F32)).astype(BF16)


def _dot3(a, b):
    ah, al = _split16(a)
    bh, bl = _split16(b)
    d = lambda x, y: jnp.dot(x, y, preferred_element_type=F32)
    return d(ah, bh) + (d(ah, bl) + d(al, bh))


def _softplus(x):
    return jnp.maximum(x, 0.0) + jnp.log(1.0 + jnp.exp(-jnp.abs(x)))


def _unit_lower_inverses(mats, ri, ci):
    c = mats[0].shape[0]
    eye = (ri == ci).astype(F32)
    base = 8
    same = ri // base == ci // base
    ds = [jnp.where(same, a, 0.0) for a in mats]
    d2s = [_dot3(d, d) for d in ds]
    xs = [(eye - d) + _dot3(eye - d, d2) for d, d2 in zip(ds, d2s)]
    d4s = [_dot3(d2, d2) for d2 in d2s]
    xs = [x + _dot3(x, d4) for x, d4 in zip(xs, d4s)]
    b = base
    while b < c:
        band = (ri // (2 * b) == ci // (2 * b)) & (ri // b != ci // b)
        lxs = [_dot3(jnp.where(band, a, 0.0), x) for a, x in zip(mats, xs)]
        xs = [x - _dot3(x, lx) for x, lx in zip(xs, lxs)]
        b *= 2
    return xs


def _conv_silu(x, prev, w):
    rowi = _row_iota(x.shape)
    conv = None
    for j in range(CONV_WIDTH - 1, -1, -1):
        if j == 0:
            xs = x
        else:
            xs = jnp.where(rowi < j, pltpu.roll(prev, j, axis=0), pltpu.roll(x, j, axis=0))
        term = xs * w[CONV_WIDTH - 1 - j:CONV_WIDTH - j]
        conv = term if conv is None else conv + term
    return _silu(conv)


def _l2n(x):
    return x * lax.rsqrt(jnp.sum(x * x, axis=-1, keepdims=True) + EPS)


GDN_STEP_CHUNKS = 4


def _gdn_chunk_body(qkv_ref, sm_ref, smt_ref, prev0_ref, cw_ref, ar_ref, dr_ref, at_ref, dt_ref,
                    u_ref, w_ref, qe_ref, kd_ref, in_ref, dl_ref, xp_ref):
    cs = GDN_CHUNK

    @pl.when(pl.program_id(0) == 0)
    def _():
        xp_ref[...] = prev0_ref[...]

    x = qkv_ref[...]
    act = _conv_silu(x, xp_ref[...], cw_ref[...])
    xp_ref[...] = x
    sm = sm_ref[...]
    ri = _row_iota((cs, cs))
    ci = lax.broadcasted_iota(jnp.int32, (cs, cs), 1)
    tri = (ri >= ci).astype(F32)
    tri_t = (ri <= ci).astype(F32)
    g_all = -jnp.exp(ar_ref[...]) * _softplus(sm + dr_ref[...])
    beta_all = _sigmoid(sm)
    scale = GDN_HEAD_DIM ** -0.5
    a_mats, rhs, where = [], [], []
    for c in range(GDN_STEP_CHUNKS):
        rows = slice(c * cs, (c + 1) * cs)
        gc_cols = _dot(tri, g_all[rows])
        g_rows = -jnp.exp(at_ref[...]) * _softplus(smt_ref[c] + dt_ref[...])
        gc_rows = _dot(g_rows, tri_t)
        intras, dls = [], []
        for h in range(GDN_HEADS):
            hs = slice(h * GDN_HEAD_DIM, (h + 1) * GDN_HEAD_DIM)
            q = _l2n(act[rows, hs]) * scale
            k = _l2n(act[rows, GDN_WIDTH + h * GDN_HEAD_DIM:GDN_WIDTH + (h + 1) * GDN_HEAD_DIM])
            v = act[rows, 2 * GDN_WIDTH + h * GDN_HEAD_DIM:2 * GDN_WIDTH + (h + 1) * GDN_HEAD_DIM]
            beta = beta_all[rows, BETA_OFF + h:BETA_OFF + h + 1]
            gc = gc_cols[:, ALPHA_OFF + h:ALPHA_OFF + h + 1]
            gr = gc_rows[GDN_HEADS + h:GDN_HEADS + h + 1, :]
            g_last = gc[cs - 1:cs, :]
            decay = jnp.where(ri >= ci, jnp.exp(gc - gr), 0.0)
            kb = k * beta
            k16 = k.astype(BF16)
            a_mats.append(jnp.where(ri > ci, _dot1_nt(kb.astype(BF16), k16) * decay, 0.0))
            rhs.append(jnp.concatenate([v * beta, kb * jnp.exp(gc)], axis=1))
            where.append((rows, hs))
            qe_ref[rows, hs] = (q * jnp.exp(gc)).astype(BF16)
            kd_ref[rows, hs] = (k * jnp.exp(g_last - gc)).astype(BF16)
            intras.append((_dot1_nt(q.astype(BF16), k16) * decay).astype(BF16))
            dls.append(jnp.broadcast_to(jnp.exp(g_last), (1, LANES)))
        in_ref[rows, :] = jnp.concatenate(intras, axis=1)
        dl_ref[c] = jnp.concatenate(dls + [jnp.zeros((2 * GDN_HEADS - GDN_HEADS, LANES), F32)], axis=0)
    for t, r, (rows, hs) in zip(_unit_lower_inverses(a_mats, ri, ci), rhs, where):
        sol = _dot3(t, r)
        u_ref[rows, hs] = sol[:, :GDN_HEAD_DIM]
        w_ref[rows, hs] = sol[:, GDN_HEAD_DIM:].astype(BF16)


def _gdn_scan_body(u_ref, w_ref, qe_ref, kd_ref, in_ref, dl_ref, z_ref, s0_ref, gn_ref, o_ref, sf_ref, s_ref):
    cs = GDN_CHUNK
    i = pl.program_id(0)

    @pl.when(i == 0)
    def _():
        s_ref[...] = s0_ref[...]

    heads = range(GDN_HEADS)
    hsl = [slice(h * GDN_HEAD_DIM, (h + 1) * GDN_HEAD_DIM) for h in heads]
    tn = (((0,), (0,)), ((), ()))
    ss = [s_ref[h] for h in heads]
    for c in range(GDN_STEP_CHUNKS):
        rows = slice(c * cs, (c + 1) * cs)
        s16 = [s.astype(BF16) for s in ss]
        v16 = [(u_ref[rows, hsl[h]] - jnp.dot(w_ref[rows, hsl[h]], s16[h], preferred_element_type=F32)).astype(BF16)
               for h in heads]
        ss = [ss[h] * dl_ref[c, h:h + 1, :] + lax.dot_general(kd_ref[rows, hsl[h]], v16[h], tn, preferred_element_type=F32)
              for h in heads]
        for h in heads:
            o = (jnp.dot(qe_ref[rows, hsl[h]], s16[h], preferred_element_type=F32)
                 + jnp.dot(in_ref[rows, h * cs:(h + 1) * cs], v16[h], preferred_element_type=F32))
            o_ref[rows, hsl[h]] = _rms(o, gn_ref[...]) * _silu(z_ref[rows, hsl[h]])
    for h in heads:
        s_ref[h] = ss[h]

    @pl.when(i == pl.num_programs(0) - 1)
    def _():
        sf_ref[...] = s_ref[...]


def _gdn_consts(a_log, dt_bias):
    lane = jnp.zeros((1, LANES), F32)
    ar = lane.at[0, ALPHA_OFF:ALPHA_OFF + GDN_HEADS].set(a_log)
    dr = lane.at[0, ALPHA_OFF:ALPHA_OFF + GDN_HEADS].set(dt_bias)
    col = jnp.zeros((2 * GDN_HEADS, 1), F32)
    at = jnp.broadcast_to(col.at[GDN_HEADS:, 0].set(a_log), (2 * GDN_HEADS, GDN_CHUNK))
    dt = jnp.broadcast_to(col.at[GDN_HEADS:, 0].set(dt_bias), (2 * GDN_HEADS, GDN_CHUNK))
    return ar, dr, at, dt


def _gdn_prompt(qkv, z, small, conv_prev, s0, conv_w, a_log, dt_bias, gdn_norm):
    n = qkv.shape[0]
    cs = GDN_CHUNK
    rows = GDN_STEP_CHUNKS * cs
    assert n % rows == 0
    nc = n // cs
    smt = small.T[BETA_OFF:BETA_OFF + 2 * GDN_HEADS].reshape(2 * GDN_HEADS, nc, cs).transpose(1, 0, 2)
    prev0 = jnp.pad(conv_prev, ((rows - (CONV_WIDTH - 1), 0), (0, 0)))
    ar, dr, at, dt = _gdn_consts(a_log, dt_bias)
    gn = gdn_norm.reshape(1, GDN_HEAD_DIM)
    row = lambda w: pl.BlockSpec((rows, w), lambda i: (i, 0))
    per_chunk = pl.BlockSpec((GDN_STEP_CHUNKS, 2 * GDN_HEADS, LANES), lambda i: (i, 0, 0))
    const = lambda a: pl.BlockSpec(a.shape, lambda i: (0,) * a.ndim)
    params = pltpu.CompilerParams(dimension_semantics=("arbitrary",), vmem_limit_bytes=VMEM_LIMIT)
    wide = lambda dt_: jax.ShapeDtypeStruct((n, GDN_WIDTH), dt_)
    u, w, qe, kd, intra, dl = pl.pallas_call(
        _gdn_chunk_body,
        grid=(n // rows,),
        in_specs=[row(3 * GDN_WIDTH), row(SMALL_COLS),
                  pl.BlockSpec((GDN_STEP_CHUNKS, 2 * GDN_HEADS, cs), lambda i: (i, 0, 0)),
                  const(prev0), const(conv_w), const(ar), const(dr), const(at), const(dt)],
        out_specs=[row(GDN_WIDTH)] * 4 + [row(GDN_HEADS * cs), per_chunk],
        out_shape=[wide(F32), wide(BF16), wide(BF16), wide(BF16), jax.ShapeDtypeStruct((n, GDN_HEADS * cs), BF16),
                   jax.ShapeDtypeStruct((nc, 2 * GDN_HEADS, LANES), F32)],
        scratch_shapes=[pltpu.VMEM((rows, 3 * GDN_WIDTH), F32)],
        compiler_params=params,
        name="gdn_chunks",
    )(qkv, small, smt, prev0, conv_w, ar, dr, at, dt)
    return pl.pallas_call(
        _gdn_scan_body,
        grid=(n // rows,),
        in_specs=[row(GDN_WIDTH)] * 4 + [row(GDN_HEADS * cs), per_chunk, row(GDN_WIDTH), const(s0), const(gn)],
        out_specs=[row(GDN_WIDTH), const(s0)],
        out_shape=[wide(F32), jax.ShapeDtypeStruct(s0.shape, F32)],
        scratch_shapes=[pltpu.VMEM(s0.shape, F32)],
        compiler_params=params,
        name="gdn_scan",
    )(u, w, qe, kd, intra, dl, z, s0, gn)


def _gdn_sample_pre_body(x0_ref, x1_ref, x2_ref, x3_ref, sm_ref, cw_ref, ar_ref, dr_ref, q_ref, k_ref, v_ref, a_ref, b_ref):
    w = cw_ref[...]
    conv = x0_ref[...] * w[0:1] + x1_ref[...] * w[1:2] + x2_ref[...] * w[2:3] + x3_ref[...] * w[3:4]
    act = _silu(conv)
    sm = sm_ref[...]
    a_cols = jnp.exp(-jnp.exp(ar_ref[...]) * _softplus(sm + dr_ref[...]))
    b_cols = _sigmoid(sm)
    rows = sm.shape[0]
    for h in range(GDN_HEADS):
        hs = slice(h * GDN_HEAD_DIM, (h + 1) * GDN_HEAD_DIM)
        q_ref[:, hs] = _l2n(act[:, hs]) * GDN_HEAD_DIM ** -0.5
        k_ref[:, hs] = _l2n(act[:, GDN_WIDTH + h * GDN_HEAD_DIM:GDN_WIDTH + (h + 1) * GDN_HEAD_DIM])
        a_ref[:, hs] = jnp.broadcast_to(a_cols[:, ALPHA_OFF + h:ALPHA_OFF + h + 1], (rows, GDN_HEAD_DIM))
        b_ref[:, hs] = jnp.broadcast_to(b_cols[:, BETA_OFF + h:BETA_OFF + h + 1], (rows, GDN_HEAD_DIM))
    v_ref[...] = act[:, 2 * GDN_WIDTH:]


GDN_SAMPLE_SEQS = 8


def _gdn_sample_body(qt_ref, kt_ref, v_ref, a_ref, b_ref, z_ref, s0_ref, gn_ref, o_ref, sf_ref, *, t_len):
    gn = gn_ref[...]

    def seq(bl, carry):
        v_all, a_all, b_all, z_all = v_ref[bl], a_ref[bl], b_ref[bl], z_ref[bl]
        heads = range(GDN_HEADS)
        hsl = [slice(h * GDN_HEAD_DIM, (h + 1) * GDN_HEAD_DIM) for h in heads]
        ss = [s0_ref[bl, h] for h in heads]
        qts, kts = [qt_ref[bl, h] for h in heads], [kt_ref[bl, h] for h in heads]
        outs = [[] for _ in heads]
        for t in range(t_len):
            sks = [jnp.sum(ss[h] * kts[h][:, t:t + 1], axis=0, keepdims=True) for h in heads]
            deltas = [b_all[t:t + 1, hsl[h]] * (v_all[t:t + 1, hsl[h]] - a_all[t:t + 1, hsl[h]] * sks[h]) for h in heads]
            ss = [a_all[t:t + 1, hsl[h]] * ss[h] + kts[h][:, t:t + 1] * deltas[h] for h in heads]
            for h in heads:
                outs[h].append(jnp.sum(ss[h] * qts[h][:, t:t + 1], axis=0, keepdims=True))
        for h in heads:
            sf_ref[bl, h] = ss[h]
            o = jnp.concatenate(outs[h], axis=0)
            o_ref[bl, :, hsl[h]] = _rms(o, gn) * _silu(z_all[:, hsl[h]])
        return carry

    lax.fori_loop(0, GDN_SAMPLE_SEQS, seq, 0)


def _gdn_sample(qkv, z, small, conv_prev, s0, conv_w, a_log, dt_bias, gdn_norm):
    b, t = qkv.shape[:2]
    n = b * t
    xp = jnp.concatenate([conv_prev, qkv], axis=1)
    xs = [xp[:, j:j + t].reshape(n, 3 * GDN_WIDTH) for j in range(CONV_WIDTH)]
    ar, dr, _, _ = _gdn_consts(a_log, dt_bias)
    tm = min(256, n)
    assert n % tm == 0
    row = lambda w: pl.BlockSpec((tm, w), lambda i: (i, 0))
    const = lambda a: pl.BlockSpec(a.shape, lambda i: (0,) * a.ndim)
    q, k, v, av, bv = pl.pallas_call(
        _gdn_sample_pre_body,
        grid=(n // tm,),
        in_specs=[row(3 * GDN_WIDTH)] * 4 + [row(SMALL_COLS), const(conv_w), const(ar), const(dr)],
        out_specs=[row(GDN_WIDTH)] * 5,
        out_shape=[jax.ShapeDtypeStruct((n, GDN_WIDTH), F32)] * 5,
        compiler_params=pltpu.CompilerParams(dimension_semantics=("arbitrary",), vmem_limit_bytes=VMEM_LIMIT),
        name="gdn_sample_pre",
    )(*xs, small.reshape(n, SMALL_COLS), conv_w, ar, dr)
    tr = lambda x: x.reshape(b, t, GDN_HEADS, GDN_HEAD_DIM).transpose(0, 2, 3, 1)
    r3 = lambda x: x.reshape(b, t, GDN_WIDTH)
    bb = GDN_SAMPLE_SEQS
    assert b % bb == 0
    gn = gdn_norm.reshape(1, GDN_HEAD_DIM)
    tspec = pl.BlockSpec((bb, GDN_HEADS, GDN_HEAD_DIM, t), lambda i: (i, 0, 0, 0))
    rspec = pl.BlockSpec((bb, t, GDN_WIDTH), lambda i: (i, 0, 0))
    sspec = pl.BlockSpec((bb, GDN_HEADS, GDN_HEAD_DIM, GDN_HEAD_DIM), lambda i: (i, 0, 0, 0))
    o, sf = pl.pallas_call(
        functools.partial(_gdn_sample_body, t_len=t),
        grid=(b // bb,),
        in_specs=[tspec, tspec, rspec, rspec, rspec, rspec, sspec, const(gn)],
        out_specs=[rspec, sspec],
        out_shape=[jax.ShapeDtypeStruct((b, t, GDN_WIDTH), F32), jax.ShapeDtypeStruct(s0.shape, F32)],
        compiler_params=pltpu.CompilerParams(dimension_semantics=("arbitrary",), vmem_limit_bytes=VMEM_LIMIT),
        name="gdn_sample",
    )(tr(q), tr(k), r3(v), r3(av), r3(bv), z, s0, gn)
    return o, sf


PAGES_PER_STEP = 64
T_PAD = 8
Q_ROWS = NSA_KV_HEADS * NSA_GROUP * T_PAD
NEW_PAD = LANES


def _nsa_sample_body(pt_ref, ckp_ref, cvp_ref, skp_ref, svp_ref, q_ref, kn_ref, vn_ref, wk_ref, wv_ref, kwn_ref, vwn_ref,
                     wck_ref, wcv_ref, g_ref, ex_ref, ext_ref, o_ref,
                     buf_ref, sem_ref, ck_ref, cv_ref, selm_ref, oc_ref, m_ref, l_ref, acc_ref, ka_ref,
                     *, n_groups, t_len, win_len, page_base):
    b, ph, g = pl.program_id(0), pl.program_id(1), pl.program_id(2)
    nb_, np_, ng_ = pl.num_programs(0), pl.num_programs(1), pl.num_programs(2)
    step = (b * np_ + ph) * ng_ + g
    pps = PAGES_PER_STEP
    rows_per_step = pps * PAGE_SIZE
    blocks_per_step = rows_per_step // BLOCK

    def copies(pool_k, pool_v, bb, gg, slot):
        out = []
        for p in range(pps):
            page = page_base + pt_ref[bb, gg * pps + p]
            for kv, pool in enumerate((pool_k, pool_v)):
                out.append(pltpu.make_async_copy(pool.at[page], buf_ref.at[slot, kv, :, pl.ds(p * PAGE_SIZE, PAGE_SIZE)],
                                                 sem_ref.at[slot, kv]))
        return out

    def start(bb, pp, gg, slot):
        @pl.when(pp == 0)
        def _():
            for cp in copies(ckp_ref, cvp_ref, bb, gg, slot):
                cp.start()

        @pl.when(pp == 1)
        def _():
            for cp in copies(skp_ref, svp_ref, bb, gg, slot):
                cp.start()

    slot = step % 2

    @pl.when(step == 0)
    def _():
        start(b, ph, g, slot)
        for gg in range(n_groups):
            ka_ref[gg, NSA_KV_WIDTH:, :] = ex_ref[gg]

    nxt = step + 1

    @pl.when(nxt < nb_ * np_ * ng_)
    def _():
        start(nxt // (np_ * ng_), (nxt // ng_) % np_, nxt % ng_, 1 - slot)

    for cp in copies(ckp_ref, cvp_ref, b, g, slot):
        cp.wait()

    q = q_ref[0]
    rowt = _row_iota((Q_ROWS, LANES)) % T_PAD
    lane = lax.broadcasted_iota(jnp.int32, (Q_ROWS, LANES), 1)
    new_mask = (lane <= rowt) & (lane < t_len)

    nt = (((1,), (1,)), ((), ()))

    kvw = NSA_KV_WIDTH

    def weighted(kv, w_ref):
        w = w_ref[...]
        return jnp.concatenate([buf_ref[slot, kv, :, p * PAGE_SIZE:(p + 1) * PAGE_SIZE] * w for p in range(pps)],
                               axis=1).astype(BF16)

    @pl.when(ph == 0)
    def _():
        @pl.when(g == 0)
        def _():
            ck_ref[...] = jnp.zeros(ck_ref.shape, F32)
            cv_ref[...] = jnp.zeros(cv_ref.shape, F32)

        ck_ref[...] += jnp.dot(weighted(0, wck_ref), ext_ref[g], preferred_element_type=F32)
        cv_ref[...] += jnp.dot(weighted(1, wcv_ref), ext_ref[g], preferred_element_type=F32)

        @pl.when(g == n_groups - 1)
        def _():
            nb = ck_ref.shape[1]
            s = jnp.dot(q, ck_ref[...].astype(BF16), preferred_element_type=F32)
            m = jnp.max(s, axis=1, keepdims=True)
            e = jnp.exp(s - m)
            p = e / jnp.maximum(jnp.sum(e, axis=1, keepdims=True), 1e-30)
            oc_ref[...] = lax.dot_general(p.astype(BF16), cv_ref[...].astype(BF16), nt, preferred_element_type=F32)
            imps = []
            for h in range(NSA_KV_HEADS):
                r = h * NSA_GROUP * T_PAD
                imp = p[r:r + T_PAD]
                for gg in range(1, NSA_GROUP):
                    imp = imp + p[r + gg * T_PAD:r + (gg + 1) * T_PAD]
                imps.append(imp)
            imps.append(jnp.zeros((nb - NSA_KV_HEADS * T_PAD, nb), F32))
            imp_t = jnp.concatenate(imps, axis=0).T
            bl = _row_iota((nb, nb))
            score = jnp.where((bl == 0) | (bl == nb - 1), FORCED_SCORE, imp_t)
            sel = _topk_rows(score, bl, min(TOP_K - 1, nb)).T
            for h in range(NSA_KV_HEADS):
                for gg in range(NSA_GROUP):
                    r = (h * NSA_GROUP + gg) * T_PAD
                    selm_ref[r:r + T_PAD, :] = sel[h * T_PAD:(h + 1) * T_PAD]
            m_ref[...] = jnp.full(m_ref.shape, NEG, F32)
            l_ref[...] = jnp.zeros(l_ref.shape, F32)
            acc_ref[...] = jnp.zeros(acc_ref.shape, F32)

    def flash(s, mask, vals):
        m_old = m_ref[...]
        if mask is None:
            m_new = jnp.maximum(m_old, jnp.max(s, axis=1, keepdims=True))
            e = jnp.exp(s - m_new)
        else:
            m_new = jnp.maximum(m_old, jnp.max(jnp.where(mask, s, NEG), axis=1, keepdims=True))
            e = jnp.where(mask, jnp.exp(s - m_new), 0.0)
        a = jnp.exp(m_old - m_new)
        m_ref[...] = m_new
        l_ref[...] = a * l_ref[...] + jnp.sum(e, axis=1, keepdims=True)
        acc_ref[...] = a * acc_ref[...] + lax.dot_general(e.astype(BF16), vals, nt, preferred_element_type=F32)

    @pl.when(ph == 1)
    def _():
        ka_ref[g, 0:kvw, :] = buf_ref[slot, 0].astype(BF16)
        vb = buf_ref[slot, 1].astype(BF16)
        q_aug = jnp.concatenate([q, jnp.where(selm_ref[...] > 0.0, 0.0, NEG).astype(BF16)], axis=1)
        s = jnp.dot(q_aug, ka_ref[g], preferred_element_type=F32)
        flash(s, None, vb)

        @pl.when(g == n_groups - 1)
        def _():
            flash(jnp.dot(q, kn_ref[0], preferred_element_type=F32), new_mask, vn_ref[0])
            o_s = acc_ref[...] / jnp.maximum(l_ref[...], 1e-30)
            wk = wk_ref[0].astype(BF16)
            s_old = jnp.dot(q, wk, preferred_element_type=F32)
            s_new = jnp.dot(q, kwn_ref[0], preferred_element_type=F32)
            j = lax.broadcasted_iota(jnp.int32, (Q_ROWS, win_len), 1)
            old_mask = j > win_len - WINDOW + _row_iota((Q_ROWS, win_len)) % T_PAD
            mw = jnp.maximum(jnp.max(jnp.where(old_mask, s_old, NEG), axis=1, keepdims=True),
                             jnp.max(jnp.where(new_mask, s_new, NEG), axis=1, keepdims=True))
            e_old = jnp.where(old_mask, jnp.exp(s_old - mw), 0.0)
            e_new = jnp.where(new_mask, jnp.exp(s_new - mw), 0.0)
            lw = jnp.sum(e_old, axis=1, keepdims=True) + jnp.sum(e_new, axis=1, keepdims=True)
            o_w = (lax.dot_general(e_old.astype(BF16), wv_ref[0].astype(BF16), nt, preferred_element_type=F32)
                   + lax.dot_general(e_new.astype(BF16), vwn_ref[0], nt, preferred_element_type=F32)) / jnp.maximum(lw, 1e-30)
            gate = _sigmoid(g_ref[0])
            o_ref[0] = gate[:, 0:1] * oc_ref[...] + gate[:, 1:2] * o_s + gate[:, 2:3] * o_w


def _feature_major(x):
    lead = x.shape[:-3]
    n = len(lead)
    return x.transpose(*range(n), n + 1, n + 2, n).reshape(*lead, NSA_KV_WIDTH, x.shape[-3])


def _nsa_sample(q, ks, vs, kw, vw, gates, pools, layer, win_k, win_v, page_table, w_cmp_k, w_cmp_v):
    b, t = q.shape[:2]
    n_pages = page_table.shape[1]
    assert t <= T_PAD and t <= BLOCK and PAGE_SIZE % BLOCK == 0
    pps = PAGES_PER_STEP
    assert n_pages % pps == 0
    n_groups = n_pages // pps
    nb = n_pages * PAGE_SIZE // BLOCK
    assert nb % LANES == 0
    wb = win_k.shape[1]
    n_pool = pools[0].shape[1]
    flat = [_feature_major(p).reshape(p.shape[0] * n_pool, NSA_KV_WIDTH, PAGE_SIZE) for p in pools]
    wt = lambda w: jnp.tile(w.T, (NSA_KV_HEADS, PAGE_SIZE // BLOCK))
    wck2, wcv2 = wt(w_cmp_k), wt(w_cmp_v)
    scale = NSA_HEAD_DIM ** -0.5
    q5 = (q * scale).reshape(b, t, NSA_KV_HEADS, NSA_GROUP, NSA_HEAD_DIM).transpose(0, 2, 3, 1, 4)
    q5 = jnp.pad(q5, ((0, 0), (0, 0), (0, 0), (0, T_PAD - t), (0, 0)))
    own = jnp.eye(NSA_KV_HEADS, dtype=F32)[None, :, None, None, :, None]
    qbd = (q5[:, :, :, :, None, :] * own).reshape(b, Q_ROWS, NSA_KV_WIDTH).astype(BF16)
    padn = lambda x: jnp.pad(x, ((0, 0), (0, NEW_PAD - t), (0, 0))).astype(BF16).transpose(0, 2, 1)
    g5 = gates.reshape(b, t, NSA_KV_HEADS, NSA_GROUP, 3).transpose(0, 2, 3, 1, 4)
    g5 = jnp.pad(g5, ((0, 0), (0, 0), (0, 0), (0, T_PAD - t), (0, 0))).reshape(b, Q_ROWS, 3)
    blocks_per_step = pps * PAGE_SIZE // BLOCK
    key_blk = jnp.arange(pps * PAGE_SIZE) // BLOCK
    expand = (jnp.arange(nb)[None, :, None] == (jnp.arange(n_groups)[:, None, None] * blocks_per_step + key_blk[None, None, :]))
    expand = expand.astype(BF16)
    expand_t = expand.transpose(0, 2, 1)
    per_b = lambda shape: pl.BlockSpec((1,) + shape, lambda i, p, g, pt: (i, 0, 0))
    const = lambda a: pl.BlockSpec(a.shape, lambda i, p, g, pt: (0,) * a.ndim)
    hbm = pl.BlockSpec(memory_space=pl.ANY)
    body = functools.partial(_nsa_sample_body, n_groups=n_groups, t_len=t, win_len=wb, page_base=layer * n_pool)
    o = pl.pallas_call(
        body,
        grid_spec=pltpu.PrefetchScalarGridSpec(
            num_scalar_prefetch=1,
            grid=(b, 2, n_groups),
            in_specs=[hbm, hbm, hbm, hbm, per_b((Q_ROWS, NSA_KV_WIDTH)), per_b((NSA_KV_WIDTH, NEW_PAD)), per_b((NSA_KV_WIDTH, NEW_PAD)),
                      per_b((NSA_KV_WIDTH, wb)), per_b((NSA_KV_WIDTH, wb)), per_b((NSA_KV_WIDTH, NEW_PAD)), per_b((NSA_KV_WIDTH, NEW_PAD)),
                      const(wck2), const(wcv2), per_b((Q_ROWS, 3)), const(expand), const(expand_t)],
            out_specs=per_b((Q_ROWS, NSA_KV_WIDTH)),
            scratch_shapes=[
                pltpu.VMEM((2, 2, NSA_KV_WIDTH, pps * PAGE_SIZE), F32),
                pltpu.SemaphoreType.DMA((2, 2)),
                pltpu.VMEM((NSA_KV_WIDTH, nb), F32), pltpu.VMEM((NSA_KV_WIDTH, nb), F32),
                pltpu.VMEM((Q_ROWS, nb), F32), pltpu.VMEM((Q_ROWS, NSA_KV_WIDTH), F32),
                pltpu.VMEM((Q_ROWS, 1), F32), pltpu.VMEM((Q_ROWS, 1), F32), pltpu.VMEM((Q_ROWS, NSA_KV_WIDTH), F32),
                pltpu.VMEM((n_groups, NSA_KV_WIDTH + nb, pps * PAGE_SIZE), BF16),
            ],
        ),
        out_shape=jax.ShapeDtypeStruct((b, Q_ROWS, NSA_KV_WIDTH), F32),
        compiler_params=pltpu.CompilerParams(dimension_semantics=("arbitrary",) * 3, vmem_limit_bytes=VMEM_LIMIT),
        name="nsa_sample",
    )(page_table, *flat, qbd, padn(ks), padn(vs), _feature_major(win_k), _feature_major(win_v), padn(kw), padn(vw),
      wck2, wcv2, g5, expand, expand_t)
    o6 = o.reshape(b, NSA_KV_HEADS, NSA_GROUP, T_PAD, NSA_KV_HEADS, NSA_HEAD_DIM)[:, :, :, :t]
    o5 = jnp.stack([o6[:, h, :, :, h, :] for h in range(NSA_KV_HEADS)], axis=1)
    return o5.transpose(0, 3, 1, 2, 4).reshape(b, t, NSA_WIDTH)


def kernel(x_prompt, x_sample, cache_cmp_k, cache_cmp_v, cache_sel_k, cache_sel_v, state_win_k, state_win_v, state_conv,
           state_ssm, page_table, norm_mix_pre, norm_mix_post, norm_ffn_pre, norm_ffn_post, w_in, w_cmp_k, w_cmp_v,
           nsa_out_norm, conv_w, a_log, dt_bias, gdn_norm, w_out, w_up, w_down):
    depth = w_in.shape[0]
    bp, sp, _ = x_prompt.shape
    bs, ts, _ = x_sample.shape
    past = page_table.shape[1] * PAGE_SIZE
    pos_p = jnp.arange(sp)
    pos_s = past + jnp.arange(ts)
    pools = (cache_cmp_k, cache_cmp_v, cache_sel_k, cache_sel_v)
    kv5 = lambda x, b, n: x.reshape(b, n, NSA_KV_HEADS, NSA_HEAD_DIM)
    y_p, y_s = x_prompt, x_sample
    p_states, s_states = [], []
    for l in range(depth):
        w_perm = _permute_w_in(w_in[l]).astype(BF16)
        wck2 = jnp.tile(w_cmp_k[l], (1, LANES // NSA_HEAD_DIM))
        wcv2 = jnp.tile(w_cmp_v[l], (1, LANES // NSA_HEAD_DIM))
        post_w = (nsa_out_norm[l], w_out[l].astype(BF16), norm_mix_post[l], norm_ffn_pre[l], w_up[l].astype(BF16),
                  w_down[l].astype(BF16), norm_ffn_post[l])
        gdn_w = (conv_w[l], a_log[l], dt_bias[l], gdn_norm[l])

        wb = min(WINDOW, sp)
        conv0 = jnp.zeros((CONV_WIDTH - 1, 3 * GDN_WIDTH), F32)
        ssm0 = jnp.zeros((GDN_HEADS, GDN_HEAD_DIM, GDN_HEAD_DIM), F32)
        ys, sts = [], []
        for b in range(bp):
            x2 = y_p[b]
            (_, _, _, ks, _, kw, _, qkv, z, small, ck, cv, qt, kct, vct, kst, vst, kwt, vwt) = _in_proj(
                x2, pos_p, norm_mix_pre[l], w_perm, wck2, wcv2, feature_major=True)
            o_nsa = _nsa_prompt(qt, ks, kw, vst, vwt, ck, cv, small)
            o_gdn, ssm = _gdn_prompt(qkv, z, small, conv0, ssm0, *gdn_w)
            ys.append(_post(o_nsa, o_gdn, x2, *post_w))
            conv_state = jnp.concatenate([conv0, qkv], axis=0)[-(CONV_WIDTH - 1):]
            rows_major = lambda x: x.reshape(NSA_KV_HEADS, NSA_HEAD_DIM, sp).transpose(2, 0, 1)
            sts.append((rows_major(kct), rows_major(vct), rows_major(kst), rows_major(vst),
                        rows_major(kwt)[sp - wb:], rows_major(vwt)[sp - wb:], conv_state, ssm))
        y_p = jnp.stack(ys)
        p_states.append(tuple(jnp.stack(t) for t in zip(*sts)))

        n = bs * ts
        q, kc, vc, ks, vs, kw, vw, qkv, z, small, _, _ = _in_proj(
            jnp.pad(y_s.reshape(n, D_MODEL), ((0, (-n) % BLOCK), (0, 0))), jnp.pad(jnp.tile(pos_s, bs), (0, (-n) % BLOCK)),
            norm_mix_pre[l], w_perm, wck2, wcv2)
        r3 = lambda x: x[:n].reshape(bs, ts, -1)
        q, kc, vc, ks, vs, kw, vw, qkv, z, small = map(r3, (q, kc, vc, ks, vs, kw, vw, qkv, z, small))
        win_k, win_v = state_win_k[l], state_win_v[l]
        o_nsa = _nsa_sample(q, ks, vs, kw, vw, small[..., GATE_OFF:GATE_OFF + 3 * NSA_HEADS], pools, l, win_k, win_v,
                            page_table, w_cmp_k[l], w_cmp_v[l])
        o_gdn, ssm = _gdn_sample(qkv, z, small, state_conv[l], state_ssm[l], *gdn_w)
        y_s = _post(o_nsa.reshape(n, NSA_WIDTH), o_gdn.reshape(n, GDN_WIDTH), y_s.reshape(n, D_MODEL), *post_w).reshape(bs, ts, D_MODEL)
        conv_state = jnp.concatenate([state_conv[l], qkv], axis=1)[:, -(CONV_WIDTH - 1):]
        new_wk = jnp.concatenate([win_k, kv5(kw, bs, ts)], axis=1)[:, ts:]
        new_wv = jnp.concatenate([win_v, kv5(vw, bs, ts)], axis=1)[:, ts:]
        s_states.append((kv5(kc, bs, ts), kv5(vc, bs, ts), kv5(ks, bs, ts), kv5(vs, bs, ts), new_wk, new_wv, conv_state, ssm))
    p_out = [jnp.stack(t) for t in zip(*p_states)]
    s_out = [jnp.stack(t) for t in zip(*s_states)]
    return (y_p, y_s, *p_out, *s_out)
```

```python
import functools
import math

import jax
import jax.numpy as jnp
import numpy as np
from jax import lax
from jax.experimental import pallas as pl
from jax.experimental.pallas import tpu as pltpu

D_MODEL = 1024
PAGE_SIZE = 128
NSA_HEAD_DIM = 64
NSA_HEADS = 8
NSA_KV_HEADS = 2
NSA_GROUP = NSA_HEADS // NSA_KV_HEADS
NSA_WIDTH = NSA_HEADS * NSA_HEAD_DIM
NSA_KV_WIDTH = NSA_KV_HEADS * NSA_HEAD_DIM
BLOCK = 64
TOP_K = 16
WINDOW = 512
Q_BLOCK = 128
FORCED_SCORE = NSA_GROUP + 1.0
ROPE_THETA = 500000.0
ROPE_DIM = NSA_HEAD_DIM // 4
GDN_HEAD_DIM = 128
GDN_HEADS = 4
GDN_WIDTH = GDN_HEADS * GDN_HEAD_DIM
CONV_WIDTH = 4
GDN_CHUNK = 64
D_FF = 4 * D_MODEL
EPS = 1e-6
SPLIT_SIZES = (NSA_WIDTH,) + (NSA_KV_WIDTH,) * 6 + (3 * NSA_HEADS, 3 * GDN_WIDTH, GDN_WIDTH, GDN_HEADS, GDN_HEADS)

LANES = 128
SMALL_COLS = LANES
GATE_OFF, BETA_OFF, ALPHA_OFF = 0, 3 * NSA_HEADS, 3 * NSA_HEADS + GDN_HEADS
PROJ_COLS = NSA_WIDTH + 6 * NSA_KV_WIDTH + 3 * GDN_WIDTH + GDN_WIDTH + SMALL_COLS
VMEM_LIMIT = 56 * 1024 * 1024
NEG = -1e30
F32 = jnp.float32
BF16 = jnp.bfloat16
HI = lax.Precision.HIGHEST


def _rms(x, w):
    return x * lax.rsqrt(jnp.mean(x * x, axis=-1, keepdims=True) + EPS) * w


def _sigmoid(x):
    return 1.0 / (1.0 + jnp.exp(-x))


def _silu(x):
    return x * _sigmoid(x)


def _rope_slab(x, c, sa, sb):
    half = ROPE_DIM // 2
    return x * c + pltpu.roll(x, LANES - half, axis=1) * sa + pltpu.roll(x, half, axis=1) * sb


Q_SCALE = NSA_HEAD_DIM ** -0.5 * math.log2(math.e)


def _in_proj_body(x_ref, nw_ref, w_ref, rc_ref, rsa_ref, rsb_ref, wk_ref, wv_ref,
                  q_ref, kc_ref, vc_ref, ks_ref, vs_ref, kw_ref, vw_ref, qkv_ref, z_ref, sm_ref, ck_ref, cv_ref,
                  *fm_refs, feature_major):
    h = _rms(x_ref[...], nw_ref[...])
    p = jnp.dot(h.astype(BF16), w_ref[...], preferred_element_type=F32)
    c, sa, sb = rc_ref[...], rsa_ref[...], rsb_ref[...]
    rows = p.shape[0]
    off = 0
    for j in range(NSA_WIDTH // LANES):
        qs = _rope_slab(p[:, off:off + LANES], c, sa, sb)
        q_ref[:, j * LANES:(j + 1) * LANES] = qs
        if feature_major:
            qst = (qs * Q_SCALE).T.astype(BF16)
            heads_per_slab = LANES // NSA_HEAD_DIM
            kvh = j * heads_per_slab // NSA_GROUP
            for half in range(heads_per_slab):
                g = (j * heads_per_slab + half) % NSA_GROUP
                for blk in range(rows // Q_BLOCK):
                    fm_refs[0][kvh, blk, :, g * Q_BLOCK:(g + 1) * Q_BLOCK] = qst[
                        half * NSA_HEAD_DIM:(half + 1) * NSA_HEAD_DIM, blk * Q_BLOCK:(blk + 1) * Q_BLOCK]
        off += LANES
    kc = _rope_slab(p[:, off:off + LANES], c, sa, sb)
    kc_ref[...] = kc
    vc = p[:, off + LANES:off + 2 * LANES]
    vc_ref[...] = vc
    ks = _rope_slab(p[:, off + 2 * LANES:off + 3 * LANES], c, sa, sb)
    ks_ref[...] = ks
    vs = p[:, off + 3 * LANES:off + 4 * LANES]
    vs_ref[...] = vs
    kw = _rope_slab(p[:, off + 4 * LANES:off + 5 * LANES], c, sa, sb)
    kw_ref[...] = kw
    vw = p[:, off + 5 * LANES:off + 6 * LANES]
    vw_ref[...] = vw
    if feature_major:
        for ref, slab in zip(fm_refs[1:], (kc, vc, ks, vs, kw, vw)):
            ref[...] = slab.T
    off += 6 * LANES
    qkv_ref[...] = p[:, off:off + 3 * GDN_WIDTH]
    off += 3 * GDN_WIDTH
    z_ref[...] = p[:, off:off + GDN_WIDTH]
    off += GDN_WIDTH
    sm_ref[...] = p[:, off:off + SMALL_COLS]
    ck_ref[...] = jnp.sum(kc.reshape(rows // BLOCK, BLOCK, LANES) * wk_ref[...][None], axis=1)
    cv_ref[...] = jnp.sum(vc.reshape(rows // BLOCK, BLOCK, LANES) * wv_ref[...][None], axis=1)


def _permute_w_in(w_in):
    pts = np.cumsum((0,) + SPLIT_SIZES)
    seg = [w_in[:, int(pts[i]):int(pts[i + 1])] for i in range(len(SPLIT_SIZES))]
    q, kc, vc, ks, vs, kw, vw, gates, qkv, z, beta, alpha = seg
    small = jnp.concatenate([gates, beta, alpha], axis=1)
    small = jnp.pad(small, ((0, 0), (0, SMALL_COLS - small.shape[1])))
    return jnp.concatenate([q, kc, vc, ks, vs, kw, vw, qkv, z, small], axis=1)


def _rope_tables(pos):
    half = ROPE_DIM // 2
    inv_freq = ROPE_THETA ** (-jnp.arange(half, dtype=F32) / half)
    ang = pos.astype(F32)[:, None] * inv_freq[None, :]
    cos, sin = jnp.cos(ang), jnp.sin(ang)
    n = pos.shape[0]
    one = jnp.ones((n, NSA_HEAD_DIM - ROPE_DIM), F32)
    zero = jnp.zeros((n, NSA_HEAD_DIM - ROPE_DIM), F32)
    zh = jnp.zeros((n, half), F32)
    c = jnp.concatenate([cos, cos, one], axis=1)
    sa = jnp.concatenate([-sin, zh, zero], axis=1)
    sb = jnp.concatenate([zh, sin, zero], axis=1)
    return tuple(jnp.tile(t, (1, LANES // NSA_HEAD_DIM)) for t in (c, sa, sb))


def _in_proj(x2d, pos, norm_w, w_perm, wk2, wv2, feature_major=False):
    n = x2d.shape[0]
    tm = min(512, n)
    assert n % tm == 0 and tm % BLOCK == 0
    rc, rsa, rsb = _rope_tables(pos)
    row = lambda w: pl.BlockSpec((tm, w), lambda i: (i, 0))
    const = lambda a: pl.BlockSpec(a.shape, lambda i: (0,) * a.ndim)
    nw = norm_w.reshape(1, D_MODEL)
    widths = (NSA_WIDTH,) + (LANES,) * 6 + (3 * GDN_WIDTH, GDN_WIDTH, SMALL_COLS)
    out_shape = [jax.ShapeDtypeStruct((n, w), F32) for w in widths]
    out_shape += [jax.ShapeDtypeStruct((n // BLOCK, LANES), F32)] * 2
    out_specs = [row(w) for w in widths] + [pl.BlockSpec((tm // BLOCK, LANES), lambda i: (i, 0))] * 2
    if feature_major:
        assert tm % Q_BLOCK == 0
        qb = tm // Q_BLOCK
        out_shape += [jax.ShapeDtypeStruct((NSA_KV_HEADS, n // Q_BLOCK, NSA_HEAD_DIM, NSA_GROUP * Q_BLOCK), BF16)]
        out_specs += [pl.BlockSpec((NSA_KV_HEADS, qb, NSA_HEAD_DIM, NSA_GROUP * Q_BLOCK), lambda i: (0, i, 0, 0))]
        out_shape += [jax.ShapeDtypeStruct((LANES, n), F32)] * 6
        out_specs += [pl.BlockSpec((LANES, tm), lambda i: (0, i))] * 6
    return pl.pallas_call(
        functools.partial(_in_proj_body, feature_major=feature_major),
        grid=(n // tm,),
        in_specs=[row(D_MODEL), const(nw), const(w_perm), row(LANES), row(LANES), row(LANES), const(wk2), const(wv2)],
        out_specs=out_specs,
        out_shape=out_shape,
        compiler_params=pltpu.CompilerParams(dimension_semantics=("arbitrary",), vmem_limit_bytes=VMEM_LIMIT),
        name="in_proj",
    )(x2d, nw, w_perm, rc, rsa, rsb, wk2, wv2)


FF_CHUNK = 1024


def _post_body(on_ref, og_ref, x_ref, nn_ref, wo_ref, nmp_ref, nfp_ref, wu_ref, wd_ref, nfo_ref, y_ref, *, nsa_feature_major):
    if nsa_feature_major:
        ont = on_ref[...]
        o_nsa = (ont * lax.rsqrt(jnp.mean(ont * ont, axis=0, keepdims=True) + EPS) * nn_ref[...]).T
    else:
        o_nsa = _rms(on_ref[...], nn_ref[...])
    mix = jnp.concatenate([o_nsa, og_ref[...]], axis=1).astype(BF16)
    y = jnp.dot(mix, wo_ref[...], preferred_element_type=F32)
    y1 = x_ref[...] + _rms(y, nmp_ref[...])
    h = _rms(y1, nfp_ref[...]).astype(BF16)
    acc = jnp.zeros_like(y1)
    for c in range(D_FF // FF_CHUNK):
        u = jnp.dot(h, wu_ref[:, c * FF_CHUNK:(c + 1) * FF_CHUNK], preferred_element_type=F32)
        u = jnp.square(jnp.maximum(u, 0.0))
        acc = acc + jnp.dot(u.astype(BF16), wd_ref[c * FF_CHUNK:(c + 1) * FF_CHUNK, :], preferred_element_type=F32)
    y_ref[...] = y1 + _rms(acc, nfo_ref[...])


def _post(o_nsa, o_gdn, x2d, nsa_norm, w_out, n_mix_post, n_ffn_pre, w_up, w_down, n_ffn_post, nsa_feature_major=False):
    n = x2d.shape[0]
    tm = min(256, n)
    assert n % tm == 0
    row = lambda w: pl.BlockSpec((tm, w), lambda i: (i, 0))
    const = lambda a: pl.BlockSpec(a.shape, lambda i: (0,) * a.ndim)
    r = lambda v: v.reshape(1, -1)
    nn = nsa_norm.reshape(-1, 1) if nsa_feature_major else r(nsa_norm)
    nsa_spec = pl.BlockSpec((NSA_WIDTH, tm), lambda i: (0, i)) if nsa_feature_major else row(NSA_WIDTH)
    args = (o_nsa, o_gdn, x2d, nn, w_out, r(n_mix_post), r(n_ffn_pre), w_up, w_down, r(n_ffn_post))
    return pl.pallas_call(
        functools.partial(_post_body, nsa_feature_major=nsa_feature_major),
        grid=(n // tm,),
        in_specs=[nsa_spec, row(GDN_WIDTH), row(D_MODEL)] + [const(a) for a in args[3:]],
        out_specs=row(D_MODEL),
        out_shape=jax.ShapeDtypeStruct((n, D_MODEL), F32),
        compiler_params=pltpu.CompilerParams(dimension_semantics=("arbitrary",), vmem_limit_bytes=VMEM_LIMIT),
        name="out_proj_ffn",
    )(*args)


SEL_CHUNK = 512
SEL_AUG = 16
V_ROWS = NSA_HEAD_DIM + 16
SELECT_TIERS = 4


def _row_iota(shape):
    return lax.broadcasted_iota(jnp.int32, shape, 0)


def _topk_rows(score, blk, k):
    nb = score.shape[0]
    sel = jnp.zeros(score.shape, F32)
    for _ in range(k):
        m = jnp.max(score, axis=0, keepdims=True)
        idx = jnp.min(jnp.where(score == m, blk, nb), axis=0, keepdims=True)
        pick = blk == idx
        sel = jnp.where(pick, 1.0, sel)
        score = jnp.where(pick, -jnp.inf, score)
    return sel


REDUCE_WAYS = 8


def _col_reduce(x, op):
    rows = x.shape[0]
    if rows % (REDUCE_WAYS * 8) == 0 and rows > REDUCE_WAYS * 8:
        x = op(x.reshape(rows // (REDUCE_WAYS * 8), REDUCE_WAYS * 8, x.shape[1]), axis=0)
    return op(x, axis=0, keepdims=True)


def _softmax_rows(s, mask):
    m = jnp.max(jnp.where(mask, s, NEG), axis=0, keepdims=True)
    e = jnp.where(mask, jnp.exp2(s - m), 0.0)
    return e, 1.0 / jnp.maximum(jnp.sum(e, axis=0, keepdims=True), 1e-30)


def _nsa_prompt_body(qt_ref, ks_ref, kw_ref, vt_ref, ck_ref, cvt_ref, g_ref, o_ref,
                     sb_ref, s0_ref, s1_ref, e0_ref, e1_ref, oc_ref, *, seq, kc, wn):
    i = pl.program_id(1)
    start = i * Q_BLOCK
    nb = seq // BLOCK
    g4 = range(NSA_GROUP)
    slab = lambda a, g: a[:, g * Q_BLOCK:(g + 1) * Q_BLOCK]
    qt = qt_ref[0, 0]
    pos = start + lax.broadcasted_iota(jnp.int32, (1, Q_BLOCK), 1)

    def compress_and_select(r):
        sc = jnp.dot(ck_ref[0, 0:r], qt, preferred_element_type=F32)
        blk = _row_iota((r, Q_BLOCK))
        cmask = (blk + 1) * BLOCK - 1 <= pos
        ps = []
        for g in g4:
            e, inv = _softmax_rows(slab(sc, g), cmask)
            ps.append(e * inv)
        p_all = jnp.concatenate(ps, axis=1).astype(BF16)
        if r < nb:
            p_all = jnp.concatenate([p_all, jnp.zeros((nb - r, NSA_GROUP * Q_BLOCK), BF16)], axis=0)
        oc_ref[...] = jnp.dot(cvt_ref[0], p_all, preferred_element_type=F32)
        imp = ps[0]
        for g in range(1, NSA_GROUP):
            imp = imp + ps[g]
        cur = pos // BLOCK
        forced = (blk == 0) | (blk == cur) | (blk == cur - 1)
        score = jnp.where(forced | (blk > cur), -1.0, imp)
        sel = _topk_rows(score, blk, max(min(TOP_K, nb) - 3, 0))
        sel_bias = jnp.where(forced | ((score >= 0) & (sel > 0.0)), 0.0, NEG)
        for g in g4:
            sb_ref[0:r, g * Q_BLOCK:(g + 1) * Q_BLOCK] = sel_bias

    n_qb = seq // Q_BLOCK
    whole_chunks = nb % SELECT_TIERS == 0 and (nb // SELECT_TIERS) % (kc // BLOCK) == 0
    tiers = SELECT_TIERS if whole_chunks and n_qb % SELECT_TIERS == 0 else 1
    for t in range(tiers):
        pl.when(i * tiers // n_qb == t)(functools.partial(compress_and_select, (t + 1) * nb // tiers))
    o_c = oc_ref[...]

    bpc = kc // BLOCK
    pad_rows = jnp.zeros((max(SEL_AUG - bpc, 1), NSA_GROUP * Q_BLOCK), F32)
    pad_tail = jnp.zeros((LANES - NSA_HEAD_DIM - SEL_AUG, NSA_GROUP * Q_BLOCK), BF16)

    n_parts = NSA_GROUP
    part = kc // n_parts

    def q_aug(c):
        rows = sb_ref[pl.ds(pl.multiple_of(c * bpc, bpc), bpc), :]
        if bpc < SEL_AUG:
            rows = jnp.concatenate([rows, pad_rows], axis=0)
        return jnp.concatenate([qt, rows.astype(BF16), pad_tail], axis=0)

    def scores_part(c, qa, j):
        k0 = pl.multiple_of(c * kc + j * part, part)
        return jnp.dot(ks_ref[0, pl.ds(k0, part), :], qa, preferred_element_type=F32)

    def values_dot(c, e_ref):
        k0 = pl.multiple_of(c * kc, kc)
        return jnp.dot(vt_ref[0, 0:V_ROWS, pl.ds(k0, kc)], e_ref[...], preferred_element_type=F32)

    def sweep_step(c, carry, cur, nxt, last):
        m, acc, alpha_prev = carry
        pv = values_dot(jnp.maximum(c - 1, 0), nxt[1])
        qa = None if last else q_aug(c + 1)
        k0 = pl.multiple_of(c * kc, kc)
        ms, alphas = [], []
        for g in g4:
            if not last:
                nxt[0][g * part:(g + 1) * part, :] = scores_part(c + 1, qa, g)
            sg = cur[0][:, g * Q_BLOCK:(g + 1) * Q_BLOCK]
            if last:
                sg = sg + jnp.where(k0 + _row_iota((kc, Q_BLOCK)) <= pos, 0.0, NEG)
            mo = slab(m, g)
            cur[1][:, g * Q_BLOCK:(g + 1) * Q_BLOCK] = jnp.exp2((sg - mo).astype(BF16))
            mn = mo if last else jnp.maximum(mo, _col_reduce(sg, jnp.max))
            ms.append(mn)
            alphas.append(jnp.exp2(mo - mn))
        cat = lambda xs: jnp.concatenate(xs, axis=1)
        return cat(ms), (acc + pv) * alpha_prev, cat(alphas)

    def finish(c, carry, cur, nxt):
        _, acc, _ = sweep_step(c, carry, cur, nxt, True)
        return acc + values_dot(c, cur[1])

    width = NSA_GROUP * Q_BLOCK
    even, odd = (s0_ref, e0_ref), (s1_ref, e1_ref)
    qa0 = q_aug(0)
    for g in g4:
        s0_ref[g * part:(g + 1) * part, :] = scores_part(0, qa0, g)
    e1_ref[...] = jnp.zeros(e1_ref.shape, BF16)
    bias0 = jnp.where(_row_iota((kc, Q_BLOCK)) <= pos, 0.0, NEG)
    m0 = jnp.concatenate([_col_reduce(s0_ref[:, g * Q_BLOCK:(g + 1) * Q_BLOCK] + bias0, jnp.max) for g in g4], axis=1)
    init = (m0, jnp.zeros((V_ROWS, width), F32), jnp.ones((1, width), F32))
    c_diag = start // kc

    def pair(p, carry):
        carry = sweep_step(2 * p, carry, even, odd, False)
        return sweep_step(2 * p + 1, carry, odd, even, False)

    carry = lax.fori_loop(0, c_diag // 4, lambda p, cr: pair(2 * p + 1, pair(2 * p, cr)), init)
    carry = lax.fori_loop(c_diag // 4 * 2, c_diag // 2, pair, carry)
    acc_s = lax.cond(
        c_diag % 2 == 1,
        lambda cr: finish(c_diag, sweep_step(c_diag - 1, cr, even, odd, False), odd, even),
        lambda cr: finish(c_diag, cr, even, odd),
        carry)
    def exact_step(c, carry):
        m, acc = carry
        s = jnp.concatenate([scores_part(c, q_aug(c), j) for j in range(n_parts)], axis=0)
        s = s + jnp.concatenate([jnp.where(c * kc + _row_iota((kc, Q_BLOCK)) <= pos, 0.0, NEG)] * NSA_GROUP, axis=1)
        mn = jnp.maximum(m, jnp.max(s, axis=0, keepdims=True))
        e = jnp.exp2((s - mn).astype(BF16))
        k0 = pl.multiple_of(c * kc, kc)
        pv = jnp.dot(vt_ref[0, 0:V_ROWS, pl.ds(k0, kc)], e, preferred_element_type=F32)
        return mn, jnp.exp2(m - mn) * acc + pv

    l_s = acc_s[NSA_HEAD_DIM:NSA_HEAD_DIM + 1]
    sane = (jnp.abs(acc_s) < 3e38) & (l_s >= 1.0) & (l_s < 1e30)
    acc_s = lax.cond(
        jnp.max(jnp.where(sane, 0.0, 1.0)) > 0.0,
        lambda a: lax.fori_loop(0, c_diag + 1, exact_step,
                                (jnp.full((1, width), NEG, F32), jnp.zeros((V_ROWS, width), F32)))[1],
        lambda a: a,
        acc_s)
    o_s = acc_s[0:NSA_HEAD_DIM] * (1.0 / jnp.maximum(acc_s[NSA_HEAD_DIM:NSA_HEAD_DIM + 1], 1e-30))

    w0 = pl.multiple_of(jnp.maximum(start + Q_BLOCK - wn, 0), Q_BLOCK)
    sw = jnp.dot(kw_ref[0, pl.ds(w0, wn), :], qt, preferred_element_type=F32)
    dist = pos - (w0 + _row_iota((wn, Q_BLOCK)))
    wbias = jnp.where((dist >= 0) & (dist < WINDOW), 0.0, NEG)
    ews = []
    for g in g4:
        sg = slab(sw, g) + wbias
        ews.append(jnp.exp2((sg - jnp.max(sg, axis=0, keepdims=True)).astype(BF16)))
    ow = jnp.dot(vt_ref[0, V_ROWS:2 * V_ROWS, pl.ds(w0, wn)], jnp.concatenate(ews, axis=1), preferred_element_type=F32)
    o_w = ow[0:NSA_HEAD_DIM] * (1.0 / jnp.maximum(ow[NSA_HEAD_DIM:NSA_HEAD_DIM + 1], 1e-30))

    gate = _sigmoid(g_ref[0])
    for g in g4:
        o_ref[0, g] = (gate[3 * g:3 * g + 1] * slab(o_c, g) + gate[3 * g + 1:3 * g + 2] * slab(o_s, g)
                       + gate[3 * g + 2:3 * g + 3] * slab(o_w, g))


def _split_heads_t(x, n):
    return x.reshape(n, NSA_KV_HEADS, NSA_HEAD_DIM).transpose(1, 0, 2)


def _nsa_prompt(qt, ks, kw, vst, vwt, ck, cv, small):
    s = ks.shape[0]
    assert s % Q_BLOCK == 0
    n_qb, nb = s // Q_BLOCK, s // BLOCK
    kc = min(SEL_CHUNK, s)
    wn = min(WINDOW + Q_BLOCK, s)
    assert s % kc == 0
    bpc = kc // BLOCK
    assert kc % Q_BLOCK == 0 and bpc <= SEL_AUG
    onehot = ((jnp.arange(s) // BLOCK) % bpc)[:, None] == jnp.arange(LANES - NSA_HEAD_DIM)[None, :]
    ksa = jnp.concatenate([_split_heads_t(ks, s).astype(BF16),
                           jnp.broadcast_to(onehot.astype(BF16), (NSA_KV_HEADS, s, LANES - NSA_HEAD_DIM))], axis=-1)
    kwh = _split_heads_t(kw, s).astype(BF16)
    ones_fill = jnp.zeros((NSA_KV_HEADS, V_ROWS - NSA_HEAD_DIM, s), BF16).at[:, 0, :].set(1.0)
    fm = lambda x: x.reshape(NSA_KV_HEADS, NSA_HEAD_DIM, s).astype(BF16)
    vt = jnp.concatenate([fm(vst), ones_fill, fm(vwt), ones_fill], axis=1)
    ckh = _split_heads_t(ck, nb).astype(BF16)
    cvt = _split_heads_t(cv, nb).astype(BF16).transpose(0, 2, 1)
    gt = small.T[GATE_OFF:GATE_OFF + 3 * NSA_HEADS].reshape(NSA_KV_HEADS, NSA_GROUP * 3, s)
    gt = jnp.pad(gt, ((0, 0), (0, 16 - NSA_GROUP * 3), (0, 0)))
    body = functools.partial(_nsa_prompt_body, seq=s, kc=kc, wn=wn)
    o = pl.pallas_call(
        body,
        grid=(NSA_KV_HEADS, n_qb),
        in_specs=[
            pl.BlockSpec((1, 1, NSA_HEAD_DIM, NSA_GROUP * Q_BLOCK), lambda h, i: (h, i, 0, 0)),
            pl.BlockSpec((1, s, LANES), lambda h, i: (h, 0, 0)),
            pl.BlockSpec((1, s, NSA_HEAD_DIM), lambda h, i: (h, 0, 0)),
            pl.BlockSpec((1, 2 * V_ROWS, s), lambda h, i: (h, 0, 0)),
            pl.BlockSpec((1, nb, NSA_HEAD_DIM), lambda h, i: (h, 0, 0)),
            pl.BlockSpec((1, NSA_HEAD_DIM, nb), lambda h, i: (h, 0, 0)),
            pl.BlockSpec((1, 16, Q_BLOCK), lambda h, i: (h, 0, i)),
        ],
        out_specs=pl.BlockSpec((1, NSA_GROUP, NSA_HEAD_DIM, Q_BLOCK), lambda h, i: (h, 0, 0, i)),
        out_shape=jax.ShapeDtypeStruct((NSA_KV_HEADS, NSA_GROUP, NSA_HEAD_DIM, s), F32),
        scratch_shapes=[pltpu.VMEM((nb, NSA_GROUP * Q_BLOCK), F32)]
        + [pltpu.VMEM((kc, NSA_GROUP * Q_BLOCK), F32)] * 2 + [pltpu.VMEM((kc, NSA_GROUP * Q_BLOCK), BF16)] * 2
        + [pltpu.VMEM((NSA_HEAD_DIM, NSA_GROUP * Q_BLOCK), F32)],
        compiler_params=pltpu.CompilerParams(dimension_semantics=("arbitrary", "arbitrary"), vmem_limit_bytes=VMEM_LIMIT),
        name="nsa_prompt",
    )(qt, ksa, kwh, vt, ckh, cvt, gt)
    return o.reshape(NSA_WIDTH, s)


def _dot(a, b):
    return jnp.dot(a, b, precision=HI, preferred_element_type=F32)


def _dot_nt(a, b):
    return lax.dot_general(a, b, (((1,), (1,)), ((), ())), precision=HI, preferred_element_type=F32)


def _dot_tn(a, b):
    return lax.dot_general(a, b, (((0,), (0,)), ((), ())), precision=HI, preferred_element_type=F32)


def _dot1_nt(a, b):
    return lax.dot_general(a, b, (((1,), (1,)), ((), ())), preferred_element_type=F32)


def _split16(a):
    hi = a.astype(BF16)
    return hi, (a - hi.astype(F32)).astype(BF16)


def _dot3(a, b):
    return jnp.dot(a.astype(BF16), b.astype(BF16), preferred_element_type=F32)


def _softplus(x):
    return jnp.maximum(x, 0.0) + jnp.log(1.0 + jnp.exp(-jnp.abs(x)))


def _unit_lower_inverses(mats, ri, ci):
    c = mats[0].shape[0]
    eye = (ri == ci).astype(F32)
    base = 8
    same = ri // base == ci // base
    ds = [jnp.where(same, a, 0.0) for a in mats]
    d2s = [_dot3(d, d) for d in ds]
    xs = [(eye - d) + _dot3(eye - d, d2) for d, d2 in zip(ds, d2s)]
    d4s = [_dot3(d2, d2) for d2 in d2s]
    xs = [x + _dot3(x, d4) for x, d4 in zip(xs, d4s)]
    b = base
    while b < c:
        band = (ri // (2 * b) == ci // (2 * b)) & (ri // b != ci // b)
        lxs = [_dot3(jnp.where(band, a, 0.0), x) for a, x in zip(mats, xs)]
        xs = [x - _dot3(x, lx) for x, lx in zip(xs, lxs)]
        b *= 2
    return xs


SUBLANES = 8


def _conv_silu(x, prev, w):
    rowi = _row_iota((SUBLANES, x.shape[1]))
    conv = None
    for j in range(CONV_WIDTH - 1, -1, -1):
        if j == 0:
            xs = x
        else:
            xr = pltpu.roll(x, j, axis=0)
            head = jnp.where(rowi < j, pltpu.roll(prev, j, axis=0), xr[0:SUBLANES])
            xs = jnp.concatenate([head, xr[SUBLANES:]], axis=0)
        term = xs * w[CONV_WIDTH - 1 - j:CONV_WIDTH - j]
        conv = term if conv is None else conv + term
    return _silu(conv)


def _l2n(x):
    return x * lax.rsqrt(jnp.sum(x * x, axis=-1, keepdims=True) + EPS)


GDN_STEP_CHUNKS = 4


def _gdn_chunk_body(qkv_ref, sm_ref, smt_ref, prev0_ref, cw_ref, ar_ref, dr_ref, at_ref, dt_ref,
                    u_ref, w_ref, qe_ref, kd_ref, in_ref, dl_ref, xp_ref):
    cs = GDN_CHUNK

    @pl.when(pl.program_id(0) == 0)
    def _():
        xp_ref[...] = prev0_ref[...]

    x = qkv_ref[...]
    act = _conv_silu(x, xp_ref[...], cw_ref[...])
    xp_ref[...] = x[x.shape[0] - SUBLANES:]
    sm = sm_ref[...]
    ri = _row_iota((cs, cs))
    ci = lax.broadcasted_iota(jnp.int32, (cs, cs), 1)
    tri = (ri >= ci).astype(F32)
    tri_t = (ri <= ci).astype(F32)
    g_all = -jnp.exp(ar_ref[...]) * _softplus(sm + dr_ref[...])
    beta_all = _sigmoid(sm)
    scale = GDN_HEAD_DIM ** -0.5
    a_mats, rhs, where = [], [], []
    for c in range(GDN_STEP_CHUNKS):
        rows = slice(c * cs, (c + 1) * cs)
        gc_cols = _dot(tri, g_all[rows])
        g_rows = -jnp.exp(at_ref[...]) * _softplus(smt_ref[c] + dt_ref[...])
        gc_rows = _dot(g_rows, tri_t)
        intras, dls = [], []
        for h in range(GDN_HEADS):
            hs = slice(h * GDN_HEAD_DIM, (h + 1) * GDN_HEAD_DIM)
            q = _l2n(act[rows, hs]) * scale
            k = _l2n(act[rows, GDN_WIDTH + h * GDN_HEAD_DIM:GDN_WIDTH + (h + 1) * GDN_HEAD_DIM])
            v = act[rows, 2 * GDN_WIDTH + h * GDN_HEAD_DIM:2 * GDN_WIDTH + (h + 1) * GDN_HEAD_DIM]
            beta = beta_all[rows, BETA_OFF + h:BETA_OFF + h + 1]
            gc = gc_cols[:, ALPHA_OFF + h:ALPHA_OFF + h + 1]
            gr = gc_rows[GDN_HEADS + h:GDN_HEADS + h + 1, :]
            g_last = gc[cs - 1:cs, :]
            decay = jnp.where(ri >= ci, jnp.exp(gc - gr), 0.0)
            kb = k * beta
            k16 = k.astype(BF16)
            a_mats.append(jnp.where(ri > ci, _dot1_nt(kb.astype(BF16), k16) * decay, 0.0))
            rhs.append(jnp.concatenate([v * beta, kb * jnp.exp(gc)], axis=1))
            where.append((rows, hs))
            qe_ref[rows, hs] = (q * jnp.exp(gc)).astype(BF16)
            kd_ref[rows, hs] = (k * jnp.exp(g_last - gc)).astype(BF16)
            intras.append((_dot1_nt(q.astype(BF16), k16) * decay).astype(BF16))
            dls.append(jnp.broadcast_to(jnp.exp(g_last), (1, LANES)))
        in_ref[rows, :] = jnp.concatenate(intras, axis=1)
        dl_ref[c] = jnp.concatenate(dls + [jnp.zeros((2 * GDN_HEADS - GDN_HEADS, LANES), F32)], axis=0)
    for t, r, (rows, hs) in zip(_unit_lower_inverses(a_mats, ri, ci), rhs, where):
        sol = _dot3(t, r)
        u_ref[rows, hs] = sol[:, :GDN_HEAD_DIM]
        w_ref[rows, hs] = sol[:, GDN_HEAD_DIM:].astype(BF16)


def _gdn_scan_body(u_ref, w_ref, qe_ref, kd_ref, in_ref, dl_ref, z_ref, s0_ref, gn_ref, o_ref, sf_ref, s_ref):
    cs = GDN_CHUNK
    i = pl.program_id(0)

    @pl.when(i == 0)
    def _():
        s_ref[...] = s0_ref[...]

    heads = range(GDN_HEADS)
    hsl = [slice(h * GDN_HEAD_DIM, (h + 1) * GDN_HEAD_DIM) for h in heads]
    tn = (((0,), (0,)), ((), ()))
    ss = [s_ref[h] for h in heads]
    for c in range(GDN_STEP_CHUNKS):
        rows = slice(c * cs, (c + 1) * cs)
        s16 = [s.astype(BF16) for s in ss]
        v16 = [(u_ref[rows, hsl[h]] - jnp.dot(w_ref[rows, hsl[h]], s16[h], preferred_element_type=F32)).astype(BF16)
               for h in heads]
        ss = [ss[h] * dl_ref[c, h:h + 1, :] + lax.dot_general(kd_ref[rows, hsl[h]], v16[h], tn, preferred_element_type=F32)
              for h in heads]
        for h in heads:
            o = (jnp.dot(qe_ref[rows, hsl[h]], s16[h], preferred_element_type=F32)
                 + jnp.dot(in_ref[rows, h * cs:(h + 1) * cs], v16[h], preferred_element_type=F32))
            o_ref[rows, hsl[h]] = _rms(o, gn_ref[...]) * _silu(z_ref[rows, hsl[h]])
    for h in heads:
        s_ref[h] = ss[h]

    @pl.when(i == pl.num_programs(0) - 1)
    def _():
        sf_ref[...] = s_ref[...]


def _gdn_consts(a_log, dt_bias):
    lane = jnp.zeros((1, LANES), F32)
    ar = lane.at[0, ALPHA_OFF:ALPHA_OFF + GDN_HEADS].set(a_log)
    dr = lane.at[0, ALPHA_OFF:ALPHA_OFF + GDN_HEADS].set(dt_bias)
    col = jnp.zeros((2 * GDN_HEADS, 1), F32)
    at = jnp.broadcast_to(col.at[GDN_HEADS:, 0].set(a_log), (2 * GDN_HEADS, GDN_CHUNK))
    dt = jnp.broadcast_to(col.at[GDN_HEADS:, 0].set(dt_bias), (2 * GDN_HEADS, GDN_CHUNK))
    return ar, dr, at, dt


def _gdn_prompt(qkv, z, small, conv_prev, s0, conv_w, a_log, dt_bias, gdn_norm):
    n = qkv.shape[0]
    cs = GDN_CHUNK
    rows = GDN_STEP_CHUNKS * cs
    assert n % rows == 0
    nc = n // cs
    smt = small.T[BETA_OFF:BETA_OFF + 2 * GDN_HEADS].reshape(2 * GDN_HEADS, nc, cs).transpose(1, 0, 2)
    prev0 = jnp.pad(conv_prev, ((SUBLANES - (CONV_WIDTH - 1), 0), (0, 0)))
    ar, dr, at, dt = _gdn_consts(a_log, dt_bias)
    gn = gdn_norm.reshape(1, GDN_HEAD_DIM)
    row = lambda w: pl.BlockSpec((rows, w), lambda i: (i, 0))
    per_chunk = pl.BlockSpec((GDN_STEP_CHUNKS, 2 * GDN_HEADS, LANES), lambda i: (i, 0, 0))
    const = lambda a: pl.BlockSpec(a.shape, lambda i: (0,) * a.ndim)
    params = pltpu.CompilerParams(dimension_semantics=("arbitrary",), vmem_limit_bytes=VMEM_LIMIT)
    wide = lambda dt_: jax.ShapeDtypeStruct((n, GDN_WIDTH), dt_)
    u, w, qe, kd, intra, dl = pl.pallas_call(
        _gdn_chunk_body,
        grid=(n // rows,),
        in_specs=[row(3 * GDN_WIDTH), row(SMALL_COLS),
                  pl.BlockSpec((GDN_STEP_CHUNKS, 2 * GDN_HEADS, cs), lambda i: (i, 0, 0)),
                  const(prev0), const(conv_w), const(ar), const(dr), const(at), const(dt)],
        out_specs=[row(GDN_WIDTH)] * 4 + [row(GDN_HEADS * cs), per_chunk],
        out_shape=[wide(F32), wide(BF16), wide(BF16), wide(BF16), jax.ShapeDtypeStruct((n, GDN_HEADS * cs), BF16),
                   jax.ShapeDtypeStruct((nc, 2 * GDN_HEADS, LANES), F32)],
        scratch_shapes=[pltpu.VMEM((SUBLANES, 3 * GDN_WIDTH), F32)],
        compiler_params=params,
        name="gdn_chunks",
    )(qkv, small, smt, prev0, conv_w, ar, dr, at, dt)
    return pl.pallas_call(
        _gdn_scan_body,
        grid=(n // rows,),
        in_specs=[row(GDN_WIDTH)] * 4 + [row(GDN_HEADS * cs), per_chunk, row(GDN_WIDTH), const(s0), const(gn)],
        out_specs=[row(GDN_WIDTH), const(s0)],
        out_shape=[wide(F32), jax.ShapeDtypeStruct(s0.shape, F32)],
        scratch_shapes=[pltpu.VMEM(s0.shape, F32)],
        compiler_params=params,
        name="gdn_scan",
    )(u, w, qe, kd, intra, dl, z, s0, gn)


def _gdn_sample_pre_body(x0_ref, x1_ref, x2_ref, x3_ref, sm_ref, cw_ref, ar_ref, dr_ref, q_ref, k_ref, v_ref, a_ref, b_ref):
    w = cw_ref[...]
    conv = x0_ref[...] * w[0:1] + x1_ref[...] * w[1:2] + x2_ref[...] * w[2:3] + x3_ref[...] * w[3:4]
    act = _silu(conv)
    sm = sm_ref[...]
    a_cols = jnp.exp(-jnp.exp(ar_ref[...]) * _softplus(sm + dr_ref[...]))
    b_cols = _sigmoid(sm)
    rows = sm.shape[0]
    for h in range(GDN_HEADS):
        hs = slice(h * GDN_HEAD_DIM, (h + 1) * GDN_HEAD_DIM)
        q_ref[:, hs] = _l2n(act[:, hs]) * GDN_HEAD_DIM ** -0.5
        k_ref[:, hs] = _l2n(act[:, GDN_WIDTH + h * GDN_HEAD_DIM:GDN_WIDTH + (h + 1) * GDN_HEAD_DIM])
        a_ref[:, hs] = jnp.broadcast_to(a_cols[:, ALPHA_OFF + h:ALPHA_OFF + h + 1], (rows, GDN_HEAD_DIM))
        b_ref[:, hs] = jnp.broadcast_to(b_cols[:, BETA_OFF + h:BETA_OFF + h + 1], (rows, GDN_HEAD_DIM))
    v_ref[...] = act[:, 2 * GDN_WIDTH:]


GDN_SAMPLE_SEQS = 8


def _gdn_sample_body(qt_ref, kt_ref, v_ref, a_ref, b_ref, z_ref, s0_ref, gn_ref, o_ref, sf_ref, *, t_len):
    gn = gn_ref[...]

    def seq(bl, carry):
        v_all, a_all, b_all, z_all = v_ref[bl], a_ref[bl], b_ref[bl], z_ref[bl]
        heads = range(GDN_HEADS)
        hsl = [slice(h * GDN_HEAD_DIM, (h + 1) * GDN_HEAD_DIM) for h in heads]
        ss = [s0_ref[bl, h] for h in heads]
        qts, kts = [qt_ref[bl, h] for h in heads], [kt_ref[bl, h] for h in heads]
        outs = [[] for _ in heads]
        for t in range(t_len):
            sks = [jnp.sum(ss[h] * kts[h][:, t:t + 1], axis=0, keepdims=True) for h in heads]
            deltas = [b_all[t:t + 1, hsl[h]] * (v_all[t:t + 1, hsl[h]] - a_all[t:t + 1, hsl[h]] * sks[h]) for h in heads]
            ss = [a_all[t:t + 1, hsl[h]] * ss[h] + kts[h][:, t:t + 1] * deltas[h] for h in heads]
            for h in heads:
                outs[h].append(jnp.sum(ss[h] * qts[h][:, t:t + 1], axis=0, keepdims=True))
        for h in heads:
            sf_ref[bl, h] = ss[h]
            o = jnp.concatenate(outs[h], axis=0)
            o_ref[bl, :, hsl[h]] = _rms(o, gn) * _silu(z_all[:, hsl[h]])
        return carry

    lax.fori_loop(0, GDN_SAMPLE_SEQS, seq, 0)


def _gdn_sample(qkv, z, small, conv_prev, s0, conv_w, a_log, dt_bias, gdn_norm):
    b, t = qkv.shape[:2]
    n = b * t
    xp = jnp.concatenate([conv_prev, qkv], axis=1)
    xs = [xp[:, j:j + t].reshape(n, 3 * GDN_WIDTH) for j in range(CONV_WIDTH)]
    ar, dr, _, _ = _gdn_consts(a_log, dt_bias)
    tm = min(256, n)
    assert n % tm == 0
    row = lambda w: pl.BlockSpec((tm, w), lambda i: (i, 0))
    const = lambda a: pl.BlockSpec(a.shape, lambda i: (0,) * a.ndim)
    q, k, v, av, bv = pl.pallas_call(
        _gdn_sample_pre_body,
        grid=(n // tm,),
        in_specs=[row(3 * GDN_WIDTH)] * 4 + [row(SMALL_COLS), const(conv_w), const(ar), const(dr)],
        out_specs=[row(GDN_WIDTH)] * 5,
        out_shape=[jax.ShapeDtypeStruct((n, GDN_WIDTH), F32)] * 5,
        compiler_params=pltpu.CompilerParams(dimension_semantics=("arbitrary",), vmem_limit_bytes=VMEM_LIMIT),
        name="gdn_sample_pre",
    )(*xs, small.reshape(n, SMALL_COLS), conv_w, ar, dr)
    tr = lambda x: x.reshape(b, t, GDN_HEADS, GDN_HEAD_DIM).transpose(0, 2, 3, 1)
    r3 = lambda x: x.reshape(b, t, GDN_WIDTH)
    bb = GDN_SAMPLE_SEQS
    assert b % bb == 0
    gn = gdn_norm.reshape(1, GDN_HEAD_DIM)
    tspec = pl.BlockSpec((bb, GDN_HEADS, GDN_HEAD_DIM, t), lambda i: (i, 0, 0, 0))
    rspec = pl.BlockSpec((bb, t, GDN_WIDTH), lambda i: (i, 0, 0))
    sspec = pl.BlockSpec((bb, GDN_HEADS, GDN_HEAD_DIM, GDN_HEAD_DIM), lambda i: (i, 0, 0, 0))
    o, sf = pl.pallas_call(
        functools.partial(_gdn_sample_body, t_len=t),
        grid=(b // bb,),
        in_specs=[tspec, tspec, rspec, rspec, rspec, rspec, sspec, const(gn)],
        out_specs=[rspec, sspec],
        out_shape=[jax.ShapeDtypeStruct((b, t, GDN_WIDTH), F32), jax.ShapeDtypeStruct(s0.shape, F32)],
        compiler_params=pltpu.CompilerParams(dimension_semantics=("arbitrary",), vmem_limit_bytes=VMEM_LIMIT),
        name="gdn_sample",
    )(tr(q), tr(k), r3(v), r3(av), r3(bv), z, s0, gn)
    return o, sf


PAGES_PER_STEP = 64
T_PAD = 8
Q_ROWS = NSA_KV_HEADS * NSA_GROUP * T_PAD
NEW_PAD = LANES


def _nsa_sample_body(pt_ref, ckp_ref, cvp_ref, skp_ref, svp_ref, q_ref, kn_ref, vn_ref, wk_ref, wv_ref, kwn_ref, vwn_ref,
                     wck_ref, wcv_ref, g_ref, ex_ref, ext_ref, o_ref,
                     buf_ref, sem_ref, ck_ref, cv_ref, selm_ref, oc_ref, m_ref, l_ref, acc_ref, ka_ref,
                     *, n_groups, t_len, win_len, page_base):
    b, ph, g = pl.program_id(0), pl.program_id(1), pl.program_id(2)
    nb_, np_, ng_ = pl.num_programs(0), pl.num_programs(1), pl.num_programs(2)
    step = (b * np_ + ph) * ng_ + g
    pps = PAGES_PER_STEP
    rows_per_step = pps * PAGE_SIZE
    blocks_per_step = rows_per_step // BLOCK

    def copies(pool_k, pool_v, bb, gg, slot):
        out = []
        for p in range(pps):
            page = page_base + pt_ref[bb, gg * pps + p]
            for kv, pool in enumerate((pool_k, pool_v)):
                out.append(pltpu.make_async_copy(pool.at[page], buf_ref.at[slot, kv, :, pl.ds(p * PAGE_SIZE, PAGE_SIZE)],
                                                 sem_ref.at[slot, kv]))
        return out

    def start(bb, pp, gg, slot):
        @pl.when(pp == 0)
        def _():
            for cp in copies(ckp_ref, cvp_ref, bb, gg, slot):
                cp.start()

        @pl.when(pp == 1)
        def _():
            for cp in copies(skp_ref, svp_ref, bb, gg, slot):
                cp.start()

    slot = step % 2

    @pl.when(step == 0)
    def _():
        start(b, ph, g, slot)
        for gg in range(n_groups):
            ka_ref[gg, NSA_KV_WIDTH:, :] = ex_ref[gg]

    nxt = step + 1

    @pl.when(nxt < nb_ * np_ * ng_)
    def _():
        start(nxt // (np_ * ng_), (nxt // ng_) % np_, nxt % ng_, 1 - slot)

    for cp in copies(ckp_ref, cvp_ref, b, g, slot):
        cp.wait()

    q = q_ref[0]
    rowt = _row_iota((Q_ROWS, LANES)) % T_PAD
    lane = lax.broadcasted_iota(jnp.int32, (Q_ROWS, LANES), 1)
    new_mask = (lane <= rowt) & (lane < t_len)

    nt = (((1,), (1,)), ((), ()))

    kvw = NSA_KV_WIDTH

    def weighted(kv, w_ref):
        w = w_ref[...]
        return jnp.concatenate([buf_ref[slot, kv, :, p * PAGE_SIZE:(p + 1) * PAGE_SIZE] * w for p in range(pps)],
                               axis=1).astype(BF16)

    @pl.when(ph == 0)
    def _():
        @pl.when(g == 0)
        def _():
            ck_ref[...] = jnp.zeros(ck_ref.shape, F32)
            cv_ref[...] = jnp.zeros(cv_ref.shape, F32)

        ck_ref[...] += jnp.dot(weighted(0, wck_ref), ext_ref[g], preferred_element_type=F32)
        cv_ref[...] += jnp.dot(weighted(1, wcv_ref), ext_ref[g], preferred_element_type=F32)

        @pl.when(g == n_groups - 1)
        def _():
            nb = ck_ref.shape[1]
            s = jnp.dot(q, ck_ref[...].astype(BF16), preferred_element_type=F32)
            m = jnp.max(s, axis=1, keepdims=True)
            e = jnp.exp(s - m)
            p = e / jnp.maximum(jnp.sum(e, axis=1, keepdims=True), 1e-30)
            oc_ref[...] = lax.dot_general(p.astype(BF16), cv_ref[...].astype(BF16), nt, preferred_element_type=F32)
            imps = []
            for h in range(NSA_KV_HEADS):
                r = h * NSA_GROUP * T_PAD
                imp = p[r:r + T_PAD]
                for gg in range(1, NSA_GROUP):
                    imp = imp + p[r + gg * T_PAD:r + (gg + 1) * T_PAD]
                imps.append(imp)
            imps.append(jnp.zeros((nb - NSA_KV_HEADS * T_PAD, nb), F32))
            imp_t = jnp.concatenate(imps, axis=0).T
            bl = _row_iota((nb, nb))
            score = jnp.where((bl == 0) | (bl == nb - 1), FORCED_SCORE, imp_t)
            sel = _topk_rows(score, bl, min(TOP_K - 1, nb)).T
            for h in range(NSA_KV_HEADS):
                for gg in range(NSA_GROUP):
                    r = (h * NSA_GROUP + gg) * T_PAD
                    selm_ref[r:r + T_PAD, :] = sel[h * T_PAD:(h + 1) * T_PAD]
            m_ref[...] = jnp.full(m_ref.shape, NEG, F32)
            l_ref[...] = jnp.zeros(l_ref.shape, F32)
            acc_ref[...] = jnp.zeros(acc_ref.shape, F32)

    def flash(s, mask, vals):
        m_old = m_ref[...]
        if mask is None:
            m_new = jnp.maximum(m_old, jnp.max(s, axis=1, keepdims=True))
            e = jnp.exp(s - m_new)
        else:
            m_new = jnp.maximum(m_old, jnp.max(jnp.where(mask, s, NEG), axis=1, keepdims=True))
            e = jnp.where(mask, jnp.exp(s - m_new), 0.0)
        a = jnp.exp(m_old - m_new)
        m_ref[...] = m_new
        l_ref[...] = a * l_ref[...] + jnp.sum(e, axis=1, keepdims=True)
        acc_ref[...] = a * acc_ref[...] + lax.dot_general(e.astype(BF16), vals, nt, preferred_element_type=F32)

    @pl.when(ph == 1)
    def _():
        ka_ref[g, 0:kvw, :] = buf_ref[slot, 0].astype(BF16)
        vb = buf_ref[slot, 1].astype(BF16)
        q_aug = jnp.concatenate([q, jnp.where(selm_ref[...] > 0.0, 0.0, NEG).astype(BF16)], axis=1)
        s = jnp.dot(q_aug, ka_ref[g], preferred_element_type=F32)
        flash(s, None, vb)

        @pl.when(g == n_groups - 1)
        def _():
            flash(jnp.dot(q, kn_ref[0], preferred_element_type=F32), new_mask, vn_ref[0])
            o_s = acc_ref[...] / jnp.maximum(l_ref[...], 1e-30)
            wk = wk_ref[0].astype(BF16)
            s_old = jnp.dot(q, wk, preferred_element_type=F32)
            s_new = jnp.dot(q, kwn_ref[0], preferred_element_type=F32)
            j = lax.broadcasted_iota(jnp.int32, (Q_ROWS, win_len), 1)
            old_mask = j > win_len - WINDOW + _row_iota((Q_ROWS, win_len)) % T_PAD
            mw = jnp.maximum(jnp.max(jnp.where(old_mask, s_old, NEG), axis=1, keepdims=True),
                             jnp.max(jnp.where(new_mask, s_new, NEG), axis=1, keepdims=True))
            e_old = jnp.where(old_mask, jnp.exp(s_old - mw), 0.0)
            e_new = jnp.where(new_mask, jnp.exp(s_new - mw), 0.0)
            lw = jnp.sum(e_old, axis=1, keepdims=True) + jnp.sum(e_new, axis=1, keepdims=True)
            o_w = (lax.dot_general(e_old.astype(BF16), wv_ref[0].astype(BF16), nt, preferred_element_type=F32)
                   + lax.dot_general(e_new.astype(BF16), vwn_ref[0], nt, preferred_element_type=F32)) / jnp.maximum(lw, 1e-30)
            gate = _sigmoid(g_ref[0])
            o_ref[0] = gate[:, 0:1] * oc_ref[...] + gate[:, 1:2] * o_s + gate[:, 2:3] * o_w


def _feature_major(x):
    lead = x.shape[:-3]
    n = len(lead)
    return x.transpose(*range(n), n + 1, n + 2, n).reshape(*lead, NSA_KV_WIDTH, x.shape[-3])


def _nsa_sample(q, ks, vs, kw, vw, gates, pools, layer, win_k, win_v, page_table, w_cmp_k, w_cmp_v):
    b, t = q.shape[:2]
    n_pages = page_table.shape[1]
    assert t <= T_PAD and t <= BLOCK and PAGE_SIZE % BLOCK == 0
    pps = PAGES_PER_STEP
    assert n_pages % pps == 0
    n_groups = n_pages // pps
    nb = n_pages * PAGE_SIZE // BLOCK
    assert nb % LANES == 0
    wb = win_k.shape[1]
    n_pool = pools[0].shape[1]
    flat = [_feature_major(p).reshape(p.shape[0] * n_pool, NSA_KV_WIDTH, PAGE_SIZE) for p in pools]
    wt = lambda w: jnp.tile(w.T, (NSA_KV_HEADS, PAGE_SIZE // BLOCK))
    wck2, wcv2 = wt(w_cmp_k), wt(w_cmp_v)
    scale = NSA_HEAD_DIM ** -0.5
    q5 = (q * scale).reshape(b, t, NSA_KV_HEADS, NSA_GROUP, NSA_HEAD_DIM).transpose(0, 2, 3, 1, 4)
    q5 = jnp.pad(q5, ((0, 0), (0, 0), (0, 0), (0, T_PAD - t), (0, 0)))
    own = jnp.eye(NSA_KV_HEADS, dtype=F32)[None, :, None, None, :, None]
    qbd = (q5[:, :, :, :, None, :] * own).reshape(b, Q_ROWS, NSA_KV_WIDTH).astype(BF16)
    padn = lambda x: jnp.pad(x, ((0, 0), (0, NEW_PAD - t), (0, 0))).astype(BF16).transpose(0, 2, 1)
    g5 = gates.reshape(b, t, NSA_KV_HEADS, NSA_GROUP, 3).transpose(0, 2, 3, 1, 4)
    g5 = jnp.pad(g5, ((0, 0), (0, 0), (0, 0), (0, T_PAD - t), (0, 0))).reshape(b, Q_ROWS, 3)
    blocks_per_step = pps * PAGE_SIZE // BLOCK
    key_blk = jnp.arange(pps * PAGE_SIZE) // BLOCK
    expand = (jnp.arange(nb)[None, :, None] == (jnp.arange(n_groups)[:, None, None] * blocks_per_step + key_blk[None, None, :]))
    expand = expand.astype(BF16)
    expand_t = expand.transpose(0, 2, 1)
    per_b = lambda shape: pl.BlockSpec((1,) + shape, lambda i, p, g, pt: (i, 0, 0))
    const = lambda a: pl.BlockSpec(a.shape, lambda i, p, g, pt: (0,) * a.ndim)
    hbm = pl.BlockSpec(memory_space=pl.ANY)
    body = functools.partial(_nsa_sample_body, n_groups=n_groups, t_len=t, win_len=wb, page_base=layer * n_pool)
    o = pl.pallas_call(
        body,
        grid_spec=pltpu.PrefetchScalarGridSpec(
            num_scalar_prefetch=1,
            grid=(b, 2, n_groups),
            in_specs=[hbm, hbm, hbm, hbm, per_b((Q_ROWS, NSA_KV_WIDTH)), per_b((NSA_KV_WIDTH, NEW_PAD)), per_b((NSA_KV_WIDTH, NEW_PAD)),
                      per_b((NSA_KV_WIDTH, wb)), per_b((NSA_KV_WIDTH, wb)), per_b((NSA_KV_WIDTH, NEW_PAD)), per_b((NSA_KV_WIDTH, NEW_PAD)),
                      const(wck2), const(wcv2), per_b((Q_ROWS, 3)), const(expand), const(expand_t)],
            out_specs=per_b((Q_ROWS, NSA_KV_WIDTH)),
            scratch_shapes=[
                pltpu.VMEM((2, 2, NSA_KV_WIDTH, pps * PAGE_SIZE), F32),
                pltpu.SemaphoreType.DMA((2, 2)),
                pltpu.VMEM((NSA_KV_WIDTH, nb), F32), pltpu.VMEM((NSA_KV_WIDTH, nb), F32),
                pltpu.VMEM((Q_ROWS, nb), F32), pltpu.VMEM((Q_ROWS, NSA_KV_WIDTH), F32),
                pltpu.VMEM((Q_ROWS, 1), F32), pltpu.VMEM((Q_ROWS, 1), F32), pltpu.VMEM((Q_ROWS, NSA_KV_WIDTH), F32),
                pltpu.VMEM((n_groups, NSA_KV_WIDTH + nb, pps * PAGE_SIZE), BF16),
            ],
        ),
        out_shape=jax.ShapeDtypeStruct((b, Q_ROWS, NSA_KV_WIDTH), F32),
        compiler_params=pltpu.CompilerParams(dimension_semantics=("arbitrary",) * 3, vmem_limit_bytes=VMEM_LIMIT),
        name="nsa_sample",
    )(page_table, *flat, qbd, padn(ks), padn(vs), _feature_major(win_k), _feature_major(win_v), padn(kw), padn(vw),
      wck2, wcv2, g5, expand, expand_t)
    o6 = o.reshape(b, NSA_KV_HEADS, NSA_GROUP, T_PAD, NSA_KV_HEADS, NSA_HEAD_DIM)[:, :, :, :t]
    o5 = jnp.stack([o6[:, h, :, :, h, :] for h in range(NSA_KV_HEADS)], axis=1)
    return o5.transpose(0, 3, 1, 2, 4).reshape(b, t, NSA_WIDTH)


def kernel(x_prompt, x_sample, cache_cmp_k, cache_cmp_v, cache_sel_k, cache_sel_v, state_win_k, state_win_v, state_conv,
           state_ssm, page_table, norm_mix_pre, norm_mix_post, norm_ffn_pre, norm_ffn_post, w_in, w_cmp_k, w_cmp_v,
           nsa_out_norm, conv_w, a_log, dt_bias, gdn_norm, w_out, w_up, w_down):
    depth = w_in.shape[0]
    bp, sp, _ = x_prompt.shape
    bs, ts, _ = x_sample.shape
    past = page_table.shape[1] * PAGE_SIZE
    pos_p = jnp.arange(sp)
    pos_s = past + jnp.arange(ts)
    pools = (cache_cmp_k, cache_cmp_v, cache_sel_k, cache_sel_v)
    kv5 = lambda x, b, n: x.reshape(b, n, NSA_KV_HEADS, NSA_HEAD_DIM)
    y_p, y_s = x_prompt, x_sample
    p_states, s_states = [], []
    for l in range(depth):
        w_perm = _permute_w_in(w_in[l]).astype(BF16)
        wck2 = jnp.tile(w_cmp_k[l], (1, LANES // NSA_HEAD_DIM))
        wcv2 = jnp.tile(w_cmp_v[l], (1, LANES // NSA_HEAD_DIM))
        post_w = (nsa_out_norm[l], w_out[l].astype(BF16), norm_mix_post[l], norm_ffn_pre[l], w_up[l].astype(BF16),
                  w_down[l].astype(BF16), norm_ffn_post[l])
        gdn_w = (conv_w[l], a_log[l], dt_bias[l], gdn_norm[l])

        wb = min(WINDOW, sp)
        conv0 = jnp.zeros((CONV_WIDTH - 1, 3 * GDN_WIDTH), F32)
        ssm0 = jnp.zeros((GDN_HEADS, GDN_HEAD_DIM, GDN_HEAD_DIM), F32)
        ys, sts = [], []
        for b in range(bp):
            x2 = y_p[b]
            (_, _, _, ks, _, kw, _, qkv, z, small, ck, cv, qt, kct, vct, kst, vst, kwt, vwt) = _in_proj(
                x2, pos_p, norm_mix_pre[l], w_perm, wck2, wcv2, feature_major=True)
            o_nsa = _nsa_prompt(qt, ks, kw, vst, vwt, ck, cv, small)
            o_gdn, ssm = _gdn_prompt(qkv, z, small, conv0, ssm0, *gdn_w)
            ys.append(_post(o_nsa, o_gdn, x2, *post_w, nsa_feature_major=True))
            conv_state = jnp.concatenate([conv0, qkv], axis=0)[-(CONV_WIDTH - 1):]
            rows_major = lambda x: x.reshape(NSA_KV_HEADS, NSA_HEAD_DIM, sp).transpose(2, 0, 1)
            sts.append((rows_major(kct), rows_major(vct), rows_major(kst), rows_major(vst),
                        rows_major(kwt)[sp - wb:], rows_major(vwt)[sp - wb:], conv_state, ssm))
        y_p = jnp.stack(ys)
        p_states.append(tuple(jnp.stack(t) for t in zip(*sts)))

        n = bs * ts
        q, kc, vc, ks, vs, kw, vw, qkv, z, small, _, _ = _in_proj(
            jnp.pad(y_s.reshape(n, D_MODEL), ((0, (-n) % BLOCK), (0, 0))), jnp.pad(jnp.tile(pos_s, bs), (0, (-n) % BLOCK)),
            norm_mix_pre[l], w_perm, wck2, wcv2)
        r3 = lambda x: x[:n].reshape(bs, ts, -1)
        q, kc, vc, ks, vs, kw, vw, qkv, z, small = map(r3, (q, kc, vc, ks, vs, kw, vw, qkv, z, small))
        win_k, win_v = state_win_k[l], state_win_v[l]
        o_nsa = _nsa_sample(q, ks, vs, kw, vw, small[..., GATE_OFF:GATE_OFF + 3 * NSA_HEADS], pools, l, win_k, win_v,
                            page_table, w_cmp_k[l], w_cmp_v[l])
        o_gdn, ssm = _gdn_sample(qkv, z, small, state_conv[l], state_ssm[l], *gdn_w)
        y_s = _post(o_nsa.reshape(n, NSA_WIDTH), o_gdn.reshape(n, GDN_WIDTH), y_s.reshape(n, D_MODEL), *post_w).reshape(bs, ts, D_MODEL)
        conv_state = jnp.concatenate([state_conv[l], qkv], axis=1)[:, -(CONV_WIDTH - 1):]
        new_wk = jnp.concatenate([win_k, kv5(kw, bs, ts)], axis=1)[:, ts:]
        new_wv = jnp.concatenate([win_v, kv5(vw, bs, ts)], axis=1)[:, ts:]
        s_states.append((kv5(kc, bs, ts), kv5(vc, bs, ts), kv5(ks, bs, ts), kv5(vs, bs, ts), new_wk, new_wv, conv_state, ssm))
    p_out = [jnp.stack(t) for t in zip(*p_states)]
    s_out = [jnp.stack(t) for t in zip(*s_states)]
    return (y_p, y_s, *p_out, *s_out)
```

```python
import functools
import math

import jax
import jax.numpy as jnp
import numpy as np
from jax import lax
from jax.experimental import pallas as pl
from jax.experimental.pallas import tpu as pltpu

D_MODEL = 1024
PAGE_SIZE = 128
NSA_HEAD_DIM = 64
NSA_HEADS = 8
NSA_KV_HEADS = 2
NSA_GROUP = NSA_HEADS // NSA_KV_HEADS
NSA_WIDTH = NSA_HEADS * NSA_HEAD_DIM
NSA_KV_WIDTH = NSA_KV_HEADS * NSA_HEAD_DIM
BLOCK = 64
TOP_K = 16
WINDOW = 512
Q_BLOCK = 128
FORCED_SCORE = NSA_GROUP + 1.0
ROPE_THETA = 500000.0
ROPE_DIM = NSA_HEAD_DIM // 4
GDN_HEAD_DIM = 128
GDN_HEADS = 4
GDN_WIDTH = GDN_HEADS * GDN_HEAD_DIM
CONV_WIDTH = 4
GDN_CHUNK = 64
D_FF = 4 * D_MODEL
EPS = 1e-6
SPLIT_SIZES = (NSA_WIDTH,) + (NSA_KV_WIDTH,) * 6 + (3 * NSA_HEADS, 3 * GDN_WIDTH, GDN_WIDTH, GDN_HEADS, GDN_HEADS)

LANES = 128
SMALL_COLS = LANES
GATE_OFF, BETA_OFF, ALPHA_OFF = 0, 3 * NSA_HEADS, 3 * NSA_HEADS + GDN_HEADS
PROJ_COLS = NSA_WIDTH + 6 * NSA_KV_WIDTH + 3 * GDN_WIDTH + GDN_WIDTH + SMALL_COLS
VMEM_LIMIT = 56 * 1024 * 1024
NEG = -1e30
F32 = jnp.float32
BF16 = jnp.bfloat16
HI = lax.Precision.HIGHEST


def _rms(x, w):
    return x * lax.rsqrt(jnp.mean(x * x, axis=-1, keepdims=True) + EPS) * w


def _sigmoid(x):
    return 1.0 / (1.0 + jnp.exp(-x))


def _silu(x):
    return x * _sigmoid(x)


def _rope_slab(x, c, sa, sb):
    half = ROPE_DIM // 2
    return x * c + pltpu.roll(x, LANES - half, axis=1) * sa + pltpu.roll(x, half, axis=1) * sb


Q_SCALE = NSA_HEAD_DIM ** -0.5 * math.log2(math.e)


def _in_proj_body(x_ref, nw_ref, w_ref, rc_ref, rsa_ref, rsb_ref, wk_ref, wv_ref,
                  q_ref, kc_ref, vc_ref, ks_ref, vs_ref, kw_ref, vw_ref, qkv_ref, z_ref, sm_ref, ck_ref, cv_ref,
                  *fm_refs, feature_major):
    h = _rms(x_ref[...], nw_ref[...])
    p = jnp.dot(h.astype(BF16), w_ref[...], preferred_element_type=F32)
    c, sa, sb = rc_ref[...], rsa_ref[...], rsb_ref[...]
    rows = p.shape[0]
    off = 0
    for j in range(NSA_WIDTH // LANES):
        qs = _rope_slab(p[:, off:off + LANES], c, sa, sb)
        q_ref[:, j * LANES:(j + 1) * LANES] = qs
        if feature_major:
            qst = (qs * Q_SCALE).T.astype(BF16)
            heads_per_slab = LANES // NSA_HEAD_DIM
            kvh = j * heads_per_slab // NSA_GROUP
            for half in range(heads_per_slab):
                g = (j * heads_per_slab + half) % NSA_GROUP
                for blk in range(rows // Q_BLOCK):
                    fm_refs[0][kvh, blk, :, g * Q_BLOCK:(g + 1) * Q_BLOCK] = qst[
                        half * NSA_HEAD_DIM:(half + 1) * NSA_HEAD_DIM, blk * Q_BLOCK:(blk + 1) * Q_BLOCK]
        off += LANES
    kc = _rope_slab(p[:, off:off + LANES], c, sa, sb)
    kc_ref[...] = kc
    vc = p[:, off + LANES:off + 2 * LANES]
    vc_ref[...] = vc
    ks = _rope_slab(p[:, off + 2 * LANES:off + 3 * LANES], c, sa, sb)
    ks_ref[...] = ks
    vs = p[:, off + 3 * LANES:off + 4 * LANES]
    vs_ref[...] = vs
    kw = _rope_slab(p[:, off + 4 * LANES:off + 5 * LANES], c, sa, sb)
    kw_ref[...] = kw
    vw = p[:, off + 5 * LANES:off + 6 * LANES]
    vw_ref[...] = vw
    if feature_major:
        for ref, slab in zip(fm_refs[1:], (kc, vc, ks, vs, kw, vw)):
            ref[...] = slab.T
    off += 6 * LANES
    qkv_ref[...] = p[:, off:off + 3 * GDN_WIDTH]
    off += 3 * GDN_WIDTH
    z_ref[...] = p[:, off:off + GDN_WIDTH]
    off += GDN_WIDTH
    sm_ref[...] = p[:, off:off + SMALL_COLS]
    ck_ref[...] = jnp.sum(kc.reshape(rows // BLOCK, BLOCK, LANES) * wk_ref[...][None], axis=1)
    cv_ref[...] = jnp.sum(vc.reshape(rows // BLOCK, BLOCK, LANES) * wv_ref[...][None], axis=1)


def _permute_w_in(w_in):
    pts = np.cumsum((0,) + SPLIT_SIZES)
    seg = [w_in[:, int(pts[i]):int(pts[i + 1])] for i in range(len(SPLIT_SIZES))]
    q, kc, vc, ks, vs, kw, vw, gates, qkv, z, beta, alpha = seg
    small = jnp.concatenate([gates, beta, alpha], axis=1)
    small = jnp.pad(small, ((0, 0), (0, SMALL_COLS - small.shape[1])))
    return jnp.concatenate([q, kc, vc, ks, vs, kw, vw, qkv, z, small], axis=1)


def _rope_tables(pos):
    half = ROPE_DIM // 2
    inv_freq = ROPE_THETA ** (-jnp.arange(half, dtype=F32) / half)
    ang = pos.astype(F32)[:, None] * inv_freq[None, :]
    cos, sin = jnp.cos(ang), jnp.sin(ang)
    n = pos.shape[0]
    one = jnp.ones((n, NSA_HEAD_DIM - ROPE_DIM), F32)
    zero = jnp.zeros((n, NSA_HEAD_DIM - ROPE_DIM), F32)
    zh = jnp.zeros((n, half), F32)
    c = jnp.concatenate([cos, cos, one], axis=1)
    sa = jnp.concatenate([-sin, zh, zero], axis=1)
    sb = jnp.concatenate([zh, sin, zero], axis=1)
    return tuple(jnp.tile(t, (1, LANES // NSA_HEAD_DIM)) for t in (c, sa, sb))


def _in_proj(x2d, pos, norm_w, w_perm, wk2, wv2, feature_major=False):
    n = x2d.shape[0]
    tm = min(512, n)
    assert n % tm == 0 and tm % BLOCK == 0
    rc, rsa, rsb = _rope_tables(pos)
    row = lambda w: pl.BlockSpec((tm, w), lambda i: (i, 0))
    const = lambda a: pl.BlockSpec(a.shape, lambda i: (0,) * a.ndim)
    nw = norm_w.reshape(1, D_MODEL)
    widths = (NSA_WIDTH,) + (LANES,) * 6 + (3 * GDN_WIDTH, GDN_WIDTH, SMALL_COLS)
    out_shape = [jax.ShapeDtypeStruct((n, w), F32) for w in widths]
    out_shape += [jax.ShapeDtypeStruct((n // BLOCK, LANES), F32)] * 2
    out_specs = [row(w) for w in widths] + [pl.BlockSpec((tm // BLOCK, LANES), lambda i: (i, 0))] * 2
    if feature_major:
        assert tm % Q_BLOCK == 0
        qb = tm // Q_BLOCK
        out_shape += [jax.ShapeDtypeStruct((NSA_KV_HEADS, n // Q_BLOCK, NSA_HEAD_DIM, NSA_GROUP * Q_BLOCK), BF16)]
        out_specs += [pl.BlockSpec((NSA_KV_HEADS, qb, NSA_HEAD_DIM, NSA_GROUP * Q_BLOCK), lambda i: (0, i, 0, 0))]
        out_shape += [jax.ShapeDtypeStruct((LANES, n), F32)] * 6
        out_specs += [pl.BlockSpec((LANES, tm), lambda i: (0, i))] * 6
    return pl.pallas_call(
        functools.partial(_in_proj_body, feature_major=feature_major),
        grid=(n // tm,),
        in_specs=[row(D_MODEL), const(nw), const(w_perm), row(LANES), row(LANES), row(LANES), const(wk2), const(wv2)],
        out_specs=out_specs,
        out_shape=out_shape,
        compiler_params=pltpu.CompilerParams(dimension_semantics=("arbitrary",), vmem_limit_bytes=VMEM_LIMIT),
        name="in_proj",
    )(x2d, nw, w_perm, rc, rsa, rsb, wk2, wv2)


FF_CHUNK = 1024


def _post_body(on_ref, og_ref, x_ref, nn_ref, wo_ref, nmp_ref, nfp_ref, wu_ref, wd_ref, nfo_ref, y_ref, *, nsa_feature_major):
    if nsa_feature_major:
        ont = on_ref[...]
        o_nsa = (ont * lax.rsqrt(jnp.mean(ont * ont, axis=0, keepdims=True) + EPS) * nn_ref[...]).T
    else:
        o_nsa = _rms(on_ref[...], nn_ref[...])
    mix = jnp.concatenate([o_nsa, og_ref[...]], axis=1).astype(BF16)
    y = jnp.dot(mix, wo_ref[...], preferred_element_type=F32)
    y1 = x_ref[...] + _rms(y, nmp_ref[...])
    h = _rms(y1, nfp_ref[...]).astype(BF16)
    acc = jnp.zeros_like(y1)
    for c in range(D_FF // FF_CHUNK):
        u = jnp.dot(h, wu_ref[:, c * FF_CHUNK:(c + 1) * FF_CHUNK], preferred_element_type=F32)
        u = jnp.square(jnp.maximum(u, 0.0))
        acc = acc + jnp.dot(u.astype(BF16), wd_ref[c * FF_CHUNK:(c + 1) * FF_CHUNK, :], preferred_element_type=F32)
    y_ref[...] = y1 + _rms(acc, nfo_ref[...])


def _post(o_nsa, o_gdn, x2d, nsa_norm, w_out, n_mix_post, n_ffn_pre, w_up, w_down, n_ffn_post, nsa_feature_major=False):
    n = x2d.shape[0]
    tm = min(256, n)
    assert n % tm == 0
    row = lambda w: pl.BlockSpec((tm, w), lambda i: (i, 0))
    const = lambda a: pl.BlockSpec(a.shape, lambda i: (0,) * a.ndim)
    r = lambda v: v.reshape(1, -1)
    nn = nsa_norm.reshape(-1, 1) if nsa_feature_major else r(nsa_norm)
    nsa_spec = pl.BlockSpec((NSA_WIDTH, tm), lambda i: (0, i)) if nsa_feature_major else row(NSA_WIDTH)
    args = (o_nsa, o_gdn, x2d, nn, w_out, r(n_mix_post), r(n_ffn_pre), w_up, w_down, r(n_ffn_post))
    return pl.pallas_call(
        functools.partial(_post_body, nsa_feature_major=nsa_feature_major),
        grid=(n // tm,),
        in_specs=[nsa_spec, row(GDN_WIDTH), row(D_MODEL)] + [const(a) for a in args[3:]],
        out_specs=row(D_MODEL),
        out_shape=jax.ShapeDtypeStruct((n, D_MODEL), F32),
        compiler_params=pltpu.CompilerParams(dimension_semantics=("arbitrary",), vmem_limit_bytes=VMEM_LIMIT),
        name="out_proj_ffn",
    )(*args)


SEL_CHUNK = 512
SEL_AUG = 16
V_ROWS = NSA_HEAD_DIM + 16
SELECT_TIERS = 4


def _row_iota(shape):
    return lax.broadcasted_iota(jnp.int32, shape, 0)


def _topk_rows(score, blk, k):
    nb = score.shape[0]
    sel = jnp.zeros(score.shape, F32)
    for _ in range(k):
        m = jnp.max(score, axis=0, keepdims=True)
        idx = jnp.min(jnp.where(score == m, blk, nb), axis=0, keepdims=True)
        pick = blk == idx
        sel = jnp.where(pick, 1.0, sel)
        score = jnp.where(pick, -jnp.inf, score)
    return sel


REDUCE_WAYS = 8


def _col_reduce(x, op):
    rows = x.shape[0]
    if rows % (REDUCE_WAYS * 8) == 0 and rows > REDUCE_WAYS * 8:
        x = op(x.reshape(rows // (REDUCE_WAYS * 8), REDUCE_WAYS * 8, x.shape[1]), axis=0)
    return op(x, axis=0, keepdims=True)


def _softmax_rows(s, mask):
    m = jnp.max(jnp.where(mask, s, NEG), axis=0, keepdims=True)
    e = jnp.where(mask, jnp.exp2(s - m), 0.0)
    return e, 1.0 / jnp.maximum(jnp.sum(e, axis=0, keepdims=True), 1e-30)


def _nsa_prompt_body(qt_ref, ks_ref, kw_ref, vt_ref, ck_ref, cvt_ref, g_ref, o_ref,
                     sb_ref, s0_ref, s1_ref, e0_ref, e1_ref, oc_ref, *, seq, kc, wn):
    i = pl.program_id(1)
    start = i * Q_BLOCK
    nb = seq // BLOCK
    g4 = range(NSA_GROUP)
    slab = lambda a, g: a[:, g * Q_BLOCK:(g + 1) * Q_BLOCK]
    qt = qt_ref[0, 0]
    pos = start + lax.broadcasted_iota(jnp.int32, (1, Q_BLOCK), 1)

    def compress_and_select(r):
        sc = jnp.dot(ck_ref[0, 0:r], qt, preferred_element_type=F32)
        blk = _row_iota((r, Q_BLOCK))
        cmask = (blk + 1) * BLOCK - 1 <= pos
        ps = []
        for g in g4:
            e, inv = _softmax_rows(slab(sc, g), cmask)
            ps.append(e * inv)
        p_all = jnp.concatenate(ps, axis=1).astype(BF16)
        if r < nb:
            p_all = jnp.concatenate([p_all, jnp.zeros((nb - r, NSA_GROUP * Q_BLOCK), BF16)], axis=0)
        oc_ref[...] = jnp.dot(cvt_ref[0], p_all, preferred_element_type=F32)
        imp = ps[0]
        for g in range(1, NSA_GROUP):
            imp = imp + ps[g]
        cur = pos // BLOCK
        forced = (blk == 0) | (blk == cur) | (blk == cur - 1)
        score = jnp.where(forced | (blk > cur), -1.0, imp)
        sel = _topk_rows(score, blk, max(min(TOP_K, nb) - 3, 0))
        sel_bias = jnp.where(forced | ((score >= 0) & (sel > 0.0)), 0.0, NEG)
        for g in g4:
            sb_ref[0:r, g * Q_BLOCK:(g + 1) * Q_BLOCK] = sel_bias

    n_qb = seq // Q_BLOCK
    whole_chunks = nb % SELECT_TIERS == 0 and (nb // SELECT_TIERS) % (kc // BLOCK) == 0
    tiers = SELECT_TIERS if whole_chunks and n_qb % SELECT_TIERS == 0 else 1
    for t in range(tiers):
        pl.when(i * tiers // n_qb == t)(functools.partial(compress_and_select, (t + 1) * nb // tiers))
    o_c = oc_ref[...]

    bpc = kc // BLOCK
    pad_rows = jnp.zeros((max(SEL_AUG - bpc, 1), NSA_GROUP * Q_BLOCK), F32)
    pad_tail = jnp.zeros((LANES - NSA_HEAD_DIM - SEL_AUG, NSA_GROUP * Q_BLOCK), BF16)

    n_parts = NSA_GROUP
    part = kc // n_parts

    def q_aug(c):
        rows = sb_ref[pl.ds(pl.multiple_of(c * bpc, bpc), bpc), :]
        if bpc < SEL_AUG:
            rows = jnp.concatenate([rows, pad_rows], axis=0)
        return jnp.concatenate([qt, rows.astype(BF16), pad_tail], axis=0)

    def scores_part(c, qa, j):
        k0 = pl.multiple_of(c * kc + j * part, part)
        return jnp.dot(ks_ref[0, pl.ds(k0, part), :], qa, preferred_element_type=F32)

    def values_dot(c, e_ref):
        k0 = pl.multiple_of(c * kc, kc)
        return jnp.dot(vt_ref[0, 0:V_ROWS, pl.ds(k0, kc)], e_ref[...], preferred_element_type=F32)

    def sweep_step(c, carry, cur, nxt, last):
        m, acc, alpha_prev = carry
        pv = values_dot(jnp.maximum(c - 1, 0), nxt[1])
        qa = None if last else q_aug(c + 1)
        k0 = pl.multiple_of(c * kc, kc)
        ms, alphas = [], []
        for g in g4:
            if not last:
                nxt[0][g * part:(g + 1) * part, :] = scores_part(c + 1, qa, g)
            sg = cur[0][:, g * Q_BLOCK:(g + 1) * Q_BLOCK]
            if last:
                sg = sg + jnp.where(k0 + _row_iota((kc, Q_BLOCK)) <= pos, 0.0, NEG)
            mo = slab(m, g)
            cur[1][:, g * Q_BLOCK:(g + 1) * Q_BLOCK] = jnp.exp2((sg - mo).astype(BF16))
            mn = mo if last else jnp.maximum(mo, _col_reduce(sg, jnp.max))
            ms.append(mn)
            alphas.append(jnp.exp2(mo - mn))
        cat = lambda xs: jnp.concatenate(xs, axis=1)
        return cat(ms), (acc + pv) * alpha_prev, cat(alphas)

    def finish(c, carry, cur, nxt):
        _, acc, _ = sweep_step(c, carry, cur, nxt, True)
        return acc + values_dot(c, cur[1])

    width = NSA_GROUP * Q_BLOCK
    even, odd = (s0_ref, e0_ref), (s1_ref, e1_ref)
    qa0 = q_aug(0)
    for g in g4:
        s0_ref[g * part:(g + 1) * part, :] = scores_part(0, qa0, g)
    e1_ref[...] = jnp.zeros(e1_ref.shape, BF16)
    bias0 = jnp.where(_row_iota((kc, Q_BLOCK)) <= pos, 0.0, NEG)
    m0 = jnp.concatenate([_col_reduce(s0_ref[:, g * Q_BLOCK:(g + 1) * Q_BLOCK] + bias0, jnp.max) for g in g4], axis=1)
    init = (m0, jnp.zeros((V_ROWS, width), F32), jnp.ones((1, width), F32))
    c_diag = start // kc

    def pair(p, carry):
        carry = sweep_step(2 * p, carry, even, odd, False)
        return sweep_step(2 * p + 1, carry, odd, even, False)

    quad = lambda p, cr: pair(2 * p + 1, pair(2 * p, cr))
    carry = lax.fori_loop(0, c_diag // 8, lambda p, cr: quad(2 * p + 1, quad(2 * p, cr)), init)
    carry = lax.fori_loop(c_diag // 8 * 2, c_diag // 4, quad, carry)
    carry = lax.fori_loop(c_diag // 4 * 2, c_diag // 2, pair, carry)
    acc_s = lax.cond(
        c_diag % 2 == 1,
        lambda cr: finish(c_diag, sweep_step(c_diag - 1, cr, even, odd, False), odd, even),
        lambda cr: finish(c_diag, cr, even, odd),
        carry)
    def exact_step(c, carry):
        m, acc = carry
        s = jnp.concatenate([scores_part(c, q_aug(c), j) for j in range(n_parts)], axis=0)
        s = s + jnp.concatenate([jnp.where(c * kc + _row_iota((kc, Q_BLOCK)) <= pos, 0.0, NEG)] * NSA_GROUP, axis=1)
        mn = jnp.maximum(m, jnp.max(s, axis=0, keepdims=True))
        e = jnp.exp2((s - mn).astype(BF16))
        k0 = pl.multiple_of(c * kc, kc)
        pv = jnp.dot(vt_ref[0, 0:V_ROWS, pl.ds(k0, kc)], e, preferred_element_type=F32)
        return mn, jnp.exp2(m - mn) * acc + pv

    l_s = acc_s[NSA_HEAD_DIM:NSA_HEAD_DIM + 1]
    sane = (jnp.abs(acc_s) < 3e38) & (l_s >= 1.0) & (l_s < 1e30)
    acc_s = lax.cond(
        jnp.max(jnp.where(sane, 0.0, 1.0)) > 0.0,
        lambda a: lax.fori_loop(0, c_diag + 1, exact_step,
                                (jnp.full((1, width), NEG, F32), jnp.zeros((V_ROWS, width), F32)))[1],
        lambda a: a,
        acc_s)
    o_s = acc_s[0:NSA_HEAD_DIM] * (1.0 / jnp.maximum(acc_s[NSA_HEAD_DIM:NSA_HEAD_DIM + 1], 1e-30))

    w0 = pl.multiple_of(jnp.maximum(start + Q_BLOCK - wn, 0), Q_BLOCK)
    sw = jnp.dot(kw_ref[0, pl.ds(w0, wn), :], qt, preferred_element_type=F32)
    dist = pos - (w0 + _row_iota((wn, Q_BLOCK)))
    wbias = jnp.where((dist >= 0) & (dist < WINDOW), 0.0, NEG)
    ews = []
    for g in g4:
        sg = slab(sw, g) + wbias
        ews.append(jnp.exp2((sg - jnp.max(sg, axis=0, keepdims=True)).astype(BF16)))
    ow = jnp.dot(vt_ref[0, V_ROWS:2 * V_ROWS, pl.ds(w0, wn)], jnp.concatenate(ews, axis=1), preferred_element_type=F32)
    o_w = ow[0:NSA_HEAD_DIM] * (1.0 / jnp.maximum(ow[NSA_HEAD_DIM:NSA_HEAD_DIM + 1], 1e-30))

    gate = _sigmoid(g_ref[0])
    for g in g4:
        o_ref[0, g] = (gate[3 * g:3 * g + 1] * slab(o_c, g) + gate[3 * g + 1:3 * g + 2] * slab(o_s, g)
                       + gate[3 * g + 2:3 * g + 3] * slab(o_w, g))


def _split_heads_t(x, n):
    return x.reshape(n, NSA_KV_HEADS, NSA_HEAD_DIM).transpose(1, 0, 2)


def _nsa_prompt(qt, ks, kw, vst, vwt, ck, cv, small):
    s = ks.shape[0]
    assert s % Q_BLOCK == 0
    n_qb, nb = s // Q_BLOCK, s // BLOCK
    kc = min(SEL_CHUNK, s)
    wn = min(WINDOW + Q_BLOCK, s)
    assert s % kc == 0
    bpc = kc // BLOCK
    assert kc % Q_BLOCK == 0 and bpc <= SEL_AUG
    onehot = ((jnp.arange(s) // BLOCK) % bpc)[:, None] == jnp.arange(LANES - NSA_HEAD_DIM)[None, :]
    ksa = jnp.concatenate([_split_heads_t(ks, s).astype(BF16),
                           jnp.broadcast_to(onehot.astype(BF16), (NSA_KV_HEADS, s, LANES - NSA_HEAD_DIM))], axis=-1)
    kwh = _split_heads_t(kw, s).astype(BF16)
    ones_fill = jnp.zeros((NSA_KV_HEADS, V_ROWS - NSA_HEAD_DIM, s), BF16).at[:, 0, :].set(1.0)
    fm = lambda x: x.reshape(NSA_KV_HEADS, NSA_HEAD_DIM, s).astype(BF16)
    vt = jnp.concatenate([fm(vst), ones_fill, fm(vwt), ones_fill], axis=1)
    ckh = _split_heads_t(ck, nb).astype(BF16)
    cvt = _split_heads_t(cv, nb).astype(BF16).transpose(0, 2, 1)
    gt = small.T[GATE_OFF:GATE_OFF + 3 * NSA_HEADS].reshape(NSA_KV_HEADS, NSA_GROUP * 3, s)
    gt = jnp.pad(gt, ((0, 0), (0, 16 - NSA_GROUP * 3), (0, 0)))
    body = functools.partial(_nsa_prompt_body, seq=s, kc=kc, wn=wn)
    o = pl.pallas_call(
        body,
        grid=(NSA_KV_HEADS, n_qb),
        in_specs=[
            pl.BlockSpec((1, 1, NSA_HEAD_DIM, NSA_GROUP * Q_BLOCK), lambda h, i: (h, i, 0, 0)),
            pl.BlockSpec((1, s, LANES), lambda h, i: (h, 0, 0)),
            pl.BlockSpec((1, s, NSA_HEAD_DIM), lambda h, i: (h, 0, 0)),
            pl.BlockSpec((1, 2 * V_ROWS, s), lambda h, i: (h, 0, 0)),
            pl.BlockSpec((1, nb, NSA_HEAD_DIM), lambda h, i: (h, 0, 0)),
            pl.BlockSpec((1, NSA_HEAD_DIM, nb), lambda h, i: (h, 0, 0)),
            pl.BlockSpec((1, 16, Q_BLOCK), lambda h, i: (h, 0, i)),
        ],
        out_specs=pl.BlockSpec((1, NSA_GROUP, NSA_HEAD_DIM, Q_BLOCK), lambda h, i: (h, 0, 0, i)),
        out_shape=jax.ShapeDtypeStruct((NSA_KV_HEADS, NSA_GROUP, NSA_HEAD_DIM, s), F32),
        scratch_shapes=[pltpu.VMEM((nb, NSA_GROUP * Q_BLOCK), F32)]
        + [pltpu.VMEM((kc, NSA_GROUP * Q_BLOCK), F32)] * 2 + [pltpu.VMEM((kc, NSA_GROUP * Q_BLOCK), BF16)] * 2
        + [pltpu.VMEM((NSA_HEAD_DIM, NSA_GROUP * Q_BLOCK), F32)],
        compiler_params=pltpu.CompilerParams(dimension_semantics=("arbitrary", "arbitrary"), vmem_limit_bytes=VMEM_LIMIT),
        name="nsa_prompt",
    )(qt, ksa, kwh, vt, ckh, cvt, gt)
    return o.reshape(NSA_WIDTH, s)


def _dot(a, b):
    return jnp.dot(a, b, precision=HI, preferred_element_type=F32)


def _dot_nt(a, b):
    return lax.dot_general(a, b, (((1,), (1,)), ((), ())), precision=HI, preferred_element_type=F32)


def _dot_tn(a, b):
    return lax.dot_general(a, b, (((0,), (0,)), ((), ())), precision=HI, preferred_element_type=F32)


def _dot1_nt(a, b):
    return lax.dot_general(a, b, (((1,), (1,)), ((), ())), preferred_element_type=F32)


def _split16(a):
    hi = a.astype(BF16)
    return hi, (a - hi.astype(F32)).astype(BF16)


def _dot3(a, b):
    return jnp.dot(a.astype(BF16), b.astype(BF16), preferred_element_type=F32)


def _softplus(x):
    return jnp.maximum(x, 0.0) + jnp.log(1.0 + jnp.exp(-jnp.abs(x)))


def _unit_lower_inverses(mats, ri, ci):
    c = mats[0].shape[0]
    eye = (ri == ci).astype(F32)
    base = 8
    same = ri // base == ci // base
    ds = [jnp.where(same, a, 0.0) for a in mats]
    d2s = [_dot3(d, d) for d in ds]
    xs = [(eye - d) + _dot3(eye - d, d2) for d, d2 in zip(ds, d2s)]
    d4s = [_dot3(d2, d2) for d2 in d2s]
    xs = [x + _dot3(x, d4) for x, d4 in zip(xs, d4s)]
    b = base
    while b < c:
        band = (ri // (2 * b) == ci // (2 * b)) & (ri // b != ci // b)
        lxs = [_dot3(jnp.where(band, a, 0.0), x) for a, x in zip(mats, xs)]
        xs = [x - _dot3(x, lx) for x, lx in zip(xs, lxs)]
        b *= 2
    return xs


SUBLANES = 8


def _conv_silu(x, prev, w):
    rowi = _row_iota((SUBLANES, x.shape[1]))
    conv = None
    for j in range(CONV_WIDTH - 1, -1, -1):
        if j == 0:
            xs = x
        else:
            xr = pltpu.roll(x, j, axis=0)
            head = jnp.where(rowi < j, pltpu.roll(prev, j, axis=0), xr[0:SUBLANES])
            xs = jnp.concatenate([head, xr[SUBLANES:]], axis=0)
        term = xs * w[CONV_WIDTH - 1 - j:CONV_WIDTH - j]
        conv = term if conv is None else conv + term
    return _silu(conv)


def _l2n(x):
    return x * lax.rsqrt(jnp.sum(x * x, axis=-1, keepdims=True) + EPS)


GDN_STEP_CHUNKS = 4


def _gdn_chunk_body(qkv_ref, sm_ref, smt_ref, prev0_ref, cw_ref, ar_ref, dr_ref, at_ref, dt_ref,
                    u_ref, w_ref, qe_ref, kd_ref, in_ref, dl_ref, xp_ref):
    cs = GDN_CHUNK

    @pl.when(pl.program_id(0) == 0)
    def _():
        xp_ref[...] = prev0_ref[...]

    x = qkv_ref[...]
    act = _conv_silu(x, xp_ref[...], cw_ref[...])
    xp_ref[...] = x[x.shape[0] - SUBLANES:]
    sm = sm_ref[...]
    ri = _row_iota((cs, cs))
    ci = lax.broadcasted_iota(jnp.int32, (cs, cs), 1)
    tri = (ri >= ci).astype(F32)
    tri_t = (ri <= ci).astype(F32)
    g_all = -jnp.exp(ar_ref[...]) * _softplus(sm + dr_ref[...])
    beta_all = _sigmoid(sm)
    scale = GDN_HEAD_DIM ** -0.5
    a_mats, rhs, where = [], [], []
    for c in range(GDN_STEP_CHUNKS):
        rows = slice(c * cs, (c + 1) * cs)
        gc_cols = _dot(tri, g_all[rows])
        g_rows = -jnp.exp(at_ref[...]) * _softplus(smt_ref[c] + dt_ref[...])
        gc_rows = _dot(g_rows, tri_t)
        intras, dls = [], []
        for h in range(GDN_HEADS):
            hs = slice(h * GDN_HEAD_DIM, (h + 1) * GDN_HEAD_DIM)
            q = _l2n(act[rows, hs]) * scale
            k = _l2n(act[rows, GDN_WIDTH + h * GDN_HEAD_DIM:GDN_WIDTH + (h + 1) * GDN_HEAD_DIM])
            v = act[rows, 2 * GDN_WIDTH + h * GDN_HEAD_DIM:2 * GDN_WIDTH + (h + 1) * GDN_HEAD_DIM]
            beta = beta_all[rows, BETA_OFF + h:BETA_OFF + h + 1]
            gc = gc_cols[:, ALPHA_OFF + h:ALPHA_OFF + h + 1]
            gr = gc_rows[GDN_HEADS + h:GDN_HEADS + h + 1, :]
            g_last = gc[cs - 1:cs, :]
            decay = jnp.where(ri >= ci, jnp.exp(gc - gr), 0.0)
            kb = k * beta
            k16 = k.astype(BF16)
            a_mats.append(jnp.where(ri > ci, _dot1_nt(kb.astype(BF16), k16) * decay, 0.0))
            rhs.append(jnp.concatenate([v * beta, kb * jnp.exp(gc)], axis=1))
            where.append((rows, hs))
            qe_ref[rows, hs] = (q * jnp.exp(gc)).astype(BF16)
            kd_ref[rows, hs] = (k * jnp.exp(g_last - gc)).astype(BF16)
            intras.append((_dot1_nt(q.astype(BF16), k16) * decay).astype(BF16))
            dls.append(jnp.broadcast_to(jnp.exp(g_last), (1, LANES)))
        in_ref[rows, :] = jnp.concatenate(intras, axis=1)
        dl_ref[c] = jnp.concatenate(dls + [jnp.zeros((2 * GDN_HEADS - GDN_HEADS, LANES), F32)], axis=0)
    for t, r, (rows, hs) in zip(_unit_lower_inverses(a_mats, ri, ci), rhs, where):
        sol = _dot3(t, r)
        u_ref[rows, hs] = sol[:, :GDN_HEAD_DIM]
        w_ref[rows, hs] = sol[:, GDN_HEAD_DIM:].astype(BF16)


def _gdn_scan_body(u_ref, w_ref, qe_ref, kd_ref, in_ref, dl_ref, z_ref, s0_ref, gn_ref, o_ref, sf_ref, s_ref):
    cs = GDN_CHUNK
    i = pl.program_id(0)

    @pl.when(i == 0)
    def _():
        s_ref[...] = s0_ref[...]

    heads = range(GDN_HEADS)
    hsl = [slice(h * GDN_HEAD_DIM, (h + 1) * GDN_HEAD_DIM) for h in heads]
    tn = (((0,), (0,)), ((), ()))
    ss = [s_ref[h] for h in heads]
    for c in range(GDN_STEP_CHUNKS):
        rows = slice(c * cs, (c + 1) * cs)
        s16 = [s.astype(BF16) for s in ss]
        v16 = [(u_ref[rows, hsl[h]] - jnp.dot(w_ref[rows, hsl[h]], s16[h], preferred_element_type=F32)).astype(BF16)
               for h in heads]
        ss = [ss[h] * dl_ref[c, h:h + 1, :] + lax.dot_general(kd_ref[rows, hsl[h]], v16[h], tn, preferred_element_type=F32)
              for h in heads]
        for h in heads:
            o = (jnp.dot(qe_ref[rows, hsl[h]], s16[h], preferred_element_type=F32)
                 + jnp.dot(in_ref[rows, h * cs:(h + 1) * cs], v16[h], preferred_element_type=F32))
            o_ref[rows, hsl[h]] = _rms(o, gn_ref[...]) * _silu(z_ref[rows, hsl[h]])
    for h in heads:
        s_ref[h] = ss[h]

    @pl.when(i == pl.num_programs(0) - 1)
    def _():
        sf_ref[...] = s_ref[...]


def _gdn_consts(a_log, dt_bias):
    lane = jnp.zeros((1, LANES), F32)
    ar = lane.at[0, ALPHA_OFF:ALPHA_OFF + GDN_HEADS].set(a_log)
    dr = lane.at[0, ALPHA_OFF:ALPHA_OFF + GDN_HEADS].set(dt_bias)
    col = jnp.zeros((2 * GDN_HEADS, 1), F32)
    at = jnp.broadcast_to(col.at[GDN_HEADS:, 0].set(a_log), (2 * GDN_HEADS, GDN_CHUNK))
    dt = jnp.broadcast_to(col.at[GDN_HEADS:, 0].set(dt_bias), (2 * GDN_HEADS, GDN_CHUNK))
    return ar, dr, at, dt


def _gdn_prompt(qkv, z, small, conv_prev, s0, conv_w, a_log, dt_bias, gdn_norm):
    n = qkv.shape[0]
    cs = GDN_CHUNK
    rows = GDN_STEP_CHUNKS * cs
    assert n % rows == 0
    nc = n // cs
    smt = small.T[BETA_OFF:BETA_OFF + 2 * GDN_HEADS].reshape(2 * GDN_HEADS, nc, cs).transpose(1, 0, 2)
    prev0 = jnp.pad(conv_prev, ((SUBLANES - (CONV_WIDTH - 1), 0), (0, 0)))
    ar, dr, at, dt = _gdn_consts(a_log, dt_bias)
    gn = gdn_norm.reshape(1, GDN_HEAD_DIM)
    row = lambda w: pl.BlockSpec((rows, w), lambda i: (i, 0))
    per_chunk = pl.BlockSpec((GDN_STEP_CHUNKS, 2 * GDN_HEADS, LANES), lambda i: (i, 0, 0))
    const = lambda a: pl.BlockSpec(a.shape, lambda i: (0,) * a.ndim)
    params = pltpu.CompilerParams(dimension_semantics=("arbitrary",), vmem_limit_bytes=VMEM_LIMIT)
    wide = lambda dt_: jax.ShapeDtypeStruct((n, GDN_WIDTH), dt_)
    u, w, qe, kd, intra, dl = pl.pallas_call(
        _gdn_chunk_body,
        grid=(n // rows,),
        in_specs=[row(3 * GDN_WIDTH), row(SMALL_COLS),
                  pl.BlockSpec((GDN_STEP_CHUNKS, 2 * GDN_HEADS, cs), lambda i: (i, 0, 0)),
                  const(prev0), const(conv_w), const(ar), const(dr), const(at), const(dt)],
        out_specs=[row(GDN_WIDTH)] * 4 + [row(GDN_HEADS * cs), per_chunk],
        out_shape=[wide(F32), wide(BF16), wide(BF16), wide(BF16), jax.ShapeDtypeStruct((n, GDN_HEADS * cs), BF16),
                   jax.ShapeDtypeStruct((nc, 2 * GDN_HEADS, LANES), F32)],
        scratch_shapes=[pltpu.VMEM((SUBLANES, 3 * GDN_WIDTH), F32)],
        compiler_params=params,
        name="gdn_chunks",
    )(qkv, small, smt, prev0, conv_w, ar, dr, at, dt)
    return pl.pallas_call(
        _gdn_scan_body,
        grid=(n // rows,),
        in_specs=[row(GDN_WIDTH)] * 4 + [row(GDN_HEADS * cs), per_chunk, row(GDN_WIDTH), const(s0), const(gn)],
        out_specs=[row(GDN_WIDTH), const(s0)],
        out_shape=[wide(F32), jax.ShapeDtypeStruct(s0.shape, F32)],
        scratch_shapes=[pltpu.VMEM(s0.shape, F32)],
        compiler_params=params,
        name="gdn_scan",
    )(u, w, qe, kd, intra, dl, z, s0, gn)


def _gdn_sample_pre_body(x0_ref, x1_ref, x2_ref, x3_ref, sm_ref, cw_ref, ar_ref, dr_ref, q_ref, k_ref, v_ref, a_ref, b_ref):
    w = cw_ref[...]
    conv = x0_ref[...] * w[0:1] + x1_ref[...] * w[1:2] + x2_ref[...] * w[2:3] + x3_ref[...] * w[3:4]
    act = _silu(conv)
    sm = sm_ref[...]
    a_cols = jnp.exp(-jnp.exp(ar_ref[...]) * _softplus(sm + dr_ref[...]))
    b_cols = _sigmoid(sm)
    rows = sm.shape[0]
    for h in range(GDN_HEADS):
        hs = slice(h * GDN_HEAD_DIM, (h + 1) * GDN_HEAD_DIM)
        q_ref[:, hs] = _l2n(act[:, hs]) * GDN_HEAD_DIM ** -0.5
        k_ref[:, hs] = _l2n(act[:, GDN_WIDTH + h * GDN_HEAD_DIM:GDN_WIDTH + (h + 1) * GDN_HEAD_DIM])
        a_ref[:, hs] = jnp.broadcast_to(a_cols[:, ALPHA_OFF + h:ALPHA_OFF + h + 1], (rows, GDN_HEAD_DIM))
        b_ref[:, hs] = jnp.broadcast_to(b_cols[:, BETA_OFF + h:BETA_OFF + h + 1], (rows, GDN_HEAD_DIM))
    v_ref[...] = act[:, 2 * GDN_WIDTH:]


GDN_SAMPLE_SEQS = 8


def _gdn_sample_body(qt_ref, kt_ref, v_ref, a_ref, b_ref, z_ref, s0_ref, gn_ref, o_ref, sf_ref, *, t_len):
    gn = gn_ref[...]

    def seq(bl, carry):
        v_all, a_all, b_all, z_all = v_ref[bl], a_ref[bl], b_ref[bl], z_ref[bl]
        heads = range(GDN_HEADS)
        hsl = [slice(h * GDN_HEAD_DIM, (h + 1) * GDN_HEAD_DIM) for h in heads]
        ss = [s0_ref[bl, h] for h in heads]
        qts, kts = [qt_ref[bl, h] for h in heads], [kt_ref[bl, h] for h in heads]
        outs = [[] for _ in heads]
        for t in range(t_len):
            sks = [jnp.sum(ss[h] * kts[h][:, t:t + 1], axis=0, keepdims=True) for h in heads]
            deltas = [b_all[t:t + 1, hsl[h]] * (v_all[t:t + 1, hsl[h]] - a_all[t:t + 1, hsl[h]] * sks[h]) for h in heads]
            ss = [a_all[t:t + 1, hsl[h]] * ss[h] + kts[h][:, t:t + 1] * deltas[h] for h in heads]
            for h in heads:
                outs[h].append(jnp.sum(ss[h] * qts[h][:, t:t + 1], axis=0, keepdims=True))
        for h in heads:
            sf_ref[bl, h] = ss[h]
            o = jnp.concatenate(outs[h], axis=0)
            o_ref[bl, :, hsl[h]] = _rms(o, gn) * _silu(z_all[:, hsl[h]])
        return carry

    lax.fori_loop(0, GDN_SAMPLE_SEQS, seq, 0)


def _gdn_sample(qkv, z, small, conv_prev, s0, conv_w, a_log, dt_bias, gdn_norm):
    b, t = qkv.shape[:2]
    n = b * t
    xp = jnp.concatenate([conv_prev, qkv], axis=1)
    xs = [xp[:, j:j + t].reshape(n, 3 * GDN_WIDTH) for j in range(CONV_WIDTH)]
    ar, dr, _, _ = _gdn_consts(a_log, dt_bias)
    tm = min(256, n)
    assert n % tm == 0
    row = lambda w: pl.BlockSpec((tm, w), lambda i: (i, 0))
    const = lambda a: pl.BlockSpec(a.shape, lambda i: (0,) * a.ndim)
    q, k, v, av, bv = pl.pallas_call(
        _gdn_sample_pre_body,
        grid=(n // tm,),
        in_specs=[row(3 * GDN_WIDTH)] * 4 + [row(SMALL_COLS), const(conv_w), const(ar), const(dr)],
        out_specs=[row(GDN_WIDTH)] * 5,
        out_shape=[jax.ShapeDtypeStruct((n, GDN_WIDTH), F32)] * 5,
        compiler_params=pltpu.CompilerParams(dimension_semantics=("arbitrary",), vmem_limit_bytes=VMEM_LIMIT),
        name="gdn_sample_pre",
    )(*xs, small.reshape(n, SMALL_COLS), conv_w, ar, dr)
    tr = lambda x: x.reshape(b, t, GDN_HEADS, GDN_HEAD_DIM).transpose(0, 2, 3, 1)
    r3 = lambda x: x.reshape(b, t, GDN_WIDTH)
    bb = GDN_SAMPLE_SEQS
    assert b % bb == 0
    gn = gdn_norm.reshape(1, GDN_HEAD_DIM)
    tspec = pl.BlockSpec((bb, GDN_HEADS, GDN_HEAD_DIM, t), lambda i: (i, 0, 0, 0))
    rspec = pl.BlockSpec((bb, t, GDN_WIDTH), lambda i: (i, 0, 0))
    sspec = pl.BlockSpec((bb, GDN_HEADS, GDN_HEAD_DIM, GDN_HEAD_DIM), lambda i: (i, 0, 0, 0))
    o, sf = pl.pallas_call(
        functools.partial(_gdn_sample_body, t_len=t),
        grid=(b // bb,),
        in_specs=[tspec, tspec, rspec, rspec, rspec, rspec, sspec, const(gn)],
        out_specs=[rspec, sspec],
        out_shape=[jax.ShapeDtypeStruct((b, t, GDN_WIDTH), F32), jax.ShapeDtypeStruct(s0.shape, F32)],
        compiler_params=pltpu.CompilerParams(dimension_semantics=("arbitrary",), vmem_limit_bytes=VMEM_LIMIT),
        name="gdn_sample",
    )(tr(q), tr(k), r3(v), r3(av), r3(bv), z, s0, gn)
    return o, sf


PAGES_PER_STEP = 64
T_PAD = 8
Q_ROWS = NSA_KV_HEADS * NSA_GROUP * T_PAD
NEW_PAD = LANES


def _nsa_sample_body(pt_ref, ckp_ref, cvp_ref, skp_ref, svp_ref, q_ref, kn_ref, vn_ref, wk_ref, wv_ref, kwn_ref, vwn_ref,
                     wck_ref, wcv_ref, g_ref, ex_ref, ext_ref, o_ref,
                     buf_ref, sem_ref, ck_ref, cv_ref, selm_ref, oc_ref, m_ref, l_ref, acc_ref, ka_ref,
                     *, n_groups, t_len, win_len, page_base):
    b, ph, g = pl.program_id(0), pl.program_id(1), pl.program_id(2)
    nb_, np_, ng_ = pl.num_programs(0), pl.num_programs(1), pl.num_programs(2)
    step = (b * np_ + ph) * ng_ + g
    pps = PAGES_PER_STEP
    rows_per_step = pps * PAGE_SIZE
    blocks_per_step = rows_per_step // BLOCK

    def copies(pool_k, pool_v, bb, gg, slot):
        out = []
        for p in range(pps):
            page = page_base + pt_ref[bb, gg * pps + p]
            for kv, pool in enumerate((pool_k, pool_v)):
                out.append(pltpu.make_async_copy(pool.at[page], buf_ref.at[slot, kv, :, pl.ds(p * PAGE_SIZE, PAGE_SIZE)],
                                                 sem_ref.at[slot, kv]))
        return out

    def start(bb, pp, gg, slot):
        @pl.when(pp == 0)
        def _():
            for cp in copies(ckp_ref, cvp_ref, bb, gg, slot):
                cp.start()

        @pl.when(pp == 1)
        def _():
            for cp in copies(skp_ref, svp_ref, bb, gg, slot):
                cp.start()

    slot = step % 2

    @pl.when(step == 0)
    def _():
        start(b, ph, g, slot)
        for gg in range(n_groups):
            ka_ref[gg, NSA_KV_WIDTH:, :] = ex_ref[gg]

    nxt = step + 1

    @pl.when(nxt < nb_ * np_ * ng_)
    def _():
        start(nxt // (np_ * ng_), (nxt // ng_) % np_, nxt % ng_, 1 - slot)

    for cp in copies(ckp_ref, cvp_ref, b, g, slot):
        cp.wait()

    q = q_ref[0]
    rowt = _row_iota((Q_ROWS, LANES)) % T_PAD
    lane = lax.broadcasted_iota(jnp.int32, (Q_ROWS, LANES), 1)
    new_mask = (lane <= rowt) & (lane < t_len)

    nt = (((1,), (1,)), ((), ()))

    kvw = NSA_KV_WIDTH

    def weighted(kv, w_ref):
        w = w_ref[...]
        return jnp.concatenate([buf_ref[slot, kv, :, p * PAGE_SIZE:(p + 1) * PAGE_SIZE] * w for p in range(pps)],
                               axis=1).astype(BF16)

    @pl.when(ph == 0)
    def _():
        @pl.when(g == 0)
        def _():
            ck_ref[...] = jnp.zeros(ck_ref.shape, F32)
            cv_ref[...] = jnp.zeros(cv_ref.shape, F32)

        ck_ref[...] += jnp.dot(weighted(0, wck_ref), ext_ref[g], preferred_element_type=F32)
        cv_ref[...] += jnp.dot(weighted(1, wcv_ref), ext_ref[g], preferred_element_type=F32)

        @pl.when(g == n_groups - 1)
        def _():
            nb = ck_ref.shape[1]
            s = jnp.dot(q, ck_ref[...].astype(BF16), preferred_element_type=F32)
            m = jnp.max(s, axis=1, keepdims=True)
            e = jnp.exp(s - m)
            p = e / jnp.maximum(jnp.sum(e, axis=1, keepdims=True), 1e-30)
            oc_ref[...] = lax.dot_general(p.astype(BF16), cv_ref[...].astype(BF16), nt, preferred_element_type=F32)
            imps = []
            for h in range(NSA_KV_HEADS):
                r = h * NSA_GROUP * T_PAD
                imp = p[r:r + T_PAD]
                for gg in range(1, NSA_GROUP):
                    imp = imp + p[r + gg * T_PAD:r + (gg + 1) * T_PAD]
                imps.append(imp)
            imps.append(jnp.zeros((nb - NSA_KV_HEADS * T_PAD, nb), F32))
            imp_t = jnp.concatenate(imps, axis=0).T
            bl = _row_iota((nb, nb))
            score = jnp.where((bl == 0) | (bl == nb - 1), FORCED_SCORE, imp_t)
            sel = _topk_rows(score, bl, min(TOP_K - 1, nb)).T
            for h in range(NSA_KV_HEADS):
                for gg in range(NSA_GROUP):
                    r = (h * NSA_GROUP + gg) * T_PAD
                    selm_ref[r:r + T_PAD, :] = sel[h * T_PAD:(h + 1) * T_PAD]
            m_ref[...] = jnp.full(m_ref.shape, NEG, F32)
            l_ref[...] = jnp.zeros(l_ref.shape, F32)
            acc_ref[...] = jnp.zeros(acc_ref.shape, F32)

    def flash(s, mask, vals):
        m_old = m_ref[...]
        if mask is None:
            m_new = jnp.maximum(m_old, jnp.max(s, axis=1, keepdims=True))
            e = jnp.exp(s - m_new)
        else:
            m_new = jnp.maximum(m_old, jnp.max(jnp.where(mask, s, NEG), axis=1, keepdims=True))
            e = jnp.where(mask, jnp.exp(s - m_new), 0.0)
        a = jnp.exp(m_old - m_new)
        m_ref[...] = m_new
        l_ref[...] = a * l_ref[...] + jnp.sum(e, axis=1, keepdims=True)
        acc_ref[...] = a * acc_ref[...] + lax.dot_general(e.astype(BF16), vals, nt, preferred_element_type=F32)

    @pl.when(ph == 1)
    def _():
        ka_ref[g, 0:kvw, :] = buf_ref[slot, 0].astype(BF16)
        vb = buf_ref[slot, 1].astype(BF16)
        q_aug = jnp.concatenate([q, jnp.where(selm_ref[...] > 0.0, 0.0, NEG).astype(BF16)], axis=1)
        s = jnp.dot(q_aug, ka_ref[g], preferred_element_type=F32)
        flash(s, None, vb)

        @pl.when(g == n_groups - 1)
        def _():
            flash(jnp.dot(q, kn_ref[0], preferred_element_type=F32), new_mask, vn_ref[0])
            o_s = acc_ref[...] / jnp.maximum(l_ref[...], 1e-30)
            wk = wk_ref[0].astype(BF16)
            s_old = jnp.dot(q, wk, preferred_element_type=F32)
            s_new = jnp.dot(q, kwn_ref[0], preferred_element_type=F32)
            j = lax.broadcasted_iota(jnp.int32, (Q_ROWS, win_len), 1)
            old_mask = j > win_len - WINDOW + _row_iota((Q_ROWS, win_len)) % T_PAD
            mw = jnp.maximum(jnp.max(jnp.where(old_mask, s_old, NEG), axis=1, keepdims=True),
                             jnp.max(jnp.where(new_mask, s_new, NEG), axis=1, keepdims=True))
            e_old = jnp.where(old_mask, jnp.exp(s_old - mw), 0.0)
            e_new = jnp.where(new_mask, jnp.exp(s_new - mw), 0.0)
            lw = jnp.sum(e_old, axis=1, keepdims=True) + jnp.sum(e_new, axis=1, keepdims=True)
            o_w = (lax.dot_general(e_old.astype(BF16), wv_ref[0].astype(BF16), nt, preferred_element_type=F32)
                   + lax.dot_general(e_new.astype(BF16), vwn_ref[0], nt, preferred_element_type=F32)) / jnp.maximum(lw, 1e-30)
            gate = _sigmoid(g_ref[0])
            o_ref[0] = gate[:, 0:1] * oc_ref[...] + gate[:, 1:2] * o_s + gate[:, 2:3] * o_w


def _feature_major(x):
    lead = x.shape[:-3]
    n = len(lead)
    return x.transpose(*range(n), n + 1, n + 2, n).reshape(*lead, NSA_KV_WIDTH, x.shape[-3])


def _nsa_sample(q, ks, vs, kw, vw, gates, pools, layer, win_k, win_v, page_table, w_cmp_k, w_cmp_v):
    b, t = q.shape[:2]
    n_pages = page_table.shape[1]
    assert t <= T_PAD and t <= BLOCK and PAGE_SIZE % BLOCK == 0
    pps = PAGES_PER_STEP
    assert n_pages % pps == 0
    n_groups = n_pages // pps
    nb = n_pages * PAGE_SIZE // BLOCK
    assert nb % LANES == 0
    wb = win_k.shape[1]
    n_pool = pools[0].shape[1]
    flat = [_feature_major(p).reshape(p.shape[0] * n_pool, NSA_KV_WIDTH, PAGE_SIZE) for p in pools]
    wt = lambda w: jnp.tile(w.T, (NSA_KV_HEADS, PAGE_SIZE // BLOCK))
    wck2, wcv2 = wt(w_cmp_k), wt(w_cmp_v)
    scale = NSA_HEAD_DIM ** -0.5
    q5 = (q * scale).reshape(b, t, NSA_KV_HEADS, NSA_GROUP, NSA_HEAD_DIM).transpose(0, 2, 3, 1, 4)
    q5 = jnp.pad(q5, ((0, 0), (0, 0), (0, 0), (0, T_PAD - t), (0, 0)))
    own = jnp.eye(NSA_KV_HEADS, dtype=F32)[None, :, None, None, :, None]
    qbd = (q5[:, :, :, :, None, :] * own).reshape(b, Q_ROWS, NSA_KV_WIDTH).astype(BF16)
    padn = lambda x: jnp.pad(x, ((0, 0), (0, NEW_PAD - t), (0, 0))).astype(BF16).transpose(0, 2, 1)
    g5 = gates.reshape(b, t, NSA_KV_HEADS, NSA_GROUP, 3).transpose(0, 2, 3, 1, 4)
    g5 = jnp.pad(g5, ((0, 0), (0, 0), (0, 0), (0, T_PAD - t), (0, 0))).reshape(b, Q_ROWS, 3)
    blocks_per_step = pps * PAGE_SIZE // BLOCK
    key_blk = jnp.arange(pps * PAGE_SIZE) // BLOCK
    expand = (jnp.arange(nb)[None, :, None] == (jnp.arange(n_groups)[:, None, None] * blocks_per_step + key_blk[None, None, :]))
    expand = expand.astype(BF16)
    expand_t = expand.transpose(0, 2, 1)
    per_b = lambda shape: pl.BlockSpec((1,) + shape, lambda i, p, g, pt: (i, 0, 0))
    const = lambda a: pl.BlockSpec(a.shape, lambda i, p, g, pt: (0,) * a.ndim)
    hbm = pl.BlockSpec(memory_space=pl.ANY)
    body = functools.partial(_nsa_sample_body, n_groups=n_groups, t_len=t, win_len=wb, page_base=layer * n_pool)
    o = pl.pallas_call(
        body,
        grid_spec=pltpu.PrefetchScalarGridSpec(
            num_scalar_prefetch=1,
            grid=(b, 2, n_groups),
            in_specs=[hbm, hbm, hbm, hbm, per_b((Q_ROWS, NSA_KV_WIDTH)), per_b((NSA_KV_WIDTH, NEW_PAD)), per_b((NSA_KV_WIDTH, NEW_PAD)),
                      per_b((NSA_KV_WIDTH, wb)), per_b((NSA_KV_WIDTH, wb)), per_b((NSA_KV_WIDTH, NEW_PAD)), per_b((NSA_KV_WIDTH, NEW_PAD)),
                      const(wck2), const(wcv2), per_b((Q_ROWS, 3)), const(expand), const(expand_t)],
            out_specs=per_b((Q_ROWS, NSA_KV_WIDTH)),
            scratch_shapes=[
                pltpu.VMEM((2, 2, NSA_KV_WIDTH, pps * PAGE_SIZE), F32),
                pltpu.SemaphoreType.DMA((2, 2)),
                pltpu.VMEM((NSA_KV_WIDTH, nb), F32), pltpu.VMEM((NSA_KV_WIDTH, nb), F32),
                pltpu.VMEM((Q_ROWS, nb), F32), pltpu.VMEM((Q_ROWS, NSA_KV_WIDTH), F32),
                pltpu.VMEM((Q_ROWS, 1), F32), pltpu.VMEM((Q_ROWS, 1), F32), pltpu.VMEM((Q_ROWS, NSA_KV_WIDTH), F32),
                pltpu.VMEM((n_groups, NSA_KV_WIDTH + nb, pps * PAGE_SIZE), BF16),
            ],
        ),
        out_shape=jax.ShapeDtypeStruct((b, Q_ROWS, NSA_KV_WIDTH), F32),
        compiler_params=pltpu.CompilerParams(dimension_semantics=("arbitrary",) * 3, vmem_limit_bytes=VMEM_LIMIT),
        name="nsa_sample",
    )(page_table, *flat, qbd, padn(ks), padn(vs), _feature_major(win_k), _feature_major(win_v), padn(kw), padn(vw),
      wck2, wcv2, g5, expand, expand_t)
    o6 = o.reshape(b, NSA_KV_HEADS, NSA_GROUP, T_PAD, NSA_KV_HEADS, NSA_HEAD_DIM)[:, :, :, :t]
    o5 = jnp.stack([o6[:, h, :, :, h, :] for h in range(NSA_KV_HEADS)], axis=1)
    return o5.transpose(0, 3, 1, 2, 4).reshape(b, t, NSA_WIDTH)


def kernel(x_prompt, x_sample, cache_cmp_k, cache_cmp_v, cache_sel_k, cache_sel_v, state_win_k, state_win_v, state_conv,
           state_ssm, page_table, norm_mix_pre, norm_mix_post, norm_ffn_pre, norm_ffn_post, w_in, w_cmp_k, w_cmp_v,
           nsa_out_norm, conv_w, a_log, dt_bias, gdn_norm, w_out, w_up, w_down):
    depth = w_in.shape[0]
    bp, sp, _ = x_prompt.shape
    bs, ts, _ = x_sample.shape
    past = page_table.shape[1] * PAGE_SIZE
    pos_p = jnp.arange(sp)
    pos_s = past + jnp.arange(ts)
    pools = (cache_cmp_k, cache_cmp_v, cache_sel_k, cache_sel_v)
    kv5 = lambda x, b, n: x.reshape(b, n, NSA_KV_HEADS, NSA_HEAD_DIM)
    y_p, y_s = x_prompt, x_sample
    p_states, s_states = [], []
    for l in range(depth):
        w_perm = _permute_w_in(w_in[l]).astype(BF16)
        wck2 = jnp.tile(w_cmp_k[l], (1, LANES // NSA_HEAD_DIM))
        wcv2 = jnp.tile(w_cmp_v[l], (1, LANES // NSA_HEAD_DIM))
        post_w = (nsa_out_norm[l], w_out[l].astype(BF16), norm_mix_post[l], norm_ffn_pre[l], w_up[l].astype(BF16),
                  w_down[l].astype(BF16), norm_ffn_post[l])
        gdn_w = (conv_w[l], a_log[l], dt_bias[l], gdn_norm[l])

        wb = min(WINDOW, sp)
        conv0 = jnp.zeros((CONV_WIDTH - 1, 3 * GDN_WIDTH), F32)
        ssm0 = jnp.zeros((GDN_HEADS, GDN_HEAD_DIM, GDN_HEAD_DIM), F32)
        ys, sts = [], []
        for b in range(bp):
            x2 = y_p[b]
            (_, _, _, ks, _, kw, _, qkv, z, small, ck, cv, qt, kct, vct, kst, vst, kwt, vwt) = _in_proj(
                x2, pos_p, norm_mix_pre[l], w_perm, wck2, wcv2, feature_major=True)
            o_nsa = _nsa_prompt(qt, ks, kw, vst, vwt, ck, cv, small)
            o_gdn, ssm = _gdn_prompt(qkv, z, small, conv0, ssm0, *gdn_w)
            ys.append(_post(o_nsa, o_gdn, x2, *post_w, nsa_feature_major=True))
            conv_state = jnp.concatenate([conv0, qkv], axis=0)[-(CONV_WIDTH - 1):]
            rows_major = lambda x: x.reshape(NSA_KV_HEADS, NSA_HEAD_DIM, sp).transpose(2, 0, 1)
            sts.append((rows_major(kct), rows_major(vct), rows_major(kst), rows_major(vst),
                        rows_major(kwt)[sp - wb:], rows_major(vwt)[sp - wb:], conv_state, ssm))
        y_p = jnp.stack(ys)
        p_states.append(tuple(jnp.stack(t) for t in zip(*sts)))

        n = bs * ts
        q, kc, vc, ks, vs, kw, vw, qkv, z, small, _, _ = _in_proj(
            jnp.pad(y_s.reshape(n, D_MODEL), ((0, (-n) % BLOCK), (0, 0))), jnp.pad(jnp.tile(pos_s, bs), (0, (-n) % BLOCK)),
            norm_mix_pre[l], w_perm, wck2, wcv2)
        r3 = lambda x: x[:n].reshape(bs, ts, -1)
        q, kc, vc, ks, vs, kw, vw, qkv, z, small = map(r3, (q, kc, vc, ks, vs, kw, vw, qkv, z, small))
        win_k, win_v = state_win_k[l], state_win_v[l]
        o_nsa = _nsa_sample(q, ks, vs, kw, vw, small[..., GATE_OFF:GATE_OFF + 3 * NSA_HEADS], pools, l, win_k, win_v,
                            page_table, w_cmp_k[l], w_cmp_v[l])
        o_gdn, ssm = _gdn_sample(qkv, z, small, state_conv[l], state_ssm[l], *gdn_w)
        y_s = _post(o_nsa.reshape(n, NSA_WIDTH), o_gdn.reshape(n, GDN_WIDTH), y_s.reshape(n, D_MODEL), *post_w).reshape(bs, ts, D_MODEL)
        conv_state = jnp.concatenate([state_conv[l], qkv], axis=1)[:, -(CONV_WIDTH - 1):]
        new_wk = jnp.concatenate([win_k, kv5(kw, bs, ts)], axis=1)[:, ts:]
        new_wv = jnp.concatenate([win_v, kv5(vw, bs, ts)], axis=1)[:, ts:]
        s_states.append((kv5(kc, bs, ts), kv5(vc, bs, ts), kv5(ks, bs, ts), kv5(vs, bs, ts), new_wk, new_wv, conv_state, ssm))
    p_out = [jnp.stack(t) for t in zip(*p_states)]
    s_out = [jnp.stack(t) for t in zip(*s_states)]
    return (y_p, y_s, *p_out, *s_out)
```

```python
import functools
import math

import jax
import jax.numpy as jnp
import numpy as np
from jax import lax
from jax.experimental import pallas as pl
from jax.experimental.pallas import tpu as pltpu

D_MODEL = 1024
PAGE_SIZE = 128
NSA_HEAD_DIM = 64
NSA_HEADS = 8
NSA_KV_HEADS = 2
NSA_GROUP = NSA_HEADS // NSA_KV_HEADS
NSA_WIDTH = NSA_HEADS * NSA_HEAD_DIM
NSA_KV_WIDTH = NSA_KV_HEADS * NSA_HEAD_DIM
BLOCK = 64
TOP_K = 16
WINDOW = 512
Q_BLOCK = 128
FORCED_SCORE = NSA_GROUP + 1.0
ROPE_THETA = 500000.0
ROPE_DIM = NSA_HEAD_DIM // 4
GDN_HEAD_DIM = 128
GDN_HEADS = 4
GDN_WIDTH = GDN_HEADS * GDN_HEAD_DIM
CONV_WIDTH = 4
GDN_CHUNK = 64
D_FF = 4 * D_MODEL
EPS = 1e-6
SPLIT_SIZES = (NSA_WIDTH,) + (NSA_KV_WIDTH,) * 6 + (3 * NSA_HEADS, 3 * GDN_WIDTH, GDN_WIDTH, GDN_HEADS, GDN_HEADS)

LANES = 128
SMALL_COLS = LANES
GATE_OFF, BETA_OFF, ALPHA_OFF = 0, 3 * NSA_HEADS, 3 * NSA_HEADS + GDN_HEADS
PROJ_COLS = NSA_WIDTH + 6 * NSA_KV_WIDTH + 3 * GDN_WIDTH + GDN_WIDTH + SMALL_COLS
VMEM_LIMIT = 56 * 1024 * 1024
NEG = -1e30
F32 = jnp.float32
BF16 = jnp.bfloat16
HI = lax.Precision.HIGHEST


def _rms(x, w):
    return x * lax.rsqrt(jnp.mean(x * x, axis=-1, keepdims=True) + EPS) * w


def _sigmoid(x):
    return 1.0 / (1.0 + jnp.exp(-x))


def _silu(x):
    return x * _sigmoid(x)


def _rope_slab(x, c, sa, sb):
    half = ROPE_DIM // 2
    return x * c + pltpu.roll(x, LANES - half, axis=1) * sa + pltpu.roll(x, half, axis=1) * sb


Q_SCALE = NSA_HEAD_DIM ** -0.5 * math.log2(math.e)


def _in_proj_body(x_ref, nw_ref, w_ref, rc_ref, rsa_ref, rsb_ref, wk_ref, wv_ref,
                  q_ref, kc_ref, vc_ref, ks_ref, vs_ref, kw_ref, vw_ref, qkv_ref, z_ref, sm_ref, ck_ref, cv_ref,
                  *fm_refs, feature_major):
    h = _rms(x_ref[...], nw_ref[...])
    p = jnp.dot(h.astype(BF16), w_ref[...], preferred_element_type=F32)
    c, sa, sb = rc_ref[...], rsa_ref[...], rsb_ref[...]
    rows = p.shape[0]
    off = 0
    for j in range(NSA_WIDTH // LANES):
        qs = _rope_slab(p[:, off:off + LANES], c, sa, sb)
        q_ref[:, j * LANES:(j + 1) * LANES] = qs
        if feature_major:
            qst = (qs * Q_SCALE).T.astype(BF16)
            heads_per_slab = LANES // NSA_HEAD_DIM
            kvh = j * heads_per_slab // NSA_GROUP
            for half in range(heads_per_slab):
                g = (j * heads_per_slab + half) % NSA_GROUP
                for blk in range(rows // Q_BLOCK):
                    fm_refs[0][kvh, blk, :, g * Q_BLOCK:(g + 1) * Q_BLOCK] = qst[
                        half * NSA_HEAD_DIM:(half + 1) * NSA_HEAD_DIM, blk * Q_BLOCK:(blk + 1) * Q_BLOCK]
        off += LANES
    kc = _rope_slab(p[:, off:off + LANES], c, sa, sb)
    kc_ref[...] = kc
    vc = p[:, off + LANES:off + 2 * LANES]
    vc_ref[...] = vc
    ks = _rope_slab(p[:, off + 2 * LANES:off + 3 * LANES], c, sa, sb)
    ks_ref[...] = ks
    vs = p[:, off + 3 * LANES:off + 4 * LANES]
    vs_ref[...] = vs
    kw = _rope_slab(p[:, off + 4 * LANES:off + 5 * LANES], c, sa, sb)
    kw_ref[...] = kw
    vw = p[:, off + 5 * LANES:off + 6 * LANES]
    vw_ref[...] = vw
    if feature_major:
        for ref, slab in zip(fm_refs[1:], (kc, vc, ks, vs, kw, vw)):
            ref[...] = slab.T
    off += 6 * LANES
    qkv_ref[...] = p[:, off:off + 3 * GDN_WIDTH]
    off += 3 * GDN_WIDTH
    z_ref[...] = p[:, off:off + GDN_WIDTH]
    off += GDN_WIDTH
    sm_ref[...] = p[:, off:off + SMALL_COLS]
    ck_ref[...] = jnp.sum(kc.reshape(rows // BLOCK, BLOCK, LANES) * wk_ref[...][None], axis=1)
    cv_ref[...] = jnp.sum(vc.reshape(rows // BLOCK, BLOCK, LANES) * wv_ref[...][None], axis=1)


def _permute_w_in(w_in):
    pts = np.cumsum((0,) + SPLIT_SIZES)
    seg = [w_in[:, int(pts[i]):int(pts[i + 1])] for i in range(len(SPLIT_SIZES))]
    q, kc, vc, ks, vs, kw, vw, gates, qkv, z, beta, alpha = seg
    small = jnp.concatenate([gates, beta, alpha], axis=1)
    small = jnp.pad(small, ((0, 0), (0, SMALL_COLS - small.shape[1])))
    return jnp.concatenate([q, kc, vc, ks, vs, kw, vw, qkv, z, small], axis=1)


def _rope_tables(pos):
    half = ROPE_DIM // 2
    inv_freq = ROPE_THETA ** (-jnp.arange(half, dtype=F32) / half)
    ang = pos.astype(F32)[:, None] * inv_freq[None, :]
    cos, sin = jnp.cos(ang), jnp.sin(ang)
    n = pos.shape[0]
    one = jnp.ones((n, NSA_HEAD_DIM - ROPE_DIM), F32)
    zero = jnp.zeros((n, NSA_HEAD_DIM - ROPE_DIM), F32)
    zh = jnp.zeros((n, half), F32)
    c = jnp.concatenate([cos, cos, one], axis=1)
    sa = jnp.concatenate([-sin, zh, zero], axis=1)
    sb = jnp.concatenate([zh, sin, zero], axis=1)
    return tuple(jnp.tile(t, (1, LANES // NSA_HEAD_DIM)) for t in (c, sa, sb))


def _in_proj(x2d, pos, norm_w, w_perm, wk2, wv2, feature_major=False):
    n = x2d.shape[0]
    tm = min(512, n)
    assert n % tm == 0 and tm % BLOCK == 0
    rc, rsa, rsb = _rope_tables(pos)
    row = lambda w: pl.BlockSpec((tm, w), lambda i: (i, 0))
    const = lambda a: pl.BlockSpec(a.shape, lambda i: (0,) * a.ndim)
    nw = norm_w.reshape(1, D_MODEL)
    widths = (NSA_WIDTH,) + (LANES,) * 6 + (3 * GDN_WIDTH, GDN_WIDTH, SMALL_COLS)
    out_shape = [jax.ShapeDtypeStruct((n, w), F32) for w in widths]
    out_shape += [jax.ShapeDtypeStruct((n // BLOCK, LANES), F32)] * 2
    out_specs = [row(w) for w in widths] + [pl.BlockSpec((tm // BLOCK, LANES), lambda i: (i, 0))] * 2
    if feature_major:
        assert tm % Q_BLOCK == 0
        qb = tm // Q_BLOCK
        out_shape += [jax.ShapeDtypeStruct((NSA_KV_HEADS, n // Q_BLOCK, NSA_HEAD_DIM, NSA_GROUP * Q_BLOCK), BF16)]
        out_specs += [pl.BlockSpec((NSA_KV_HEADS, qb, NSA_HEAD_DIM, NSA_GROUP * Q_BLOCK), lambda i: (0, i, 0, 0))]
        out_shape += [jax.ShapeDtypeStruct((LANES, n), F32)] * 6
        out_specs += [pl.BlockSpec((LANES, tm), lambda i: (0, i))] * 6
    return pl.pallas_call(
        functools.partial(_in_proj_body, feature_major=feature_major),
        grid=(n // tm,),
        in_specs=[row(D_MODEL), const(nw), const(w_perm), row(LANES), row(LANES), row(LANES), const(wk2), const(wv2)],
        out_specs=out_specs,
        out_shape=out_shape,
        compiler_params=pltpu.CompilerParams(dimension_semantics=("arbitrary",), vmem_limit_bytes=VMEM_LIMIT),
        name="in_proj",
    )(x2d, nw, w_perm, rc, rsa, rsb, wk2, wv2)


FF_CHUNK = 1024


def _post_body(on_ref, og_ref, x_ref, nn_ref, wo_ref, nmp_ref, nfp_ref, wu_ref, wd_ref, nfo_ref, y_ref, *, nsa_feature_major):
    if nsa_feature_major:
        ont = on_ref[...]
        o_nsa = (ont * lax.rsqrt(jnp.mean(ont * ont, axis=0, keepdims=True) + EPS) * nn_ref[...]).T
    else:
        o_nsa = _rms(on_ref[...], nn_ref[...])
    mix = jnp.concatenate([o_nsa, og_ref[...]], axis=1).astype(BF16)
    y = jnp.dot(mix, wo_ref[...], preferred_element_type=F32)
    y1 = x_ref[...] + _rms(y, nmp_ref[...])
    h = _rms(y1, nfp_ref[...]).astype(BF16)
    acc = jnp.zeros_like(y1)
    for c in range(D_FF // FF_CHUNK):
        u = jnp.dot(h, wu_ref[:, c * FF_CHUNK:(c + 1) * FF_CHUNK], preferred_element_type=F32)
        u = jnp.square(jnp.maximum(u, 0.0))
        acc = acc + jnp.dot(u.astype(BF16), wd_ref[c * FF_CHUNK:(c + 1) * FF_CHUNK, :], preferred_element_type=F32)
    y_ref[...] = y1 + _rms(acc, nfo_ref[...])


def _post(o_nsa, o_gdn, x2d, nsa_norm, w_out, n_mix_post, n_ffn_pre, w_up, w_down, n_ffn_post, nsa_feature_major=False):
    n = x2d.shape[0]
    tm = min(256, n)
    assert n % tm == 0
    row = lambda w: pl.BlockSpec((tm, w), lambda i: (i, 0))
    const = lambda a: pl.BlockSpec(a.shape, lambda i: (0,) * a.ndim)
    r = lambda v: v.reshape(1, -1)
    nn = nsa_norm.reshape(-1, 1) if nsa_feature_major else r(nsa_norm)
    nsa_spec = pl.BlockSpec((NSA_WIDTH, tm), lambda i: (0, i)) if nsa_feature_major else row(NSA_WIDTH)
    args = (o_nsa, o_gdn, x2d, nn, w_out, r(n_mix_post), r(n_ffn_pre), w_up, w_down, r(n_ffn_post))
    return pl.pallas_call(
        functools.partial(_post_body, nsa_feature_major=nsa_feature_major),
        grid=(n // tm,),
        in_specs=[nsa_spec, row(GDN_WIDTH), row(D_MODEL)] + [const(a) for a in args[3:]],
        out_specs=row(D_MODEL),
        out_shape=jax.ShapeDtypeStruct((n, D_MODEL), F32),
        compiler_params=pltpu.CompilerParams(dimension_semantics=("arbitrary",), vmem_limit_bytes=VMEM_LIMIT),
        name="out_proj_ffn",
    )(*args)


SEL_CHUNK = 512
SEL_AUG = 16
V_ROWS = NSA_HEAD_DIM + 16
SELECT_TIERS = 4


def _row_iota(shape):
    return lax.broadcasted_iota(jnp.int32, shape, 0)


def _topk_rows(score, blk, k):
    nb = score.shape[0]
    sel = jnp.zeros(score.shape, F32)
    for _ in range(k):
        m = jnp.max(score, axis=0, keepdims=True)
        idx = jnp.min(jnp.where(score == m, blk, nb), axis=0, keepdims=True)
        pick = blk == idx
        sel = jnp.where(pick, 1.0, sel)
        score = jnp.where(pick, -jnp.inf, score)
    return sel


REDUCE_WAYS = 8


def _col_reduce(x, op):
    rows = x.shape[0]
    if rows % (REDUCE_WAYS * 8) == 0 and rows > REDUCE_WAYS * 8:
        x = op(x.reshape(rows // (REDUCE_WAYS * 8), REDUCE_WAYS * 8, x.shape[1]), axis=0)
    return op(x, axis=0, keepdims=True)


def _softmax_rows(s, mask):
    m = jnp.max(jnp.where(mask, s, NEG), axis=0, keepdims=True)
    e = jnp.where(mask, jnp.exp2(s - m), 0.0)
    return e, 1.0 / jnp.maximum(jnp.sum(e, axis=0, keepdims=True), 1e-30)


def _nsa_prompt_body(qt_ref, ks_ref, kw_ref, vt_ref, ck_ref, cvt_ref, g_ref, o_ref,
                     sb_ref, s0_ref, s1_ref, e0_ref, e1_ref, oc_ref, *, seq, kc, wn):
    i = pl.program_id(1)
    start = i * Q_BLOCK
    nb = seq // BLOCK
    g4 = range(NSA_GROUP)
    slab = lambda a, g: a[:, g * Q_BLOCK:(g + 1) * Q_BLOCK]
    qt = qt_ref[0, 0]
    pos = start + lax.broadcasted_iota(jnp.int32, (1, Q_BLOCK), 1)

    def compress_and_select(r):
        sc = jnp.dot(ck_ref[0, 0:r], qt, preferred_element_type=F32)
        blk = _row_iota((r, Q_BLOCK))
        cmask = (blk + 1) * BLOCK - 1 <= pos
        ps = []
        for g in g4:
            e, inv = _softmax_rows(slab(sc, g), cmask)
            ps.append(e * inv)
        p_all = jnp.concatenate(ps, axis=1).astype(BF16)
        if r < nb:
            p_all = jnp.concatenate([p_all, jnp.zeros((nb - r, NSA_GROUP * Q_BLOCK), BF16)], axis=0)
        oc_ref[...] = jnp.dot(cvt_ref[0], p_all, preferred_element_type=F32)
        imp = ps[0]
        for g in range(1, NSA_GROUP):
            imp = imp + ps[g]
        cur = pos // BLOCK
        forced = (blk == 0) | (blk == cur) | (blk == cur - 1)
        score = jnp.where(forced | (blk > cur), -1.0, imp)
        sel = _topk_rows(score, blk, max(min(TOP_K, nb) - 3, 0))
        sel_bias = jnp.where(forced | ((score >= 0) & (sel > 0.0)), 0.0, NEG)
        for g in g4:
            sb_ref[0:r, g * Q_BLOCK:(g + 1) * Q_BLOCK] = sel_bias

    n_qb = seq // Q_BLOCK
    whole_chunks = nb % SELECT_TIERS == 0 and (nb // SELECT_TIERS) % (kc // BLOCK) == 0
    tiers = SELECT_TIERS if whole_chunks and n_qb % SELECT_TIERS == 0 else 1
    for t in range(tiers):
        pl.when(i * tiers // n_qb == t)(functools.partial(compress_and_select, (t + 1) * nb // tiers))
    o_c = oc_ref[...]

    bpc = kc // BLOCK
    pad_rows = jnp.zeros((max(SEL_AUG - bpc, 1), NSA_GROUP * Q_BLOCK), F32)
    pad_tail = jnp.zeros((LANES - NSA_HEAD_DIM - SEL_AUG, NSA_GROUP * Q_BLOCK), BF16)

    n_parts = NSA_GROUP
    part = kc // n_parts

    def q_aug(c):
        rows = sb_ref[pl.ds(pl.multiple_of(c * bpc, bpc), bpc), :]
        if bpc < SEL_AUG:
            rows = jnp.concatenate([rows, pad_rows], axis=0)
        return jnp.concatenate([qt, rows.astype(BF16), pad_tail], axis=0)

    def scores_part(c, qa, j):
        k0 = pl.multiple_of(c * kc + j * part, part)
        return jnp.dot(ks_ref[0, pl.ds(k0, part), :], qa, preferred_element_type=F32)

    def values_dot(c, e_ref):
        k0 = pl.multiple_of(c * kc, kc)
        return jnp.dot(vt_ref[0, 0:V_ROWS, pl.ds(k0, kc)], e_ref[...], preferred_element_type=F32)

    def sweep_step(c, carry, cur, nxt, last):
        m, acc, alpha_prev = carry
        pv = values_dot(jnp.maximum(c - 1, 0), nxt[1])
        qa = None if last else q_aug(c + 1)
        k0 = pl.multiple_of(c * kc, kc)
        ms, alphas = [], []
        for g in g4:
            if not last:
                nxt[0][g * part:(g + 1) * part, :] = scores_part(c + 1, qa, g)
            sg = cur[0][:, g * Q_BLOCK:(g + 1) * Q_BLOCK]
            if last:
                sg = sg + jnp.where(k0 + _row_iota((kc, Q_BLOCK)) <= pos, 0.0, NEG)
            mo = slab(m, g)
            cur[1][:, g * Q_BLOCK:(g + 1) * Q_BLOCK] = jnp.exp2((sg - mo).astype(BF16))
            mn = mo if last else jnp.maximum(mo, _col_reduce(sg, jnp.max))
            ms.append(mn)
            alphas.append(jnp.exp2(mo - mn))
        cat = lambda xs: jnp.concatenate(xs, axis=1)
        return cat(ms), (acc + pv) * alpha_prev, cat(alphas)

    def finish(c, carry, cur, nxt):
        _, acc, _ = sweep_step(c, carry, cur, nxt, True)
        return acc + values_dot(c, cur[1])

    width = NSA_GROUP * Q_BLOCK
    even, odd = (s0_ref, e0_ref), (s1_ref, e1_ref)
    qa0 = q_aug(0)
    for g in g4:
        s0_ref[g * part:(g + 1) * part, :] = scores_part(0, qa0, g)
    e1_ref[...] = jnp.zeros(e1_ref.shape, BF16)
    bias0 = jnp.where(_row_iota((kc, Q_BLOCK)) <= pos, 0.0, NEG)
    m0 = jnp.concatenate([_col_reduce(s0_ref[:, g * Q_BLOCK:(g + 1) * Q_BLOCK] + bias0, jnp.max) for g in g4], axis=1)
    init = (m0, jnp.zeros((V_ROWS, width), F32), jnp.ones((1, width), F32))
    c_diag = start // kc

    def pair(p, carry):
        carry = sweep_step(2 * p, carry, even, odd, False)
        return sweep_step(2 * p + 1, carry, odd, even, False)

    quad = lambda p, cr: pair(2 * p + 1, pair(2 * p, cr))
    octo = lambda p, cr: quad(2 * p + 1, quad(2 * p, cr))
    carry = lax.fori_loop(0, c_diag // 16, lambda p, cr: octo(2 * p + 1, octo(2 * p, cr)), init)
    carry = lax.fori_loop(c_diag // 16 * 2, c_diag // 8, octo, carry)
    carry = lax.fori_loop(c_diag // 8 * 2, c_diag // 4, quad, carry)
    carry = lax.fori_loop(c_diag // 4 * 2, c_diag // 2, pair, carry)
    acc_s = lax.cond(
        c_diag % 2 == 1,
        lambda cr: finish(c_diag, sweep_step(c_diag - 1, cr, even, odd, False), odd, even),
        lambda cr: finish(c_diag, cr, even, odd),
        carry)
    def exact_step(c, carry):
        m, acc = carry
        s = jnp.concatenate([scores_part(c, q_aug(c), j) for j in range(n_parts)], axis=0)
        s = s + jnp.concatenate([jnp.where(c * kc + _row_iota((kc, Q_BLOCK)) <= pos, 0.0, NEG)] * NSA_GROUP, axis=1)
        mn = jnp.maximum(m, jnp.max(s, axis=0, keepdims=True))
        e = jnp.exp2((s - mn).astype(BF16))
        k0 = pl.multiple_of(c * kc, kc)
        pv = jnp.dot(vt_ref[0, 0:V_ROWS, pl.ds(k0, kc)], e, preferred_element_type=F32)
        return mn, jnp.exp2(m - mn) * acc + pv

    l_s = acc_s[NSA_HEAD_DIM:NSA_HEAD_DIM + 1]
    sane = (jnp.abs(acc_s) < 3e38) & (l_s >= 1.0) & (l_s < 1e30)
    acc_s = lax.cond(
        jnp.max(jnp.where(sane, 0.0, 1.0)) > 0.0,
        lambda a: lax.fori_loop(0, c_diag + 1, exact_step,
                                (jnp.full((1, width), NEG, F32), jnp.zeros((V_ROWS, width), F32)))[1],
        lambda a: a,
        acc_s)
    o_s = acc_s[0:NSA_HEAD_DIM] * (1.0 / jnp.maximum(acc_s[NSA_HEAD_DIM:NSA_HEAD_DIM + 1], 1e-30))

    w0 = pl.multiple_of(jnp.maximum(start + Q_BLOCK - wn, 0), Q_BLOCK)
    sw = jnp.dot(kw_ref[0, pl.ds(w0, wn), :], qt, preferred_element_type=F32)
    dist = pos - (w0 + _row_iota((wn, Q_BLOCK)))
    wbias = jnp.where((dist >= 0) & (dist < WINDOW), 0.0, NEG)
    ews = []
    for g in g4:
        sg = slab(sw, g) + wbias
        ews.append(jnp.exp2((sg - jnp.max(sg, axis=0, keepdims=True)).astype(BF16)))
    ow = jnp.dot(vt_ref[0, V_ROWS:2 * V_ROWS, pl.ds(w0, wn)], jnp.concatenate(ews, axis=1), preferred_element_type=F32)
    o_w = ow[0:NSA_HEAD_DIM] * (1.0 / jnp.maximum(ow[NSA_HEAD_DIM:NSA_HEAD_DIM + 1], 1e-30))

    gate = _sigmoid(g_ref[0])
    for g in g4:
        o_ref[0, g] = (gate[3 * g:3 * g + 1] * slab(o_c, g) + gate[3 * g + 1:3 * g + 2] * slab(o_s, g)
                       + gate[3 * g + 2:3 * g + 3] * slab(o_w, g))


def _split_heads_t(x, n):
    return x.reshape(n, NSA_KV_HEADS, NSA_HEAD_DIM).transpose(1, 0, 2)


def _nsa_prompt(qt, ks, kw, vst, vwt, ck, cv, small):
    s = ks.shape[0]
    assert s % Q_BLOCK == 0
    n_qb, nb = s // Q_BLOCK, s // BLOCK
    kc = min(SEL_CHUNK, s)
    wn = min(WINDOW + Q_BLOCK, s)
    assert s % kc == 0
    bpc = kc // BLOCK
    assert kc % Q_BLOCK == 0 and bpc <= SEL_AUG
    onehot = ((jnp.arange(s) // BLOCK) % bpc)[:, None] == jnp.arange(LANES - NSA_HEAD_DIM)[None, :]
    ksa = jnp.concatenate([_split_heads_t(ks, s).astype(BF16),
                           jnp.broadcast_to(onehot.astype(BF16), (NSA_KV_HEADS, s, LANES - NSA_HEAD_DIM))], axis=-1)
    kwh = _split_heads_t(kw, s).astype(BF16)
    ones_fill = jnp.zeros((NSA_KV_HEADS, V_ROWS - NSA_HEAD_DIM, s), BF16).at[:, 0, :].set(1.0)
    fm = lambda x: x.reshape(NSA_KV_HEADS, NSA_HEAD_DIM, s).astype(BF16)
    vt = jnp.concatenate([fm(vst), ones_fill, fm(vwt), ones_fill], axis=1)
    ckh = _split_heads_t(ck, nb).astype(BF16)
    cvt = _split_heads_t(cv, nb).astype(BF16).transpose(0, 2, 1)
    gt = small.T[GATE_OFF:GATE_OFF + 3 * NSA_HEADS].reshape(NSA_KV_HEADS, NSA_GROUP * 3, s)
    gt = jnp.pad(gt, ((0, 0), (0, 16 - NSA_GROUP * 3), (0, 0)))
    body = functools.partial(_nsa_prompt_body, seq=s, kc=kc, wn=wn)
    o = pl.pallas_call(
        body,
        grid=(NSA_KV_HEADS, n_qb),
        in_specs=[
            pl.BlockSpec((1, 1, NSA_HEAD_DIM, NSA_GROUP * Q_BLOCK), lambda h, i: (h, i, 0, 0)),
            pl.BlockSpec((1, s, LANES), lambda h, i: (h, 0, 0)),
            pl.BlockSpec((1, s, NSA_HEAD_DIM), lambda h, i: (h, 0, 0)),
            pl.BlockSpec((1, 2 * V_ROWS, s), lambda h, i: (h, 0, 0)),
            pl.BlockSpec((1, nb, NSA_HEAD_DIM), lambda h, i: (h, 0, 0)),
            pl.BlockSpec((1, NSA_HEAD_DIM, nb), lambda h, i: (h, 0, 0)),
            pl.BlockSpec((1, 16, Q_BLOCK), lambda h, i: (h, 0, i)),
        ],
        out_specs=pl.BlockSpec((1, NSA_GROUP, NSA_HEAD_DIM, Q_BLOCK), lambda h, i: (h, 0, 0, i)),
        out_shape=jax.ShapeDtypeStruct((NSA_KV_HEADS, NSA_GROUP, NSA_HEAD_DIM, s), F32),
        scratch_shapes=[pltpu.VMEM((nb, NSA_GROUP * Q_BLOCK), F32)]
        + [pltpu.VMEM((kc, NSA_GROUP * Q_BLOCK), F32)] * 2 + [pltpu.VMEM((kc, NSA_GROUP * Q_BLOCK), BF16)] * 2
        + [pltpu.VMEM((NSA_HEAD_DIM, NSA_GROUP * Q_BLOCK), F32)],
        compiler_params=pltpu.CompilerParams(dimension_semantics=("arbitrary", "arbitrary"), vmem_limit_bytes=VMEM_LIMIT),
        name="nsa_prompt",
    )(qt, ksa, kwh, vt, ckh, cvt, gt)
    return o.reshape(NSA_WIDTH, s)


def _dot(a, b):
    return jnp.dot(a, b, precision=HI, preferred_element_type=F32)


def _dot_nt(a, b):
    return lax.dot_general(a, b, (((1,), (1,)), ((), ())), precision=HI, preferred_element_type=F32)


def _dot_tn(a, b):
    return lax.dot_general(a, b, (((0,), (0,)), ((), ())), precision=HI, preferred_element_type=F32)


def _dot1_nt(a, b):
    return lax.dot_general(a, b, (((1,), (1,)), ((), ())), preferred_element_type=F32)


def _split16(a):
    hi = a.astype(BF16)
    return hi, (a - hi.astype(F32)).astype(BF16)


def _dot3(a, b):
    return jnp.dot(a.astype(BF16), b.astype(BF16), preferred_element_type=F32)


def _softplus(x):
    return jnp.maximum(x, 0.0) + jnp.log(1.0 + jnp.exp(-jnp.abs(x)))


def _unit_lower_inverses(mats, ri, ci):
    c = mats[0].shape[0]
    eye = (ri == ci).astype(F32)
    base = 8
    same = ri // base == ci // base
    ds = [jnp.where(same, a, 0.0) for a in mats]
    d2s = [_dot3(d, d) for d in ds]
    xs = [(eye - d) + _dot3(eye - d, d2) for d, d2 in zip(ds, d2s)]
    d4s = [_dot3(d2, d2) for d2 in d2s]
    xs = [x + _dot3(x, d4) for x, d4 in zip(xs, d4s)]
    b = base
    while b < c:
        band = (ri // (2 * b) == ci // (2 * b)) & (ri // b != ci // b)
        lxs = [_dot3(jnp.where(band, a, 0.0), x) for a, x in zip(mats, xs)]
        xs = [x - _dot3(x, lx) for x, lx in zip(xs, lxs)]
        b *= 2
    return xs


SUBLANES = 8


def _conv_silu(x, prev, w):
    rowi = _row_iota((SUBLANES, x.shape[1]))
    conv = None
    for j in range(CONV_WIDTH - 1, -1, -1):
        if j == 0:
            xs = x
        else:
            xr = pltpu.roll(x, j, axis=0)
            head = jnp.where(rowi < j, pltpu.roll(prev, j, axis=0), xr[0:SUBLANES])
            xs = jnp.concatenate([head, xr[SUBLANES:]], axis=0)
        term = xs * w[CONV_WIDTH - 1 - j:CONV_WIDTH - j]
        conv = term if conv is None else conv + term
    return _silu(conv)


def _l2n(x):
    return x * lax.rsqrt(jnp.sum(x * x, axis=-1, keepdims=True) + EPS)


GDN_STEP_CHUNKS = 4


def _gdn_chunk_body(qkv_ref, sm_ref, smt_ref, prev0_ref, cw_ref, ar_ref, dr_ref, at_ref, dt_ref,
                    u_ref, w_ref, qe_ref, kd_ref, in_ref, dl_ref, xp_ref):
    cs = GDN_CHUNK

    @pl.when(pl.program_id(0) == 0)
    def _():
        xp_ref[...] = prev0_ref[...]

    x = qkv_ref[...]
    act = _conv_silu(x, xp_ref[...], cw_ref[...])
    xp_ref[...] = x[x.shape[0] - SUBLANES:]
    sm = sm_ref[...]
    ri = _row_iota((cs, cs))
    ci = lax.broadcasted_iota(jnp.int32, (cs, cs), 1)
    tri = (ri >= ci).astype(F32)
    tri_t = (ri <= ci).astype(F32)
    g_all = -jnp.exp(ar_ref[...]) * _softplus(sm + dr_ref[...])
    beta_all = _sigmoid(sm)
    scale = GDN_HEAD_DIM ** -0.5
    a_mats, rhs, where = [], [], []
    for c in range(GDN_STEP_CHUNKS):
        rows = slice(c * cs, (c + 1) * cs)
        gc_cols = _dot(tri, g_all[rows])
        g_rows = -jnp.exp(at_ref[...]) * _softplus(smt_ref[c] + dt_ref[...])
        gc_rows = _dot(g_rows, tri_t)
        intras, dls = [], []
        for h in range(GDN_HEADS):
            hs = slice(h * GDN_HEAD_DIM, (h + 1) * GDN_HEAD_DIM)
            q = _l2n(act[rows, hs]) * scale
            k = _l2n(act[rows, GDN_WIDTH + h * GDN_HEAD_DIM:GDN_WIDTH + (h + 1) * GDN_HEAD_DIM])
            v = act[rows, 2 * GDN_WIDTH + h * GDN_HEAD_DIM:2 * GDN_WIDTH + (h + 1) * GDN_HEAD_DIM]
            beta = beta_all[rows, BETA_OFF + h:BETA_OFF + h + 1]
            gc = gc_cols[:, ALPHA_OFF + h:ALPHA_OFF + h + 1]
            gr = gc_rows[GDN_HEADS + h:GDN_HEADS + h + 1, :]
            g_last = gc[cs - 1:cs, :]
            decay = jnp.where(ri >= ci, jnp.exp(gc - gr), 0.0)
            kb = k * beta
            k16 = k.astype(BF16)
            a_mats.append(jnp.where(ri > ci, _dot1_nt(kb.astype(BF16), k16) * decay, 0.0))
            rhs.append(jnp.concatenate([v * beta, kb * jnp.exp(gc)], axis=1))
            where.append((rows, hs))
            qe_ref[rows, hs] = (q * jnp.exp(gc)).astype(BF16)
            kd_ref[rows, hs] = (k * jnp.exp(g_last - gc)).astype(BF16)
            intras.append((_dot1_nt(q.astype(BF16), k16) * decay).astype(BF16))
            dls.append(jnp.broadcast_to(jnp.exp(g_last), (1, LANES)))
        in_ref[rows, :] = jnp.concatenate(intras, axis=1)
        dl_ref[c] = jnp.concatenate(dls + [jnp.zeros((2 * GDN_HEADS - GDN_HEADS, LANES), F32)], axis=0)
    for t, r, (rows, hs) in zip(_unit_lower_inverses(a_mats, ri, ci), rhs, where):
        sol = _dot3(t, r)
        u_ref[rows, hs] = sol[:, :GDN_HEAD_DIM]
        w_ref[rows, hs] = sol[:, GDN_HEAD_DIM:].astype(BF16)


def _gdn_scan_body(u_ref, w_ref, qe_ref, kd_ref, in_ref, dl_ref, z_ref, s0_ref, gn_ref, o_ref, sf_ref, s_ref):
    cs = GDN_CHUNK
    i = pl.program_id(0)

    @pl.when(i == 0)
    def _():
        s_ref[...] = s0_ref[...]

    heads = range(GDN_HEADS)
    hsl = [slice(h * GDN_HEAD_DIM, (h + 1) * GDN_HEAD_DIM) for h in heads]
    tn = (((0,), (0,)), ((), ()))
    ss = [s_ref[h] for h in heads]
    for c in range(GDN_STEP_CHUNKS):
        rows = slice(c * cs, (c + 1) * cs)
        s16 = [s.astype(BF16) for s in ss]
        v16 = [(u_ref[rows, hsl[h]] - jnp.dot(w_ref[rows, hsl[h]], s16[h], preferred_element_type=F32)).astype(BF16)
               for h in heads]
        ss = [ss[h] * dl_ref[c, h:h + 1, :] + lax.dot_general(kd_ref[rows, hsl[h]], v16[h], tn, preferred_element_type=F32)
              for h in heads]
        for h in heads:
            o = (jnp.dot(qe_ref[rows, hsl[h]], s16[h], preferred_element_type=F32)
                 + jnp.dot(in_ref[rows, h * cs:(h + 1) * cs], v16[h], preferred_element_type=F32))
            o_ref[rows, hsl[h]] = _rms(o, gn_ref[...]) * _silu(z_ref[rows, hsl[h]])
    for h in heads:
        s_ref[h] = ss[h]

    @pl.when(i == pl.num_programs(0) - 1)
    def _():
        sf_ref[...] = s_ref[...]


def _gdn_consts(a_log, dt_bias):
    lane = jnp.zeros((1, LANES), F32)
    ar = lane.at[0, ALPHA_OFF:ALPHA_OFF + GDN_HEADS].set(a_log)
    dr = lane.at[0, ALPHA_OFF:ALPHA_OFF + GDN_HEADS].set(dt_bias)
    col = jnp.zeros((2 * GDN_HEADS, 1), F32)
    at = jnp.broadcast_to(col.at[GDN_HEADS:, 0].set(a_log), (2 * GDN_HEADS, GDN_CHUNK))
    dt = jnp.broadcast_to(col.at[GDN_HEADS:, 0].set(dt_bias), (2 * GDN_HEADS, GDN_CHUNK))
    return ar, dr, at, dt


def _gdn_prompt(qkv, z, small, conv_prev, s0, conv_w, a_log, dt_bias, gdn_norm):
    n = qkv.shape[0]
    cs = GDN_CHUNK
    rows = GDN_STEP_CHUNKS * cs
    assert n % rows == 0
    nc = n // cs
    smt = small.T[BETA_OFF:BETA_OFF + 2 * GDN_HEADS].reshape(2 * GDN_HEADS, nc, cs).transpose(1, 0, 2)
    prev0 = jnp.pad(conv_prev, ((SUBLANES - (CONV_WIDTH - 1), 0), (0, 0)))
    ar, dr, at, dt = _gdn_consts(a_log, dt_bias)
    gn = gdn_norm.reshape(1, GDN_HEAD_DIM)
    row = lambda w: pl.BlockSpec((rows, w), lambda i: (i, 0))
    per_chunk = pl.BlockSpec((GDN_STEP_CHUNKS, 2 * GDN_HEADS, LANES), lambda i: (i, 0, 0))
    const = lambda a: pl.BlockSpec(a.shape, lambda i: (0,) * a.ndim)
    params = pltpu.CompilerParams(dimension_semantics=("arbitrary",), vmem_limit_bytes=VMEM_LIMIT)
    wide = lambda dt_: jax.ShapeDtypeStruct((n, GDN_WIDTH), dt_)
    u, w, qe, kd, intra, dl = pl.pallas_call(
        _gdn_chunk_body,
        grid=(n // rows,),
        in_specs=[row(3 * GDN_WIDTH), row(SMALL_COLS),
                  pl.BlockSpec((GDN_STEP_CHUNKS, 2 * GDN_HEADS, cs), lambda i: (i, 0, 0)),
                  const(prev0), const(conv_w), const(ar), const(dr), const(at), const(dt)],
        out_specs=[row(GDN_WIDTH)] * 4 + [row(GDN_HEADS * cs), per_chunk],
        out_shape=[wide(F32), wide(BF16), wide(BF16), wide(BF16), jax.ShapeDtypeStruct((n, GDN_HEADS * cs), BF16),
                   jax.ShapeDtypeStruct((nc, 2 * GDN_HEADS, LANES), F32)],
        scratch_shapes=[pltpu.VMEM((SUBLANES, 3 * GDN_WIDTH), F32)],
        compiler_params=params,
        name="gdn_chunks",
    )(qkv, small, smt, prev0, conv_w, ar, dr, at, dt)
    return pl.pallas_call(
        _gdn_scan_body,
        grid=(n // rows,),
        in_specs=[row(GDN_WIDTH)] * 4 + [row(GDN_HEADS * cs), per_chunk, row(GDN_WIDTH), const(s0), const(gn)],
        out_specs=[row(GDN_WIDTH), const(s0)],
        out_shape=[wide(F32), jax.ShapeDtypeStruct(s0.shape, F32)],
        scratch_shapes=[pltpu.VMEM(s0.shape, F32)],
        compiler_params=params,
        name="gdn_scan",
    )(u, w, qe, kd, intra, dl, z, s0, gn)


def _gdn_sample_pre_body(x0_ref, x1_ref, x2_ref, x3_ref, sm_ref, cw_ref, ar_ref, dr_ref, q_ref, k_ref, v_ref, a_ref, b_ref):
    w = cw_ref[...]
    conv = x0_ref[...] * w[0:1] + x1_ref[...] * w[1:2] + x2_ref[...] * w[2:3] + x3_ref[...] * w[3:4]
    act = _silu(conv)
    sm = sm_ref[...]
    a_cols = jnp.exp(-jnp.exp(ar_ref[...]) * _softplus(sm + dr_ref[...]))
    b_cols = _sigmoid(sm)
    rows = sm.shape[0]
    for h in range(GDN_HEADS):
        hs = slice(h * GDN_HEAD_DIM, (h + 1) * GDN_HEAD_DIM)
        q_ref[:, hs] = _l2n(act[:, hs]) * GDN_HEAD_DIM ** -0.5
        k_ref[:, hs] = _l2n(act[:, GDN_WIDTH + h * GDN_HEAD_DIM:GDN_WIDTH + (h + 1) * GDN_HEAD_DIM])
        a_ref[:, hs] = jnp.broadcast_to(a_cols[:, ALPHA_OFF + h:ALPHA_OFF + h + 1], (rows, GDN_HEAD_DIM))
        b_ref[:, hs] = jnp.broadcast_to(b_cols[:, BETA_OFF + h:BETA_OFF + h + 1], (rows, GDN_HEAD_DIM))
    v_ref[...] = act[:, 2 * GDN_WIDTH:]


GDN_SAMPLE_SEQS = 8


def _gdn_sample_body(qt_ref, kt_ref, v_ref, a_ref, b_ref, z_ref, s0_ref, gn_ref, o_ref, sf_ref, *, t_len):
    gn = gn_ref[...]

    def seq(bl, carry):
        v_all, a_all, b_all, z_all = v_ref[bl], a_ref[bl], b_ref[bl], z_ref[bl]
        heads = range(GDN_HEADS)
        hsl = [slice(h * GDN_HEAD_DIM, (h + 1) * GDN_HEAD_DIM) for h in heads]
        ss = [s0_ref[bl, h] for h in heads]
        qts, kts = [qt_ref[bl, h] for h in heads], [kt_ref[bl, h] for h in heads]
        outs = [[] for _ in heads]
        for t in range(t_len):
            sks = [jnp.sum(ss[h] * kts[h][:, t:t + 1], axis=0, keepdims=True) for h in heads]
            deltas = [b_all[t:t + 1, hsl[h]] * (v_all[t:t + 1, hsl[h]] - a_all[t:t + 1, hsl[h]] * sks[h]) for h in heads]
            ss = [a_all[t:t + 1, hsl[h]] * ss[h] + kts[h][:, t:t + 1] * deltas[h] for h in heads]
            for h in heads:
                outs[h].append(jnp.sum(ss[h] * qts[h][:, t:t + 1], axis=0, keepdims=True))
        for h in heads:
            sf_ref[bl, h] = ss[h]
            o = jnp.concatenate(outs[h], axis=0)
            o_ref[bl, :, hsl[h]] = _rms(o, gn) * _silu(z_all[:, hsl[h]])
        return carry

    lax.fori_loop(0, GDN_SAMPLE_SEQS, seq, 0)


def _gdn_sample(qkv, z, small, conv_prev, s0, conv_w, a_log, dt_bias, gdn_norm):
    b, t = qkv.shape[:2]
    n = b * t
    xp = jnp.concatenate([conv_prev, qkv], axis=1)
    xs = [xp[:, j:j + t].reshape(n, 3 * GDN_WIDTH) for j in range(CONV_WIDTH)]
    ar, dr, _, _ = _gdn_consts(a_log, dt_bias)
    tm = min(256, n)
    assert n % tm == 0
    row = lambda w: pl.BlockSpec((tm, w), lambda i: (i, 0))
    const = lambda a: pl.BlockSpec(a.shape, lambda i: (0,) * a.ndim)
    q, k, v, av, bv = pl.pallas_call(
        _gdn_sample_pre_body,
        grid=(n // tm,),
        in_specs=[row(3 * GDN_WIDTH)] * 4 + [row(SMALL_COLS), const(conv_w), const(ar), const(dr)],
        out_specs=[row(GDN_WIDTH)] * 5,
        out_shape=[jax.ShapeDtypeStruct((n, GDN_WIDTH), F32)] * 5,
        compiler_params=pltpu.CompilerParams(dimension_semantics=("arbitrary",), vmem_limit_bytes=VMEM_LIMIT),
        name="gdn_sample_pre",
    )(*xs, small.reshape(n, SMALL_COLS), conv_w, ar, dr)
    tr = lambda x: x.reshape(b, t, GDN_HEADS, GDN_HEAD_DIM).transpose(0, 2, 3, 1)
    r3 = lambda x: x.reshape(b, t, GDN_WIDTH)
    bb = GDN_SAMPLE_SEQS
    assert b % bb == 0
    gn = gdn_norm.reshape(1, GDN_HEAD_DIM)
    tspec = pl.BlockSpec((bb, GDN_HEADS, GDN_HEAD_DIM, t), lambda i: (i, 0, 0, 0))
    rspec = pl.BlockSpec((bb, t, GDN_WIDTH), lambda i: (i, 0, 0))
    sspec = pl.BlockSpec((bb, GDN_HEADS, GDN_HEAD_DIM, GDN_HEAD_DIM), lambda i: (i, 0, 0, 0))
    o, sf = pl.pallas_call(
        functools.partial(_gdn_sample_body, t_len=t),
        grid=(b // bb,),
        in_specs=[tspec, tspec, rspec, rspec, rspec, rspec, sspec, const(gn)],
        out_specs=[rspec, sspec],
        out_shape=[jax.ShapeDtypeStruct((b, t, GDN_WIDTH), F32), jax.ShapeDtypeStruct(s0.shape, F32)],
        compiler_params=pltpu.CompilerParams(dimension_semantics=("arbitrary",), vmem_limit_bytes=VMEM_LIMIT),
        name="gdn_sample",
    )(tr(q), tr(k), r3(v), r3(av), r3(bv), z, s0, gn)
    return o, sf


PAGES_PER_STEP = 64
T_PAD = 8
Q_ROWS = NSA_KV_HEADS * NSA_GROUP * T_PAD
NEW_PAD = LANES


def _nsa_sample_body(pt_ref, ckp_ref, cvp_ref, skp_ref, svp_ref, q_ref, kn_ref, vn_ref, wk_ref, wv_ref, kwn_ref, vwn_ref,
                     wck_ref, wcv_ref, g_ref, ex_ref, ext_ref, o_ref,
                     buf_ref, sem_ref, ck_ref, cv_ref, selm_ref, oc_ref, m_ref, l_ref, acc_ref, ka_ref,
                     *, n_groups, t_len, win_len, page_base):
    b, ph, g = pl.program_id(0), pl.program_id(1), pl.program_id(2)
    nb_, np_, ng_ = pl.num_programs(0), pl.num_programs(1), pl.num_programs(2)
    step = (b * np_ + ph) * ng_ + g
    pps = PAGES_PER_STEP
    rows_per_step = pps * PAGE_SIZE
    blocks_per_step = rows_per_step // BLOCK

    def copies(pool_k, pool_v, bb, gg, slot):
        out = []
        for p in range(pps):
            page = page_base + pt_ref[bb, gg * pps + p]
            for kv, pool in enumerate((pool_k, pool_v)):
                out.append(pltpu.make_async_copy(pool.at[page], buf_ref.at[slot, kv, :, pl.ds(p * PAGE_SIZE, PAGE_SIZE)],
                                                 sem_ref.at[slot, kv]))
        return out

    def start(bb, pp, gg, slot):
        @pl.when(pp == 0)
        def _():
            for cp in copies(ckp_ref, cvp_ref, bb, gg, slot):
                cp.start()

        @pl.when(pp == 1)
        def _():
            for cp in copies(skp_ref, svp_ref, bb, gg, slot):
                cp.start()

    slot = step % 2

    @pl.when(step == 0)
    def _():
        start(b, ph, g, slot)
        for gg in range(n_groups):
            ka_ref[gg, NSA_KV_WIDTH:, :] = ex_ref[gg]

    nxt = step + 1

    @pl.when(nxt < nb_ * np_ * ng_)
    def _():
        start(nxt // (np_ * ng_), (nxt // ng_) % np_, nxt % ng_, 1 - slot)

    for cp in copies(ckp_ref, cvp_ref, b, g, slot):
        cp.wait()

    q = q_ref[0]
    rowt = _row_iota((Q_ROWS, LANES)) % T_PAD
    lane = lax.broadcasted_iota(jnp.int32, (Q_ROWS, LANES), 1)
    new_mask = (lane <= rowt) & (lane < t_len)

    nt = (((1,), (1,)), ((), ()))

    kvw = NSA_KV_WIDTH

    def weighted(kv, w_ref):
        w = w_ref[...]
        return jnp.concatenate([buf_ref[slot, kv, :, p * PAGE_SIZE:(p + 1) * PAGE_SIZE] * w for p in range(pps)],
                               axis=1).astype(BF16)

    @pl.when(ph == 0)
    def _():
        @pl.when(g == 0)
        def _():
            ck_ref[...] = jnp.zeros(ck_ref.shape, F32)
            cv_ref[...] = jnp.zeros(cv_ref.shape, F32)

        ck_ref[...] += jnp.dot(weighted(0, wck_ref), ext_ref[g], preferred_element_type=F32)
        cv_ref[...] += jnp.dot(weighted(1, wcv_ref), ext_ref[g], preferred_element_type=F32)

        @pl.when(g == n_groups - 1)
        def _():
            nb = ck_ref.shape[1]
            s = jnp.dot(q, ck_ref[...].astype(BF16), preferred_element_type=F32)
            m = jnp.max(s, axis=1, keepdims=True)
            e = jnp.exp(s - m)
            p = e / jnp.maximum(jnp.sum(e, axis=1, keepdims=True), 1e-30)
            oc_ref[...] = lax.dot_general(p.astype(BF16), cv_ref[...].astype(BF16), nt, preferred_element_type=F32)
            imps = []
            for h in range(NSA_KV_HEADS):
                r = h * NSA_GROUP * T_PAD
                imp = p[r:r + T_PAD]
                for gg in range(1, NSA_GROUP):
                    imp = imp + p[r + gg * T_PAD:r + (gg + 1) * T_PAD]
                imps.append(imp)
            imps.append(jnp.zeros((nb - NSA_KV_HEADS * T_PAD, nb), F32))
            imp_t = jnp.concatenate(imps, axis=0).T
            bl = _row_iota((nb, nb))
            score = jnp.where((bl == 0) | (bl == nb - 1), FORCED_SCORE, imp_t)
            sel = _topk_rows(score, bl, min(TOP_K - 1, nb)).T
            for h in range(NSA_KV_HEADS):
                for gg in range(NSA_GROUP):
                    r = (h * NSA_GROUP + gg) * T_PAD
                    selm_ref[r:r + T_PAD, :] = sel[h * T_PAD:(h + 1) * T_PAD]
            m_ref[...] = jnp.full(m_ref.shape, NEG, F32)
            l_ref[...] = jnp.zeros(l_ref.shape, F32)
            acc_ref[...] = jnp.zeros(acc_ref.shape, F32)

    def flash(s, mask, vals):
        m_old = m_ref[...]
        if mask is None:
            m_new = jnp.maximum(m_old, jnp.max(s, axis=1, keepdims=True))
            e = jnp.exp(s - m_new)
        else:
            m_new = jnp.maximum(m_old, jnp.max(jnp.where(mask, s, NEG), axis=1, keepdims=True))
            e = jnp.where(mask, jnp.exp(s - m_new), 0.0)
        a = jnp.exp(m_old - m_new)
        m_ref[...] = m_new
        l_ref[...] = a * l_ref[...] + jnp.sum(e, axis=1, keepdims=True)
        acc_ref[...] = a * acc_ref[...] + lax.dot_general(e.astype(BF16), vals, nt, preferred_element_type=F32)

    @pl.when(ph == 1)
    def _():
        ka_ref[g, 0:kvw, :] = buf_ref[slot, 0].astype(BF16)
        vb = buf_ref[slot, 1].astype(BF16)
        q_aug = jnp.concatenate([q, jnp.where(selm_ref[...] > 0.0, 0.0, NEG).astype(BF16)], axis=1)
        s = jnp.dot(q_aug, ka_ref[g], preferred_element_type=F32)
        flash(s, None, vb)

        @pl.when(g == n_groups - 1)
        def _():
            flash(jnp.dot(q, kn_ref[0], preferred_element_type=F32), new_mask, vn_ref[0])
            o_s = acc_ref[...] / jnp.maximum(l_ref[...], 1e-30)
            wk = wk_ref[0].astype(BF16)
            s_old = jnp.dot(q, wk, preferred_element_type=F32)
            s_new = jnp.dot(q, kwn_ref[0], preferred_element_type=F32)
            j = lax.broadcasted_iota(jnp.int32, (Q_ROWS, win_len), 1)
            old_mask = j > win_len - WINDOW + _row_iota((Q_ROWS, win_len)) % T_PAD
            mw = jnp.maximum(jnp.max(jnp.where(old_mask, s_old, NEG), axis=1, keepdims=True),
                             jnp.max(jnp.where(new_mask, s_new, NEG), axis=1, keepdims=True))
            e_old = jnp.where(old_mask, jnp.exp(s_old - mw), 0.0)
            e_new = jnp.where(new_mask, jnp.exp(s_new - mw), 0.0)
            lw = jnp.sum(e_old, axis=1, keepdims=True) + jnp.sum(e_new, axis=1, keepdims=True)
            o_w = (lax.dot_general(e_old.astype(BF16), wv_ref[0].astype(BF16), nt, preferred_element_type=F32)
                   + lax.dot_general(e_new.astype(BF16), vwn_ref[0], nt, preferred_element_type=F32)) / jnp.maximum(lw, 1e-30)
            gate = _sigmoid(g_ref[0])
            o_ref[0] = gate[:, 0:1] * oc_ref[...] + gate[:, 1:2] * o_s + gate[:, 2:3] * o_w


def _feature_major(x):
    lead = x.shape[:-3]
    n = len(lead)
    return x.transpose(*range(n), n + 1, n + 2, n).reshape(*lead, NSA_KV_WIDTH, x.shape[-3])


def _nsa_sample(q, ks, vs, kw, vw, gates, pools, layer, win_k, win_v, page_table, w_cmp_k, w_cmp_v):
    b, t = q.shape[:2]
    n_pages = page_table.shape[1]
    assert t <= T_PAD and t <= BLOCK and PAGE_SIZE % BLOCK == 0
    pps = PAGES_PER_STEP
    assert n_pages % pps == 0
    n_groups = n_pages // pps
    nb = n_pages * PAGE_SIZE // BLOCK
    assert nb % LANES == 0
    wb = win_k.shape[1]
    n_pool = pools[0].shape[1]
    flat = [_feature_major(p).reshape(p.shape[0] * n_pool, NSA_KV_WIDTH, PAGE_SIZE) for p in pools]
    wt = lambda w: jnp.tile(w.T, (NSA_KV_HEADS, PAGE_SIZE // BLOCK))
    wck2, wcv2 = wt(w_cmp_k), wt(w_cmp_v)
    scale = NSA_HEAD_DIM ** -0.5
    q5 = (q * scale).reshape(b, t, NSA_KV_HEADS, NSA_GROUP, NSA_HEAD_DIM).transpose(0, 2, 3, 1, 4)
    q5 = jnp.pad(q5, ((0, 0), (0, 0), (0, 0), (0, T_PAD - t), (0, 0)))
    own = jnp.eye(NSA_KV_HEADS, dtype=F32)[None, :, None, None, :, None]
    qbd = (q5[:, :, :, :, None, :] * own).reshape(b, Q_ROWS, NSA_KV_WIDTH).astype(BF16)
    padn = lambda x: jnp.pad(x, ((0, 0), (0, NEW_PAD - t), (0, 0))).astype(BF16).transpose(0, 2, 1)
    g5 = gates.reshape(b, t, NSA_KV_HEADS, NSA_GROUP, 3).transpose(0, 2, 3, 1, 4)
    g5 = jnp.pad(g5, ((0, 0), (0, 0), (0, 0), (0, T_PAD - t), (0, 0))).reshape(b, Q_ROWS, 3)
    blocks_per_step = pps * PAGE_SIZE // BLOCK
    key_blk = jnp.arange(pps * PAGE_SIZE) // BLOCK
    expand = (jnp.arange(nb)[None, :, None] == (jnp.arange(n_groups)[:, None, None] * blocks_per_step + key_blk[None, None, :]))
    expand = expand.astype(BF16)
    expand_t = expand.transpose(0, 2, 1)
    per_b = lambda shape: pl.BlockSpec((1,) + shape, lambda i, p, g, pt: (i, 0, 0))
    const = lambda a: pl.BlockSpec(a.shape, lambda i, p, g, pt: (0,) * a.ndim)
    hbm = pl.BlockSpec(memory_space=pl.ANY)
    body = functools.partial(_nsa_sample_body, n_groups=n_groups, t_len=t, win_len=wb, page_base=layer * n_pool)
    o = pl.pallas_call(
        body,
        grid_spec=pltpu.PrefetchScalarGridSpec(
            num_scalar_prefetch=1,
            grid=(b, 2, n_groups),
            in_specs=[hbm, hbm, hbm, hbm, per_b((Q_ROWS, NSA_KV_WIDTH)), per_b((NSA_KV_WIDTH, NEW_PAD)), per_b((NSA_KV_WIDTH, NEW_PAD)),
                      per_b((NSA_KV_WIDTH, wb)), per_b((NSA_KV_WIDTH, wb)), per_b((NSA_KV_WIDTH, NEW_PAD)), per_b((NSA_KV_WIDTH, NEW_PAD)),
                      const(wck2), const(wcv2), per_b((Q_ROWS, 3)), const(expand), const(expand_t)],
            out_specs=per_b((Q_ROWS, NSA_KV_WIDTH)),
            scratch_shapes=[
                pltpu.VMEM((2, 2, NSA_KV_WIDTH, pps * PAGE_SIZE), F32),
                pltpu.SemaphoreType.DMA((2, 2)),
                pltpu.VMEM((NSA_KV_WIDTH, nb), F32), pltpu.VMEM((NSA_KV_WIDTH, nb), F32),
                pltpu.VMEM((Q_ROWS, nb), F32), pltpu.VMEM((Q_ROWS, NSA_KV_WIDTH), F32),
                pltpu.VMEM((Q_ROWS, 1), F32), pltpu.VMEM((Q_ROWS, 1), F32), pltpu.VMEM((Q_ROWS, NSA_KV_WIDTH), F32),
                pltpu.VMEM((n_groups, NSA_KV_WIDTH + nb, pps * PAGE_SIZE), BF16),
            ],
        ),
        out_shape=jax.ShapeDtypeStruct((b, Q_ROWS, NSA_KV_WIDTH), F32),
        compiler_params=pltpu.CompilerParams(dimension_semantics=("arbitrary",) * 3, vmem_limit_bytes=VMEM_LIMIT),
        name="nsa_sample",
    )(page_table, *flat, qbd, padn(ks), padn(vs), _feature_major(win_k), _feature_major(win_v), padn(kw), padn(vw),
      wck2, wcv2, g5, expand, expand_t)
    o6 = o.reshape(b, NSA_KV_HEADS, NSA_GROUP, T_PAD, NSA_KV_HEADS, NSA_HEAD_DIM)[:, :, :, :t]
    o5 = jnp.stack([o6[:, h, :, :, h, :] for h in range(NSA_KV_HEADS)], axis=1)
    return o5.transpose(0, 3, 1, 2, 4).reshape(b, t, NSA_WIDTH)


def kernel(x_prompt, x_sample, cache_cmp_k, cache_cmp_v, cache_sel_k, cache_sel_v, state_win_k, state_win_v, state_conv,
           state_ssm, page_table, norm_mix_pre, norm_mix_post, norm_ffn_pre, norm_ffn_post, w_in, w_cmp_k, w_cmp_v,
           nsa_out_norm, conv_w, a_log, dt_bias, gdn_norm, w_out, w_up, w_down):
    depth = w_in.shape[0]
    bp, sp, _ = x_prompt.shape
    bs, ts, _ = x_sample.shape
    past = page_table.shape[1] * PAGE_SIZE
    pos_p = jnp.arange(sp)
    pos_s = past + jnp.arange(ts)
    pools = (cache_cmp_k, cache_cmp_v, cache_sel_k, cache_sel_v)
    kv5 = lambda x, b, n: x.reshape(b, n, NSA_KV_HEADS, NSA_HEAD_DIM)
    y_p, y_s = x_prompt, x_sample
    p_states, s_states = [], []
    for l in range(depth):
        w_perm = _permute_w_in(w_in[l]).astype(BF16)
        wck2 = jnp.tile(w_cmp_k[l], (1, LANES // NSA_HEAD_DIM))
        wcv2 = jnp.tile(w_cmp_v[l], (1, LANES // NSA_HEAD_DIM))
        post_w = (nsa_out_norm[l], w_out[l].astype(BF16), norm_mix_post[l], norm_ffn_pre[l], w_up[l].astype(BF16),
                  w_down[l].astype(BF16), norm_ffn_post[l])
        gdn_w = (conv_w[l], a_log[l], dt_bias[l], gdn_norm[l])

        wb = min(WINDOW, sp)
        conv0 = jnp.zeros((CONV_WIDTH - 1, 3 * GDN_WIDTH), F32)
        ssm0 = jnp.zeros((GDN_HEADS, GDN_HEAD_DIM, GDN_HEAD_DIM), F32)
        ys, sts = [], []
        for b in range(bp):
            x2 = y_p[b]
            (_, _, _, ks, _, kw, _, qkv, z, small, ck, cv, qt, kct, vct, kst, vst, kwt, vwt) = _in_proj(
                x2, pos_p, norm_mix_pre[l], w_perm, wck2, wcv2, feature_major=True)
            o_nsa = _nsa_prompt(qt, ks, kw, vst, vwt, ck, cv, small)
            o_gdn, ssm = _gdn_prompt(qkv, z, small, conv0, ssm0, *gdn_w)
            ys.append(_post(o_nsa, o_gdn, x2, *post_w, nsa_feature_major=True))
            conv_state = jnp.concatenate([conv0, qkv], axis=0)[-(CONV_WIDTH - 1):]
            rows_major = lambda x: x.reshape(NSA_KV_HEADS, NSA_HEAD_DIM, sp).transpose(2, 0, 1)
            sts.append((rows_major(kct), rows_major(vct), rows_major(kst), rows_major(vst),
                        rows_major(kwt)[sp - wb:], rows_major(vwt)[sp - wb:], conv_state, ssm))
        y_p = jnp.stack(ys)
        p_states.append(tuple(jnp.stack(t) for t in zip(*sts)))

        n = bs * ts
        q, kc, vc, ks, vs, kw, vw, qkv, z, small, _, _ = _in_proj(
            jnp.pad(y_s.reshape(n, D_MODEL), ((0, (-n) % BLOCK), (0, 0))), jnp.pad(jnp.tile(pos_s, bs), (0, (-n) % BLOCK)),
            norm_mix_pre[l], w_perm, wck2, wcv2)
        r3 = lambda x: x[:n].reshape(bs, ts, -1)
        q, kc, vc, ks, vs, kw, vw, qkv, z, small = map(r3, (q, kc, vc, ks, vs, kw, vw, qkv, z, small))
        win_k, win_v = state_win_k[l], state_win_v[l]
        o_nsa = _nsa_sample(q, ks, vs, kw, vw, small[..., GATE_OFF:GATE_OFF + 3 * NSA_HEADS], pools, l, win_k, win_v,
                            page_table, w_cmp_k[l], w_cmp_v[l])
        o_gdn, ssm = _gdn_sample(qkv, z, small, state_conv[l], state_ssm[l], *gdn_w)
        y_s = _post(o_nsa.reshape(n, NSA_WIDTH), o_gdn.reshape(n, GDN_WIDTH), y_s.reshape(n, D_MODEL), *post_w).reshape(bs, ts, D_MODEL)
        conv_state = jnp.concatenate([state_conv[l], qkv], axis=1)[:, -(CONV_WIDTH - 1):]
        new_wk = jnp.concatenate([win_k, kv5(kw, bs, ts)], axis=1)[:, ts:]
        new_wv = jnp.concatenate([win_v, kv5(vw, bs, ts)], axis=1)[:, ts:]
        s_states.append((kv5(kc, bs, ts), kv5(vc, bs, ts), kv5(ks, bs, ts), kv5(vs, bs, ts), new_wk, new_wv, conv_state, ssm))
    p_out = [jnp.stack(t) for t in zip(*p_states)]
    s_out = [jnp.stack(t) for t in zip(*s_states)]
    return (y_p, y_s, *p_out, *s_out)
```

```python
import functools
import math

import jax
import jax.numpy as jnp
import numpy as np
from jax import lax
from jax.experimental import pallas as pl
from jax.experimental.pallas import tpu as pltpu

D_MODEL = 1024
PAGE_SIZE = 128
NSA_HEAD_DIM = 64
NSA_HEADS = 8
NSA_KV_HEADS = 2
NSA_GROUP = NSA_HEADS // NSA_KV_HEADS
NSA_WIDTH = NSA_HEADS * NSA_HEAD_DIM
NSA_KV_WIDTH = NSA_KV_HEADS * NSA_HEAD_DIM
BLOCK = 64
TOP_K = 16
WINDOW = 512
Q_BLOCK = 128
FORCED_SCORE = NSA_GROUP + 1.0
ROPE_THETA = 500000.0
ROPE_DIM = NSA_HEAD_DIM // 4
GDN_HEAD_DIM = 128
GDN_HEADS = 4
GDN_WIDTH = GDN_HEADS * GDN_HEAD_DIM
CONV_WIDTH = 4
GDN_CHUNK = 64
D_FF = 4 * D_MODEL
EPS = 1e-6
SPLIT_SIZES = (NSA_WIDTH,) + (NSA_KV_WIDTH,) * 6 + (3 * NSA_HEADS, 3 * GDN_WIDTH, GDN_WIDTH, GDN_HEADS, GDN_HEADS)

LANES = 128
SMALL_COLS = LANES
GATE_OFF, BETA_OFF, ALPHA_OFF = 0, 3 * NSA_HEADS, 3 * NSA_HEADS + GDN_HEADS
PROJ_COLS = NSA_WIDTH + 6 * NSA_KV_WIDTH + 3 * GDN_WIDTH + GDN_WIDTH + SMALL_COLS
VMEM_LIMIT = 56 * 1024 * 1024
NEG = -1e30
F32 = jnp.float32
BF16 = jnp.bfloat16
HI = lax.Precision.HIGHEST


def _rms(x, w):
    return x * lax.rsqrt(jnp.mean(x * x, axis=-1, keepdims=True) + EPS) * w


def _sigmoid(x):
    return 1.0 / (1.0 + jnp.exp(-x))


def _silu(x):
    return x * _sigmoid(x)


def _rope_slab(x, c, sa, sb):
    half = ROPE_DIM // 2
    return x * c + pltpu.roll(x, LANES - half, axis=1) * sa + pltpu.roll(x, half, axis=1) * sb


Q_SCALE = NSA_HEAD_DIM ** -0.5 * math.log2(math.e)


def _in_proj_body(x_ref, nw_ref, w_ref, rc_ref, rsa_ref, rsb_ref, wk_ref, wv_ref,
                  q_ref, kc_ref, vc_ref, ks_ref, vs_ref, kw_ref, vw_ref, qkv_ref, z_ref, sm_ref, ck_ref, cv_ref,
                  *fm_refs, feature_major):
    h = _rms(x_ref[...], nw_ref[...])
    p = jnp.dot(h.astype(BF16), w_ref[...], preferred_element_type=F32)
    c, sa, sb = rc_ref[...], rsa_ref[...], rsb_ref[...]
    rows = p.shape[0]
    off = 0
    for j in range(NSA_WIDTH // LANES):
        qs = _rope_slab(p[:, off:off + LANES], c, sa, sb)
        q_ref[:, j * LANES:(j + 1) * LANES] = qs
        if feature_major:
            qst = (qs * Q_SCALE).T.astype(BF16)
            heads_per_slab = LANES // NSA_HEAD_DIM
            kvh = j * heads_per_slab // NSA_GROUP
            for half in range(heads_per_slab):
                g = (j * heads_per_slab + half) % NSA_GROUP
                for blk in range(rows // Q_BLOCK):
                    fm_refs[0][kvh, blk, :, g * Q_BLOCK:(g + 1) * Q_BLOCK] = qst[
                        half * NSA_HEAD_DIM:(half + 1) * NSA_HEAD_DIM, blk * Q_BLOCK:(blk + 1) * Q_BLOCK]
        off += LANES
    kc = _rope_slab(p[:, off:off + LANES], c, sa, sb)
    kc_ref[...] = kc
    vc = p[:, off + LANES:off + 2 * LANES]
    vc_ref[...] = vc
    ks = _rope_slab(p[:, off + 2 * LANES:off + 3 * LANES], c, sa, sb)
    ks_ref[...] = ks
    vs = p[:, off + 3 * LANES:off + 4 * LANES]
    vs_ref[...] = vs
    kw = _rope_slab(p[:, off + 4 * LANES:off + 5 * LANES], c, sa, sb)
    kw_ref[...] = kw
    vw = p[:, off + 5 * LANES:off + 6 * LANES]
    vw_ref[...] = vw
    if feature_major:
        for ref, slab in zip(fm_refs[1:], (kc, vc, ks, vs, kw, vw)):
            ref[...] = slab.T
    off += 6 * LANES
    qkv_ref[...] = p[:, off:off + 3 * GDN_WIDTH]
    off += 3 * GDN_WIDTH
    z_ref[...] = p[:, off:off + GDN_WIDTH]
    off += GDN_WIDTH
    sm_ref[...] = p[:, off:off + SMALL_COLS]
    ck_ref[...] = jnp.sum(kc.reshape(rows // BLOCK, BLOCK, LANES) * wk_ref[...][None], axis=1)
    cv_ref[...] = jnp.sum(vc.reshape(rows // BLOCK, BLOCK, LANES) * wv_ref[...][None], axis=1)


def _permute_w_in(w_in):
    pts = np.cumsum((0,) + SPLIT_SIZES)
    seg = [w_in[:, int(pts[i]):int(pts[i + 1])] for i in range(len(SPLIT_SIZES))]
    q, kc, vc, ks, vs, kw, vw, gates, qkv, z, beta, alpha = seg
    small = jnp.concatenate([gates, beta, alpha], axis=1)
    small = jnp.pad(small, ((0, 0), (0, SMALL_COLS - small.shape[1])))
    return jnp.concatenate([q, kc, vc, ks, vs, kw, vw, qkv, z, small], axis=1)


def _rope_tables(pos):
    half = ROPE_DIM // 2
    inv_freq = ROPE_THETA ** (-jnp.arange(half, dtype=F32) / half)
    ang = pos.astype(F32)[:, None] * inv_freq[None, :]
    cos, sin = jnp.cos(ang), jnp.sin(ang)
    n = pos.shape[0]
    one = jnp.ones((n, NSA_HEAD_DIM - ROPE_DIM), F32)
    zero = jnp.zeros((n, NSA_HEAD_DIM - ROPE_DIM), F32)
    zh = jnp.zeros((n, half), F32)
    c = jnp.concatenate([cos, cos, one], axis=1)
    sa = jnp.concatenate([-sin, zh, zero], axis=1)
    sb = jnp.concatenate([zh, sin, zero], axis=1)
    return tuple(jnp.tile(t, (1, LANES // NSA_HEAD_DIM)) for t in (c, sa, sb))


def _in_proj(x2d, pos, norm_w, w_perm, wk2, wv2, feature_major=False):
    n = x2d.shape[0]
    tm = min(512, n)
    assert n % tm == 0 and tm % BLOCK == 0
    rc, rsa, rsb = _rope_tables(pos)
    row = lambda w: pl.BlockSpec((tm, w), lambda i: (i, 0))
    const = lambda a: pl.BlockSpec(a.shape, lambda i: (0,) * a.ndim)
    nw = norm_w.reshape(1, D_MODEL)
    widths = (NSA_WIDTH,) + (LANES,) * 6 + (3 * GDN_WIDTH, GDN_WIDTH, SMALL_COLS)
    out_shape = [jax.ShapeDtypeStruct((n, w), F32) for w in widths]
    out_shape += [jax.ShapeDtypeStruct((n // BLOCK, LANES), F32)] * 2
    out_specs = [row(w) for w in widths] + [pl.BlockSpec((tm // BLOCK, LANES), lambda i: (i, 0))] * 2
    if feature_major:
        assert tm % Q_BLOCK == 0
        qb = tm // Q_BLOCK
        out_shape += [jax.ShapeDtypeStruct((NSA_KV_HEADS, n // Q_BLOCK, NSA_HEAD_DIM, NSA_GROUP * Q_BLOCK), BF16)]
        out_specs += [pl.BlockSpec((NSA_KV_HEADS, qb, NSA_HEAD_DIM, NSA_GROUP * Q_BLOCK), lambda i: (0, i, 0, 0))]
        out_shape += [jax.ShapeDtypeStruct((LANES, n), F32)] * 6
        out_specs += [pl.BlockSpec((LANES, tm), lambda i: (0, i))] * 6
    return pl.pallas_call(
        functools.partial(_in_proj_body, feature_major=feature_major),
        grid=(n // tm,),
        in_specs=[row(D_MODEL), const(nw), const(w_perm), row(LANES), row(LANES), row(LANES), const(wk2), const(wv2)],
        out_specs=out_specs,
        out_shape=out_shape,
        compiler_params=pltpu.CompilerParams(dimension_semantics=("arbitrary",), vmem_limit_bytes=VMEM_LIMIT),
        name="in_proj",
    )(x2d, nw, w_perm, rc, rsa, rsb, wk2, wv2)


FF_CHUNK = 1024


def _post_body(on_ref, og_ref, x_ref, nn_ref, wo_ref, nmp_ref, nfp_ref, wu_ref, wd_ref, nfo_ref, y_ref, *, nsa_feature_major):
    if nsa_feature_major:
        ont = on_ref[...]
        o_nsa = (ont * lax.rsqrt(jnp.mean(ont * ont, axis=0, keepdims=True) + EPS) * nn_ref[...]).T
    else:
        o_nsa = _rms(on_ref[...], nn_ref[...])
    mix = jnp.concatenate([o_nsa, og_ref[...]], axis=1).astype(BF16)
    y = jnp.dot(mix, wo_ref[...], preferred_element_type=F32)
    y1 = x_ref[...] + _rms(y, nmp_ref[...])
    h = _rms(y1, nfp_ref[...]).astype(BF16)
    acc = jnp.zeros_like(y1)
    for c in range(D_FF // FF_CHUNK):
        u = jnp.dot(h, wu_ref[:, c * FF_CHUNK:(c + 1) * FF_CHUNK], preferred_element_type=F32)
        u = jnp.square(jnp.maximum(u, 0.0))
        acc = acc + jnp.dot(u.astype(BF16), wd_ref[c * FF_CHUNK:(c + 1) * FF_CHUNK, :], preferred_element_type=F32)
    y_ref[...] = y1 + _rms(acc, nfo_ref[...])


def _post(o_nsa, o_gdn, x2d, nsa_norm, w_out, n_mix_post, n_ffn_pre, w_up, w_down, n_ffn_post, nsa_feature_major=False):
    n = x2d.shape[0]
    tm = min(256, n)
    assert n % tm == 0
    row = lambda w: pl.BlockSpec((tm, w), lambda i: (i, 0))
    const = lambda a: pl.BlockSpec(a.shape, lambda i: (0,) * a.ndim)
    r = lambda v: v.reshape(1, -1)
    nn = nsa_norm.reshape(-1, 1) if nsa_feature_major else r(nsa_norm)
    nsa_spec = pl.BlockSpec((NSA_WIDTH, tm), lambda i: (0, i)) if nsa_feature_major else row(NSA_WIDTH)
    args = (o_nsa, o_gdn, x2d, nn, w_out, r(n_mix_post), r(n_ffn_pre), w_up, w_down, r(n_ffn_post))
    return pl.pallas_call(
        functools.partial(_post_body, nsa_feature_major=nsa_feature_major),
        grid=(n // tm,),
        in_specs=[nsa_spec, row(GDN_WIDTH), row(D_MODEL)] + [const(a) for a in args[3:]],
        out_specs=row(D_MODEL),
        out_shape=jax.ShapeDtypeStruct((n, D_MODEL), F32),
        compiler_params=pltpu.CompilerParams(dimension_semantics=("arbitrary",), vmem_limit_bytes=VMEM_LIMIT),
        name="out_proj_ffn",
    )(*args)


SEL_CHUNK = 512
SEL_AUG = 16
V_ROWS = NSA_HEAD_DIM + 16
SELECT_TIERS = 4


def _row_iota(shape):
    return lax.broadcasted_iota(jnp.int32, shape, 0)


def _topk_rows(score, blk, k):
    nb = score.shape[0]
    sel = jnp.zeros(score.shape, F32)
    for _ in range(k):
        m = jnp.max(score, axis=0, keepdims=True)
        idx = jnp.min(jnp.where(score == m, blk, nb), axis=0, keepdims=True)
        pick = blk == idx
        sel = jnp.where(pick, 1.0, sel)
        score = jnp.where(pick, -jnp.inf, score)
    return sel


REDUCE_WAYS = 8


def _col_reduce(x, op):
    rows = x.shape[0]
    if rows % (REDUCE_WAYS * 8) == 0 and rows > REDUCE_WAYS * 8:
        x = op(x.reshape(rows // (REDUCE_WAYS * 8), REDUCE_WAYS * 8, x.shape[1]), axis=0)
    return op(x, axis=0, keepdims=True)


def _softmax_rows(s, mask):
    m = jnp.max(jnp.where(mask, s, NEG), axis=0, keepdims=True)
    e = jnp.where(mask, jnp.exp2(s - m), 0.0)
    return e, 1.0 / jnp.maximum(jnp.sum(e, axis=0, keepdims=True), 1e-30)


def _nsa_prompt_body(qt_ref, ks_ref, kw_ref, vt_ref, ck_ref, cvt_ref, g_ref, o_ref,
                     sb_ref, s0_ref, s1_ref, e0_ref, e1_ref, oc_ref, *, seq, kc, wn):
    i = pl.program_id(1)
    start = i * Q_BLOCK
    nb = seq // BLOCK
    g4 = range(NSA_GROUP)
    slab = lambda a, g: a[:, g * Q_BLOCK:(g + 1) * Q_BLOCK]
    qt = qt_ref[0, 0]
    pos = start + lax.broadcasted_iota(jnp.int32, (1, Q_BLOCK), 1)

    def compress_and_select(r):
        sc = jnp.dot(ck_ref[0, 0:r], qt, preferred_element_type=F32)
        blk = _row_iota((r, Q_BLOCK))
        cmask = (blk + 1) * BLOCK - 1 <= pos
        ps = []
        for g in g4:
            e, inv = _softmax_rows(slab(sc, g), cmask)
            ps.append(e * inv)
        p_all = jnp.concatenate(ps, axis=1).astype(BF16)
        if r < nb:
            p_all = jnp.concatenate([p_all, jnp.zeros((nb - r, NSA_GROUP * Q_BLOCK), BF16)], axis=0)
        oc_ref[...] = jnp.dot(cvt_ref[0], p_all, preferred_element_type=F32)
        imp = ps[0]
        for g in range(1, NSA_GROUP):
            imp = imp + ps[g]
        cur = pos // BLOCK
        forced = (blk == 0) | (blk == cur) | (blk == cur - 1)
        score = jnp.where(forced | (blk > cur), -1.0, imp)
        sel = _topk_rows(score, blk, max(min(TOP_K, nb) - 3, 0))
        sel_bias = jnp.where(forced | ((score >= 0) & (sel > 0.0)), 0.0, NEG)
        for g in g4:
            sb_ref[0:r, g * Q_BLOCK:(g + 1) * Q_BLOCK] = sel_bias

    n_qb = seq // Q_BLOCK
    whole_chunks = nb % SELECT_TIERS == 0 and (nb // SELECT_TIERS) % (kc // BLOCK) == 0
    tiers = SELECT_TIERS if whole_chunks and n_qb % SELECT_TIERS == 0 else 1
    for t in range(tiers):
        pl.when(i * tiers // n_qb == t)(functools.partial(compress_and_select, (t + 1) * nb // tiers))
    o_c = oc_ref[...]

    bpc = kc // BLOCK
    pad_rows = jnp.zeros((max(SEL_AUG - bpc, 1), NSA_GROUP * Q_BLOCK), F32)
    pad_tail = jnp.zeros((LANES - NSA_HEAD_DIM - SEL_AUG, NSA_GROUP * Q_BLOCK), BF16)

    n_parts = NSA_GROUP
    part = kc // n_parts

    def q_aug(c):
        rows = sb_ref[pl.ds(pl.multiple_of(c * bpc, bpc), bpc), :]
        if bpc < SEL_AUG:
            rows = jnp.concatenate([rows, pad_rows], axis=0)
        return jnp.concatenate([qt, rows.astype(BF16), pad_tail], axis=0)

    def scores_part(c, qa, j):
        k0 = pl.multiple_of(c * kc + j * part, part)
        return jnp.dot(ks_ref[0, pl.ds(k0, part), :], qa, preferred_element_type=F32)

    def values_dot(c, e_ref):
        k0 = pl.multiple_of(c * kc, kc)
        return jnp.dot(vt_ref[0, 0:V_ROWS, pl.ds(k0, kc)], e_ref[...], preferred_element_type=F32)

    def sweep_step(c, carry, cur, nxt, last):
        m, acc, alpha_prev = carry
        pv = values_dot(jnp.maximum(c - 1, 0), nxt[1])
        qa = None if last else q_aug(c + 1)
        k0 = pl.multiple_of(c * kc, kc)
        ms, alphas = [], []
        for g in g4:
            if not last:
                nxt[0][g * part:(g + 1) * part, :] = scores_part(c + 1, qa, g)
            sg = cur[0][:, g * Q_BLOCK:(g + 1) * Q_BLOCK]
            if last:
                sg = sg + jnp.where(k0 + _row_iota((kc, Q_BLOCK)) <= pos, 0.0, NEG)
            mo = slab(m, g)
            cur[1][:, g * Q_BLOCK:(g + 1) * Q_BLOCK] = jnp.exp2((sg - mo).astype(BF16))
            mn = mo if last else jnp.maximum(mo, _col_reduce(sg, jnp.max))
            ms.append(mn)
            alphas.append(jnp.exp2(mo - mn))
        cat = lambda xs: jnp.concatenate(xs, axis=1)
        return cat(ms), (acc + pv) * alpha_prev, cat(alphas)

    def finish(c, carry, cur, nxt):
        _, acc, _ = sweep_step(c, carry, cur, nxt, True)
        return acc + values_dot(c, cur[1])

    width = NSA_GROUP * Q_BLOCK
    even, odd = (s0_ref, e0_ref), (s1_ref, e1_ref)
    qa0 = q_aug(0)
    for g in g4:
        s0_ref[g * part:(g + 1) * part, :] = scores_part(0, qa0, g)
    e1_ref[...] = jnp.zeros(e1_ref.shape, BF16)
    bias0 = jnp.where(_row_iota((kc, Q_BLOCK)) <= pos, 0.0, NEG)
    m0 = jnp.concatenate([_col_reduce(s0_ref[:, g * Q_BLOCK:(g + 1) * Q_BLOCK] + bias0, jnp.max) for g in g4], axis=1)
    init = (m0, jnp.zeros((V_ROWS, width), F32), jnp.ones((1, width), F32))
    c_diag = start // kc

    def pair(p, carry):
        carry = sweep_step(2 * p, carry, even, odd, False)
        return sweep_step(2 * p + 1, carry, odd, even, False)

    quad = lambda p, cr: pair(2 * p + 1, pair(2 * p, cr))
    octo = lambda p, cr: quad(2 * p + 1, quad(2 * p, cr))
    carry = lax.fori_loop(0, c_diag // 16, lambda p, cr: octo(2 * p + 1, octo(2 * p, cr)), init)
    carry = lax.fori_loop(c_diag // 16 * 2, c_diag // 8, octo, carry)
    carry = lax.fori_loop(c_diag // 8 * 2, c_diag // 4, quad, carry)
    carry = lax.fori_loop(c_diag // 4 * 2, c_diag // 2, pair, carry)
    acc_s = lax.cond(
        c_diag % 2 == 1,
        lambda cr: finish(c_diag, sweep_step(c_diag - 1, cr, even, odd, False), odd, even),
        lambda cr: finish(c_diag, cr, even, odd),
        carry)
    def exact_step(c, carry):
        m, acc = carry
        s = jnp.concatenate([scores_part(c, q_aug(c), j) for j in range(n_parts)], axis=0)
        s = s + jnp.concatenate([jnp.where(c * kc + _row_iota((kc, Q_BLOCK)) <= pos, 0.0, NEG)] * NSA_GROUP, axis=1)
        mn = jnp.maximum(m, jnp.max(s, axis=0, keepdims=True))
        e = jnp.exp2((s - mn).astype(BF16))
        k0 = pl.multiple_of(c * kc, kc)
        pv = jnp.dot(vt_ref[0, 0:V_ROWS, pl.ds(k0, kc)], e, preferred_element_type=F32)
        return mn, jnp.exp2(m - mn) * acc + pv

    l_s = acc_s[NSA_HEAD_DIM:NSA_HEAD_DIM + 1]
    sane = (jnp.abs(acc_s) < 3e38) & (l_s >= 1.0) & (l_s < 1e30)
    acc_s = lax.cond(
        jnp.max(jnp.where(sane, 0.0, 1.0)) > 0.0,
        lambda a: lax.fori_loop(0, c_diag + 1, exact_step,
                                (jnp.full((1, width), NEG, F32), jnp.zeros((V_ROWS, width), F32)))[1],
        lambda a: a,
        acc_s)
    o_s = acc_s[0:NSA_HEAD_DIM] * (1.0 / jnp.maximum(acc_s[NSA_HEAD_DIM:NSA_HEAD_DIM + 1], 1e-30))

    w0 = pl.multiple_of(jnp.maximum(start + Q_BLOCK - wn, 0), Q_BLOCK)
    sw = jnp.dot(kw_ref[0, pl.ds(w0, wn), :], qt, preferred_element_type=F32)
    dist = pos - (w0 + _row_iota((wn, Q_BLOCK)))
    wbias = jnp.where((dist >= 0) & (dist < WINDOW), 0.0, NEG)
    ews = []
    for g in g4:
        sg = slab(sw, g) + wbias
        ews.append(jnp.exp2((sg - jnp.max(sg, axis=0, keepdims=True)).astype(BF16)))
    ow = jnp.dot(vt_ref[0, V_ROWS:2 * V_ROWS, pl.ds(w0, wn)], jnp.concatenate(ews, axis=1), preferred_element_type=F32)
    o_w = ow[0:NSA_HEAD_DIM] * (1.0 / jnp.maximum(ow[NSA_HEAD_DIM:NSA_HEAD_DIM + 1], 1e-30))

    gate = _sigmoid(g_ref[0])
    for g in g4:
        o_ref[0, g] = (gate[3 * g:3 * g + 1] * slab(o_c, g) + gate[3 * g + 1:3 * g + 2] * slab(o_s, g)
                       + gate[3 * g + 2:3 * g + 3] * slab(o_w, g))


def _split_heads_t(x, n):
    return x.reshape(n, NSA_KV_HEADS, NSA_HEAD_DIM).transpose(1, 0, 2)


def _nsa_prompt(qt, ks, kw, vst, vwt, ck, cv, small):
    s = ks.shape[0]
    assert s % Q_BLOCK == 0
    n_qb, nb = s // Q_BLOCK, s // BLOCK
    kc = min(SEL_CHUNK, s)
    wn = min(WINDOW + Q_BLOCK, s)
    assert s % kc == 0
    bpc = kc // BLOCK
    assert kc % Q_BLOCK == 0 and bpc <= SEL_AUG
    onehot = ((jnp.arange(s) // BLOCK) % bpc)[:, None] == jnp.arange(LANES - NSA_HEAD_DIM)[None, :]
    ksa = jnp.concatenate([_split_heads_t(ks, s).astype(BF16),
                           jnp.broadcast_to(onehot.astype(BF16), (NSA_KV_HEADS, s, LANES - NSA_HEAD_DIM))], axis=-1)
    kwh = _split_heads_t(kw, s).astype(BF16)
    ones_fill = jnp.zeros((NSA_KV_HEADS, V_ROWS - NSA_HEAD_DIM, s), BF16).at[:, 0, :].set(1.0)
    fm = lambda x: x.reshape(NSA_KV_HEADS, NSA_HEAD_DIM, s).astype(BF16)
    vt = jnp.concatenate([fm(vst), ones_fill, fm(vwt), ones_fill], axis=1)
    ckh = _split_heads_t(ck, nb).astype(BF16)
    cvt = _split_heads_t(cv, nb).astype(BF16).transpose(0, 2, 1)
    gt = small.T[GATE_OFF:GATE_OFF + 3 * NSA_HEADS].reshape(NSA_KV_HEADS, NSA_GROUP * 3, s)
    gt = jnp.pad(gt, ((0, 0), (0, 16 - NSA_GROUP * 3), (0, 0)))
    body = functools.partial(_nsa_prompt_body, seq=s, kc=kc, wn=wn)
    o = pl.pallas_call(
        body,
        grid=(NSA_KV_HEADS, n_qb),
        in_specs=[
            pl.BlockSpec((1, 1, NSA_HEAD_DIM, NSA_GROUP * Q_BLOCK), lambda h, i: (h, i, 0, 0)),
            pl.BlockSpec((1, s, LANES), lambda h, i: (h, 0, 0)),
            pl.BlockSpec((1, s, NSA_HEAD_DIM), lambda h, i: (h, 0, 0)),
            pl.BlockSpec((1, 2 * V_ROWS, s), lambda h, i: (h, 0, 0)),
            pl.BlockSpec((1, nb, NSA_HEAD_DIM), lambda h, i: (h, 0, 0)),
            pl.BlockSpec((1, NSA_HEAD_DIM, nb), lambda h, i: (h, 0, 0)),
            pl.BlockSpec((1, 16, Q_BLOCK), lambda h, i: (h, 0, i)),
        ],
        out_specs=pl.BlockSpec((1, NSA_GROUP, NSA_HEAD_DIM, Q_BLOCK), lambda h, i: (h, 0, 0, i)),
        out_shape=jax.ShapeDtypeStruct((NSA_KV_HEADS, NSA_GROUP, NSA_HEAD_DIM, s), F32),
        scratch_shapes=[pltpu.VMEM((nb, NSA_GROUP * Q_BLOCK), F32)]
        + [pltpu.VMEM((kc, NSA_GROUP * Q_BLOCK), F32)] * 2 + [pltpu.VMEM((kc, NSA_GROUP * Q_BLOCK), BF16)] * 2
        + [pltpu.VMEM((NSA_HEAD_DIM, NSA_GROUP * Q_BLOCK), F32)],
        compiler_params=pltpu.CompilerParams(dimension_semantics=("arbitrary", "arbitrary"), vmem_limit_bytes=VMEM_LIMIT),
        name="nsa_prompt",
    )(qt, ksa, kwh, vt, ckh, cvt, gt)
    return o.reshape(NSA_WIDTH, s)


def _dot(a, b):
    return jnp.dot(a, b, precision=HI, preferred_element_type=F32)


def _dot_nt(a, b):
    return lax.dot_general(a, b, (((1,), (1,)), ((), ())), precision=HI, preferred_element_type=F32)


def _dot_tn(a, b):
    return lax.dot_general(a, b, (((0,), (0,)), ((), ())), precision=HI, preferred_element_type=F32)


def _dot1_nt(a, b):
    return lax.dot_general(a, b, (((1,), (1,)), ((), ())), preferred_element_type=F32)


def _split16(a):
    hi = a.astype(BF16)
    return hi, (a - hi.astype(F32)).astype(BF16)


def _dot3(a, b):
    return jnp.dot(a.astype(BF16), b.astype(BF16), preferred_element_type=F32)


def _softplus(x):
    return jnp.maximum(x, 0.0) + jnp.log(1.0 + jnp.exp(-jnp.abs(x)))


def _unit_lower_inverses(mats, ri, ci):
    c = mats[0].shape[0]
    eye = (ri == ci).astype(F32)
    base = 8
    same = ri // base == ci // base
    ds = [jnp.where(same, a, 0.0) for a in mats]
    d2s = [_dot3(d, d) for d in ds]
    xs = [(eye - d) + _dot3(eye - d, d2) for d, d2 in zip(ds, d2s)]
    d4s = [_dot3(d2, d2) for d2 in d2s]
    xs = [x + _dot3(x, d4) for x, d4 in zip(xs, d4s)]
    b = base
    while b < c:
        band = (ri // (2 * b) == ci // (2 * b)) & (ri // b != ci // b)
        lxs = [_dot3(jnp.where(band, a, 0.0), x) for a, x in zip(mats, xs)]
        xs = [x - _dot3(x, lx) for x, lx in zip(xs, lxs)]
        b *= 2
    return xs


SUBLANES = 8


def _conv_silu(x, prev, w):
    rowi = _row_iota((SUBLANES, x.shape[1]))
    conv = None
    for j in range(CONV_WIDTH - 1, -1, -1):
        if j == 0:
            xs = x
        else:
            xr = pltpu.roll(x, j, axis=0)
            head = jnp.where(rowi < j, pltpu.roll(prev, j, axis=0), xr[0:SUBLANES])
            xs = jnp.concatenate([head, xr[SUBLANES:]], axis=0)
        term = xs * w[CONV_WIDTH - 1 - j:CONV_WIDTH - j]
        conv = term if conv is None else conv + term
    return _silu(conv)


def _l2n(x):
    return x * lax.rsqrt(jnp.sum(x * x, axis=-1, keepdims=True) + EPS)


GDN_STEP_CHUNKS = 4


def _gdn_chunk_body(qkv_ref, sm_ref, smt_ref, prev0_ref, cw_ref, ar_ref, dr_ref, at_ref, dt_ref,
                    u_ref, w_ref, qe_ref, kd_ref, in_ref, dl_ref, xp_ref):
    cs = GDN_CHUNK

    @pl.when(pl.program_id(0) == 0)
    def _():
        xp_ref[...] = prev0_ref[...]

    x = qkv_ref[...]
    act = _conv_silu(x, xp_ref[...], cw_ref[...])
    xp_ref[...] = x[x.shape[0] - SUBLANES:]
    sm = sm_ref[...]
    ri = _row_iota((cs, cs))
    ci = lax.broadcasted_iota(jnp.int32, (cs, cs), 1)
    tri = (ri >= ci).astype(F32)
    tri_t = (ri <= ci).astype(F32)
    g_all = -jnp.exp(ar_ref[...]) * _softplus(sm + dr_ref[...])
    beta_all = _sigmoid(sm)
    scale = GDN_HEAD_DIM ** -0.5
    a_mats, rhs, where = [], [], []
    for c in range(GDN_STEP_CHUNKS):
        rows = slice(c * cs, (c + 1) * cs)
        gc_cols = _dot(tri, g_all[rows])
        g_rows = -jnp.exp(at_ref[...]) * _softplus(smt_ref[c] + dt_ref[...])
        gc_rows = _dot(g_rows, tri_t)
        intras, dls = [], []
        for h in range(GDN_HEADS):
            hs = slice(h * GDN_HEAD_DIM, (h + 1) * GDN_HEAD_DIM)
            q = _l2n(act[rows, hs]) * scale
            k = _l2n(act[rows, GDN_WIDTH + h * GDN_HEAD_DIM:GDN_WIDTH + (h + 1) * GDN_HEAD_DIM])
            v = act[rows, 2 * GDN_WIDTH + h * GDN_HEAD_DIM:2 * GDN_WIDTH + (h + 1) * GDN_HEAD_DIM]
            beta = beta_all[rows, BETA_OFF + h:BETA_OFF + h + 1]
            gc = gc_cols[:, ALPHA_OFF + h:ALPHA_OFF + h + 1]
            gr = gc_rows[GDN_HEADS + h:GDN_HEADS + h + 1, :]
            g_last = gc[cs - 1:cs, :]
            decay = jnp.where(ri >= ci, jnp.exp(gc - gr), 0.0)
            kb = k * beta
            k16 = k.astype(BF16)
            a_mats.append(jnp.where(ri > ci, _dot1_nt(kb.astype(BF16), k16) * decay, 0.0))
            rhs.append(jnp.concatenate([v * beta, kb * jnp.exp(gc)], axis=1))
            where.append((rows, hs))
            qe_ref[rows, hs] = (q * jnp.exp(gc)).astype(BF16)
            kd_ref[rows, hs] = (k * jnp.exp(g_last - gc)).astype(BF16)
            intras.append((_dot1_nt(q.astype(BF16), k16) * decay).astype(BF16))
            dls.append(jnp.broadcast_to(jnp.exp(g_last), (1, LANES)))
        in_ref[rows, :] = jnp.concatenate(intras, axis=1)
        dl_ref[c] = jnp.concatenate(dls + [jnp.zeros((2 * GDN_HEADS - GDN_HEADS, LANES), F32)], axis=0)
    for t, r, (rows, hs) in zip(_unit_lower_inverses(a_mats, ri, ci), rhs, where):
        sol = _dot3(t, r)
        u_ref[rows, hs] = sol[:, :GDN_HEAD_DIM]
        w_ref[rows, hs] = sol[:, GDN_HEAD_DIM:].astype(BF16)


def _gdn_scan_body(u_ref, w_ref, qe_ref, kd_ref, in_ref, dl_ref, z_ref, s0_ref, gn_ref, o_ref, sf_ref, s_ref):
    cs = GDN_CHUNK
    i = pl.program_id(0)

    @pl.when(i == 0)
    def _():
        s_ref[...] = s0_ref[...]

    heads = range(GDN_HEADS)
    hsl = [slice(h * GDN_HEAD_DIM, (h + 1) * GDN_HEAD_DIM) for h in heads]
    tn = (((0,), (0,)), ((), ()))
    ss = [s_ref[h] for h in heads]
    for c in range(GDN_STEP_CHUNKS):
        rows = slice(c * cs, (c + 1) * cs)
        s16 = [s.astype(BF16) for s in ss]
        v16 = [(u_ref[rows, hsl[h]] - jnp.dot(w_ref[rows, hsl[h]], s16[h], preferred_element_type=F32)).astype(BF16)
               for h in heads]
        ss = [ss[h] * dl_ref[c, h:h + 1, :] + lax.dot_general(kd_ref[rows, hsl[h]], v16[h], tn, preferred_element_type=F32)
              for h in heads]
        for h in heads:
            o = (jnp.dot(qe_ref[rows, hsl[h]], s16[h], preferred_element_type=F32)
                 + jnp.dot(in_ref[rows, h * cs:(h + 1) * cs], v16[h], preferred_element_type=F32))
            o_ref[rows, hsl[h]] = _rms(o, gn_ref[...]) * _silu(z_ref[rows, hsl[h]])
    for h in heads:
        s_ref[h] = ss[h]

    @pl.when(i == pl.num_programs(0) - 1)
    def _():
        sf_ref[...] = s_ref[...]


def _gdn_consts(a_log, dt_bias):
    lane = jnp.zeros((1, LANES), F32)
    ar = lane.at[0, ALPHA_OFF:ALPHA_OFF + GDN_HEADS].set(a_log)
    dr = lane.at[0, ALPHA_OFF:ALPHA_OFF + GDN_HEADS].set(dt_bias)
    col = jnp.zeros((2 * GDN_HEADS, 1), F32)
    at = jnp.broadcast_to(col.at[GDN_HEADS:, 0].set(a_log), (2 * GDN_HEADS, GDN_CHUNK))
    dt = jnp.broadcast_to(col.at[GDN_HEADS:, 0].set(dt_bias), (2 * GDN_HEADS, GDN_CHUNK))
    return ar, dr, at, dt


def _gdn_prompt(qkv, z, small, conv_prev, s0, conv_w, a_log, dt_bias, gdn_norm):
    n = qkv.shape[0]
    cs = GDN_CHUNK
    rows = GDN_STEP_CHUNKS * cs
    assert n % rows == 0
    nc = n // cs
    smt = small.T[BETA_OFF:BETA_OFF + 2 * GDN_HEADS].reshape(2 * GDN_HEADS, nc, cs).transpose(1, 0, 2)
    prev0 = jnp.pad(conv_prev, ((SUBLANES - (CONV_WIDTH - 1), 0), (0, 0)))
    ar, dr, at, dt = _gdn_consts(a_log, dt_bias)
    gn = gdn_norm.reshape(1, GDN_HEAD_DIM)
    row = lambda w: pl.BlockSpec((rows, w), lambda i: (i, 0))
    per_chunk = pl.BlockSpec((GDN_STEP_CHUNKS, 2 * GDN_HEADS, LANES), lambda i: (i, 0, 0))
    const = lambda a: pl.BlockSpec(a.shape, lambda i: (0,) * a.ndim)
    params = pltpu.CompilerParams(dimension_semantics=("arbitrary",), vmem_limit_bytes=VMEM_LIMIT)
    wide = lambda dt_: jax.ShapeDtypeStruct((n, GDN_WIDTH), dt_)
    u, w, qe, kd, intra, dl = pl.pallas_call(
        _gdn_chunk_body,
        grid=(n // rows,),
        in_specs=[row(3 * GDN_WIDTH), row(SMALL_COLS),
                  pl.BlockSpec((GDN_STEP_CHUNKS, 2 * GDN_HEADS, cs), lambda i: (i, 0, 0)),
                  const(prev0), const(conv_w), const(ar), const(dr), const(at), const(dt)],
        out_specs=[row(GDN_WIDTH)] * 4 + [row(GDN_HEADS * cs), per_chunk],
        out_shape=[wide(F32), wide(BF16), wide(BF16), wide(BF16), jax.ShapeDtypeStruct((n, GDN_HEADS * cs), BF16),
                   jax.ShapeDtypeStruct((nc, 2 * GDN_HEADS, LANES), F32)],
        scratch_shapes=[pltpu.VMEM((SUBLANES, 3 * GDN_WIDTH), F32)],
        compiler_params=params,
        name="gdn_chunks",
    )(qkv, small, smt, prev0, conv_w, ar, dr, at, dt)
    return pl.pallas_call(
        _gdn_scan_body,
        grid=(n // rows,),
        in_specs=[row(GDN_WIDTH)] * 4 + [row(GDN_HEADS * cs), per_chunk, row(GDN_WIDTH), const(s0), const(gn)],
        out_specs=[row(GDN_WIDTH), const(s0)],
        out_shape=[wide(F32), jax.ShapeDtypeStruct(s0.shape, F32)],
        scratch_shapes=[pltpu.VMEM(s0.shape, F32)],
        compiler_params=params,
        name="gdn_scan",
    )(u, w, qe, kd, intra, dl, z, s0, gn)


def _gdn_sample_pre_body(x0_ref, x1_ref, x2_ref, x3_ref, sm_ref, cw_ref, ar_ref, dr_ref, q_ref, k_ref, v_ref, a_ref, b_ref):
    w = cw_ref[...]
    conv = x0_ref[...] * w[0:1] + x1_ref[...] * w[1:2] + x2_ref[...] * w[2:3] + x3_ref[...] * w[3:4]
    act = _silu(conv)
    sm = sm_ref[...]
    a_cols = jnp.exp(-jnp.exp(ar_ref[...]) * _softplus(sm + dr_ref[...]))
    b_cols = _sigmoid(sm)
    rows = sm.shape[0]
    for h in range(GDN_HEADS):
        hs = slice(h * GDN_HEAD_DIM, (h + 1) * GDN_HEAD_DIM)
        q_ref[:, hs] = _l2n(act[:, hs]) * GDN_HEAD_DIM ** -0.5
        k_ref[:, hs] = _l2n(act[:, GDN_WIDTH + h * GDN_HEAD_DIM:GDN_WIDTH + (h + 1) * GDN_HEAD_DIM])
        a_ref[:, hs] = jnp.broadcast_to(a_cols[:, ALPHA_OFF + h:ALPHA_OFF + h + 1], (rows, GDN_HEAD_DIM))
        b_ref[:, hs] = jnp.broadcast_to(b_cols[:, BETA_OFF + h:BETA_OFF + h + 1], (rows, GDN_HEAD_DIM))
    v_ref[...] = act[:, 2 * GDN_WIDTH:]


GDN_SAMPLE_SEQS = 8


def _gdn_sample_body(qt_ref, kt_ref, v_ref, a_ref, b_ref, z_ref, s0_ref, gn_ref, o_ref, sf_ref, *, t_len):
    gn = gn_ref[...]

    def seq(bl, carry):
        v_all, a_all, b_all, z_all = v_ref[bl], a_ref[bl], b_ref[bl], z_ref[bl]
        heads = range(GDN_HEADS)
        hsl = [slice(h * GDN_HEAD_DIM, (h + 1) * GDN_HEAD_DIM) for h in heads]
        ss = [s0_ref[bl, h] for h in heads]
        qts, kts = [qt_ref[bl, h] for h in heads], [kt_ref[bl, h] for h in heads]
        outs = [[] for _ in heads]
        for t in range(t_len):
            sks = [jnp.sum(ss[h] * kts[h][:, t:t + 1], axis=0, keepdims=True) for h in heads]
            deltas = [b_all[t:t + 1, hsl[h]] * (v_all[t:t + 1, hsl[h]] - a_all[t:t + 1, hsl[h]] * sks[h]) for h in heads]
            ss = [a_all[t:t + 1, hsl[h]] * ss[h] + kts[h][:, t:t + 1] * deltas[h] for h in heads]
            for h in heads:
                outs[h].append(jnp.sum(ss[h] * qts[h][:, t:t + 1], axis=0, keepdims=True))
        for h in heads:
            sf_ref[bl, h] = ss[h]
            o = jnp.concatenate(outs[h], axis=0)
            o_ref[bl, :, hsl[h]] = _rms(o, gn) * _silu(z_all[:, hsl[h]])
        return carry

    lax.fori_loop(0, GDN_SAMPLE_SEQS, seq, 0)


def _gdn_sample(qkv, z, small, conv_prev, s0, conv_w, a_log, dt_bias, gdn_norm):
    b, t = qkv.shape[:2]
    n = b * t
    xp = jnp.concatenate([conv_prev, qkv], axis=1)
    xs = [xp[:, j:j + t].reshape(n, 3 * GDN_WIDTH) for j in range(CONV_WIDTH)]
    ar, dr, _, _ = _gdn_consts(a_log, dt_bias)
    tm = min(256, n)
    assert n % tm == 0
    row = lambda w: pl.BlockSpec((tm, w), lambda i: (i, 0))
    const = lambda a: pl.BlockSpec(a.shape, lambda i: (0,) * a.ndim)
    q, k, v, av, bv = pl.pallas_call(
        _gdn_sample_pre_body,
        grid=(n // tm,),
        in_specs=[row(3 * GDN_WIDTH)] * 4 + [row(SMALL_COLS), const(conv_w), const(ar), const(dr)],
        out_specs=[row(GDN_WIDTH)] * 5,
        out_shape=[jax.ShapeDtypeStruct((n, GDN_WIDTH), F32)] * 5,
        compiler_params=pltpu.CompilerParams(dimension_semantics=("arbitrary",), vmem_limit_bytes=VMEM_LIMIT),
        name="gdn_sample_pre",
    )(*xs, small.reshape(n, SMALL_COLS), conv_w, ar, dr)
    tr = lambda x: x.reshape(b, t, GDN_HEADS, GDN_HEAD_DIM).transpose(0, 2, 3, 1)
    r3 = lambda x: x.reshape(b, t, GDN_WIDTH)
    bb = GDN_SAMPLE_SEQS
    assert b % bb == 0
    gn = gdn_norm.reshape(1, GDN_HEAD_DIM)
    tspec = pl.BlockSpec((bb, GDN_HEADS, GDN_HEAD_DIM, t), lambda i: (i, 0, 0, 0))
    rspec = pl.BlockSpec((bb, t, GDN_WIDTH), lambda i: (i, 0, 0))
    sspec = pl.BlockSpec((bb, GDN_HEADS, GDN_HEAD_DIM, GDN_HEAD_DIM), lambda i: (i, 0, 0, 0))
    o, sf = pl.pallas_call(
        functools.partial(_gdn_sample_body, t_len=t),
        grid=(b // bb,),
        in_specs=[tspec, tspec, rspec, rspec, rspec, rspec, sspec, const(gn)],
        out_specs=[rspec, sspec],
        out_shape=[jax.ShapeDtypeStruct((b, t, GDN_WIDTH), F32), jax.ShapeDtypeStruct(s0.shape, F32)],
        compiler_params=pltpu.CompilerParams(dimension_semantics=("arbitrary",), vmem_limit_bytes=VMEM_LIMIT),
        name="gdn_sample",
    )(tr(q), tr(k), r3(v), r3(av), r3(bv), z, s0, gn)
    return o, sf


PAGES_PER_STEP = 64
T_PAD = 8
Q_ROWS = NSA_KV_HEADS * NSA_GROUP * T_PAD
NEW_PAD = LANES


def _nsa_sample_body(pt_ref, ckp_ref, cvp_ref, skp_ref, svp_ref, q_ref, kn_ref, vn_ref, wk_ref, wv_ref, kwn_ref, vwn_ref,
                     wck_ref, wcv_ref, g_ref, ex_ref, ext_ref, o_ref,
                     buf_ref, sem_ref, ck_ref, cv_ref, selm_ref, oc_ref, m_ref, l_ref, acc_ref, ka_ref,
                     *, n_groups, t_len, win_len, page_base):
    b, ph, g = pl.program_id(0), pl.program_id(1), pl.program_id(2)
    nb_, np_, ng_ = pl.num_programs(0), pl.num_programs(1), pl.num_programs(2)
    step = (b * np_ + ph) * ng_ + g
    pps = PAGES_PER_STEP
    rows_per_step = pps * PAGE_SIZE
    blocks_per_step = rows_per_step // BLOCK

    def copies(pool_k, pool_v, bb, gg, slot):
        out = []
        for p in range(pps):
            page = page_base + pt_ref[bb, gg * pps + p]
            for kv, pool in enumerate((pool_k, pool_v)):
                out.append(pltpu.make_async_copy(pool.at[page], buf_ref.at[slot, kv, :, pl.ds(p * PAGE_SIZE, PAGE_SIZE)],
                                                 sem_ref.at[slot, kv]))
        return out

    def start(bb, pp, gg, slot):
        @pl.when(pp == 0)
        def _():
            for n, cp in enumerate(copies(ckp_ref, cvp_ref, bb, gg, slot)):
                cp.start(priority=n % 2)

        @pl.when(pp == 1)
        def _():
            for n, cp in enumerate(copies(skp_ref, svp_ref, bb, gg, slot)):
                cp.start(priority=n % 2)

    slot = step % 2

    @pl.when(step == 0)
    def _():
        start(b, ph, g, slot)
        for gg in range(n_groups):
            ka_ref[gg, NSA_KV_WIDTH:, :] = ex_ref[gg]

    nxt = step + 1

    @pl.when(nxt < nb_ * np_ * ng_)
    def _():
        start(nxt // (np_ * ng_), (nxt // ng_) % np_, nxt % ng_, 1 - slot)

    for cp in copies(ckp_ref, cvp_ref, b, g, slot):
        cp.wait()

    q = q_ref[0]
    rowt = _row_iota((Q_ROWS, LANES)) % T_PAD
    lane = lax.broadcasted_iota(jnp.int32, (Q_ROWS, LANES), 1)
    new_mask = (lane <= rowt) & (lane < t_len)

    nt = (((1,), (1,)), ((), ()))

    kvw = NSA_KV_WIDTH

    def weighted(kv, w_ref):
        w = w_ref[...]
        return jnp.concatenate([buf_ref[slot, kv, :, p * PAGE_SIZE:(p + 1) * PAGE_SIZE] * w for p in range(pps)],
                               axis=1).astype(BF16)

    @pl.when(ph == 0)
    def _():
        @pl.when(g == 0)
        def _():
            ck_ref[...] = jnp.zeros(ck_ref.shape, F32)
            cv_ref[...] = jnp.zeros(cv_ref.shape, F32)

        ck_ref[...] += jnp.dot(weighted(0, wck_ref), ext_ref[g], preferred_element_type=F32)
        cv_ref[...] += jnp.dot(weighted(1, wcv_ref), ext_ref[g], preferred_element_type=F32)

        @pl.when(g == n_groups - 1)
        def _():
            nb = ck_ref.shape[1]
            s = jnp.dot(q, ck_ref[...].astype(BF16), preferred_element_type=F32)
            m = jnp.max(s, axis=1, keepdims=True)
            e = jnp.exp(s - m)
            p = e / jnp.maximum(jnp.sum(e, axis=1, keepdims=True), 1e-30)
            oc_ref[...] = lax.dot_general(p.astype(BF16), cv_ref[...].astype(BF16), nt, preferred_element_type=F32)
            imps = []
            for h in range(NSA_KV_HEADS):
                r = h * NSA_GROUP * T_PAD
                imp = p[r:r + T_PAD]
                for gg in range(1, NSA_GROUP):
                    imp = imp + p[r + gg * T_PAD:r + (gg + 1) * T_PAD]
                imps.append(imp)
            imps.append(jnp.zeros((nb - NSA_KV_HEADS * T_PAD, nb), F32))
            imp_t = jnp.concatenate(imps, axis=0).T
            bl = _row_iota((nb, nb))
            score = jnp.where((bl == 0) | (bl == nb - 1), FORCED_SCORE, imp_t)
            sel = _topk_rows(score, bl, min(TOP_K - 1, nb)).T
            for h in range(NSA_KV_HEADS):
                for gg in range(NSA_GROUP):
                    r = (h * NSA_GROUP + gg) * T_PAD
                    selm_ref[r:r + T_PAD, :] = sel[h * T_PAD:(h + 1) * T_PAD]
            m_ref[...] = jnp.full(m_ref.shape, NEG, F32)
            l_ref[...] = jnp.zeros(l_ref.shape, F32)
            acc_ref[...] = jnp.zeros(acc_ref.shape, F32)

    def flash(s, mask, vals):
        m_old = m_ref[...]
        if mask is None:
            m_new = jnp.maximum(m_old, jnp.max(s, axis=1, keepdims=True))
            e = jnp.exp(s - m_new)
        else:
            m_new = jnp.maximum(m_old, jnp.max(jnp.where(mask, s, NEG), axis=1, keepdims=True))
            e = jnp.where(mask, jnp.exp(s - m_new), 0.0)
        a = jnp.exp(m_old - m_new)
        m_ref[...] = m_new
        l_ref[...] = a * l_ref[...] + jnp.sum(e, axis=1, keepdims=True)
        acc_ref[...] = a * acc_ref[...] + lax.dot_general(e.astype(BF16), vals, nt, preferred_element_type=F32)

    @pl.when(ph == 1)
    def _():
        ka_ref[g, 0:kvw, :] = buf_ref[slot, 0].astype(BF16)
        vb = buf_ref[slot, 1].astype(BF16)
        q_aug = jnp.concatenate([q, jnp.where(selm_ref[...] > 0.0, 0.0, NEG).astype(BF16)], axis=1)
        s = jnp.dot(q_aug, ka_ref[g], preferred_element_type=F32)
        flash(s, None, vb)

        @pl.when(g == n_groups - 1)
        def _():
            flash(jnp.dot(q, kn_ref[0], preferred_element_type=F32), new_mask, vn_ref[0])
            o_s = acc_ref[...] / jnp.maximum(l_ref[...], 1e-30)
            wk = wk_ref[0].astype(BF16)
            s_old = jnp.dot(q, wk, preferred_element_type=F32)
            s_new = jnp.dot(q, kwn_ref[0], preferred_element_type=F32)
            j = lax.broadcasted_iota(jnp.int32, (Q_ROWS, win_len), 1)
            old_mask = j > win_len - WINDOW + _row_iota((Q_ROWS, win_len)) % T_PAD
            mw = jnp.maximum(jnp.max(jnp.where(old_mask, s_old, NEG), axis=1, keepdims=True),
                             jnp.max(jnp.where(new_mask, s_new, NEG), axis=1, keepdims=True))
            e_old = jnp.where(old_mask, jnp.exp(s_old - mw), 0.0)
            e_new = jnp.where(new_mask, jnp.exp(s_new - mw), 0.0)
            lw = jnp.sum(e_old, axis=1, keepdims=True) + jnp.sum(e_new, axis=1, keepdims=True)
            o_w = (lax.dot_general(e_old.astype(BF16), wv_ref[0].astype(BF16), nt, preferred_element_type=F32)
                   + lax.dot_general(e_new.astype(BF16), vwn_ref[0], nt, preferred_element_type=F32)) / jnp.maximum(lw, 1e-30)
            gate = _sigmoid(g_ref[0])
            o_ref[0] = gate[:, 0:1] * oc_ref[...] + gate[:, 1:2] * o_s + gate[:, 2:3] * o_w


def _feature_major(x):
    lead = x.shape[:-3]
    n = len(lead)
    return x.transpose(*range(n), n + 1, n + 2, n).reshape(*lead, NSA_KV_WIDTH, x.shape[-3])


def _nsa_sample(q, ks, vs, kw, vw, gates, pools, layer, win_k, win_v, page_table, w_cmp_k, w_cmp_v):
    b, t = q.shape[:2]
    n_pages = page_table.shape[1]
    assert t <= T_PAD and t <= BLOCK and PAGE_SIZE % BLOCK == 0
    pps = PAGES_PER_STEP
    assert n_pages % pps == 0
    n_groups = n_pages // pps
    nb = n_pages * PAGE_SIZE // BLOCK
    assert nb % LANES == 0
    wb = win_k.shape[1]
    n_pool = pools[0].shape[1]
    flat = [_feature_major(p).reshape(p.shape[0] * n_pool, NSA_KV_WIDTH, PAGE_SIZE) for p in pools]
    wt = lambda w: jnp.tile(w.T, (NSA_KV_HEADS, PAGE_SIZE // BLOCK))
    wck2, wcv2 = wt(w_cmp_k), wt(w_cmp_v)
    scale = NSA_HEAD_DIM ** -0.5
    q5 = (q * scale).reshape(b, t, NSA_KV_HEADS, NSA_GROUP, NSA_HEAD_DIM).transpose(0, 2, 3, 1, 4)
    q5 = jnp.pad(q5, ((0, 0), (0, 0), (0, 0), (0, T_PAD - t), (0, 0)))
    own = jnp.eye(NSA_KV_HEADS, dtype=F32)[None, :, None, None, :, None]
    qbd = (q5[:, :, :, :, None, :] * own).reshape(b, Q_ROWS, NSA_KV_WIDTH).astype(BF16)
    padn = lambda x: jnp.pad(x, ((0, 0), (0, NEW_PAD - t), (0, 0))).astype(BF16).transpose(0, 2, 1)
    g5 = gates.reshape(b, t, NSA_KV_HEADS, NSA_GROUP, 3).transpose(0, 2, 3, 1, 4)
    g5 = jnp.pad(g5, ((0, 0), (0, 0), (0, 0), (0, T_PAD - t), (0, 0))).reshape(b, Q_ROWS, 3)
    blocks_per_step = pps * PAGE_SIZE // BLOCK
    key_blk = jnp.arange(pps * PAGE_SIZE) // BLOCK
    expand = (jnp.arange(nb)[None, :, None] == (jnp.arange(n_groups)[:, None, None] * blocks_per_step + key_blk[None, None, :]))
    expand = expand.astype(BF16)
    expand_t = expand.transpose(0, 2, 1)
    per_b = lambda shape: pl.BlockSpec((1,) + shape, lambda i, p, g, pt: (i, 0, 0))
    const = lambda a: pl.BlockSpec(a.shape, lambda i, p, g, pt: (0,) * a.ndim)
    hbm = pl.BlockSpec(memory_space=pl.ANY)
    body = functools.partial(_nsa_sample_body, n_groups=n_groups, t_len=t, win_len=wb, page_base=layer * n_pool)
    o = pl.pallas_call(
        body,
        grid_spec=pltpu.PrefetchScalarGridSpec(
            num_scalar_prefetch=1,
            grid=(b, 2, n_groups),
            in_specs=[hbm, hbm, hbm, hbm, per_b((Q_ROWS, NSA_KV_WIDTH)), per_b((NSA_KV_WIDTH, NEW_PAD)), per_b((NSA_KV_WIDTH, NEW_PAD)),
                      per_b((NSA_KV_WIDTH, wb)), per_b((NSA_KV_WIDTH, wb)), per_b((NSA_KV_WIDTH, NEW_PAD)), per_b((NSA_KV_WIDTH, NEW_PAD)),
                      const(wck2), const(wcv2), per_b((Q_ROWS, 3)), const(expand), const(expand_t)],
            out_specs=per_b((Q_ROWS, NSA_KV_WIDTH)),
            scratch_shapes=[
                pltpu.VMEM((2, 2, NSA_KV_WIDTH, pps * PAGE_SIZE), F32),
                pltpu.SemaphoreType.DMA((2, 2)),
                pltpu.VMEM((NSA_KV_WIDTH, nb), F32), pltpu.VMEM((NSA_KV_WIDTH, nb), F32),
                pltpu.VMEM((Q_ROWS, nb), F32), pltpu.VMEM((Q_ROWS, NSA_KV_WIDTH), F32),
                pltpu.VMEM((Q_ROWS, 1), F32), pltpu.VMEM((Q_ROWS, 1), F32), pltpu.VMEM((Q_ROWS, NSA_KV_WIDTH), F32),
                pltpu.VMEM((n_groups, NSA_KV_WIDTH + nb, pps * PAGE_SIZE), BF16),
            ],
        ),
        out_shape=jax.ShapeDtypeStruct((b, Q_ROWS, NSA_KV_WIDTH), F32),
        compiler_params=pltpu.CompilerParams(dimension_semantics=("arbitrary",) * 3, vmem_limit_bytes=VMEM_LIMIT),
        name="nsa_sample",
    )(page_table, *flat, qbd, padn(ks), padn(vs), _feature_major(win_k), _feature_major(win_v), padn(kw), padn(vw),
      wck2, wcv2, g5, expand, expand_t)
    o6 = o.reshape(b, NSA_KV_HEADS, NSA_GROUP, T_PAD, NSA_KV_HEADS, NSA_HEAD_DIM)[:, :, :, :t]
    o5 = jnp.stack([o6[:, h, :, :, h, :] for h in range(NSA_KV_HEADS)], axis=1)
    return o5.transpose(0, 3, 1, 2, 4).reshape(b, t, NSA_WIDTH)


def kernel(x_prompt, x_sample, cache_cmp_k, cache_cmp_v, cache_sel_k, cache_sel_v, state_win_k, state_win_v, state_conv,
           state_ssm, page_table, norm_mix_pre, norm_mix_post, norm_ffn_pre, norm_ffn_post, w_in, w_cmp_k, w_cmp_v,
           nsa_out_norm, conv_w, a_log, dt_bias, gdn_norm, w_out, w_up, w_down):
    depth = w_in.shape[0]
    bp, sp, _ = x_prompt.shape
    bs, ts, _ = x_sample.shape
    past = page_table.shape[1] * PAGE_SIZE
    pos_p = jnp.arange(sp)
    pos_s = past + jnp.arange(ts)
    pools = (cache_cmp_k, cache_cmp_v, cache_sel_k, cache_sel_v)
    kv5 = lambda x, b, n: x.reshape(b, n, NSA_KV_HEADS, NSA_HEAD_DIM)
    y_p, y_s = x_prompt, x_sample
    p_states, s_states = [], []
    for l in range(depth):
        w_perm = _permute_w_in(w_in[l]).astype(BF16)
        wck2 = jnp.tile(w_cmp_k[l], (1, LANES // NSA_HEAD_DIM))
        wcv2 = jnp.tile(w_cmp_v[l], (1, LANES // NSA_HEAD_DIM))
        post_w = (nsa_out_norm[l], w_out[l].astype(BF16), norm_mix_post[l], norm_ffn_pre[l], w_up[l].astype(BF16),
                  w_down[l].astype(BF16), norm_ffn_post[l])
        gdn_w = (conv_w[l], a_log[l], dt_bias[l], gdn_norm[l])

        wb = min(WINDOW, sp)
        conv0 = jnp.zeros((CONV_WIDTH - 1, 3 * GDN_WIDTH), F32)
        ssm0 = jnp.zeros((GDN_HEADS, GDN_HEAD_DIM, GDN_HEAD_DIM), F32)
        ys, sts = [], []
        for b in range(bp):
            x2 = y_p[b]
            (_, _, _, ks, _, kw, _, qkv, z, small, ck, cv, qt, kct, vct, kst, vst, kwt, vwt) = _in_proj(
                x2, pos_p, norm_mix_pre[l], w_perm, wck2, wcv2, feature_major=True)
            o_nsa = _nsa_prompt(qt, ks, kw, vst, vwt, ck, cv, small)
            o_gdn, ssm = _gdn_prompt(qkv, z, small, conv0, ssm0, *gdn_w)
            ys.append(_post(o_nsa, o_gdn, x2, *post_w, nsa_feature_major=True))
            conv_state = jnp.concatenate([conv0, qkv], axis=0)[-(CONV_WIDTH - 1):]
            rows_major = lambda x: x.reshape(NSA_KV_HEADS, NSA_HEAD_DIM, sp).transpose(2, 0, 1)
            sts.append((rows_major(kct), rows_major(vct), rows_major(kst), rows_major(vst),
                        rows_major(kwt)[sp - wb:], rows_major(vwt)[sp - wb:], conv_state, ssm))
        y_p = jnp.stack(ys)
        p_states.append(tuple(jnp.stack(t) for t in zip(*sts)))

        n = bs * ts
        q, kc, vc, ks, vs, kw, vw, qkv, z, small, _, _ = _in_proj(
            jnp.pad(y_s.reshape(n, D_MODEL), ((0, (-n) % BLOCK), (0, 0))), jnp.pad(jnp.tile(pos_s, bs), (0, (-n) % BLOCK)),
            norm_mix_pre[l], w_perm, wck2, wcv2)
        r3 = lambda x: x[:n].reshape(bs, ts, -1)
        q, kc, vc, ks, vs, kw, vw, qkv, z, small = map(r3, (q, kc, vc, ks, vs, kw, vw, qkv, z, small))
        win_k, win_v = state_win_k[l], state_win_v[l]
        o_nsa = _nsa_sample(q, ks, vs, kw, vw, small[..., GATE_OFF:GATE_OFF + 3 * NSA_HEADS], pools, l, win_k, win_v,
                            page_table, w_cmp_k[l], w_cmp_v[l])
        o_gdn, ssm = _gdn_sample(qkv, z, small, state_conv[l], state_ssm[l], *gdn_w)
        y_s = _post(o_nsa.reshape(n, NSA_WIDTH), o_gdn.reshape(n, GDN_WIDTH), y_s.reshape(n, D_MODEL), *post_w).reshape(bs, ts, D_MODEL)
        conv_state = jnp.concatenate([state_conv[l], qkv], axis=1)[:, -(CONV_WIDTH - 1):]
        new_wk = jnp.concatenate([win_k, kv5(kw, bs, ts)], axis=1)[:, ts:]
        new_wv = jnp.concatenate([win_v, kv5(vw, bs, ts)], axis=1)[:, ts:]
        s_states.append((kv5(kc, bs, ts), kv5(vc, bs, ts), kv5(ks, bs, ts), kv5(vs, bs, ts), new_wk, new_wv, conv_state, ssm))
    p_out = [jnp.stack(t) for t in zip(*p_states)]
    s_out = [jnp.stack(t) for t in zip(*s_states)]
    return (y_p, y_s, *p_out, *s_out)
```
